```python
import jax, jax.numpy as jnp
from jax import lax
import numpy as np

D_MODEL = 1024
BATCH = 8
SEQ = 8192
DEPTH = 4

N_EVEN = (DEPTH + 1) // 2
N_ODD = DEPTH // 2
EPS = 1e-5
CONV_DIM = D_MODEL // 2
CONV_GROUPS = 8
CONV_WIDTH = 3
GMLP_DIM = D_MODEL // 2
GMLP_HEADS = 4
GMLP_HEAD_DIM = GMLP_DIM // GMLP_HEADS
CHUNK = 128
MIX_IN = 3 * CONV_DIM + 2 * GMLP_DIM
HEAD_DIM = 64
N_Q_HEADS = D_MODEL // HEAD_DIM
N_KV_HEADS = max(1, N_Q_HEADS // 8)
Q_PER_KV = N_Q_HEADS // N_KV_HEADS
WINDOW = 128
ROPE_DIM = HEAD_DIM // 4
ROPE_THETA = 500000.0
ATTN_SCALE = HEAD_DIM ** -0.5
QKV_DIM = (N_Q_HEADS + 2 * N_KV_HEADS) * HEAD_DIM
MAX_POS_OFFSET = 4096
FFN_DIM = ((8 * D_MODEL // 3 + 255) // 256) * 256
N_EXPERTS = 8
TOP_K = 2
EXPERT_DIM = 7 * D_MODEL // 2
MOE_BLOCK = 256

kernel_name = "hybrid_conv_gmlp_swa_moe_trunk"


def rms_norm(x, g):
    xf = x.astype(jnp.float32)
    y = xf * lax.rsqrt(jnp.mean(xf * xf, axis=-1, keepdims=True) + EPS)
    return (y * g.astype(jnp.float32)).astype(x.dtype)


def layer_norm(x, g, b):
    xf = x.astype(jnp.float32)
    mu = jnp.mean(xf, axis=-1, keepdims=True)
    var = jnp.mean(jnp.square(xf - mu), axis=-1, keepdims=True)
    y = (xf - mu) * lax.rsqrt(var + EPS)
    return (y * g.astype(jnp.float32) + b.astype(jnp.float32)).astype(x.dtype)


def swiglu(x, wg, wu, wd):
    return (jax.nn.silu(x @ wg) * (x @ wu)) @ wd


def causal_depthwise_conv(h, w):
    K = w.shape[-1]
    S = h.shape[1]
    hp = jnp.pad(h, ((0, 0), (K - 1, 0), (0, 0)))
    out = hp[:, 0:S, :] * w[:, 0]
    for k in range(1, K):
        out = out + hp[:, k:k + S, :] * w[:, k]
    return out


def gmlp_spatial_gate(u, v, ln_g, ln_b, w_s, b_s):
    B, S, _ = v.shape
    nc = S // CHUNK
    vn = layer_norm(v, ln_g, ln_b).reshape(B, nc, CHUNK, GMLP_HEADS, GMLP_HEAD_DIM)
    causal = jnp.tril(jnp.ones((CHUNK, CHUNK), dtype=bool))
    w = jnp.where(causal[None], w_s, jnp.zeros_like(w_s)).astype(v.dtype)
    mixed = jnp.einsum('hij,bcjhd->bcihd', w, vn) + b_s.T.astype(v.dtype)[None, None, :, :, None]
    return u * mixed.reshape(B, S, GMLP_DIM)


def conv_gmlp_mixer(xn, w_in, conv_w, ln_g, ln_b, w_s, b_s, w_out):
    z = xn @ w_in
    a_b, a_c, a_x, b_u, b_v = jnp.split(
        z, [CONV_DIM, 2 * CONV_DIM, 3 * CONV_DIM, 3 * CONV_DIM + GMLP_DIM], axis=-1)
    y_a = a_b * causal_depthwise_conv(a_c * a_x, conv_w)
    y_b = gmlp_spatial_gate(jax.nn.gelu(b_u, approximate=False), jax.nn.gelu(b_v, approximate=False),
                            ln_g, ln_b, w_s, b_s)
    return jnp.concatenate([y_a, y_b], axis=-1) @ w_out


def partial_rope(t, cos, sin):
    half = ROPE_DIM // 2
    tr = t[..., :ROPE_DIM].astype(jnp.float32)
    t1, t2 = tr[..., :half], tr[..., half:]
    rot = jnp.concatenate([t1 * cos - t2 * sin, t2 * cos + t1 * sin], axis=-1).astype(t.dtype)
    return jnp.concatenate([rot, t[..., ROPE_DIM:]], axis=-1)


def swa_sink_attention(xn, positions, w_qkv, b_qkv, sinks, w_o, b_o):
    B, S, _ = xn.shape
    nb = S // WINDOW
    qkv = xn @ w_qkv + b_qkv
    q, k, v = jnp.split(qkv, [N_Q_HEADS * HEAD_DIM, (N_Q_HEADS + N_KV_HEADS) * HEAD_DIM], axis=-1)
    q = q.reshape(B, S, N_Q_HEADS, HEAD_DIM)
    k = k.reshape(B, S, N_KV_HEADS, HEAD_DIM)
    v = v.reshape(B, S, N_KV_HEADS, HEAD_DIM)
    inv_freq = ROPE_THETA ** (-jnp.arange(0, ROPE_DIM, 2, dtype=jnp.float32) / ROPE_DIM)
    ang = positions.astype(jnp.float32)[..., None] * inv_freq
    cos = jnp.cos(ang)[:, :, None, :]
    sin = jnp.sin(ang)[:, :, None, :]
    q = partial_rope(q, cos, sin)
    k = partial_rope(k, cos, sin)
    qb = q.reshape(B, nb, WINDOW, N_KV_HEADS, Q_PER_KV, HEAD_DIM)

    def band(t):
        tb = t.reshape(B, nb, WINDOW, N_KV_HEADS, HEAD_DIM)
        prev = jnp.pad(tb, ((0, 0), (1, 0), (0, 0), (0, 0), (0, 0)))[:, :-1]
        return jnp.concatenate([prev, tb], axis=2)

    kb, vb = band(k), band(v)
    s = jnp.einsum('bnqhgd,bnshd->bnhgqs', qb, kb, preferred_element_type=jnp.float32) * ATTN_SCALE
    qi = jnp.arange(WINDOW)[:, None]
    sj = jnp.arange(2 * WINDOW)[None, :]
    rel = qi + WINDOW - sj
    local = (rel >= 0) & (rel < WINDOW)
    valid = (jnp.arange(nb)[:, None] * WINDOW - WINDOW + sj) >= 0
    mask = local[None] & valid[:, None, :]
    s = jnp.where(mask[None, :, None, None], s, -jnp.inf)
    sink = sinks.astype(jnp.float32).reshape(N_KV_HEADS, Q_PER_KV)[None, None, :, :, None]
    m = jnp.maximum(jnp.max(s, axis=-1), sink)
    p = jnp.exp(s - m[..., None])
    denom = jnp.sum(p, axis=-1) + jnp.exp(sink - m)
    probs = (p / denom[..., None]).astype(vb.dtype)
    o = jnp.einsum('bnhgqs,bnshd->bnqhgd', probs, vb).reshape(B, S, N_Q_HEADS * HEAD_DIM)
    return o @ w_o + b_o


def moe_swiglu(xn, router_w, wg, wu, wd):
    B, S, D = xn.shape
    N = B * S
    NP = N * TOP_K
    xf = xn.reshape(N, D)
    logits = jnp.dot(xf, router_w, preferred_element_type=jnp.float32)
    top_vals, top_idx = lax.top_k(logits, TOP_K)
    gates = jax.nn.softmax(top_vals, axis=-1)
    expert_ids = top_idx.reshape(NP).astype(jnp.int32)
    token_ids = jnp.repeat(jnp.arange(N, dtype=jnp.int32), TOP_K)
    pair_gates = gates.reshape(NP)
    order = jnp.argsort(expert_ids)
    sorted_e = expert_ids[order]
    sizes = jnp.bincount(expert_ids, length=N_EXPERTS).astype(jnp.int32)
    padded = ((sizes + MOE_BLOCK - 1) // MOE_BLOCK) * MOE_BLOCK
    starts = jnp.cumsum(sizes) - sizes
    pends = jnp.cumsum(padded)
    pstarts = pends - padded
    dest = pstarts[sorted_e] + jnp.arange(NP, dtype=jnp.int32) - starts[sorted_e]
    n_rows = (-(-NP // MOE_BLOCK)) * MOE_BLOCK + N_EXPERTS * MOE_BLOCK
    n_blocks = n_rows // MOE_BLOCK
    row_token = jnp.zeros((n_rows,), jnp.int32).at[dest].set(token_ids[order])
    row_gate = jnp.zeros((n_rows,), jnp.float32).at[dest].set(pair_gates[order])
    block_start = jnp.arange(n_blocks, dtype=jnp.int32) * MOE_BLOCK
    block_expert = jnp.minimum(jnp.searchsorted(pends, block_start, side='right'), N_EXPERTS - 1)
    x_rows = xf[row_token].reshape(n_blocks, MOE_BLOCK, D)

    def expert_block(args):
        xb, e = args
        return (jax.nn.silu(xb @ wg[e]) * (xb @ wu[e])) @ wd[e]

    y_rows = lax.map(expert_block, (x_rows, block_expert)).reshape(n_rows, D)
    y = jnp.zeros((N, D), xn.dtype).at[row_token].add(y_rows * row_gate[:, None].astype(xn.dtype))
    return y.reshape(B, S, D)


def setup_inputs(seed: int = 0) -> dict:
    key = jax.random.key(seed)
    ks = iter(jax.random.split(key, 32))

    def nrm(shape, scale):
        return jax.random.normal(next(ks), shape, jnp.float32) * scale

    def gain(shape):
        return 1.0 + nrm(shape, 0.05)

    NE, NO, D = N_EVEN, N_ODD, D_MODEL
    x = nrm((BATCH, SEQ, D), 1.0)
    positions = (jax.random.randint(next(ks), (BATCH, 1), 0, MAX_POS_OFFSET, dtype=jnp.int32)
                 + jnp.arange(SEQ, dtype=jnp.int32)[None, :])
    return {
        "x": x,
        "positions": positions,
        "final_norm_g": gain((D,)),
        "ev_norm1_g": gain((NE, D)),
        "ev_w_in": nrm((NE, D, MIX_IN), D ** -0.5),
        "ev_conv_w": nrm((NE, CONV_DIM, CONV_WIDTH), CONV_WIDTH ** -0.5),
        "ev_ln_g": gain((NE, GMLP_DIM)),
        "ev_ln_b": nrm((NE, GMLP_DIM), 0.02),
        "ev_spatial_w": nrm((NE, GMLP_HEADS, CHUNK, CHUNK), CHUNK ** -0.5),
        "ev_spatial_b": 1.0 + nrm((NE, GMLP_HEADS, CHUNK), 0.1),
        "ev_w_out": nrm((NE, CONV_DIM + GMLP_DIM, D), (CONV_DIM + GMLP_DIM) ** -0.5),
        "ev_norm2_g": gain((NE, D)),
        "ev_ffn_wg": nrm((NE, D, FFN_DIM), D ** -0.5),
        "ev_ffn_wu": nrm((NE, D, FFN_DIM), D ** -0.5),
        "ev_ffn_wd": nrm((NE, FFN_DIM, D), FFN_DIM ** -0.5),
        "od_norm1_g": gain((NO, D)),
        "od_w_qkv": nrm((NO, D, QKV_DIM), D ** -0.5),
        "od_b_qkv": nrm((NO, QKV_DIM), 0.02),
        "od_sinks": nrm((NO, N_Q_HEADS), 1.0),
        "od_w_o": nrm((NO, N_Q_HEADS * HEAD_DIM, D), (N_Q_HEADS * HEAD_DIM) ** -0.5),
        "od_b_o": nrm((NO, D), 0.02),
        "od_norm2_g": gain((NO, D)),
        "od_router_w": nrm((NO, D, N_EXPERTS), D ** -0.5),
        "od_exp_wg": nrm((NO, N_EXPERTS, D, EXPERT_DIM), D ** -0.5),
        "od_exp_wu": nrm((NO, N_EXPERTS, D, EXPERT_DIM), D ** -0.5),
        "od_exp_wd": nrm((NO, N_EXPERTS, EXPERT_DIM, D), EXPERT_DIM ** -0.5),
    }


def reference(x, positions, final_norm_g,
              ev_norm1_g, ev_w_in, ev_conv_w, ev_ln_g, ev_ln_b, ev_spatial_w, ev_spatial_b,
              ev_w_out, ev_norm2_g, ev_ffn_wg, ev_ffn_wu, ev_ffn_wd,
              od_norm1_g, od_w_qkv, od_b_qkv, od_sinks, od_w_o, od_b_o, od_norm2_g,
              od_router_w, od_exp_wg, od_exp_wu, od_exp_wd):
    h = x
    for layer in range(DEPTH):
        i = layer // 2
        if layer % 2 == 0:
            h = h + conv_gmlp_mixer(rms_norm(h, ev_norm1_g[i]), ev_w_in[i], ev_conv_w[i],
                                    ev_ln_g[i], ev_ln_b[i], ev_spatial_w[i], ev_spatial_b[i],
                                    ev_w_out[i])
            h = h + swiglu(rms_norm(h, ev_norm2_g[i]), ev_ffn_wg[i], ev_ffn_wu[i], ev_ffn_wd[i])
        else:
            h = h + swa_sink_attention(rms_norm(h, od_norm1_g[i]), positions, od_w_qkv[i],
                                       od_b_qkv[i], od_sinks[i], od_w_o[i], od_b_o[i])
            h = h + moe_swiglu(rms_norm(h, od_norm2_g[i]), od_router_w[i], od_exp_wg[i],
                               od_exp_wu[i], od_exp_wd[i])
    return rms_norm(h, final_norm_g)
```

```python
import functools

import jax
import jax.numpy as jnp
import numpy as np
from jax import lax
from jax.experimental import pallas as pl
from jax.experimental.pallas import tpu as pltpu
from jax.experimental.pallas import tpu_sc as plsc

F32 = jnp.float32
BF16 = jnp.bfloat16

EPS = 1e-5
CHUNK = 128
GMLP_HEADS = 4
CONV_WIDTH = 3
HEAD_DIM = 64
WINDOW = 128
ROPE_DIM = HEAD_DIM // 4
ROPE_THETA = 500000.0
ATTN_SCALE = HEAD_DIM ** -0.5
N_EXPERTS = 8
LANES = 128
VMEM_LIMIT = 56 * 1024 * 1024

MIXER_ROWS = 512
FFN_ROWS = 1024
PROJ_ROWS = 1024
ATTN_ROWS = 256
ROUTER_ROWS = 1024
MOE_ROWS = 512
SC_WORKERS = 32
SC_GATHER_WORDS = 256
SC_GATHER_WINDOW = 128


def _row_block(n, pref):
    b = min(n, pref)
    while n % b:
        b -= LANES
    return b


def _col_block(f, pref):
    b = min(f, pref)
    b -= b % LANES
    while f % b:
        b -= LANES
    return b


def _params(sem):
    return pltpu.CompilerParams(dimension_semantics=sem, vmem_limit_bytes=VMEM_LIMIT)


def _rms(x, g):
    return x * lax.rsqrt(jnp.mean(x * x, axis=-1, keepdims=True) + EPS) * g


def _gelu(x):
    return 0.5 * x * (1.0 + lax.erf(x * np.float32(np.sqrt(0.5))))


def _full(shape):
    return pl.BlockSpec(shape, lambda *_: (0,) * len(shape))


def _mixer_kernel(h_ref, g1_ref, win_ref, cw_ref, lng_ref, lnb_ref, ws_ref, bst_ref,
                  wout_ref, o_ref, tail_ref, yb_ref, *, blocks_per_seq):
    rows = h_ref.shape[0]
    cd = cw_ref.shape[1]
    gd = lng_ref.shape[1]
    hd = gd // GMLP_HEADS
    i = pl.program_id(0)

    x = h_ref[...]
    xn = _rms(x, g1_ref[...]).astype(BF16)
    z = jnp.dot(xn, win_ref[...], preferred_element_type=F32)
    a_b = z[:, 0:cd]
    a_c = z[:, cd:2 * cd]
    a_x = z[:, 2 * cd:3 * cd]
    b_u = z[:, 3 * cd:3 * cd + gd]
    b_v = z[:, 3 * cd + gd:3 * cd + 2 * gd]

    g = a_c * a_x

    @pl.when(i % blocks_per_seq == 0)
    def _():
        tail_ref[...] = jnp.zeros_like(tail_ref)

    tail = tail_ref[...]
    row = lax.broadcasted_iota(jnp.int32, g.shape, 0)
    gm1 = jnp.where(row == 0, tail[7:8], pltpu.roll(g, 1, 0))
    gm2 = jnp.where(row == 0, tail[6:7], jnp.where(row == 1, tail[7:8], pltpu.roll(g, 2, 0)))
    tail_ref[...] = g[rows - 8:rows]
    cw = cw_ref[...]
    y_a = a_b * (gm2 * cw[0:1] + gm1 * cw[1:2] + g * cw[2:3])

    u = _gelu(b_u)
    v = _gelu(b_v)
    mu = jnp.mean(v, axis=-1, keepdims=True)
    vc = v - mu
    var = jnp.mean(vc * vc, axis=-1, keepdims=True)
    vn = (vc * lax.rsqrt(var + EPS) * lng_ref[...] + lnb_ref[...]).astype(BF16)
    ri = lax.broadcasted_iota(jnp.int32, (CHUNK, CHUNK), 0)
    ci = lax.broadcasted_iota(jnp.int32, (CHUNK, CHUNK), 1)
    causal = ri >= ci
    bst = bst_ref[...]
    for k in range(GMLP_HEADS):
        w_k = jnp.where(causal, ws_ref[k], 0.0).astype(BF16)
        b_k = bst[:, k:k + 1]
        for c in range(rows // CHUNK):
            rs = slice(c * CHUNK, (c + 1) * CHUNK)
            cs = slice(k * hd, (k + 1) * hd)
            mixed = jnp.dot(w_k, vn[rs, cs], preferred_element_type=F32) + b_k
            yb_ref[rs, cs] = (u[rs, cs] * mixed).astype(BF16)

    out = jnp.dot(y_a.astype(BF16), wout_ref[0:cd, :], preferred_element_type=F32)
    out = out + jnp.dot(yb_ref[...], wout_ref[cd:cd + gd, :], preferred_element_type=F32)
    o_ref[...] = x + out


def _mixer(h, seq, g1, w_in, conv_w, ln_g, ln_b, w_s, b_s, w_out):
    n, d = h.shape
    rows = _row_block(seq, MIXER_ROWS)
    cd = conv_w.shape[0]
    gd = ln_g.shape[0]
    kern = functools.partial(_mixer_kernel, blocks_per_seq=seq // rows)
    return pl.pallas_call(
        kern,
        grid=(n // rows,),
        in_specs=[
            pl.BlockSpec((rows, d), lambda i: (i, 0)),
            _full((1, d)),
            _full(w_in.shape),
            _full((CONV_WIDTH, cd)),
            _full((1, gd)),
            _full((1, gd)),
            _full(w_s.shape),
            _full((CHUNK, GMLP_HEADS)),
            _full(w_out.shape),
        ],
        out_specs=pl.BlockSpec((rows, d), lambda i: (i, 0)),
        out_shape=jax.ShapeDtypeStruct((n, d), F32),
        scratch_shapes=[pltpu.VMEM((8, cd), F32), pltpu.VMEM((rows, gd), BF16)],
        compiler_params=_params(("arbitrary",)),
        name="mixer",
    )(h, g1.reshape(1, d), w_in.astype(BF16), conv_w.T, ln_g.reshape(1, gd),
      ln_b.reshape(1, gd), w_s, b_s.T, w_out.astype(BF16))


def _swiglu_step(xn, wg_ref, wu_ref, wd_ref, acc_ref):
    h1 = jnp.dot(xn, wg_ref[0], preferred_element_type=F32)
    h2 = jnp.dot(xn, wu_ref[0], preferred_element_type=F32)
    a = (h1 / (1.0 + jnp.exp(-h1)) * h2).astype(BF16)
    acc_ref[...] += jnp.dot(a, wd_ref[0], preferred_element_type=F32)


def _dense_ffn_kernel(x_ref, g_ref, wg_ref, wu_ref, wd_ref, o_ref, xn_ref, acc_ref):
    f = pl.program_id(1)

    @pl.when(f == 0)
    def _():
        xn_ref[...] = _rms(x_ref[...], g_ref[...]).astype(BF16)
        acc_ref[...] = jnp.zeros_like(acc_ref)

    _swiglu_step(xn_ref[...], wg_ref, wu_ref, wd_ref, acc_ref)

    @pl.when(f == pl.num_programs(1) - 1)
    def _():
        o_ref[...] = x_ref[...] + acc_ref[...]


def _dense_ffn(h, g, wg, wu, wd):
    n, d = h.shape
    fdim = wg.shape[1]
    rows = _row_block(n, FFN_ROWS)
    cols = _col_block(fdim, fdim // 2)
    return pl.pallas_call(
        _dense_ffn_kernel,
        grid=(n // rows, fdim // cols),
        in_specs=[
            pl.BlockSpec((rows, d), lambda i, f: (i, 0)),
            pl.BlockSpec((1, d), lambda i, f: (0, 0)),
            pl.BlockSpec((1, d, cols), lambda i, f: (0, 0, f)),
            pl.BlockSpec((1, d, cols), lambda i, f: (0, 0, f)),
            pl.BlockSpec((1, cols, d), lambda i, f: (0, f, 0)),
        ],
        out_specs=pl.BlockSpec((rows, d), lambda i, f: (i, 0)),
        out_shape=jax.ShapeDtypeStruct((n, d), F32),
        scratch_shapes=[pltpu.VMEM((rows, d), BF16), pltpu.VMEM((rows, d), F32)],
        compiler_params=_params(("arbitrary", "arbitrary")),
        name="dense_ffn",
    )(h, g.reshape(1, d), wg.astype(BF16)[None], wu.astype(BF16)[None], wd.astype(BF16)[None])


def _unpack_bf16_pair(packed):
    lo = lax.bitcast_convert_type(packed << 16, F32).astype(BF16)
    hi = lax.bitcast_convert_type(packed & jnp.uint32(0xFFFF0000), F32).astype(BF16)
    return lo, hi


def _moe_ffn_kernel(be_ref, x_ref, wg_ref, wu_ref, wd_ref, o_ref, xn_ref, acc_ref):
    del be_ref
    f = pl.program_id(1)
    half = x_ref.shape[1]

    @pl.when(f == 0)
    def _():
        lo, hi = _unpack_bf16_pair(x_ref[...])
        xn_ref[:, 0:half] = lo
        xn_ref[:, half:2 * half] = hi
        acc_ref[...] = jnp.zeros_like(acc_ref)

    _swiglu_step(xn_ref[...], wg_ref, wu_ref, wd_ref, acc_ref)

    @pl.when(f == pl.num_programs(1) - 1)
    def _():
        o_ref[...] = acc_ref[...]


def _moe_ffn(x_sorted, block_expert, wg, wu, wd):
    n_rows, half = x_sorted.shape
    d = 2 * half
    fdim = wg.shape[2]
    rows = MOE_ROWS
    cols = _col_block(fdim, fdim // 4)
    grid_spec = pltpu.PrefetchScalarGridSpec(
        num_scalar_prefetch=1,
        grid=(n_rows // rows, fdim // cols),
        in_specs=[
            pl.BlockSpec((rows, half), lambda i, f, be: (i, 0)),
            pl.BlockSpec((1, d, cols), lambda i, f, be: (be[i], 0, f)),
            pl.BlockSpec((1, d, cols), lambda i, f, be: (be[i], 0, f)),
            pl.BlockSpec((1, cols, d), lambda i, f, be: (be[i], f, 0)),
        ],
        out_specs=pl.BlockSpec((rows, d), lambda i, f, be: (i, 0)),
        scratch_shapes=[pltpu.VMEM((rows, d), BF16), pltpu.VMEM((rows, d), F32)],
    )
    return pl.pallas_call(
        _moe_ffn_kernel,
        grid_spec=grid_spec,
        out_shape=jax.ShapeDtypeStruct((n_rows, d), F32),
        compiler_params=_params(("arbitrary", "arbitrary")),
        name="moe_ffn",
    )(block_expert, x_sorted, wg.astype(BF16), wu.astype(BF16), wd.astype(BF16))


def _qkv_kernel(h_ref, g_ref, pos_ref, w_ref, b_ref, invf_ref, m1_ref, m2_ref,
                q_ref, kv_ref, *, q_dim):
    x = h_ref[...]
    xn = _rms(x, g_ref[...]).astype(BF16)
    z = jnp.dot(xn, w_ref[...], preferred_element_type=F32) + b_ref[...]
    ang = pos_ref[...].astype(F32) * invf_ref[...]
    cos = jnp.cos(ang)
    sin = jnp.sin(ang)
    s_lo = sin * m1_ref[...]
    s_hi = sin * m2_ref[...]
    half = ROPE_DIM // 2

    def rope(t):
        return t * cos + pltpu.roll(t, LANES - half, 1) * s_lo + pltpu.roll(t, half, 1) * s_hi

    for j in range(q_dim // LANES):
        cs = slice(j * LANES, (j + 1) * LANES)
        q_ref[:, cs] = (rope(z[:, cs]) * ATTN_SCALE).astype(BF16)
    kv_ref[:, 0:LANES] = rope(z[:, q_dim:q_dim + LANES]).astype(BF16)
    kv_ref[:, LANES:2 * LANES] = z[:, q_dim + LANES:q_dim + 2 * LANES].astype(BF16)


def _rope_lane_tables():
    lane = np.arange(LANES) % HEAD_DIM
    half = ROPE_DIM // 2
    first = (lane < half).astype(np.float32)
    second = ((lane >= half) & (lane < ROPE_DIM)).astype(np.float32)
    return lane, -first[None, :], second[None, :]


def _qkv(h, g, positions, w_qkv, b_qkv, q_dim):
    n, d = h.shape
    qkv_dim = w_qkv.shape[1]
    assert qkv_dim == q_dim + 2 * LANES
    rows = _row_block(n, PROJ_ROWS)
    lane, m1, m2 = _rope_lane_tables()
    inv_freq = ROPE_THETA ** (-jnp.arange(0, ROPE_DIM, 2, dtype=F32) / ROPE_DIM)
    invf = jnp.where(lane < ROPE_DIM, inv_freq[lane % (ROPE_DIM // 2)], 0.0).reshape(1, LANES)
    kern = functools.partial(_qkv_kernel, q_dim=q_dim)
    return pl.pallas_call(
        kern,
        grid=(n // rows,),
        in_specs=[
            pl.BlockSpec((rows, d), lambda i: (i, 0)),
            _full((1, d)),
            pl.BlockSpec((rows, 1), lambda i: (i, 0)),
            _full(w_qkv.shape),
            _full((1, qkv_dim)),
            _full((1, LANES)),
            _full((1, LANES)),
            _full((1, LANES)),
        ],
        out_specs=[
            pl.BlockSpec((rows, q_dim), lambda i: (i, 0)),
            pl.BlockSpec((rows, 2 * LANES), lambda i: (i, 0)),
        ],
        out_shape=[
            jax.ShapeDtypeStruct((n, q_dim), BF16),
            jax.ShapeDtypeStruct((n, 2 * LANES), BF16),
        ],
        compiler_params=_params(("arbitrary",)),
        name="qkv_rope",
    )(h, g.reshape(1, d), positions.reshape(n, 1), w_qkv.astype(BF16),
      b_qkv.reshape(1, qkv_dim), invf, jnp.asarray(m1), jnp.asarray(m2))


def _attn_kernel(sink_ref, q_ref, kc_ref, vc_ref, kp_ref, vp_ref, o_ref, *, n_q_heads):
    rows = q_ref.shape[0]
    n_kv = kc_ref.shape[1] // HEAD_DIM
    group = n_q_heads // n_kv
    j = pl.program_id(1)
    qi = lax.broadcasted_iota(jnp.int32, (WINDOW, 2 * WINDOW), 0)
    sj = lax.broadcasted_iota(jnp.int32, (WINDOW, 2 * WINDOW), 1)
    dist = sj - qi
    local = (dist >= 1) & (dist <= WINDOW)
    for n in range(rows // WINDOW):
        if n == 0:
            kcat = jnp.concatenate([kp_ref[...], kc_ref[0:WINDOW, :]], axis=0)
            vcat = jnp.concatenate([vp_ref[...], vc_ref[0:WINDOW, :]], axis=0)
            mask = local & ((sj >= WINDOW) | (j > 0))
        else:
            kcat = kc_ref[(n - 1) * WINDOW:(n + 1) * WINDOW, :]
            vcat = vc_ref[(n - 1) * WINDOW:(n + 1) * WINDOW, :]
            mask = local
        rs = slice(n * WINDOW, (n + 1) * WINDOW)
        for kh in range(n_kv):
            k_h = kcat[:, kh * HEAD_DIM:(kh + 1) * HEAD_DIM]
            v_h = vcat[:, kh * HEAD_DIM:(kh + 1) * HEAD_DIM]
            for gi in range(group):
                hq = kh * group + gi
                cs = slice(hq * HEAD_DIM, (hq + 1) * HEAD_DIM)
                s = lax.dot_general(q_ref[rs, cs], k_h, (((1,), (1,)), ((), ())),
                                    preferred_element_type=F32)
                s = jnp.where(mask, s, -jnp.inf)
                sink = sink_ref[hq]
                m = jnp.maximum(jnp.max(s, axis=-1, keepdims=True), sink)
                p = jnp.exp(s - m)
                den = jnp.sum(p, axis=-1, keepdims=True) + jnp.exp(sink - m)
                o = jnp.dot(p.astype(BF16), v_h, preferred_element_type=F32)
                o_ref[rs, cs] = (o / den).astype(BF16)


def _attention(q, kv, sinks, batch, seq):
    n, q_dim = q.shape
    rows = _row_block(seq, ATTN_ROWS)
    bps = seq // rows
    wpb = rows // WINDOW
    wps = seq // WINDOW
    n_q_heads = q_dim // HEAD_DIM

    def prev(b, j, s):
        return b * wps + jnp.maximum(j * wpb - 1, 0)

    grid_spec = pltpu.PrefetchScalarGridSpec(
        num_scalar_prefetch=1,
        grid=(batch, bps),
        in_specs=[
            pl.BlockSpec((rows, q_dim), lambda b, j, s: (b * bps + j, 0)),
            pl.BlockSpec((rows, LANES), lambda b, j, s: (b * bps + j, 0)),
            pl.BlockSpec((rows, LANES), lambda b, j, s: (b * bps + j, 1)),
            pl.BlockSpec((WINDOW, LANES), lambda b, j, s: (prev(b, j, s), 0)),
            pl.BlockSpec((WINDOW, LANES), lambda b, j, s: (prev(b, j, s), 1)),
        ],
        out_specs=pl.BlockSpec((rows, q_dim), lambda b, j, s: (b * bps + j, 0)),
    )
    kern = functools.partial(_attn_kernel, n_q_heads=n_q_heads)
    return pl.pallas_call(
        kern,
        grid_spec=grid_spec,
        out_shape=jax.ShapeDtypeStruct((n, q_dim), BF16),
        compiler_params=_params(("arbitrary", "arbitrary")),
        name="swa_attention",
    )(sinks.astype(F32), q, kv, kv, kv, kv)


def _out_proj_kernel(h_ref, o_ref, w_ref, b_ref, out_ref):
    out_ref[...] = (h_ref[...] + jnp.dot(o_ref[...], w_ref[...], preferred_element_type=F32)
                    + b_ref[...])


def _out_proj(h, o, w_o, b_o):
    n, d = h.shape
    rows = _row_block(n, PROJ_ROWS)
    return pl.pallas_call(
        _out_proj_kernel,
        grid=(n // rows,),
        in_specs=[
            pl.BlockSpec((rows, d), lambda i: (i, 0)),
            pl.BlockSpec((rows, o.shape[1]), lambda i: (i, 0)),
            _full(w_o.shape),
            _full((1, d)),
        ],
        out_specs=pl.BlockSpec((rows, d), lambda i: (i, 0)),
        out_shape=jax.ShapeDtypeStruct((n, d), F32),
        compiler_params=_params(("arbitrary",)),
        name="attn_out_proj",
    )(h, o, w_o.astype(BF16), b_o.reshape(1, d))


R_IDX0, R_IDX1, R_GATE0, R_GATE1, R_RANK0, R_RANK1 = range(6)


def _router_kernel(h_ref, g_ref, rw_ref, xpk_ref, route_ref, cnt_ref, tri_ref, carry_ref):
    rows, d = h_ref.shape
    half = d // 2
    i = pl.program_id(0)

    @pl.when(i == 0)
    def _():
        r = lax.broadcasted_iota(jnp.int32, (rows, rows), 0)
        c = lax.broadcasted_iota(jnp.int32, (rows, rows), 1)
        tri_ref[...] = jnp.where(r > c, 1.0, 0.0).astype(BF16)
        carry_ref[...] = jnp.zeros_like(carry_ref)

    xn = _rms(h_ref[...], g_ref[...])
    xb = xn.astype(BF16)
    bits = lax.bitcast_convert_type(xb.astype(F32), jnp.uint32)
    xpk_ref[...] = (bits[:, half:d] & jnp.uint32(0xFFFF0000)) | (bits[:, 0:half] >> 16)

    logits = jnp.dot(xb, rw_ref[...], preferred_element_type=F32)
    lane = lax.broadcasted_iota(jnp.int32, logits.shape, 1)
    lg = jnp.where(lane < N_EXPERTS, logits, -jnp.inf)
    m1 = jnp.max(lg, axis=-1, keepdims=True)
    i1 = jnp.min(jnp.where(lg == m1, lane, LANES), axis=-1, keepdims=True)
    lg2 = jnp.where(lane == i1, -jnp.inf, lg)
    m2 = jnp.max(lg2, axis=-1, keepdims=True)
    i2 = jnp.min(jnp.where(lg2 == m2, lane, LANES), axis=-1, keepdims=True)
    e = jnp.exp(m2 - m1)
    g1 = 1.0 / (1.0 + e)
    g2 = e / (1.0 + e)

    sel = (lane == i1) | (lane == i2)
    sel_f = jnp.where(sel, 1.0, 0.0)
    carry = carry_ref[0:1, :]
    before = jnp.dot(tri_ref[...], sel_f.astype(BF16), preferred_element_type=F32) + carry
    r1 = jnp.sum(jnp.where(lane == i1, before, 0.0), axis=-1, keepdims=True)
    r2 = jnp.sum(jnp.where(lane == i2, before, 0.0), axis=-1, keepdims=True)
    carry = carry + jnp.sum(sel_f, axis=0, keepdims=True)
    carry_ref[...] = jnp.broadcast_to(carry, carry_ref.shape)
    cnt_ref[...] = jnp.broadcast_to(carry, cnt_ref.shape)

    route = jnp.zeros(logits.shape, F32)
    for k, val in ((R_IDX0, i1.astype(F32)), (R_IDX1, i2.astype(F32)), (R_GATE0, g1),
                   (R_GATE1, g2), (R_RANK0, r1), (R_RANK1, r2)):
        route = jnp.where(lane == k, val, route)
    route_ref[...] = route


def _router(h, g, router_w):
    n, d = h.shape
    rows = _row_block(n, ROUTER_ROWS)
    rw = jnp.zeros((d, LANES), BF16).at[:, 0:N_EXPERTS].set(router_w.astype(BF16))
    return pl.pallas_call(
        _router_kernel,
        grid=(n // rows,),
        in_specs=[
            pl.BlockSpec((rows, d), lambda i: (i, 0)),
            _full((1, d)),
            _full((d, LANES)),
        ],
        out_specs=[
            pl.BlockSpec((rows, d // 2), lambda i: (i, 0)),
            pl.BlockSpec((rows, LANES), lambda i: (i, 0)),
            _full((8, LANES)),
        ],
        out_shape=[
            jax.ShapeDtypeStruct((n, d // 2), jnp.uint32),
            jax.ShapeDtypeStruct((n, LANES), F32),
            jax.ShapeDtypeStruct((8, LANES), F32),
        ],
        scratch_shapes=[pltpu.VMEM((rows, rows), BF16), pltpu.VMEM((8, LANES), F32)],
        compiler_params=_params(("arbitrary",)),
        name="moe_router",
    )(h, g.reshape(1, d), rw)


def _gather_rows(src, idx):
    parts = src.shape[1] // SC_GATHER_WORDS
    piece_idx = (idx[:, None] * parts + jnp.arange(parts, dtype=jnp.int32)[None, :]).reshape(-1)
    pieces = _gather_pieces(src.reshape(-1, SC_GATHER_WORDS), piece_idx)
    return pieces.reshape(idx.shape[0], src.shape[1])


def _gather_pieces(src, idx):
    m = idx.shape[0]
    width = src.shape[1]
    window = SC_GATHER_WINDOW
    assert m % (window * SC_WORKERS) == 0
    mesh = plsc.VectorSubcoreMesh(core_axis_name="core", subcore_axis_name="subcore")

    @functools.partial(pl.kernel, out_type=jax.ShapeDtypeStruct((m, width), src.dtype),
                       mesh=mesh, scratch_types=[])
    def gather_kernel(src_hbm, idx_hbm, out_hbm):
        def body(idx_vmem, out_vmem):
            pltpu.sync_copy(src_hbm.at[idx_vmem.at[0]], out_vmem)

        pltpu.emit_pipeline(
            body,
            grid=(m // window,),
            in_specs=[pl.BlockSpec((1, window), lambda i: (0, i))],
            out_specs=[pl.BlockSpec((window, width), lambda i: (i, 0))],
            core_axis_name=("core", "subcore"),
            dimension_semantics=(pltpu.PARALLEL,),
        )(idx_hbm, out_hbm)

    return gather_kernel(src, idx.reshape(1, m))


def _combine_kernel(h_ref, y_ref, route_ref, g_ref, o_ref, *, final_norm):
    d = h_ref.shape[1]
    route = route_ref[...]
    y = y_ref[...]
    out = h_ref[...] + (route[:, R_GATE0:R_GATE0 + 1] * y[:, 0:d]
                        + route[:, R_GATE1:R_GATE1 + 1] * y[:, d:2 * d])
    if final_norm:
        out = _rms(out, g_ref[...])
    o_ref[...] = out


def _combine(h, y_pairs, route, final_g):
    n, d = h.shape
    rows = _row_block(n, PROJ_ROWS)
    final_norm = final_g is not None
    g = final_g if final_norm else jnp.ones((d,), F32)
    kern = functools.partial(_combine_kernel, final_norm=final_norm)
    return pl.pallas_call(
        kern,
        grid=(n // rows,),
        in_specs=[
            pl.BlockSpec((rows, d), lambda i: (i, 0)),
            pl.BlockSpec((rows, 2 * d), lambda i: (i, 0)),
            pl.BlockSpec((rows, LANES), lambda i: (i, 0)),
            _full((1, d)),
        ],
        out_specs=pl.BlockSpec((rows, d), lambda i: (i, 0)),
        out_shape=jax.ShapeDtypeStruct((n, d), F32),
        compiler_params=_params(("arbitrary",)),
        name="moe_combine",
    )(h, y_pairs, route, g.reshape(1, d))


def _moe(h, g, router_w, wg, wu, wd, final_g):
    n, d = h.shape
    xpk, route, cnt = _router(h, g, router_w)

    top_idx = route[:, R_IDX0:R_IDX1 + 1].astype(jnp.int32)
    rank = route[:, R_RANK0:R_RANK1 + 1].astype(jnp.int32)
    sizes = cnt[0, 0:N_EXPERTS].astype(jnp.int32)
    padded = ((sizes + MOE_ROWS - 1) // MOE_ROWS) * MOE_ROWS
    pends = jnp.cumsum(padded)
    pstarts = pends - padded
    dest = (pstarts[top_idx] + rank).reshape(2 * n)
    n_rows = 2 * n + N_EXPERTS * MOE_ROWS
    n_blocks = n_rows // MOE_ROWS
    block_start = jnp.arange(n_blocks, dtype=jnp.int32) * MOE_ROWS
    block_expert = jnp.minimum(jnp.searchsorted(pends, block_start, side="right"),
                               N_EXPERTS - 1).astype(jnp.int32)
    token_ids = jnp.repeat(jnp.arange(n, dtype=jnp.int32), 2)
    row_token = (jnp.arange(n_rows, dtype=jnp.int32) % n).at[dest].set(
        token_ids, unique_indices=True)

    x_sorted = _gather_rows(xpk, row_token)
    y_rows = _moe_ffn(x_sorted, block_expert, wg, wu, wd)
    y_pairs = _gather_rows(y_rows, dest).reshape(n, 2 * d)
    return _combine(h, y_pairs, route, final_g)


def kernel(x, positions, final_norm_g, ev_norm1_g, ev_w_in, ev_conv_w, ev_ln_g, ev_ln_b, ev_spatial_w, ev_spatial_b, ev_w_out, ev_norm2_g, ev_ffn_wg, ev_ffn_wu, ev_ffn_wd, od_norm1_g, od_w_qkv, od_b_qkv, od_sinks, od_w_o, od_b_o, od_norm2_g, od_router_w, od_exp_wg, od_exp_wu, od_exp_wd):
    batch, seq, d = x.shape
    depth = ev_norm1_g.shape[0] + od_norm1_g.shape[0]
    assert depth % 2 == 0, "the final norm is fused into the last (odd) layer's combine"
    n_q_heads = od_sinks.shape[1]
    h = x.reshape(batch * seq, d)
    for layer in range(depth):
        i = layer // 2
        if layer % 2 == 0:
            h = _mixer(h, seq, ev_norm1_g[i], ev_w_in[i], ev_conv_w[i], ev_ln_g[i], ev_ln_b[i],
                       ev_spatial_w[i], ev_spatial_b[i], ev_w_out[i])
            h = _dense_ffn(h, ev_norm2_g[i], ev_ffn_wg[i], ev_ffn_wu[i], ev_ffn_wd[i])
        else:
            q, kv = _qkv(h, od_norm1_g[i], positions, od_w_qkv[i], od_b_qkv[i],
                         n_q_heads * HEAD_DIM)
            o = _attention(q, kv, od_sinks[i], batch, seq)
            h = _out_proj(h, o, od_w_o[i], od_b_o[i])
            last = layer == depth - 1
            h = _moe(h, od_norm2_g[i], od_router_w[i], od_exp_wg[i], od_exp_wu[i], od_exp_wd[i],
                     final_norm_g if last else None)
    return h.reshape(batch, seq, d)
```

```python
import functools

import jax
import jax.numpy as jnp
import numpy as np
from jax import lax
from jax.experimental import pallas as pl
from jax.experimental.pallas import tpu as pltpu
from jax.experimental.pallas import tpu_sc as plsc

F32 = jnp.float32
BF16 = jnp.bfloat16

EPS = 1e-5
CHUNK = 128
GMLP_HEADS = 4
CONV_WIDTH = 3
HEAD_DIM = 64
WINDOW = 128
ROPE_DIM = HEAD_DIM // 4
ROPE_THETA = 500000.0
ATTN_SCALE = HEAD_DIM ** -0.5
N_EXPERTS = 8
LANES = 128
VMEM_LIMIT = 56 * 1024 * 1024

MIXER_ROWS = 512
FFN_ROWS = 1024
PROJ_ROWS = 1024
ATTN_ROWS = 1024
ROUTER_ROWS = 1024
MOE_ROWS = 512
MOE_COL_SPLIT = 2
SC_WORKERS = 32
SC_PIECE = 256
SC_WINDOW = 128


def _row_block(n, pref):
    b = min(n, pref)
    while n % b:
        b -= LANES
    return b


def _col_block(f, pref):
    b = min(f, pref)
    b -= b % LANES
    while f % b:
        b -= LANES
    return b


def _params(sem):
    return pltpu.CompilerParams(dimension_semantics=sem, vmem_limit_bytes=VMEM_LIMIT)


def _rms(x, g):
    return x * lax.rsqrt(jnp.mean(x * x, axis=-1, keepdims=True) + EPS) * g


def _gelu(x):
    return 0.5 * x * (1.0 + lax.erf(x * np.float32(np.sqrt(0.5))))


def _full(shape):
    return pl.BlockSpec(shape, lambda *_: (0,) * len(shape))


def _mixer_kernel(h_ref, g1_ref, win_ref, cw_ref, lng_ref, lnb_ref, ws_ref, bst_ref,
                  wout_ref, o_ref, tail_ref, yb_ref, *, blocks_per_seq):
    rows = h_ref.shape[0]
    cd = cw_ref.shape[1]
    gd = lng_ref.shape[1]
    hd = gd // GMLP_HEADS
    i = pl.program_id(0)

    x = h_ref[...]
    xn = _rms(x, g1_ref[...]).astype(BF16)
    z = jnp.dot(xn, win_ref[...], preferred_element_type=F32)
    a_b = z[:, 0:cd]
    a_c = z[:, cd:2 * cd]
    a_x = z[:, 2 * cd:3 * cd]
    b_u = z[:, 3 * cd:3 * cd + gd]
    b_v = z[:, 3 * cd + gd:3 * cd + 2 * gd]

    g = a_c * a_x

    @pl.when(i % blocks_per_seq == 0)
    def _():
        tail_ref[...] = jnp.zeros_like(tail_ref)

    tail = tail_ref[...]
    row = lax.broadcasted_iota(jnp.int32, g.shape, 0)
    gm1 = jnp.where(row == 0, tail[7:8], pltpu.roll(g, 1, 0))
    gm2 = jnp.where(row == 0, tail[6:7], jnp.where(row == 1, tail[7:8], pltpu.roll(g, 2, 0)))
    tail_ref[...] = g[rows - 8:rows]
    cw = cw_ref[...]
    y_a = a_b * (gm2 * cw[0:1] + gm1 * cw[1:2] + g * cw[2:3])

    u = _gelu(b_u)
    v = _gelu(b_v)
    mu = jnp.mean(v, axis=-1, keepdims=True)
    vc = v - mu
    var = jnp.mean(vc * vc, axis=-1, keepdims=True)
    vn = (vc * lax.rsqrt(var + EPS) * lng_ref[...] + lnb_ref[...]).astype(BF16)
    ri = lax.broadcasted_iota(jnp.int32, (CHUNK, CHUNK), 0)
    ci = lax.broadcasted_iota(jnp.int32, (CHUNK, CHUNK), 1)
    causal = ri >= ci
    bst = bst_ref[...]
    for k in range(GMLP_HEADS):
        w_k = jnp.where(causal, ws_ref[k], 0.0).astype(BF16)
        b_k = bst[:, k:k + 1]
        for c in range(rows // CHUNK):
            rs = slice(c * CHUNK, (c + 1) * CHUNK)
            cs = slice(k * hd, (k + 1) * hd)
            mixed = jnp.dot(w_k, vn[rs, cs], preferred_element_type=F32) + b_k
            yb_ref[rs, cs] = (u[rs, cs] * mixed).astype(BF16)

    out = jnp.dot(y_a.astype(BF16), wout_ref[0:cd, :], preferred_element_type=F32)
    out = out + jnp.dot(yb_ref[...], wout_ref[cd:cd + gd, :], preferred_element_type=F32)
    o_ref[...] = x + out


def _mixer(h, seq, g1, w_in, conv_w, ln_g, ln_b, w_s, b_s, w_out):
    n, d = h.shape
    rows = _row_block(seq, MIXER_ROWS)
    cd = conv_w.shape[0]
    gd = ln_g.shape[0]
    kern = functools.partial(_mixer_kernel, blocks_per_seq=seq // rows)
    return pl.pallas_call(
        kern,
        grid=(n // rows,),
        in_specs=[
            pl.BlockSpec((rows, d), lambda i: (i, 0)),
            _full((1, d)),
            _full(w_in.shape),
            _full((CONV_WIDTH, cd)),
            _full((1, gd)),
            _full((1, gd)),
            _full(w_s.shape),
            _full((CHUNK, GMLP_HEADS)),
            _full(w_out.shape),
        ],
        out_specs=pl.BlockSpec((rows, d), lambda i: (i, 0)),
        out_shape=jax.ShapeDtypeStruct((n, d), F32),
        scratch_shapes=[pltpu.VMEM((8, cd), F32), pltpu.VMEM((rows, gd), BF16)],
        compiler_params=_params(("arbitrary",)),
        name="mixer",
    )(h, g1.reshape(1, d), w_in.astype(BF16), conv_w.T, ln_g.reshape(1, gd),
      ln_b.reshape(1, gd), w_s, b_s.T, w_out.astype(BF16))


def _swiglu_step(xn, wg_ref, wu_ref, wd_ref, acc_ref):
    h1 = jnp.dot(xn, wg_ref[0], preferred_element_type=F32)
    h2 = jnp.dot(xn, wu_ref[0], preferred_element_type=F32)
    a = (h1 / (1.0 + jnp.exp(-h1)) * h2).astype(BF16)
    acc_ref[...] += jnp.dot(a, wd_ref[0], preferred_element_type=F32)


def _dense_ffn_kernel(x_ref, g_ref, wg_ref, wu_ref, wd_ref, o_ref, xn_ref, acc_ref):
    f = pl.program_id(1)

    @pl.when(f == 0)
    def _():
        xn_ref[...] = _rms(x_ref[...], g_ref[...]).astype(BF16)
        acc_ref[...] = jnp.zeros_like(acc_ref)

    _swiglu_step(xn_ref[...], wg_ref, wu_ref, wd_ref, acc_ref)

    @pl.when(f == pl.num_programs(1) - 1)
    def _():
        o_ref[...] = x_ref[...] + acc_ref[...]


def _dense_ffn(h, g, wg, wu, wd):
    n, d = h.shape
    fdim = wg.shape[1]
    rows = _row_block(n, FFN_ROWS)
    cols = _col_block(fdim, fdim // 2)
    return pl.pallas_call(
        _dense_ffn_kernel,
        grid=(n // rows, fdim // cols),
        in_specs=[
            pl.BlockSpec((rows, d), lambda i, f: (i, 0)),
            pl.BlockSpec((1, d), lambda i, f: (0, 0)),
            pl.BlockSpec((1, d, cols), lambda i, f: (0, 0, f)),
            pl.BlockSpec((1, d, cols), lambda i, f: (0, 0, f)),
            pl.BlockSpec((1, cols, d), lambda i, f: (0, f, 0)),
        ],
        out_specs=pl.BlockSpec((rows, d), lambda i, f: (i, 0)),
        out_shape=jax.ShapeDtypeStruct((n, d), F32),
        scratch_shapes=[pltpu.VMEM((rows, d), BF16), pltpu.VMEM((rows, d), F32)],
        compiler_params=_params(("arbitrary", "arbitrary")),
        name="dense_ffn",
    )(h, g.reshape(1, d), wg.astype(BF16)[None], wu.astype(BF16)[None], wd.astype(BF16)[None])


def _unpack_bf16_pair(packed):
    lo = lax.bitcast_convert_type(packed << 16, F32).astype(BF16)
    hi = lax.bitcast_convert_type(packed & jnp.uint32(0xFFFF0000), F32).astype(BF16)
    return lo, hi


def _moe_ffn_kernel(be_ref, x_ref, wg_ref, wu_ref, wd_ref, o_ref, xn_ref, acc_ref):
    del be_ref
    f = pl.program_id(1)
    parts, _, piece = x_ref.shape
    half = parts * piece

    @pl.when(f == 0)
    def _():
        for p in range(parts):
            lo, hi = _unpack_bf16_pair(x_ref[p])
            xn_ref[:, p * piece:(p + 1) * piece] = lo
            xn_ref[:, half + p * piece:half + (p + 1) * piece] = hi
        acc_ref[...] = jnp.zeros_like(acc_ref)

    _swiglu_step(xn_ref[...], wg_ref, wu_ref, wd_ref, acc_ref)

    @pl.when(f == pl.num_programs(1) - 1)
    def _():
        for p in range(o_ref.shape[0]):
            o_ref[p] = acc_ref[:, p * piece:(p + 1) * piece]


def _moe_ffn(x_sorted, block_expert, wg, wu, wd):
    parts, n_rows, piece = x_sorted.shape
    d = 2 * parts * piece
    fdim = wg.shape[2]
    rows = MOE_ROWS
    cols = _col_block(fdim, fdim // MOE_COL_SPLIT)
    grid_spec = pltpu.PrefetchScalarGridSpec(
        num_scalar_prefetch=1,
        grid=(n_rows // rows, fdim // cols),
        in_specs=[
            pl.BlockSpec((parts, rows, piece), lambda i, f, be: (0, i, 0)),
            pl.BlockSpec((1, d, cols), lambda i, f, be: (be[i], 0, f)),
            pl.BlockSpec((1, d, cols), lambda i, f, be: (be[i], 0, f)),
            pl.BlockSpec((1, cols, d), lambda i, f, be: (be[i], f, 0)),
        ],
        out_specs=pl.BlockSpec((d // piece, rows, piece), lambda i, f, be: (0, i, 0)),
        scratch_shapes=[pltpu.VMEM((rows, d), BF16), pltpu.VMEM((rows, d), F32)],
    )
    return pl.pallas_call(
        _moe_ffn_kernel,
        grid_spec=grid_spec,
        out_shape=jax.ShapeDtypeStruct((d // piece, n_rows, piece), F32),
        compiler_params=_params(("arbitrary", "arbitrary")),
        name="moe_ffn",
    )(block_expert, x_sorted, wg.astype(BF16), wu.astype(BF16), wd.astype(BF16))


def _qkv_kernel(h_ref, g_ref, pos_ref, w_ref, b_ref, invf_ref, m1_ref, m2_ref,
                q_ref, kv_ref, *, q_dim):
    x = h_ref[...]
    xn = _rms(x, g_ref[...]).astype(BF16)
    z = jnp.dot(xn, w_ref[...], preferred_element_type=F32) + b_ref[...]
    ang = pos_ref[...].astype(F32) * invf_ref[...]
    cos = jnp.cos(ang)
    sin = jnp.sin(ang)
    s_lo = sin * m1_ref[...]
    s_hi = sin * m2_ref[...]
    half = ROPE_DIM // 2

    def rope(t):
        return t * cos + pltpu.roll(t, LANES - half, 1) * s_lo + pltpu.roll(t, half, 1) * s_hi

    for j in range(q_dim // LANES):
        cs = slice(j * LANES, (j + 1) * LANES)
        q_ref[:, cs] = (rope(z[:, cs]) * ATTN_SCALE).astype(BF16)
    kv_ref[:, 0:LANES] = rope(z[:, q_dim:q_dim + LANES]).astype(BF16)
    kv_ref[:, LANES:2 * LANES] = z[:, q_dim + LANES:q_dim + 2 * LANES].astype(BF16)


def _rope_lane_tables():
    lane = np.arange(LANES) % HEAD_DIM
    half = ROPE_DIM // 2
    first = (lane < half).astype(np.float32)
    second = ((lane >= half) & (lane < ROPE_DIM)).astype(np.float32)
    return lane, -first[None, :], second[None, :]


def _qkv(h, g, positions, w_qkv, b_qkv, q_dim):
    n, d = h.shape
    qkv_dim = w_qkv.shape[1]
    assert qkv_dim == q_dim + 2 * LANES
    rows = _row_block(n, PROJ_ROWS)
    lane, m1, m2 = _rope_lane_tables()
    inv_freq = ROPE_THETA ** (-jnp.arange(0, ROPE_DIM, 2, dtype=F32) / ROPE_DIM)
    invf = jnp.where(lane < ROPE_DIM, inv_freq[lane % (ROPE_DIM // 2)], 0.0).reshape(1, LANES)
    kern = functools.partial(_qkv_kernel, q_dim=q_dim)
    return pl.pallas_call(
        kern,
        grid=(n // rows,),
        in_specs=[
            pl.BlockSpec((rows, d), lambda i: (i, 0)),
            _full((1, d)),
            pl.BlockSpec((rows, 1), lambda i: (i, 0)),
            _full(w_qkv.shape),
            _full((1, qkv_dim)),
            _full((1, LANES)),
            _full((1, LANES)),
            _full((1, LANES)),
        ],
        out_specs=[
            pl.BlockSpec((rows, q_dim), lambda i: (i, 0)),
            pl.BlockSpec((rows, 2 * LANES), lambda i: (i, 0)),
        ],
        out_shape=[
            jax.ShapeDtypeStruct((n, q_dim), BF16),
            jax.ShapeDtypeStruct((n, 2 * LANES), BF16),
        ],
        compiler_params=_params(("arbitrary",)),
        name="qkv_rope",
    )(h, g.reshape(1, d), positions.reshape(n, 1), w_qkv.astype(BF16),
      b_qkv.reshape(1, qkv_dim), invf, jnp.asarray(m1), jnp.asarray(m2))


def _attn_kernel(sink_ref, q_ref, kvc_ref, kvp_ref, o_ref, kbuf, vbuf, *, wpb):
    rows = q_ref.shape[0]
    tiles = q_ref.shape[1] // LANES // 2
    j = pl.program_id(1)
    kbuf[0:WINDOW, :] = kvp_ref[:, 0:LANES]
    kbuf[WINDOW:WINDOW + rows, :] = kvc_ref[:, 0:LANES]
    vbuf[0:WINDOW, :] = kvp_ref[:, LANES:2 * LANES]
    vbuf[WINDOW:WINDOW + rows, :] = kvc_ref[:, LANES:2 * LANES]

    qi = lax.broadcasted_iota(jnp.int32, (WINDOW, 2 * WINDOW), 0)
    sj = lax.broadcasted_iota(jnp.int32, (WINDOW, 2 * WINDOW), 1)
    dist = sj - qi
    local = (dist >= 1) & (dist <= WINDOW)
    first_half = lax.broadcasted_iota(jnp.int32, (2 * WINDOW, LANES), 1) < HEAD_DIM
    out_first_half = lax.broadcasted_iota(jnp.int32, (WINDOW, LANES), 1) < HEAD_DIM
    nt = (((1,), (1,)), ((), ()))

    def window(n, carry):
        r0 = pl.multiple_of(n * WINDOW, WINDOW)
        kt = kbuf[pl.ds(r0, 2 * WINDOW), :].astype(F32)
        vt = vbuf[pl.ds(r0, 2 * WINDOW), :].astype(F32)
        mask = local & ((sj >= WINDOW) | (j * wpb + n > 0))
        k0_lo = jnp.where(first_half, kt, 0.0)
        k1_hi = jnp.where(first_half, 0.0, kt)
        v0_lo = jnp.where(first_half, vt, 1.0)
        v1_hi = jnp.where(first_half, 1.0, vt)
        k_even = (k0_lo.astype(BF16), pltpu.roll(k1_hi, HEAD_DIM, 1).astype(BF16))
        k_odd = (pltpu.roll(k0_lo, HEAD_DIM, 1).astype(BF16), k1_hi.astype(BF16))
        v_even = (v0_lo.astype(BF16), pltpu.roll(v1_hi, HEAD_DIM, 1).astype(BF16))
        v_odd = (pltpu.roll(v0_lo, HEAD_DIM, 1).astype(BF16), v1_hi.astype(BF16))
        for kh in range(2):
            q_stack = jnp.concatenate(
                [q_ref[pl.ds(r0, WINDOW), (kh * tiles + t) * LANES:(kh * tiles + t + 1) * LANES]
                 for t in range(tiles)], axis=0)
            outs = []
            for parity, k_rhs, v_rhs in ((0, k_even[kh], v_even[kh]), (1, k_odd[kh], v_odd[kh])):
                s_all = lax.dot_general(q_stack, k_rhs, nt, preferred_element_type=F32)
                p_tiles, corr = [], []
                for t in range(tiles):
                    sink = sink_ref[(kh * tiles + t) * 2 + parity]
                    s = jnp.where(mask, s_all[t * WINDOW:(t + 1) * WINDOW], -jnp.inf)
                    m = jnp.maximum(jnp.max(s, axis=-1, keepdims=True), sink)
                    p_tiles.append(jnp.exp(s - m).astype(BF16))
                    corr.append(jnp.exp(sink - m))
                pv = jnp.dot(jnp.concatenate(p_tiles, axis=0), v_rhs, preferred_element_type=F32)
                outs.append((pv, corr))
            for t in range(tiles):
                pv_e = outs[0][0][t * WINDOW:(t + 1) * WINDOW]
                pv_o = outs[1][0][t * WINDOW:(t + 1) * WINDOW]
                den_e = pltpu.roll(pv_e, HEAD_DIM, 1) + outs[0][1][t]
                den_o = pltpu.roll(pv_o, HEAD_DIM, 1) + outs[1][1][t]
                o_tile = jnp.where(out_first_half, pv_e / den_e, pv_o / den_o)
                o_ref[pl.ds(r0, WINDOW), (kh * tiles + t) * LANES:(kh * tiles + t + 1) * LANES] = (
                    o_tile.astype(BF16))
        return carry

    lax.fori_loop(0, rows // WINDOW, window, 0)


def _attention(q, kv, sinks, batch, seq):
    n, q_dim = q.shape
    assert kv.shape[1] == 2 * LANES and (q_dim // HEAD_DIM) % 4 == 0
    rows = _row_block(seq, ATTN_ROWS)
    bps = seq // rows
    wpb = rows // WINDOW
    wps = seq // WINDOW

    grid_spec = pltpu.PrefetchScalarGridSpec(
        num_scalar_prefetch=1,
        grid=(batch, bps),
        in_specs=[
            pl.BlockSpec((rows, q_dim), lambda b, j, s: (b * bps + j, 0)),
            pl.BlockSpec((rows, 2 * LANES), lambda b, j, s: (b * bps + j, 0)),
            pl.BlockSpec((WINDOW, 2 * LANES),
                         lambda b, j, s: (b * wps + jnp.maximum(j * wpb - 1, 0), 0)),
        ],
        out_specs=pl.BlockSpec((rows, q_dim), lambda b, j, s: (b * bps + j, 0)),
        scratch_shapes=[pltpu.VMEM((rows + WINDOW, LANES), BF16),
                        pltpu.VMEM((rows + WINDOW, LANES), BF16)],
    )
    kern = functools.partial(_attn_kernel, wpb=wpb)
    return pl.pallas_call(
        kern,
        grid_spec=grid_spec,
        out_shape=jax.ShapeDtypeStruct((n, q_dim), BF16),
        compiler_params=_params(("arbitrary", "arbitrary")),
        name="swa_attention",
    )(sinks.astype(F32), q, kv, kv)


def _out_proj_kernel(h_ref, o_ref, w_ref, b_ref, out_ref):
    out_ref[...] = (h_ref[...] + jnp.dot(o_ref[...], w_ref[...], preferred_element_type=F32)
                    + b_ref[...])


def _out_proj(h, o, w_o, b_o):
    n, d = h.shape
    rows = _row_block(n, PROJ_ROWS)
    return pl.pallas_call(
        _out_proj_kernel,
        grid=(n // rows,),
        in_specs=[
            pl.BlockSpec((rows, d), lambda i: (i, 0)),
            pl.BlockSpec((rows, o.shape[1]), lambda i: (i, 0)),
            _full(w_o.shape),
            _full((1, d)),
        ],
        out_specs=pl.BlockSpec((rows, d), lambda i: (i, 0)),
        out_shape=jax.ShapeDtypeStruct((n, d), F32),
        compiler_params=_params(("arbitrary",)),
        name="attn_out_proj",
    )(h, o, w_o.astype(BF16), b_o.reshape(1, d))


R_IDX0, R_IDX1, R_GATE0, R_GATE1, R_RANK0, R_RANK1 = range(6)


def _router_kernel(h_ref, g_ref, rw_ref, xpk_ref, route_ref, cnt_ref, tri_ref, carry_ref):
    rows, d = h_ref.shape
    half = d // 2
    parts, _, piece = xpk_ref.shape
    i = pl.program_id(0)

    @pl.when(i == 0)
    def _():
        r = lax.broadcasted_iota(jnp.int32, (rows, rows), 0)
        c = lax.broadcasted_iota(jnp.int32, (rows, rows), 1)
        tri_ref[...] = jnp.where(r > c, 1.0, 0.0).astype(BF16)
        carry_ref[...] = jnp.zeros_like(carry_ref)

    xn = _rms(h_ref[...], g_ref[...])
    xb = xn.astype(BF16)
    bits = lax.bitcast_convert_type(xb.astype(F32), jnp.uint32)
    for p in range(parts):
        lo = bits[:, p * piece:(p + 1) * piece]
        hi = bits[:, half + p * piece:half + (p + 1) * piece]
        xpk_ref[p] = (hi & jnp.uint32(0xFFFF0000)) | (lo >> 16)

    logits = jnp.dot(xb, rw_ref[...], preferred_element_type=F32)
    lane = lax.broadcasted_iota(jnp.int32, logits.shape, 1)
    lg = jnp.where(lane < N_EXPERTS, logits, -jnp.inf)
    m1 = jnp.max(lg, axis=-1, keepdims=True)
    i1 = jnp.min(jnp.where(lg == m1, lane, LANES), axis=-1, keepdims=True)
    lg2 = jnp.where(lane == i1, -jnp.inf, lg)
    m2 = jnp.max(lg2, axis=-1, keepdims=True)
    i2 = jnp.min(jnp.where(lg2 == m2, lane, LANES), axis=-1, keepdims=True)
    e = jnp.exp(m2 - m1)
    g1 = 1.0 / (1.0 + e)
    g2 = e / (1.0 + e)

    sel = (lane == i1) | (lane == i2)
    sel_f = jnp.where(sel, 1.0, 0.0)
    carry = carry_ref[0:1, :]
    before = jnp.dot(tri_ref[...], sel_f.astype(BF16), preferred_element_type=F32) + carry
    r1 = jnp.sum(jnp.where(lane == i1, before, 0.0), axis=-1, keepdims=True)
    r2 = jnp.sum(jnp.where(lane == i2, before, 0.0), axis=-1, keepdims=True)
    carry = carry + jnp.sum(sel_f, axis=0, keepdims=True)
    carry_ref[...] = jnp.broadcast_to(carry, carry_ref.shape)
    cnt_ref[...] = jnp.broadcast_to(carry, cnt_ref.shape)

    route = jnp.zeros(logits.shape, F32)
    for k, val in ((R_IDX0, i1.astype(F32)), (R_IDX1, i2.astype(F32)), (R_GATE0, g1),
                   (R_GATE1, g2), (R_RANK0, r1), (R_RANK1, r2)):
        route = jnp.where(lane == k, val, route)
    route_ref[...] = route


def _router(h, g, router_w):
    n, d = h.shape
    rows = _row_block(n, ROUTER_ROWS)
    parts = d // 2 // SC_PIECE
    rw = jnp.zeros((d, LANES), BF16).at[:, 0:N_EXPERTS].set(router_w.astype(BF16))
    return pl.pallas_call(
        _router_kernel,
        grid=(n // rows,),
        in_specs=[
            pl.BlockSpec((rows, d), lambda i: (i, 0)),
            _full((1, d)),
            _full((d, LANES)),
        ],
        out_specs=[
            pl.BlockSpec((parts, rows, SC_PIECE), lambda i: (0, i, 0)),
            pl.BlockSpec((rows, LANES), lambda i: (i, 0)),
            _full((8, LANES)),
        ],
        out_shape=[
            jax.ShapeDtypeStruct((parts, n, SC_PIECE), jnp.uint32),
            jax.ShapeDtypeStruct((n, LANES), F32),
            jax.ShapeDtypeStruct((8, LANES), F32),
        ],
        scratch_shapes=[pltpu.VMEM((rows, rows), BF16), pltpu.VMEM((8, LANES), F32)],
        compiler_params=_params(("arbitrary",)),
        name="moe_router",
    )(h, g.reshape(1, d), rw)


def _sc_mesh():
    return plsc.VectorSubcoreMesh(core_axis_name="core", subcore_axis_name="subcore")


def _gather_pieces(src, idx):
    m = idx.shape[0]
    width = src.shape[1]
    assert m % (SC_WINDOW * SC_WORKERS) == 0

    @functools.partial(pl.kernel, out_type=jax.ShapeDtypeStruct((m, width), src.dtype),
                       mesh=_sc_mesh(), scratch_types=[])
    def gather_kernel(src_hbm, idx_hbm, out_hbm):
        def body(idx_vmem, out_vmem):
            pltpu.sync_copy(src_hbm.at[idx_vmem.at[0]], out_vmem)

        pltpu.emit_pipeline(
            body,
            grid=(m // SC_WINDOW,),
            in_specs=[pl.BlockSpec((1, SC_WINDOW), lambda i: (0, i))],
            out_specs=[pl.BlockSpec((SC_WINDOW, width), lambda i: (i, 0))],
            core_axis_name=("core", "subcore"),
            dimension_semantics=(pltpu.PARALLEL,),
        )(idx_hbm, out_hbm)

    return gather_kernel(src, idx.reshape(1, m))


def _scatter_pieces(src, idx, out_rows):
    m = idx.shape[0]
    width = src.shape[1]
    src_windows = src.shape[0] // SC_WINDOW
    assert m % (SC_WINDOW * SC_WORKERS) == 0 and src.shape[0] % SC_WINDOW == 0

    @functools.partial(pl.kernel, out_type=jax.ShapeDtypeStruct((out_rows, width), src.dtype),
                       mesh=_sc_mesh(), scratch_types=[])
    def scatter_kernel(src_hbm, idx_hbm, out_hbm):
        def body(src_vmem, idx_vmem):
            pltpu.sync_copy(src_vmem, out_hbm.at[idx_vmem.at[0]])

        pltpu.emit_pipeline(
            body,
            grid=(m // SC_WINDOW,),
            in_specs=[pl.BlockSpec((SC_WINDOW, width), lambda i: (i % src_windows, 0)),
                      pl.BlockSpec((1, SC_WINDOW), lambda i: (0, i))],
            out_specs=[],
            core_axis_name=("core", "subcore"),
            dimension_semantics=(pltpu.PARALLEL,),
        )(src_hbm, idx_hbm)

    return scatter_kernel(src, idx.reshape(1, m))


def _combine_kernel(h_ref, y_ref, route_ref, g_ref, o_ref, *, final_norm):
    parts = y_ref.shape[0]
    piece = y_ref.shape[3]
    route = route_ref[...]
    g0 = route[:, R_GATE0:R_GATE0 + 1]
    g1 = route[:, R_GATE1:R_GATE1 + 1]
    out = jnp.concatenate(
        [h_ref[:, p * piece:(p + 1) * piece] + (g0 * y_ref[p, 0] + g1 * y_ref[p, 1])
         for p in range(parts)], axis=1)
    if final_norm:
        out = _rms(out, g_ref[...])
    o_ref[...] = out


def _combine(h, y_pairs, route, final_g):
    n, d = h.shape
    parts, _, _, piece = y_pairs.shape
    rows = _row_block(n, PROJ_ROWS)
    final_norm = final_g is not None
    g = final_g if final_norm else jnp.ones((d,), F32)
    kern = functools.partial(_combine_kernel, final_norm=final_norm)
    return pl.pallas_call(
        kern,
        grid=(n // rows,),
        in_specs=[
            pl.BlockSpec((rows, d), lambda i: (i, 0)),
            pl.BlockSpec((parts, 2, rows, piece), lambda i: (0, 0, i, 0)),
            pl.BlockSpec((rows, LANES), lambda i: (i, 0)),
            _full((1, d)),
        ],
        out_specs=pl.BlockSpec((rows, d), lambda i: (i, 0)),
        out_shape=jax.ShapeDtypeStruct((n, d), F32),
        compiler_params=_params(("arbitrary",)),
        name="moe_combine",
    )(h, y_pairs, route, g.reshape(1, d))


def _moe(h, g, router_w, wg, wu, wd, final_g):
    n, d = h.shape
    xpk, route, cnt = _router(h, g, router_w)
    x_parts = xpk.shape[0]
    y_parts = d // SC_PIECE

    top_idx = route[:, R_IDX0:R_IDX1 + 1].astype(jnp.int32)
    rank = route[:, R_RANK0:R_RANK1 + 1].astype(jnp.int32)
    sizes = cnt[0, 0:N_EXPERTS].astype(jnp.int32)
    padded = ((sizes + MOE_ROWS - 1) // MOE_ROWS) * MOE_ROWS
    pends = jnp.cumsum(padded)
    pstarts = pends - padded
    dest = (pstarts[top_idx] + rank).T
    n_rows = 2 * n + N_EXPERTS * MOE_ROWS
    n_blocks = n_rows // MOE_ROWS
    block_start = jnp.arange(n_blocks, dtype=jnp.int32) * MOE_ROWS
    block_expert = jnp.minimum(jnp.searchsorted(pends, block_start, side="right"),
                               N_EXPERTS - 1).astype(jnp.int32)

    x_off = (jnp.arange(x_parts, dtype=jnp.int32) * n_rows)[None, :, None]
    scatter_idx = (dest[:, None, :] + x_off).reshape(-1)
    x_sorted = _scatter_pieces(xpk.reshape(x_parts * n, SC_PIECE), scatter_idx, x_parts * n_rows)
    y_rows = _moe_ffn(x_sorted.reshape(x_parts, n_rows, SC_PIECE), block_expert, wg, wu, wd)

    y_off = (jnp.arange(y_parts, dtype=jnp.int32) * n_rows)[:, None, None]
    gather_idx = (dest[None, :, :] + y_off).reshape(-1)
    y_pairs = _gather_pieces(y_rows.reshape(y_parts * n_rows, SC_PIECE), gather_idx)
    return _combine(h, y_pairs.reshape(y_parts, 2, n, SC_PIECE), route, final_g)


def kernel(x, positions, final_norm_g, ev_norm1_g, ev_w_in, ev_conv_w, ev_ln_g, ev_ln_b, ev_spatial_w, ev_spatial_b, ev_w_out, ev_norm2_g, ev_ffn_wg, ev_ffn_wu, ev_ffn_wd, od_norm1_g, od_w_qkv, od_b_qkv, od_sinks, od_w_o, od_b_o, od_norm2_g, od_router_w, od_exp_wg, od_exp_wu, od_exp_wd):
    batch, seq, d = x.shape
    depth = ev_norm1_g.shape[0] + od_norm1_g.shape[0]
    assert depth % 2 == 0, "the final norm is fused into the last (odd) layer's combine"
    n_q_heads = od_sinks.shape[1]
    h = x.reshape(batch * seq, d)
    for layer in range(depth):
        i = layer // 2
        if layer % 2 == 0:
            h = _mixer(h, seq, ev_norm1_g[i], ev_w_in[i], ev_conv_w[i], ev_ln_g[i], ev_ln_b[i],
                       ev_spatial_w[i], ev_spatial_b[i], ev_w_out[i])
            h = _dense_ffn(h, ev_norm2_g[i], ev_ffn_wg[i], ev_ffn_wu[i], ev_ffn_wd[i])
        else:
            q, kv = _qkv(h, od_norm1_g[i], positions, od_w_qkv[i], od_b_qkv[i],
                         n_q_heads * HEAD_DIM)
            o = _attention(q, kv, od_sinks[i], batch, seq)
            h = _out_proj(h, o, od_w_o[i], od_b_o[i])
            last = layer == depth - 1
            h = _moe(h, od_norm2_g[i], od_router_w[i], od_exp_wg[i], od_exp_wu[i], od_exp_wd[i],
                     final_norm_g if last else None)
    return h.reshape(batch, seq, d)
```

```python
import functools

import jax
import jax.numpy as jnp
import numpy as np
from jax import lax
from jax.experimental import pallas as pl
from jax.experimental.pallas import tpu as pltpu
from jax.experimental.pallas import tpu_sc as plsc

F32 = jnp.float32
BF16 = jnp.bfloat16

EPS = 1e-5
CHUNK = 128
GMLP_HEADS = 4
CONV_WIDTH = 3
HEAD_DIM = 64
WINDOW = 128
ROPE_DIM = HEAD_DIM // 4
ROPE_THETA = 500000.0
ATTN_SCALE = HEAD_DIM ** -0.5
LOG2_E = float(np.log2(np.e))
Q_SCALE = ATTN_SCALE * LOG2_E
N_EXPERTS = 8
LANES = 128
VMEM_LIMIT = 56 * 1024 * 1024

MIXER_ROWS = 512
FFN_ROWS = 512
CAST_BLOCK_BYTES = 8 * 1024 * 1024
PROJ_ROWS = 1024
ATTN_ROWS = 1024
ROUTER_ROWS = 1024
MOE_ROWS = 512
MOE_COL_SPLIT = 2
SC_WORKERS = 32
SC_PIECE = 256
SC_WINDOW = 128


def _row_block(n, pref):
    b = min(n, pref)
    while n % b:
        b -= LANES
    return b


def _col_block(f, pref):
    b = min(f, pref)
    b -= b % LANES
    while f % b:
        b -= LANES
    return b


def _params(sem):
    return pltpu.CompilerParams(dimension_semantics=sem, vmem_limit_bytes=VMEM_LIMIT)


def _rms(x, g):
    return x * lax.rsqrt(jnp.mean(x * x, axis=-1, keepdims=True) + EPS) * g


def _gelu(x):
    return 0.5 * x * (1.0 + lax.erf(x * np.float32(np.sqrt(0.5))))


def _full(shape):
    return pl.BlockSpec(shape, lambda *_: (0,) * len(shape))


def _mixer_kernel(h_ref, g1_ref, win_ref, cw_ref, lng_ref, lnb_ref, ws_ref, bst_ref,
                  wout_ref, o_ref, tail_ref, yb_ref, *, blocks_per_seq):
    rows = h_ref.shape[0]
    cd = cw_ref.shape[1]
    gd = lng_ref.shape[1]
    hd = gd // GMLP_HEADS
    i = pl.program_id(0)

    x = h_ref[...]
    xn = _rms(x, g1_ref[...]).astype(BF16)
    z = jnp.dot(xn, win_ref[...], preferred_element_type=F32)
    a_b = z[:, 0:cd]
    a_c = z[:, cd:2 * cd]
    a_x = z[:, 2 * cd:3 * cd]
    b_u = z[:, 3 * cd:3 * cd + gd]
    b_v = z[:, 3 * cd + gd:3 * cd + 2 * gd]

    g = a_c * a_x

    @pl.when(i % blocks_per_seq == 0)
    def _():
        tail_ref[...] = jnp.zeros_like(tail_ref)

    tail = tail_ref[...]
    row = lax.broadcasted_iota(jnp.int32, g.shape, 0)
    gm1 = jnp.where(row == 0, tail[7:8], pltpu.roll(g, 1, 0))
    gm2 = jnp.where(row == 0, tail[6:7], jnp.where(row == 1, tail[7:8], pltpu.roll(g, 2, 0)))
    tail_ref[...] = g[rows - 8:rows]
    cw = cw_ref[...]
    y_a = a_b * (gm2 * cw[0:1] + gm1 * cw[1:2] + g * cw[2:3])

    u = _gelu(b_u)
    v = _gelu(b_v)
    mu = jnp.mean(v, axis=-1, keepdims=True)
    vc = v - mu
    var = jnp.mean(vc * vc, axis=-1, keepdims=True)
    vn = (vc * lax.rsqrt(var + EPS) * lng_ref[...] + lnb_ref[...]).astype(BF16)
    ri = lax.broadcasted_iota(jnp.int32, (CHUNK, CHUNK), 0)
    ci = lax.broadcasted_iota(jnp.int32, (CHUNK, CHUNK), 1)
    causal = ri >= ci
    bst = bst_ref[...]
    for k in range(GMLP_HEADS):
        w_k = jnp.where(causal, ws_ref[k], 0.0).astype(BF16)
        b_k = bst[:, k:k + 1]
        for c in range(rows // CHUNK):
            rs = slice(c * CHUNK, (c + 1) * CHUNK)
            cs = slice(k * hd, (k + 1) * hd)
            mixed = jnp.dot(w_k, vn[rs, cs], preferred_element_type=F32) + b_k
            yb_ref[rs, cs] = (u[rs, cs] * mixed).astype(BF16)

    out = jnp.dot(y_a.astype(BF16), wout_ref[0:cd, :], preferred_element_type=F32)
    out = out + jnp.dot(yb_ref[...], wout_ref[cd:cd + gd, :], preferred_element_type=F32)
    o_ref[...] = x + out


def _mixer(h, seq, g1, w_in, conv_w, ln_g, ln_b, w_s, b_s, w_out):
    n, d = h.shape
    rows = _row_block(seq, MIXER_ROWS)
    cd = conv_w.shape[0]
    gd = ln_g.shape[0]
    kern = functools.partial(_mixer_kernel, blocks_per_seq=seq // rows)
    return pl.pallas_call(
        kern,
        grid=(n // rows,),
        in_specs=[
            pl.BlockSpec((rows, d), lambda i: (i, 0)),
            _full((1, d)),
            _full(w_in.shape),
            _full((CONV_WIDTH, cd)),
            _full((1, gd)),
            _full((1, gd)),
            _full(w_s.shape),
            _full((CHUNK, GMLP_HEADS)),
            _full(w_out.shape),
        ],
        out_specs=pl.BlockSpec((rows, d), lambda i: (i, 0)),
        out_shape=jax.ShapeDtypeStruct((n, d), F32),
        scratch_shapes=[pltpu.VMEM((8, cd), F32), pltpu.VMEM((rows, gd), BF16)],
        compiler_params=_params(("arbitrary",)),
        name="mixer",
    )(h, g1.reshape(1, d), w_in.astype(BF16), conv_w.T, ln_g.reshape(1, gd),
      ln_b.reshape(1, gd), w_s, b_s.T, w_out.astype(BF16))


def _swiglu(xn, wg, wu, wd):
    h1 = jnp.dot(xn, wg, preferred_element_type=F32)
    h2 = jnp.dot(xn, wu, preferred_element_type=F32)
    a = (h1 / (1.0 + jnp.exp(-h1)) * h2).astype(BF16)
    return jnp.dot(a, wd, preferred_element_type=F32)


def _dense_ffn_kernel(x_ref, g_ref, wg_ref, wu_ref, wd_ref, o_ref):
    x = x_ref[...]
    xn = _rms(x, g_ref[...]).astype(BF16)
    o_ref[...] = x + _swiglu(xn, wg_ref[...], wu_ref[...], wd_ref[...])


def _dense_ffn(h, g, wg, wu, wd):
    n, d = h.shape
    rows = _row_block(n, FFN_ROWS)
    return pl.pallas_call(
        _dense_ffn_kernel,
        grid=(n // rows,),
        in_specs=[
            pl.BlockSpec((rows, d), lambda i: (i, 0)),
            _full((1, d)),
            _full(wg.shape),
            _full(wu.shape),
            _full(wd.shape),
        ],
        out_specs=pl.BlockSpec((rows, d), lambda i: (i, 0)),
        out_shape=jax.ShapeDtypeStruct((n, d), F32),
        compiler_params=_params(("arbitrary",)),
        name="dense_ffn",
    )(h, g.reshape(1, d), wg.astype(BF16), wu.astype(BF16), wd.astype(BF16))


def _unpack_bf16_pair(packed):
    lo = lax.bitcast_convert_type(packed << 16, F32).astype(BF16)
    hi = lax.bitcast_convert_type(packed & jnp.uint32(0xFFFF0000), F32).astype(BF16)
    return lo, hi


def _moe_ffn_kernel(be_ref, used_ref, x_ref, wg_ref, wu_ref, wd_ref, o_ref, xn_ref, acc_ref, *,
                    n_steps):
    del be_ref
    i = pl.program_id(0)
    f = pl.program_id(1)
    parts, _, piece = x_ref.shape
    half = parts * piece
    active = i < used_ref[0]

    def write_out(val):
        for p in range(o_ref.shape[0]):
            o_ref[p] = val[:, p * piece:(p + 1) * piece]

    def step(first, last):
        if first:
            for p in range(parts):
                lo, hi = _unpack_bf16_pair(x_ref[p])
                xn_ref[:, p * piece:(p + 1) * piece] = lo
                xn_ref[:, half + p * piece:half + (p + 1) * piece] = hi
        part = _swiglu(xn_ref[...], wg_ref[0, 0], wu_ref[0, 0], wd_ref[0, 0])
        if last:
            write_out(part if first else acc_ref[...] + part)
        elif first:
            acc_ref[...] = part
        else:
            acc_ref[...] += part

    if n_steps == 1:
        pl.when(active)(functools.partial(step, True, True))
    else:
        pl.when(active & (f == 0))(functools.partial(step, True, False))
        if n_steps > 2:
            pl.when(active & (f > 0) & (f < n_steps - 1))(functools.partial(step, False, False))
        pl.when(active & (f == n_steps - 1))(functools.partial(step, False, True))

    @pl.when(jnp.logical_not(active) & (f == n_steps - 1))
    def _():
        o_ref[...] = jnp.zeros_like(o_ref)


def _moe_ffn(x_sorted, block_expert, n_used, layer, wg, wu, wd):
    parts, n_rows, piece = x_sorted.shape
    d = 2 * parts * piece
    fdim = wg.shape[3]
    rows = MOE_ROWS
    cols = _col_block(fdim, fdim // MOE_COL_SPLIT)
    n_steps = fdim // cols

    def col(i, f, used):
        return jnp.where(i < used[0], f, n_steps - 1)

    grid_spec = pltpu.PrefetchScalarGridSpec(
        num_scalar_prefetch=2,
        grid=(n_rows // rows, n_steps),
        in_specs=[
            pl.BlockSpec((parts, rows, piece),
                         lambda i, f, be, used: (0, jnp.minimum(i, used[0] - 1), 0)),
            pl.BlockSpec((1, 1, d, cols), lambda i, f, be, used: (layer, be[i], 0, col(i, f, used))),
            pl.BlockSpec((1, 1, d, cols), lambda i, f, be, used: (layer, be[i], 0, col(i, f, used))),
            pl.BlockSpec((1, 1, cols, d), lambda i, f, be, used: (layer, be[i], col(i, f, used), 0)),
        ],
        out_specs=pl.BlockSpec((d // piece, rows, piece), lambda i, f, be, used: (0, i, 0)),
        scratch_shapes=[pltpu.VMEM((rows, d), BF16), pltpu.VMEM((rows, d), F32)],
    )
    kern = functools.partial(_moe_ffn_kernel, n_steps=n_steps)
    return pl.pallas_call(
        kern,
        grid_spec=grid_spec,
        out_shape=jax.ShapeDtypeStruct((d // piece, n_rows, piece), F32),
        compiler_params=_params(("arbitrary", "arbitrary")),
        name="moe_ffn",
    )(block_expert, n_used, x_sorted, wg, wu, wd)


def _cast_kernel(x_ref, o_ref):
    o_ref[...] = x_ref[...].astype(o_ref.dtype)


def _to_bf16(w):
    shape = w.shape
    w2 = w.reshape(-1, shape[-1])
    pref = CAST_BLOCK_BYTES // (4 * shape[-1]) // LANES * LANES
    rows = _row_block(w2.shape[0], pref)
    out = pl.pallas_call(
        _cast_kernel,
        grid=(w2.shape[0] // rows,),
        in_specs=[pl.BlockSpec((rows, shape[-1]), lambda i: (i, 0))],
        out_specs=pl.BlockSpec((rows, shape[-1]), lambda i: (i, 0)),
        out_shape=jax.ShapeDtypeStruct(w2.shape, BF16),
        compiler_params=_params(("arbitrary",)),
        name="cast_bf16",
    )(w2)
    return out.reshape(shape)


def _qkv_kernel(h_ref, g_ref, pos_ref, w_ref, b_ref, invf_ref, m1_ref, m2_ref,
                q_ref, kv_ref, *, q_dim):
    x = h_ref[...]
    xn = _rms(x, g_ref[...]).astype(BF16)
    z = jnp.dot(xn, w_ref[...], preferred_element_type=F32) + b_ref[...]
    ang = pos_ref[...].astype(F32) * invf_ref[...]
    cos = jnp.cos(ang)
    sin = jnp.sin(ang)
    s_lo = sin * m1_ref[...]
    s_hi = sin * m2_ref[...]
    half = ROPE_DIM // 2

    def rope(t):
        return t * cos + pltpu.roll(t, LANES - half, 1) * s_lo + pltpu.roll(t, half, 1) * s_hi

    for j in range(q_dim // LANES):
        cs = slice(j * LANES, (j + 1) * LANES)
        q_ref[:, cs] = (rope(z[:, cs]) * Q_SCALE).astype(BF16)
    kv_ref[:, 0:LANES] = rope(z[:, q_dim:q_dim + LANES]).astype(BF16)
    kv_ref[:, LANES:2 * LANES] = z[:, q_dim + LANES:q_dim + 2 * LANES].astype(BF16)


def _rope_lane_tables():
    lane = np.arange(LANES) % HEAD_DIM
    half = ROPE_DIM // 2
    first = (lane < half).astype(np.float32)
    second = ((lane >= half) & (lane < ROPE_DIM)).astype(np.float32)
    return lane, -first[None, :], second[None, :]


def _qkv(h, g, positions, w_qkv, b_qkv, q_dim):
    n, d = h.shape
    qkv_dim = w_qkv.shape[1]
    assert qkv_dim == q_dim + 2 * LANES
    rows = _row_block(n, PROJ_ROWS)
    lane, m1, m2 = _rope_lane_tables()
    inv_freq = ROPE_THETA ** (-jnp.arange(0, ROPE_DIM, 2, dtype=F32) / ROPE_DIM)
    invf = jnp.where(lane < ROPE_DIM, inv_freq[lane % (ROPE_DIM // 2)], 0.0).reshape(1, LANES)
    kern = functools.partial(_qkv_kernel, q_dim=q_dim)
    return pl.pallas_call(
        kern,
        grid=(n // rows,),
        in_specs=[
            pl.BlockSpec((rows, d), lambda i: (i, 0)),
            _full((1, d)),
            pl.BlockSpec((rows, 1), lambda i: (i, 0)),
            _full(w_qkv.shape),
            _full((1, qkv_dim)),
            _full((1, LANES)),
            _full((1, LANES)),
            _full((1, LANES)),
        ],
        out_specs=[
            pl.BlockSpec((rows, q_dim), lambda i: (i, 0)),
            pl.BlockSpec((rows, 2 * LANES), lambda i: (i, 0)),
        ],
        out_shape=[
            jax.ShapeDtypeStruct((n, q_dim), BF16),
            jax.ShapeDtypeStruct((n, 2 * LANES), BF16),
        ],
        compiler_params=_params(("arbitrary",)),
        name="qkv_rope",
    )(h, g.reshape(1, d), positions.reshape(n, 1), w_qkv.astype(BF16),
      b_qkv.reshape(1, qkv_dim), invf, jnp.asarray(m1), jnp.asarray(m2))


def _attn_kernel(sink_ref, q_ref, kvc_ref, kvp_ref, o_ref, kbuf, vbuf, *, wpb):
    rows = q_ref.shape[0]
    tiles = q_ref.shape[1] // LANES // 2
    j = pl.program_id(1)
    kbuf[0:WINDOW, :] = kvp_ref[:, 0:LANES]
    kbuf[WINDOW:WINDOW + rows, :] = kvc_ref[:, 0:LANES]
    vbuf[0:WINDOW, :] = kvp_ref[:, LANES:2 * LANES]
    vbuf[WINDOW:WINDOW + rows, :] = kvc_ref[:, LANES:2 * LANES]

    qi = lax.broadcasted_iota(jnp.int32, (WINDOW, 2 * WINDOW), 0)
    sj = lax.broadcasted_iota(jnp.int32, (WINDOW, 2 * WINDOW), 1)
    dist = sj - qi
    local = (dist >= 1) & (dist <= WINDOW)
    first_half = lax.broadcasted_iota(jnp.int32, (2 * WINDOW, LANES), 1) < HEAD_DIM
    out_first_half = lax.broadcasted_iota(jnp.int32, (WINDOW, LANES), 1) < HEAD_DIM
    nt = (((1,), (1,)), ((), ()))

    def window(n, carry):
        r0 = pl.multiple_of(n * WINDOW, WINDOW)
        kt = kbuf[pl.ds(r0, 2 * WINDOW), :].astype(F32)
        vt = vbuf[pl.ds(r0, 2 * WINDOW), :].astype(F32)
        mask = local & ((sj >= WINDOW) | (j * wpb + n > 0))
        k0_lo = jnp.where(first_half, kt, 0.0)
        k1_hi = jnp.where(first_half, 0.0, kt)
        v0_lo = jnp.where(first_half, vt, 1.0)
        v1_hi = jnp.where(first_half, 1.0, vt)
        k_even = (k0_lo.astype(BF16), pltpu.roll(k1_hi, HEAD_DIM, 1).astype(BF16))
        k_odd = (pltpu.roll(k0_lo, HEAD_DIM, 1).astype(BF16), k1_hi.astype(BF16))
        v_even = (v0_lo.astype(BF16), pltpu.roll(v1_hi, HEAD_DIM, 1).astype(BF16))
        v_odd = (pltpu.roll(v0_lo, HEAD_DIM, 1).astype(BF16), v1_hi.astype(BF16))
        for kh in range(2):
            q_stack = jnp.concatenate(
                [q_ref[pl.ds(r0, WINDOW), (kh * tiles + t) * LANES:(kh * tiles + t + 1) * LANES]
                 for t in range(tiles)], axis=0)
            outs = []
            for parity, k_rhs, v_rhs in ((0, k_even[kh], v_even[kh]), (1, k_odd[kh], v_odd[kh])):
                s_all = lax.dot_general(q_stack, k_rhs, nt, preferred_element_type=F32)
                p_tiles, corr = [], []
                for t in range(tiles):
                    sink = sink_ref[(kh * tiles + t) * 2 + parity]
                    s = jnp.where(mask, s_all[t * WINDOW:(t + 1) * WINDOW], -jnp.inf)
                    m = jnp.maximum(jnp.max(s, axis=-1, keepdims=True), sink)
                    p_tiles.append(jnp.exp2(s - m).astype(BF16))
                    corr.append(jnp.exp2(sink - m))
                pv = jnp.dot(jnp.concatenate(p_tiles, axis=0), v_rhs, preferred_element_type=F32)
                outs.append((pv, corr))
            for t in range(tiles):
                pv_e = outs[0][0][t * WINDOW:(t + 1) * WINDOW]
                pv_o = outs[1][0][t * WINDOW:(t + 1) * WINDOW]
                den_e = pltpu.roll(pv_e, HEAD_DIM, 1) + outs[0][1][t]
                den_o = pltpu.roll(pv_o, HEAD_DIM, 1) + outs[1][1][t]
                o_tile = jnp.where(out_first_half, pv_e / den_e, pv_o / den_o)
                o_ref[pl.ds(r0, WINDOW), (kh * tiles + t) * LANES:(kh * tiles + t + 1) * LANES] = (
                    o_tile.astype(BF16))
        return carry

    lax.fori_loop(0, rows // WINDOW, window, 0, unroll=2)


def _attention(q, kv, sinks, batch, seq):
    n, q_dim = q.shape
    assert kv.shape[1] == 2 * LANES and (q_dim // HEAD_DIM) % 4 == 0
    rows = _row_block(seq, ATTN_ROWS)
    bps = seq // rows
    wpb = rows // WINDOW
    wps = seq // WINDOW

    grid_spec = pltpu.PrefetchScalarGridSpec(
        num_scalar_prefetch=1,
        grid=(batch, bps),
        in_specs=[
            pl.BlockSpec((rows, q_dim), lambda b, j, s: (b * bps + j, 0)),
            pl.BlockSpec((rows, 2 * LANES), lambda b, j, s: (b * bps + j, 0)),
            pl.BlockSpec((WINDOW, 2 * LANES),
                         lambda b, j, s: (b * wps + jnp.maximum(j * wpb - 1, 0), 0)),
        ],
        out_specs=pl.BlockSpec((rows, q_dim), lambda b, j, s: (b * bps + j, 0)),
        scratch_shapes=[pltpu.VMEM((rows + WINDOW, LANES), BF16),
                        pltpu.VMEM((rows + WINDOW, LANES), BF16)],
    )
    kern = functools.partial(_attn_kernel, wpb=wpb)
    return pl.pallas_call(
        kern,
        grid_spec=grid_spec,
        out_shape=jax.ShapeDtypeStruct((n, q_dim), BF16),
        compiler_params=_params(("arbitrary", "arbitrary")),
        name="swa_attention",
    )(sinks.astype(F32) * LOG2_E, q, kv, kv)


def _out_proj_kernel(h_ref, o_ref, w_ref, b_ref, out_ref):
    out_ref[...] = (h_ref[...] + jnp.dot(o_ref[...], w_ref[...], preferred_element_type=F32)
                    + b_ref[...])


def _out_proj(h, o, w_o, b_o):
    n, d = h.shape
    rows = _row_block(n, PROJ_ROWS)
    return pl.pallas_call(
        _out_proj_kernel,
        grid=(n // rows,),
        in_specs=[
            pl.BlockSpec((rows, d), lambda i: (i, 0)),
            pl.BlockSpec((rows, o.shape[1]), lambda i: (i, 0)),
            _full(w_o.shape),
            _full((1, d)),
        ],
        out_specs=pl.BlockSpec((rows, d), lambda i: (i, 0)),
        out_shape=jax.ShapeDtypeStruct((n, d), F32),
        compiler_params=_params(("arbitrary",)),
        name="attn_out_proj",
    )(h, o, w_o.astype(BF16), b_o.reshape(1, d))


R_IDX0, R_IDX1, R_GATE0, R_GATE1, R_RANK0, R_RANK1 = range(6)


def _router_kernel(h_ref, g_ref, rw_ref, xpk_ref, route_ref, cnt_ref, tri_ref, carry_ref):
    rows, d = h_ref.shape
    half = d // 2
    parts, _, piece = xpk_ref.shape
    i = pl.program_id(0)

    @pl.when(i == 0)
    def _():
        r = lax.broadcasted_iota(jnp.int32, (rows, rows), 0)
        c = lax.broadcasted_iota(jnp.int32, (rows, rows), 1)
        tri_ref[...] = jnp.where(r > c, 1.0, 0.0).astype(BF16)
        carry_ref[...] = jnp.zeros_like(carry_ref)

    xn = _rms(h_ref[...], g_ref[...])
    xb = xn.astype(BF16)
    bits = lax.bitcast_convert_type(xb.astype(F32), jnp.uint32)
    for p in range(parts):
        lo = bits[:, p * piece:(p + 1) * piece]
        hi = bits[:, half + p * piece:half + (p + 1) * piece]
        xpk_ref[p] = (hi & jnp.uint32(0xFFFF0000)) | (lo >> 16)

    logits = jnp.dot(xb, rw_ref[...], preferred_element_type=F32)
    lane = lax.broadcasted_iota(jnp.int32, logits.shape, 1)
    lg = jnp.where(lane < N_EXPERTS, logits, -jnp.inf)
    m1 = jnp.max(lg, axis=-1, keepdims=True)
    i1 = jnp.min(jnp.where(lg == m1, lane, LANES), axis=-1, keepdims=True)
    lg2 = jnp.where(lane == i1, -jnp.inf, lg)
    m2 = jnp.max(lg2, axis=-1, keepdims=True)
    i2 = jnp.min(jnp.where(lg2 == m2, lane, LANES), axis=-1, keepdims=True)
    e = jnp.exp(m2 - m1)
    g1 = 1.0 / (1.0 + e)
    g2 = e / (1.0 + e)

    sel = (lane == i1) | (lane == i2)
    sel_f = jnp.where(sel, 1.0, 0.0)
    carry = carry_ref[0:1, :]
    before = jnp.dot(tri_ref[...], sel_f.astype(BF16), preferred_element_type=F32) + carry
    r1 = jnp.sum(jnp.where(lane == i1, before, 0.0), axis=-1, keepdims=True)
    r2 = jnp.sum(jnp.where(lane == i2, before, 0.0), axis=-1, keepdims=True)
    carry = carry + jnp.sum(sel_f, axis=0, keepdims=True)
    carry_ref[...] = jnp.broadcast_to(carry, carry_ref.shape)
    cnt_ref[...] = jnp.broadcast_to(carry, cnt_ref.shape)

    route = jnp.zeros(logits.shape, F32)
    for k, val in ((R_IDX0, i1.astype(F32)), (R_IDX1, i2.astype(F32)), (R_GATE0, g1),
                   (R_GATE1, g2), (R_RANK0, r1), (R_RANK1, r2)):
        route = jnp.where(lane == k, val, route)
    route_ref[...] = route


def _router(h, g, router_w):
    n, d = h.shape
    rows = _row_block(n, ROUTER_ROWS)
    parts = d // 2 // SC_PIECE
    rw = jnp.zeros((d, LANES), BF16).at[:, 0:N_EXPERTS].set(router_w.astype(BF16))
    return pl.pallas_call(
        _router_kernel,
        grid=(n // rows,),
        in_specs=[
            pl.BlockSpec((rows, d), lambda i: (i, 0)),
            _full((1, d)),
            _full((d, LANES)),
        ],
        out_specs=[
            pl.BlockSpec((parts, rows, SC_PIECE), lambda i: (0, i, 0)),
            pl.BlockSpec((rows, LANES), lambda i: (i, 0)),
            _full((8, LANES)),
        ],
        out_shape=[
            jax.ShapeDtypeStruct((parts, n, SC_PIECE), jnp.uint32),
            jax.ShapeDtypeStruct((n, LANES), F32),
            jax.ShapeDtypeStruct((8, LANES), F32),
        ],
        scratch_shapes=[pltpu.VMEM((rows, rows), BF16), pltpu.VMEM((8, LANES), F32)],
        compiler_params=_params(("arbitrary",)),
        name="moe_router",
    )(h, g.reshape(1, d), rw)


def _sc_mesh():
    return plsc.VectorSubcoreMesh(core_axis_name="core", subcore_axis_name="subcore")


def _gather_pieces(src, idx):
    m = idx.shape[0]
    width = src.shape[1]
    assert m % (SC_WINDOW * SC_WORKERS) == 0

    @functools.partial(pl.kernel, out_type=jax.ShapeDtypeStruct((m, width), src.dtype),
                       mesh=_sc_mesh(), scratch_types=[])
    def gather_kernel(src_hbm, idx_hbm, out_hbm):
        def body(idx_vmem, out_vmem):
            pltpu.sync_copy(src_hbm.at[idx_vmem.at[0]], out_vmem)

        pltpu.emit_pipeline(
            body,
            grid=(m // SC_WINDOW,),
            in_specs=[pl.BlockSpec((1, SC_WINDOW), lambda i: (0, i))],
            out_specs=[pl.BlockSpec((SC_WINDOW, width), lambda i: (i, 0))],
            core_axis_name=("core", "subcore"),
            dimension_semantics=(pltpu.PARALLEL,),
        )(idx_hbm, out_hbm)

    return gather_kernel(src, idx.reshape(1, m))


def _scatter_pieces(src, idx, out_rows):
    m = idx.shape[0]
    width = src.shape[1]
    src_windows = src.shape[0] // SC_WINDOW
    assert m % (SC_WINDOW * SC_WORKERS) == 0 and src.shape[0] % SC_WINDOW == 0

    @functools.partial(pl.kernel, out_type=jax.ShapeDtypeStruct((out_rows, width), src.dtype),
                       mesh=_sc_mesh(), scratch_types=[])
    def scatter_kernel(src_hbm, idx_hbm, out_hbm):
        def body(src_vmem, idx_vmem):
            pltpu.sync_copy(src_vmem, out_hbm.at[idx_vmem.at[0]])

        pltpu.emit_pipeline(
            body,
            grid=(m // SC_WINDOW,),
            in_specs=[pl.BlockSpec((SC_WINDOW, width), lambda i: (i % src_windows, 0)),
                      pl.BlockSpec((1, SC_WINDOW), lambda i: (0, i))],
            out_specs=[],
            core_axis_name=("core", "subcore"),
            dimension_semantics=(pltpu.PARALLEL,),
        )(src_hbm, idx_hbm)

    return scatter_kernel(src, idx.reshape(1, m))


def _combine_kernel(h_ref, y_ref, route_ref, g_ref, o_ref, *, final_norm):
    parts = y_ref.shape[0]
    piece = y_ref.shape[3]
    route = route_ref[...]
    g0 = route[:, R_GATE0:R_GATE0 + 1]
    g1 = route[:, R_GATE1:R_GATE1 + 1]
    out = jnp.concatenate(
        [h_ref[:, p * piece:(p + 1) * piece] + (g0 * y_ref[p, 0] + g1 * y_ref[p, 1])
         for p in range(parts)], axis=1)
    if final_norm:
        out = _rms(out, g_ref[...])
    o_ref[...] = out


def _combine(h, y_pairs, route, final_g):
    n, d = h.shape
    parts, _, _, piece = y_pairs.shape
    rows = _row_block(n, PROJ_ROWS)
    final_norm = final_g is not None
    g = final_g if final_norm else jnp.ones((d,), F32)
    kern = functools.partial(_combine_kernel, final_norm=final_norm)
    return pl.pallas_call(
        kern,
        grid=(n // rows,),
        in_specs=[
            pl.BlockSpec((rows, d), lambda i: (i, 0)),
            pl.BlockSpec((parts, 2, rows, piece), lambda i: (0, 0, i, 0)),
            pl.BlockSpec((rows, LANES), lambda i: (i, 0)),
            _full((1, d)),
        ],
        out_specs=pl.BlockSpec((rows, d), lambda i: (i, 0)),
        out_shape=jax.ShapeDtypeStruct((n, d), F32),
        compiler_params=_params(("arbitrary",)),
        name="moe_combine",
    )(h, y_pairs, route, g.reshape(1, d))


def _moe(h, g, router_w, layer, wg, wu, wd, final_g):
    n, d = h.shape
    xpk, route, cnt = _router(h, g, router_w)
    x_parts = xpk.shape[0]
    y_parts = d // SC_PIECE

    top_idx = route[:, R_IDX0:R_IDX1 + 1].astype(jnp.int32)
    rank = route[:, R_RANK0:R_RANK1 + 1].astype(jnp.int32)
    sizes = cnt[0, 0:N_EXPERTS].astype(jnp.int32)
    padded = ((sizes + MOE_ROWS - 1) // MOE_ROWS) * MOE_ROWS
    pends = jnp.cumsum(padded)
    pstarts = pends - padded
    dest = (pstarts[top_idx] + rank).T
    n_rows = 2 * n + N_EXPERTS * MOE_ROWS
    n_blocks = n_rows // MOE_ROWS
    block_start = jnp.arange(n_blocks, dtype=jnp.int32) * MOE_ROWS
    block_expert = jnp.minimum(jnp.searchsorted(pends, block_start, side="right"),
                               N_EXPERTS - 1).astype(jnp.int32)
    n_used = (pends[N_EXPERTS - 1:] // MOE_ROWS).astype(jnp.int32)

    x_off = (jnp.arange(x_parts, dtype=jnp.int32) * n_rows)[None, :, None]
    scatter_idx = (dest[:, None, :] + x_off).reshape(-1)
    x_sorted = _scatter_pieces(xpk.reshape(x_parts * n, SC_PIECE), scatter_idx, x_parts * n_rows)
    y_rows = _moe_ffn(x_sorted.reshape(x_parts, n_rows, SC_PIECE), block_expert, n_used, layer,
                      wg, wu, wd)

    y_off = (jnp.arange(y_parts, dtype=jnp.int32) * n_rows)[:, None, None]
    gather_idx = (dest[None, :, :] + y_off).reshape(-1)
    y_pairs = _gather_pieces(y_rows.reshape(y_parts * n_rows, SC_PIECE), gather_idx)
    return _combine(h, y_pairs.reshape(y_parts, 2, n, SC_PIECE), route, final_g)


def kernel(x, positions, final_norm_g, ev_norm1_g, ev_w_in, ev_conv_w, ev_ln_g, ev_ln_b, ev_spatial_w, ev_spatial_b, ev_w_out, ev_norm2_g, ev_ffn_wg, ev_ffn_wu, ev_ffn_wd, od_norm1_g, od_w_qkv, od_b_qkv, od_sinks, od_w_o, od_b_o, od_norm2_g, od_router_w, od_exp_wg, od_exp_wu, od_exp_wd):
    batch, seq, d = x.shape
    depth = ev_norm1_g.shape[0] + od_norm1_g.shape[0]
    assert depth % 2 == 0, "the final norm is fused into the last (odd) layer's combine"
    n_q_heads = od_sinks.shape[1]
    h = x.reshape(batch * seq, d)
    exp_wg, exp_wu, exp_wd = _to_bf16(od_exp_wg), _to_bf16(od_exp_wu), _to_bf16(od_exp_wd)
    for layer in range(depth):
        i = layer // 2
        if layer % 2 == 0:
            h = _mixer(h, seq, ev_norm1_g[i], ev_w_in[i], ev_conv_w[i], ev_ln_g[i], ev_ln_b[i],
                       ev_spatial_w[i], ev_spatial_b[i], ev_w_out[i])
            h = _dense_ffn(h, ev_norm2_g[i], ev_ffn_wg[i], ev_ffn_wu[i], ev_ffn_wd[i])
        else:
            q, kv = _qkv(h, od_norm1_g[i], positions, od_w_qkv[i], od_b_qkv[i],
                         n_q_heads * HEAD_DIM)
            o = _attention(q, kv, od_sinks[i], batch, seq)
            h = _out_proj(h, o, od_w_o[i], od_b_o[i])
            last = layer == depth - 1
            h = _moe(h, od_norm2_g[i], od_router_w[i], i, exp_wg, exp_wu, exp_wd,
                     final_norm_g if last else None)
    return h.reshape(batch, seq, d)
```

```python
import functools

import jax
import jax.numpy as jnp
import numpy as np
from jax import lax
from jax.experimental import pallas as pl
from jax.experimental.pallas import tpu as pltpu
from jax.experimental.pallas import tpu_sc as plsc

F32 = jnp.float32
BF16 = jnp.bfloat16

EPS = 1e-5
CHUNK = 128
GMLP_HEADS = 4
CONV_WIDTH = 3
HEAD_DIM = 64
WINDOW = 128
ROPE_DIM = HEAD_DIM // 4
ROPE_THETA = 500000.0
ATTN_SCALE = HEAD_DIM ** -0.5
LOG2_E = float(np.log2(np.e))
Q_SCALE = ATTN_SCALE * LOG2_E
N_EXPERTS = 8
LANES = 128
VMEM_LIMIT = 56 * 1024 * 1024

MIXER_ROWS = 512
FFN_ROWS = 512
CAST_BLOCK_BYTES = 8 * 1024 * 1024
PROJ_ROWS = 1024
ATTN_ROWS = 1024
ROUTER_ROWS = 1024
MOE_ROWS = 512
MOE_COL_SPLIT = 2
SC_WORKERS = 32
SC_PIECE = 256
SC_WINDOW = 128


def _row_block(n, pref):
    b = min(n, pref)
    while n % b:
        b -= LANES
    return b


def _col_block(f, pref):
    b = min(f, pref)
    b -= b % LANES
    while f % b:
        b -= LANES
    return b


def _params(sem):
    return pltpu.CompilerParams(dimension_semantics=sem, vmem_limit_bytes=VMEM_LIMIT)


def _rms(x, g):
    return x * lax.rsqrt(jnp.mean(x * x, axis=-1, keepdims=True) + EPS) * g


def _gelu(x):
    return 0.5 * x * (1.0 + lax.erf(x * np.float32(np.sqrt(0.5))))


def _full(shape):
    return pl.BlockSpec(shape, lambda *_: (0,) * len(shape))


def _mixer_kernel(*refs, blocks_per_seq, has_pending):
    if has_pending:
        h_ref, y_ref, route_ref = refs[:3]
        refs = refs[3:]
    else:
        h_ref = refs[0]
        refs = refs[1:]
    (g1_ref, win_ref, cw_ref, lng_ref, lnb_ref, ws_ref, bst_ref, wout_ref, o_ref,
     tail_ref, yb_ref) = refs
    rows = h_ref.shape[0]
    cd = cw_ref.shape[1]
    gd = lng_ref.shape[1]
    hd = gd // GMLP_HEADS
    i = pl.program_id(0)

    x = _moe_combined(h_ref, y_ref, route_ref) if has_pending else h_ref[...]
    xn = _rms(x, g1_ref[...]).astype(BF16)
    z = jnp.dot(xn, win_ref[...], preferred_element_type=F32)
    a_b = z[:, 0:cd]
    a_c = z[:, cd:2 * cd]
    a_x = z[:, 2 * cd:3 * cd]
    b_u = z[:, 3 * cd:3 * cd + gd]
    b_v = z[:, 3 * cd + gd:3 * cd + 2 * gd]

    g = a_c * a_x

    @pl.when(i % blocks_per_seq == 0)
    def _():
        tail_ref[...] = jnp.zeros_like(tail_ref)

    tail = tail_ref[...]
    row = lax.broadcasted_iota(jnp.int32, g.shape, 0)
    gm1 = jnp.where(row == 0, tail[7:8], pltpu.roll(g, 1, 0))
    gm2 = jnp.where(row == 0, tail[6:7], jnp.where(row == 1, tail[7:8], pltpu.roll(g, 2, 0)))
    tail_ref[...] = g[rows - 8:rows]
    cw = cw_ref[...]
    y_a = a_b * (gm2 * cw[0:1] + gm1 * cw[1:2] + g * cw[2:3])

    u = _gelu(b_u)
    v = _gelu(b_v)
    mu = jnp.mean(v, axis=-1, keepdims=True)
    vc = v - mu
    var = jnp.mean(vc * vc, axis=-1, keepdims=True)
    vn = (vc * lax.rsqrt(var + EPS) * lng_ref[...] + lnb_ref[...]).astype(BF16)
    ri = lax.broadcasted_iota(jnp.int32, (CHUNK, CHUNK), 0)
    ci = lax.broadcasted_iota(jnp.int32, (CHUNK, CHUNK), 1)
    causal = ri >= ci
    bst = bst_ref[...]
    for k in range(GMLP_HEADS):
        w_k = jnp.where(causal, ws_ref[k], 0.0).astype(BF16)
        b_k = bst[:, k:k + 1]
        for c in range(rows // CHUNK):
            rs = slice(c * CHUNK, (c + 1) * CHUNK)
            cs = slice(k * hd, (k + 1) * hd)
            mixed = jnp.dot(w_k, vn[rs, cs], preferred_element_type=F32) + b_k
            yb_ref[rs, cs] = (u[rs, cs] * mixed).astype(BF16)

    out = jnp.dot(y_a.astype(BF16), wout_ref[0:cd, :], preferred_element_type=F32)
    out = out + jnp.dot(yb_ref[...], wout_ref[cd:cd + gd, :], preferred_element_type=F32)
    o_ref[...] = x + out


def _mixer(h, pending, seq, g1, w_in, conv_w, ln_g, ln_b, w_s, b_s, w_out):
    n, d = h.shape
    rows = _row_block(seq, MIXER_ROWS)
    cd = conv_w.shape[0]
    gd = ln_g.shape[0]
    has_pending = pending is not None
    kern = functools.partial(_mixer_kernel, blocks_per_seq=seq // rows, has_pending=has_pending)
    return pl.pallas_call(
        kern,
        grid=(n // rows,),
        in_specs=[pl.BlockSpec((rows, d), lambda i: (i, 0))]
        + (_moe_specs(pending, rows) if has_pending else [])
        + [
            _full((1, d)),
            _full(w_in.shape),
            _full((CONV_WIDTH, cd)),
            _full((1, gd)),
            _full((1, gd)),
            _full(w_s.shape),
            _full((CHUNK, GMLP_HEADS)),
            _full(w_out.shape),
        ],
        out_specs=pl.BlockSpec((rows, d), lambda i: (i, 0)),
        out_shape=jax.ShapeDtypeStruct((n, d), F32),
        scratch_shapes=[pltpu.VMEM((8, cd), F32), pltpu.VMEM((rows, gd), BF16)],
        compiler_params=_params(("arbitrary",)),
        name="mixer",
    )(h, *(pending or ()), g1.reshape(1, d), w_in.astype(BF16), conv_w.T, ln_g.reshape(1, gd),
      ln_b.reshape(1, gd), w_s, b_s.T, w_out.astype(BF16))


def _swiglu(xn, wg, wu, wd):
    h1 = jnp.dot(xn, wg, preferred_element_type=F32)
    h2 = jnp.dot(xn, wu, preferred_element_type=F32)
    a = (h1 / (1.0 + jnp.exp(-h1)) * h2).astype(BF16)
    return jnp.dot(a, wd, preferred_element_type=F32)


def _dense_ffn_kernel(x_ref, g_ref, wg_ref, wu_ref, wd_ref, o_ref):
    x = x_ref[...]
    xn = _rms(x, g_ref[...]).astype(BF16)
    o_ref[...] = x + _swiglu(xn, wg_ref[...], wu_ref[...], wd_ref[...])


def _dense_ffn(h, g, wg, wu, wd):
    n, d = h.shape
    rows = _row_block(n, FFN_ROWS)
    return pl.pallas_call(
        _dense_ffn_kernel,
        grid=(n // rows,),
        in_specs=[
            pl.BlockSpec((rows, d), lambda i: (i, 0)),
            _full((1, d)),
            _full(wg.shape),
            _full(wu.shape),
            _full(wd.shape),
        ],
        out_specs=pl.BlockSpec((rows, d), lambda i: (i, 0)),
        out_shape=jax.ShapeDtypeStruct((n, d), F32),
        compiler_params=_params(("arbitrary",)),
        name="dense_ffn",
    )(h, g.reshape(1, d), wg.astype(BF16), wu.astype(BF16), wd.astype(BF16))


def _pack_bf16_pairs(x, piece):
    half = x.shape[1] // 2
    bits = lax.bitcast_convert_type(x.astype(BF16).astype(F32), jnp.uint32)
    return [(bits[:, half + p * piece:half + (p + 1) * piece] & jnp.uint32(0xFFFF0000))
            | (bits[:, p * piece:(p + 1) * piece] >> 16) for p in range(half // piece)]


def _unpack_bf16_pair(packed):
    lo = lax.bitcast_convert_type(packed << 16, F32)
    hi = lax.bitcast_convert_type(packed & jnp.uint32(0xFFFF0000), F32)
    return lo, hi


def _moe_ffn_kernel(be_ref, used_ref, x_ref, wg_ref, wu_ref, wd_ref, o_ref, xn_ref, acc_ref, *,
                    n_steps):
    del be_ref
    i = pl.program_id(0)
    f = pl.program_id(1)
    parts, _, piece = x_ref.shape
    half = parts * piece
    active = i < used_ref[0]

    def write_out(val):
        for p, words in enumerate(_pack_bf16_pairs(val, piece)):
            o_ref[p] = words

    def step(first, last):
        if first:
            for p in range(parts):
                lo, hi = _unpack_bf16_pair(x_ref[p])
                xn_ref[:, p * piece:(p + 1) * piece] = lo.astype(BF16)
                xn_ref[:, half + p * piece:half + (p + 1) * piece] = hi.astype(BF16)
        part = _swiglu(xn_ref[...], wg_ref[0, 0], wu_ref[0, 0], wd_ref[0, 0])
        if last:
            write_out(part if first else acc_ref[...] + part)
        elif first:
            acc_ref[...] = part
        else:
            acc_ref[...] += part

    if n_steps == 1:
        pl.when(active)(functools.partial(step, True, True))
    else:
        pl.when(active & (f == 0))(functools.partial(step, True, False))
        if n_steps > 2:
            pl.when(active & (f > 0) & (f < n_steps - 1))(functools.partial(step, False, False))
        pl.when(active & (f == n_steps - 1))(functools.partial(step, False, True))

    @pl.when(jnp.logical_not(active) & (f == n_steps - 1))
    def _():
        o_ref[...] = jnp.zeros_like(o_ref)


def _moe_ffn(x_sorted, block_expert, n_used, layer, wg, wu, wd):
    parts, n_rows, piece = x_sorted.shape
    d = 2 * parts * piece
    fdim = wg.shape[3]
    rows = MOE_ROWS
    cols = _col_block(fdim, fdim // MOE_COL_SPLIT)
    n_steps = fdim // cols

    def col(i, f, used):
        return jnp.where(i < used[0], f, n_steps - 1)

    grid_spec = pltpu.PrefetchScalarGridSpec(
        num_scalar_prefetch=2,
        grid=(n_rows // rows, n_steps),
        in_specs=[
            pl.BlockSpec((parts, rows, piece),
                         lambda i, f, be, used: (0, jnp.minimum(i, used[0] - 1), 0)),
            pl.BlockSpec((1, 1, d, cols), lambda i, f, be, used: (layer, be[i], 0, col(i, f, used))),
            pl.BlockSpec((1, 1, d, cols), lambda i, f, be, used: (layer, be[i], 0, col(i, f, used))),
            pl.BlockSpec((1, 1, cols, d), lambda i, f, be, used: (layer, be[i], col(i, f, used), 0)),
        ],
        out_specs=pl.BlockSpec((parts, rows, piece), lambda i, f, be, used: (0, i, 0)),
        scratch_shapes=[pltpu.VMEM((rows, d), BF16), pltpu.VMEM((rows, d), F32)],
    )
    kern = functools.partial(_moe_ffn_kernel, n_steps=n_steps)
    return pl.pallas_call(
        kern,
        grid_spec=grid_spec,
        out_shape=jax.ShapeDtypeStruct((parts, n_rows, piece), jnp.uint32),
        compiler_params=_params(("arbitrary", "arbitrary")),
        name="moe_ffn",
    )(block_expert, n_used, x_sorted, wg, wu, wd)


def _cast_kernel(x_ref, o_ref):
    o_ref[...] = x_ref[...].astype(o_ref.dtype)


def _to_bf16(w):
    shape = w.shape
    w2 = w.reshape(-1, shape[-1])
    pref = CAST_BLOCK_BYTES // (4 * shape[-1]) // LANES * LANES
    rows = _row_block(w2.shape[0], pref)
    out = pl.pallas_call(
        _cast_kernel,
        grid=(w2.shape[0] // rows,),
        in_specs=[pl.BlockSpec((rows, shape[-1]), lambda i: (i, 0))],
        out_specs=pl.BlockSpec((rows, shape[-1]), lambda i: (i, 0)),
        out_shape=jax.ShapeDtypeStruct(w2.shape, BF16),
        compiler_params=_params(("arbitrary",)),
        name="cast_bf16",
    )(w2)
    return out.reshape(shape)


def _qkv_kernel(h_ref, g_ref, pos_ref, w_ref, b_ref, invf_ref, m1_ref, m2_ref,
                q_ref, kv_ref, *, q_dim):
    x = h_ref[...]
    xn = _rms(x, g_ref[...]).astype(BF16)
    z = jnp.dot(xn, w_ref[...], preferred_element_type=F32) + b_ref[...]
    ang = pos_ref[...].astype(F32) * invf_ref[...]
    cos = jnp.cos(ang)
    sin = jnp.sin(ang)
    s_lo = sin * m1_ref[...]
    s_hi = sin * m2_ref[...]
    half = ROPE_DIM // 2

    def rope(t):
        return t * cos + pltpu.roll(t, LANES - half, 1) * s_lo + pltpu.roll(t, half, 1) * s_hi

    for j in range(q_dim // LANES):
        cs = slice(j * LANES, (j + 1) * LANES)
        q_ref[:, cs] = (rope(z[:, cs]) * Q_SCALE).astype(BF16)
    kv_ref[:, 0:LANES] = rope(z[:, q_dim:q_dim + LANES]).astype(BF16)
    kv_ref[:, LANES:2 * LANES] = z[:, q_dim + LANES:q_dim + 2 * LANES].astype(BF16)


def _rope_lane_tables():
    lane = np.arange(LANES) % HEAD_DIM
    half = ROPE_DIM // 2
    first = (lane < half).astype(np.float32)
    second = ((lane >= half) & (lane < ROPE_DIM)).astype(np.float32)
    return lane, -first[None, :], second[None, :]


def _qkv(h, g, positions, w_qkv, b_qkv, q_dim):
    n, d = h.shape
    qkv_dim = w_qkv.shape[1]
    assert qkv_dim == q_dim + 2 * LANES
    rows = _row_block(n, PROJ_ROWS)
    lane, m1, m2 = _rope_lane_tables()
    inv_freq = ROPE_THETA ** (-jnp.arange(0, ROPE_DIM, 2, dtype=F32) / ROPE_DIM)
    invf = jnp.where(lane < ROPE_DIM, inv_freq[lane % (ROPE_DIM // 2)], 0.0).reshape(1, LANES)
    kern = functools.partial(_qkv_kernel, q_dim=q_dim)
    return pl.pallas_call(
        kern,
        grid=(n // rows,),
        in_specs=[
            pl.BlockSpec((rows, d), lambda i: (i, 0)),
            _full((1, d)),
            pl.BlockSpec((rows, 1), lambda i: (i, 0)),
            _full(w_qkv.shape),
            _full((1, qkv_dim)),
            _full((1, LANES)),
            _full((1, LANES)),
            _full((1, LANES)),
        ],
        out_specs=[
            pl.BlockSpec((rows, q_dim), lambda i: (i, 0)),
            pl.BlockSpec((rows, 2 * LANES), lambda i: (i, 0)),
        ],
        out_shape=[
            jax.ShapeDtypeStruct((n, q_dim), BF16),
            jax.ShapeDtypeStruct((n, 2 * LANES), BF16),
        ],
        compiler_params=_params(("arbitrary",)),
        name="qkv_rope",
    )(h, g.reshape(1, d), positions.reshape(n, 1), w_qkv.astype(BF16),
      b_qkv.reshape(1, qkv_dim), invf, jnp.asarray(m1), jnp.asarray(m2))


def _attn_kernel(sink_ref, q_ref, kvc_ref, kvp_ref, h_ref, wo_ref, bo_ref, out_ref,
                 kbuf, vbuf, o_buf, *, wpb):
    rows = q_ref.shape[0]
    tiles = q_ref.shape[1] // LANES // 2
    j = pl.program_id(1)
    kbuf[0:WINDOW, :] = kvp_ref[:, 0:LANES]
    kbuf[WINDOW:WINDOW + rows, :] = kvc_ref[:, 0:LANES]
    vbuf[0:WINDOW, :] = kvp_ref[:, LANES:2 * LANES]
    vbuf[WINDOW:WINDOW + rows, :] = kvc_ref[:, LANES:2 * LANES]

    qi = lax.broadcasted_iota(jnp.int32, (WINDOW, 2 * WINDOW), 0)
    sj = lax.broadcasted_iota(jnp.int32, (WINDOW, 2 * WINDOW), 1)
    dist = sj - qi
    local = (dist >= 1) & (dist <= WINDOW)
    first_half = lax.broadcasted_iota(jnp.int32, (2 * WINDOW, LANES), 1) < HEAD_DIM
    out_first_half = lax.broadcasted_iota(jnp.int32, (WINDOW, LANES), 1) < HEAD_DIM
    nt = (((1,), (1,)), ((), ()))

    def window(n, carry):
        r0 = pl.multiple_of(n * WINDOW, WINDOW)
        kt = kbuf[pl.ds(r0, 2 * WINDOW), :].astype(F32)
        vt = vbuf[pl.ds(r0, 2 * WINDOW), :].astype(F32)
        mask = local & ((sj >= WINDOW) | (j * wpb + n > 0))
        k0_lo = jnp.where(first_half, kt, 0.0)
        k1_hi = jnp.where(first_half, 0.0, kt)
        v0_lo = jnp.where(first_half, vt, 1.0)
        v1_hi = jnp.where(first_half, 1.0, vt)
        k_even = (k0_lo.astype(BF16), pltpu.roll(k1_hi, HEAD_DIM, 1).astype(BF16))
        k_odd = (pltpu.roll(k0_lo, HEAD_DIM, 1).astype(BF16), k1_hi.astype(BF16))
        v_even = (v0_lo.astype(BF16), pltpu.roll(v1_hi, HEAD_DIM, 1).astype(BF16))
        v_odd = (pltpu.roll(v0_lo, HEAD_DIM, 1).astype(BF16), v1_hi.astype(BF16))
        for kh in range(2):
            q_stack = jnp.concatenate(
                [q_ref[pl.ds(r0, WINDOW), (kh * tiles + t) * LANES:(kh * tiles + t + 1) * LANES]
                 for t in range(tiles)], axis=0)
            outs = []
            for parity, k_rhs, v_rhs in ((0, k_even[kh], v_even[kh]), (1, k_odd[kh], v_odd[kh])):
                s_all = lax.dot_general(q_stack, k_rhs, nt, preferred_element_type=F32)
                p_tiles, corr = [], []
                for t in range(tiles):
                    sink = sink_ref[(kh * tiles + t) * 2 + parity]
                    s = jnp.where(mask, s_all[t * WINDOW:(t + 1) * WINDOW], -jnp.inf)
                    m = jnp.maximum(jnp.max(s, axis=-1, keepdims=True), sink)
                    p_tiles.append(jnp.exp2(s - m).astype(BF16))
                    corr.append(jnp.exp2(sink - m))
                pv = jnp.dot(jnp.concatenate(p_tiles, axis=0), v_rhs, preferred_element_type=F32)
                outs.append((pv, corr))
            for t in range(tiles):
                pv_e = outs[0][0][t * WINDOW:(t + 1) * WINDOW]
                pv_o = outs[1][0][t * WINDOW:(t + 1) * WINDOW]
                num = jnp.where(out_first_half, pv_e, pv_o)
                sums = pltpu.roll(jnp.where(out_first_half, pv_o, pv_e), HEAD_DIM, 1)
                den = sums + jnp.where(out_first_half, outs[0][1][t], outs[1][1][t])
                o_buf[pl.ds(r0, WINDOW), (kh * tiles + t) * LANES:(kh * tiles + t + 1) * LANES] = (
                    (num / den).astype(BF16))
        return carry

    lax.fori_loop(0, rows // WINDOW, window, 0, unroll=2)
    out_ref[...] = (h_ref[...] + jnp.dot(o_buf[...], wo_ref[...], preferred_element_type=F32)
                    + bo_ref[...])


def _attention(h, q, kv, sinks, w_o, b_o, batch, seq):
    n, q_dim = q.shape
    d = h.shape[1]
    assert kv.shape[1] == 2 * LANES and (q_dim // HEAD_DIM) % 4 == 0
    rows = _row_block(seq, ATTN_ROWS)
    bps = seq // rows
    wpb = rows // WINDOW
    wps = seq // WINDOW

    grid_spec = pltpu.PrefetchScalarGridSpec(
        num_scalar_prefetch=1,
        grid=(batch, bps),
        in_specs=[
            pl.BlockSpec((rows, q_dim), lambda b, j, s: (b * bps + j, 0)),
            pl.BlockSpec((rows, 2 * LANES), lambda b, j, s: (b * bps + j, 0)),
            pl.BlockSpec((WINDOW, 2 * LANES),
                         lambda b, j, s: (b * wps + jnp.maximum(j * wpb - 1, 0), 0)),
            pl.BlockSpec((rows, d), lambda b, j, s: (b * bps + j, 0)),
            pl.BlockSpec((q_dim, d), lambda b, j, s: (0, 0)),
            pl.BlockSpec((1, d), lambda b, j, s: (0, 0)),
        ],
        out_specs=pl.BlockSpec((rows, d), lambda b, j, s: (b * bps + j, 0)),
        scratch_shapes=[pltpu.VMEM((rows + WINDOW, LANES), BF16),
                        pltpu.VMEM((rows + WINDOW, LANES), BF16),
                        pltpu.VMEM((rows, q_dim), BF16)],
    )
    kern = functools.partial(_attn_kernel, wpb=wpb)
    return pl.pallas_call(
        kern,
        grid_spec=grid_spec,
        out_shape=jax.ShapeDtypeStruct((n, d), F32),
        compiler_params=_params(("arbitrary", "arbitrary")),
        name="swa_attention",
    )(sinks.astype(F32) * LOG2_E, q, kv, kv, h, w_o.astype(BF16), b_o.reshape(1, d))


R_IDX0, R_IDX1, R_GATE0, R_GATE1, R_RANK0, R_RANK1 = range(6)


def _router_kernel(h_ref, g_ref, rw_ref, xpk_ref, route_ref, cnt_ref, tri_ref, carry_ref):
    rows = h_ref.shape[0]
    i = pl.program_id(0)

    @pl.when(i == 0)
    def _():
        r = lax.broadcasted_iota(jnp.int32, (rows, rows), 0)
        c = lax.broadcasted_iota(jnp.int32, (rows, rows), 1)
        tri_ref[...] = jnp.where(r > c, 1.0, 0.0).astype(BF16)
        carry_ref[...] = jnp.zeros_like(carry_ref)

    xn = _rms(h_ref[...], g_ref[...])
    xb = xn.astype(BF16)
    for p, words in enumerate(_pack_bf16_pairs(xn, xpk_ref.shape[2])):
        xpk_ref[p] = words

    logits = jnp.dot(xb, rw_ref[...], preferred_element_type=F32)
    lane = lax.broadcasted_iota(jnp.int32, logits.shape, 1)
    lg = jnp.where(lane < N_EXPERTS, logits, -jnp.inf)
    m1 = jnp.max(lg, axis=-1, keepdims=True)
    i1 = jnp.min(jnp.where(lg == m1, lane, LANES), axis=-1, keepdims=True)
    lg2 = jnp.where(lane == i1, -jnp.inf, lg)
    m2 = jnp.max(lg2, axis=-1, keepdims=True)
    i2 = jnp.min(jnp.where(lg2 == m2, lane, LANES), axis=-1, keepdims=True)
    e = jnp.exp(m2 - m1)
    g1 = 1.0 / (1.0 + e)
    g2 = e / (1.0 + e)

    sel = (lane == i1) | (lane == i2)
    sel_f = jnp.where(sel, 1.0, 0.0)
    carry = carry_ref[0:1, :]
    before = jnp.dot(tri_ref[...], sel_f.astype(BF16), preferred_element_type=F32) + carry
    r1 = jnp.sum(jnp.where(lane == i1, before, 0.0), axis=-1, keepdims=True)
    r2 = jnp.sum(jnp.where(lane == i2, before, 0.0), axis=-1, keepdims=True)
    carry = carry + jnp.sum(sel_f, axis=0, keepdims=True)
    carry_ref[...] = jnp.broadcast_to(carry, carry_ref.shape)
    cnt_ref[...] = jnp.broadcast_to(carry, cnt_ref.shape)

    route = jnp.zeros(logits.shape, F32)
    for k, val in ((R_IDX0, i1.astype(F32)), (R_IDX1, i2.astype(F32)), (R_GATE0, g1),
                   (R_GATE1, g2), (R_RANK0, r1), (R_RANK1, r2)):
        route = jnp.where(lane == k, val, route)
    route_ref[...] = route


def _router(h, g, router_w):
    n, d = h.shape
    rows = _row_block(n, ROUTER_ROWS)
    parts = d // 2 // SC_PIECE
    rw = jnp.zeros((d, LANES), BF16).at[:, 0:N_EXPERTS].set(router_w.astype(BF16))
    return pl.pallas_call(
        _router_kernel,
        grid=(n // rows,),
        in_specs=[
            pl.BlockSpec((rows, d), lambda i: (i, 0)),
            _full((1, d)),
            _full((d, LANES)),
        ],
        out_specs=[
            pl.BlockSpec((parts, rows, SC_PIECE), lambda i: (0, i, 0)),
            pl.BlockSpec((rows, LANES), lambda i: (i, 0)),
            _full((8, LANES)),
        ],
        out_shape=[
            jax.ShapeDtypeStruct((parts, n, SC_PIECE), jnp.uint32),
            jax.ShapeDtypeStruct((n, LANES), F32),
            jax.ShapeDtypeStruct((8, LANES), F32),
        ],
        scratch_shapes=[pltpu.VMEM((rows, rows), BF16), pltpu.VMEM((8, LANES), F32)],
        compiler_params=_params(("arbitrary",)),
        name="moe_router",
    )(h, g.reshape(1, d), rw)


def _sc_mesh():
    return plsc.VectorSubcoreMesh(core_axis_name="core", subcore_axis_name="subcore")


def _gather_pieces(src, idx):
    m = idx.shape[0]
    width = src.shape[1]
    assert m % (SC_WINDOW * SC_WORKERS) == 0

    @functools.partial(pl.kernel, out_type=jax.ShapeDtypeStruct((m, width), src.dtype),
                       mesh=_sc_mesh(), scratch_types=[])
    def gather_kernel(src_hbm, idx_hbm, out_hbm):
        def body(idx_vmem, out_vmem):
            pltpu.sync_copy(src_hbm.at[idx_vmem.at[0]], out_vmem)

        pltpu.emit_pipeline(
            body,
            grid=(m // SC_WINDOW,),
            in_specs=[pl.BlockSpec((1, SC_WINDOW), lambda i: (0, i))],
            out_specs=[pl.BlockSpec((SC_WINDOW, width), lambda i: (i, 0))],
            core_axis_name=("core", "subcore"),
            dimension_semantics=(pltpu.PARALLEL,),
        )(idx_hbm, out_hbm)

    return gather_kernel(src, idx.reshape(1, m))


def _scatter_pieces(src, idx, out_rows):
    m = idx.shape[0]
    width = src.shape[1]
    src_windows = src.shape[0] // SC_WINDOW
    assert m % (SC_WINDOW * SC_WORKERS) == 0 and src.shape[0] % SC_WINDOW == 0

    @functools.partial(pl.kernel, out_type=jax.ShapeDtypeStruct((out_rows, width), src.dtype),
                       mesh=_sc_mesh(), scratch_types=[])
    def scatter_kernel(src_hbm, idx_hbm, out_hbm):
        def body(src_vmem, idx_vmem):
            pltpu.sync_copy(src_vmem, out_hbm.at[idx_vmem.at[0]])

        pltpu.emit_pipeline(
            body,
            grid=(m // SC_WINDOW,),
            in_specs=[pl.BlockSpec((SC_WINDOW, width), lambda i: (i % src_windows, 0)),
                      pl.BlockSpec((1, SC_WINDOW), lambda i: (0, i))],
            out_specs=[],
            core_axis_name=("core", "subcore"),
            dimension_semantics=(pltpu.PARALLEL,),
        )(src_hbm, idx_hbm)

    return scatter_kernel(src, idx.reshape(1, m))


def _moe_combined(h_ref, y_ref, route_ref):
    parts, _, _, piece = y_ref.shape
    route = route_ref[...]
    g0 = route[:, R_GATE0:R_GATE0 + 1]
    g1 = route[:, R_GATE1:R_GATE1 + 1]
    lo, hi = [], []
    for p in range(parts):
        lo0, hi0 = _unpack_bf16_pair(y_ref[p, 0])
        lo1, hi1 = _unpack_bf16_pair(y_ref[p, 1])
        lo.append(g0 * lo0 + g1 * lo1)
        hi.append(g0 * hi0 + g1 * hi1)
    return h_ref[...] + jnp.concatenate(lo + hi, axis=1)


def _moe_specs(pending, rows):
    y_pairs, route = pending
    parts, _, _, piece = y_pairs.shape
    return [pl.BlockSpec((parts, 2, rows, piece), lambda i: (0, 0, i, 0)),
            pl.BlockSpec((rows, LANES), lambda i: (i, 0))]


def _final_kernel(h_ref, y_ref, route_ref, g_ref, o_ref):
    o_ref[...] = _rms(_moe_combined(h_ref, y_ref, route_ref), g_ref[...])


def _final(h, pending, final_g):
    n, d = h.shape
    rows = _row_block(n, PROJ_ROWS)
    return pl.pallas_call(
        _final_kernel,
        grid=(n // rows,),
        in_specs=[pl.BlockSpec((rows, d), lambda i: (i, 0))] + _moe_specs(pending, rows)
        + [_full((1, d))],
        out_specs=pl.BlockSpec((rows, d), lambda i: (i, 0)),
        out_shape=jax.ShapeDtypeStruct((n, d), F32),
        compiler_params=_params(("arbitrary",)),
        name="moe_combine_final_norm",
    )(h, *pending, final_g.reshape(1, d))


def _moe(h, g, router_w, layer, wg, wu, wd):
    n, d = h.shape
    xpk, route, cnt = _router(h, g, router_w)
    parts = xpk.shape[0]

    top_idx = route[:, R_IDX0:R_IDX1 + 1].astype(jnp.int32)
    rank = route[:, R_RANK0:R_RANK1 + 1].astype(jnp.int32)
    sizes = cnt[0, 0:N_EXPERTS].astype(jnp.int32)
    padded = ((sizes + MOE_ROWS - 1) // MOE_ROWS) * MOE_ROWS
    pends = jnp.cumsum(padded)
    pstarts = pends - padded
    dest = (pstarts[top_idx] + rank).T
    n_rows = 2 * n + N_EXPERTS * MOE_ROWS
    n_blocks = n_rows // MOE_ROWS
    block_start = jnp.arange(n_blocks, dtype=jnp.int32) * MOE_ROWS
    block_expert = jnp.minimum(jnp.searchsorted(pends, block_start, side="right"),
                               N_EXPERTS - 1).astype(jnp.int32)
    n_used = (pends[N_EXPERTS - 1:] // MOE_ROWS).astype(jnp.int32)

    off = jnp.arange(parts, dtype=jnp.int32) * n_rows
    scatter_idx = (dest[:, None, :] + off[None, :, None]).reshape(-1)
    x_sorted = _scatter_pieces(xpk.reshape(parts * n, SC_PIECE), scatter_idx, parts * n_rows)
    y_rows = _moe_ffn(x_sorted.reshape(parts, n_rows, SC_PIECE), block_expert, n_used, layer,
                      wg, wu, wd)

    gather_idx = (dest[None, :, :] + off[:, None, None]).reshape(-1)
    y_pairs = _gather_pieces(y_rows.reshape(parts * n_rows, SC_PIECE), gather_idx)
    return y_pairs.reshape(parts, 2, n, SC_PIECE), route


def kernel(x, positions, final_norm_g, ev_norm1_g, ev_w_in, ev_conv_w, ev_ln_g, ev_ln_b, ev_spatial_w, ev_spatial_b, ev_w_out, ev_norm2_g, ev_ffn_wg, ev_ffn_wu, ev_ffn_wd, od_norm1_g, od_w_qkv, od_b_qkv, od_sinks, od_w_o, od_b_o, od_norm2_g, od_router_w, od_exp_wg, od_exp_wu, od_exp_wd):
    batch, seq, d = x.shape
    depth = ev_norm1_g.shape[0] + od_norm1_g.shape[0]
    assert depth % 2 == 0, "the final norm is fused into the last (odd) layer's MoE combine"
    n_q_heads = od_sinks.shape[1]
    h = x.reshape(batch * seq, d)
    exp_wg, exp_wu, exp_wd = _to_bf16(od_exp_wg), _to_bf16(od_exp_wu), _to_bf16(od_exp_wd)
    pending = None
    for layer in range(depth):
        i = layer // 2
        if layer % 2 == 0:
            h = _mixer(h, pending, seq, ev_norm1_g[i], ev_w_in[i], ev_conv_w[i], ev_ln_g[i],
                       ev_ln_b[i], ev_spatial_w[i], ev_spatial_b[i], ev_w_out[i])
            h = _dense_ffn(h, ev_norm2_g[i], ev_ffn_wg[i], ev_ffn_wu[i], ev_ffn_wd[i])
        else:
            q, kv = _qkv(h, od_norm1_g[i], positions, od_w_qkv[i], od_b_qkv[i],
                         n_q_heads * HEAD_DIM)
            h = _attention(h, q, kv, od_sinks[i], od_w_o[i], od_b_o[i], batch, seq)
            pending = _moe(h, od_norm2_g[i], od_router_w[i], i, exp_wg, exp_wu, exp_wd)
    return _final(h, pending, final_norm_g).reshape(batch, seq, d)
```

```python
import functools

import jax
import jax.numpy as jnp
import numpy as np
from jax import lax
from jax.experimental import pallas as pl
from jax.experimental.pallas import tpu as pltpu
from jax.experimental.pallas import tpu_sc as plsc

F32 = jnp.float32
BF16 = jnp.bfloat16

EPS = 1e-5
CHUNK = 128
GMLP_HEADS = 4
CONV_WIDTH = 3
HEAD_DIM = 64
WINDOW = 128
ROPE_DIM = HEAD_DIM // 4
ROPE_THETA = 500000.0
ATTN_SCALE = HEAD_DIM ** -0.5
LOG2_E = float(np.log2(np.e))
Q_SCALE = ATTN_SCALE * LOG2_E
N_EXPERTS = 8
LANES = 128
VMEM_LIMIT = 56 * 1024 * 1024

MIXER_ROWS = 512
FFN_ROWS = 512
CAST_BLOCK_BYTES = 8 * 1024 * 1024
PROJ_ROWS = 1024
ATTN_ROWS = 1024
ROUTER_ROWS = 1024
MOE_ROWS = 512
MOE_COL_SPLIT = 2
SC_WORKERS = 32
SC_PIECE = 256
SC_WINDOW = 128


def _row_block(n, pref):
    b = min(n, pref)
    while n % b:
        b -= LANES
    return b


def _col_block(f, pref):
    b = min(f, pref)
    b -= b % LANES
    while f % b:
        b -= LANES
    return b


def _params(sem):
    return pltpu.CompilerParams(dimension_semantics=sem, vmem_limit_bytes=VMEM_LIMIT)


def _rms(x, g):
    return x * lax.rsqrt(jnp.mean(x * x, axis=-1, keepdims=True) + EPS) * g


def _gelu(x):
    return 0.5 * x * (1.0 + lax.erf(x * np.float32(np.sqrt(0.5))))


def _full(shape):
    return pl.BlockSpec(shape, lambda *_: (0,) * len(shape))


def _mixer_kernel(*refs, blocks_per_seq, has_pending):
    if has_pending:
        h_ref, y_ref, route_ref = refs[:3]
        refs = refs[3:]
    else:
        h_ref = refs[0]
        refs = refs[1:]
    (g1_ref, win_ref, cw_ref, lng_ref, lnb_ref, ws_ref, bst_ref, wout_ref, o_ref,
     tail_ref, yb_ref) = refs
    rows = h_ref.shape[0]
    cd = cw_ref.shape[1]
    gd = lng_ref.shape[1]
    hd = gd // GMLP_HEADS
    i = pl.program_id(0)

    x = _moe_combined(h_ref, y_ref, route_ref) if has_pending else h_ref[...]
    xn = _rms(x, g1_ref[...]).astype(BF16)
    z = jnp.dot(xn, win_ref[...], preferred_element_type=F32)
    a_b = z[:, 0:cd]
    a_c = z[:, cd:2 * cd]
    a_x = z[:, 2 * cd:3 * cd]
    b_u = z[:, 3 * cd:3 * cd + gd]
    b_v = z[:, 3 * cd + gd:3 * cd + 2 * gd]

    g = a_c * a_x

    @pl.when(i % blocks_per_seq == 0)
    def _():
        tail_ref[...] = jnp.zeros_like(tail_ref)

    tail = tail_ref[...]
    row = lax.broadcasted_iota(jnp.int32, g.shape, 0)
    gm1 = jnp.where(row == 0, tail[7:8], pltpu.roll(g, 1, 0))
    gm2 = jnp.where(row == 0, tail[6:7], jnp.where(row == 1, tail[7:8], pltpu.roll(g, 2, 0)))
    tail_ref[...] = g[rows - 8:rows]
    cw = cw_ref[...]
    y_a = a_b * (gm2 * cw[0:1] + gm1 * cw[1:2] + g * cw[2:3])

    u = _gelu(b_u)
    v = _gelu(b_v)
    mu = jnp.mean(v, axis=-1, keepdims=True)
    vc = v - mu
    var = jnp.mean(vc * vc, axis=-1, keepdims=True)
    vn = (vc * lax.rsqrt(var + EPS) * lng_ref[...] + lnb_ref[...]).astype(BF16)
    ri = lax.broadcasted_iota(jnp.int32, (CHUNK, CHUNK), 0)
    ci = lax.broadcasted_iota(jnp.int32, (CHUNK, CHUNK), 1)
    causal = ri >= ci
    bst = bst_ref[...]
    for k in range(GMLP_HEADS):
        w_k = jnp.where(causal, ws_ref[k], 0.0).astype(BF16)
        b_k = bst[:, k:k + 1]
        for c in range(rows // CHUNK):
            rs = slice(c * CHUNK, (c + 1) * CHUNK)
            cs = slice(k * hd, (k + 1) * hd)
            mixed = jnp.dot(w_k, vn[rs, cs], preferred_element_type=F32) + b_k
            yb_ref[rs, cs] = (u[rs, cs] * mixed).astype(BF16)

    out = jnp.dot(y_a.astype(BF16), wout_ref[0:cd, :], preferred_element_type=F32)
    out = out + jnp.dot(yb_ref[...], wout_ref[cd:cd + gd, :], preferred_element_type=F32)
    o_ref[...] = x + out


def _mixer(h, pending, seq, g1, w_in, conv_w, ln_g, ln_b, w_s, b_s, w_out):
    n, d = h.shape
    rows = _row_block(seq, MIXER_ROWS)
    cd = conv_w.shape[0]
    gd = ln_g.shape[0]
    has_pending = pending is not None
    kern = functools.partial(_mixer_kernel, blocks_per_seq=seq // rows, has_pending=has_pending)
    return pl.pallas_call(
        kern,
        grid=(n // rows,),
        in_specs=[pl.BlockSpec((rows, d), lambda i: (i, 0))]
        + (_moe_specs(pending, rows) if has_pending else [])
        + [
            _full((1, d)),
            _full(w_in.shape),
            _full((CONV_WIDTH, cd)),
            _full((1, gd)),
            _full((1, gd)),
            _full(w_s.shape),
            _full((CHUNK, GMLP_HEADS)),
            _full(w_out.shape),
        ],
        out_specs=pl.BlockSpec((rows, d), lambda i: (i, 0)),
        out_shape=jax.ShapeDtypeStruct((n, d), F32),
        scratch_shapes=[pltpu.VMEM((8, cd), F32), pltpu.VMEM((rows, gd), BF16)],
        compiler_params=_params(("arbitrary",)),
        name="mixer",
    )(h, *(pending or ()), g1.reshape(1, d), w_in.astype(BF16), conv_w.T, ln_g.reshape(1, gd),
      ln_b.reshape(1, gd), w_s, b_s.T, w_out.astype(BF16))


def _swiglu(xn, wg, wu, wd):
    h1 = jnp.dot(xn, wg, preferred_element_type=F32)
    h2 = jnp.dot(xn, wu, preferred_element_type=F32)
    a = (h1 / (1.0 + jnp.exp(-h1)) * h2).astype(BF16)
    return jnp.dot(a, wd, preferred_element_type=F32)


def _dense_ffn_kernel(x_ref, g_ref, wg_ref, wu_ref, wd_ref, *rest):
    n_side = (len(rest) - 1) // 2
    side_in, o_ref, side_out = rest[:n_side], rest[n_side], rest[n_side + 1:]
    x = x_ref[...]
    xn = _rms(x, g_ref[...]).astype(BF16)
    o_ref[...] = x + _swiglu(xn, wg_ref[...], wu_ref[...], wd_ref[...])
    for src, dst in zip(side_in, side_out):
        dst[...] = src[...].astype(dst.dtype)


def _side_cast_ok(w, steps):
    rows = int(np.prod(w.shape[:-1]))
    return rows % steps == 0 and (rows // steps) % 16 == 0 and w.shape[-1] % LANES == 0


def _dense_ffn(h, g, wg, wu, wd, side=()):
    n, d = h.shape
    rows = _row_block(n, FFN_ROWS)
    steps = n // rows
    side2d = [w.reshape(-1, w.shape[-1]) for w in side]
    side_specs = [pl.BlockSpec((w.shape[0] // steps, w.shape[1]), lambda i: (i, 0)) for w in side2d]
    outs = pl.pallas_call(
        _dense_ffn_kernel,
        grid=(steps,),
        in_specs=[
            pl.BlockSpec((rows, d), lambda i: (i, 0)),
            _full((1, d)),
            _full(wg.shape),
            _full(wu.shape),
            _full(wd.shape),
        ] + side_specs,
        out_specs=[pl.BlockSpec((rows, d), lambda i: (i, 0))] + side_specs,
        out_shape=[jax.ShapeDtypeStruct((n, d), F32)]
        + [jax.ShapeDtypeStruct(w.shape, BF16) for w in side2d],
        compiler_params=_params(("arbitrary",)),
        name="dense_ffn",
    )(h, g.reshape(1, d), wg.astype(BF16), wu.astype(BF16), wd.astype(BF16), *side2d)
    return outs[0], [o.reshape(w.shape) for o, w in zip(outs[1:], side)]


def _pack_bf16_pairs(x, piece):
    half = x.shape[1] // 2
    bits = lax.bitcast_convert_type(x.astype(BF16).astype(F32), jnp.uint32)
    return [(bits[:, half + p * piece:half + (p + 1) * piece] & jnp.uint32(0xFFFF0000))
            | (bits[:, p * piece:(p + 1) * piece] >> 16) for p in range(half // piece)]


def _unpack_bf16_pair(packed):
    lo = lax.bitcast_convert_type(packed << 16, F32)
    hi = lax.bitcast_convert_type(packed & jnp.uint32(0xFFFF0000), F32)
    return lo, hi


def _moe_ffn_kernel(be_ref, used_ref, x_ref, wg_ref, wu_ref, wd_ref, o_ref, xn_ref, acc_ref, *,
                    n_steps):
    del be_ref
    i = pl.program_id(0)
    f = pl.program_id(1)
    parts, _, piece = x_ref.shape
    half = parts * piece
    active = i < used_ref[0]

    def write_out(val):
        for p, words in enumerate(_pack_bf16_pairs(val, piece)):
            o_ref[p] = words

    def step(first, last):
        if first:
            for p in range(parts):
                lo, hi = _unpack_bf16_pair(x_ref[p])
                xn_ref[:, p * piece:(p + 1) * piece] = lo.astype(BF16)
                xn_ref[:, half + p * piece:half + (p + 1) * piece] = hi.astype(BF16)
        part = _swiglu(xn_ref[...], wg_ref[0, 0], wu_ref[0, 0], wd_ref[0, 0])
        if last:
            write_out(part if first else acc_ref[...] + part)
        elif first:
            acc_ref[...] = part
        else:
            acc_ref[...] += part

    if n_steps == 1:
        pl.when(active)(functools.partial(step, True, True))
    else:
        pl.when(active & (f == 0))(functools.partial(step, True, False))
        if n_steps > 2:
            pl.when(active & (f > 0) & (f < n_steps - 1))(functools.partial(step, False, False))
        pl.when(active & (f == n_steps - 1))(functools.partial(step, False, True))

    @pl.when(jnp.logical_not(active) & (f == n_steps - 1))
    def _():
        o_ref[...] = jnp.zeros_like(o_ref)


def _moe_ffn(x_sorted, block_expert, n_used, layer, wg, wu, wd):
    parts, n_rows, piece = x_sorted.shape
    d = 2 * parts * piece
    fdim = wg.shape[3]
    rows = MOE_ROWS
    cols = _col_block(fdim, fdim // MOE_COL_SPLIT)
    n_steps = fdim // cols

    def col(i, f, used):
        return jnp.where(i < used[0], f, n_steps - 1)

    grid_spec = pltpu.PrefetchScalarGridSpec(
        num_scalar_prefetch=2,
        grid=(n_rows // rows, n_steps),
        in_specs=[
            pl.BlockSpec((parts, rows, piece),
                         lambda i, f, be, used: (0, jnp.minimum(i, used[0] - 1), 0)),
            pl.BlockSpec((1, 1, d, cols), lambda i, f, be, used: (layer, be[i], 0, col(i, f, used))),
            pl.BlockSpec((1, 1, d, cols), lambda i, f, be, used: (layer, be[i], 0, col(i, f, used))),
            pl.BlockSpec((1, 1, cols, d), lambda i, f, be, used: (layer, be[i], col(i, f, used), 0)),
        ],
        out_specs=pl.BlockSpec((parts, rows, piece), lambda i, f, be, used: (0, i, 0)),
        scratch_shapes=[pltpu.VMEM((rows, d), BF16), pltpu.VMEM((rows, d), F32)],
    )
    kern = functools.partial(_moe_ffn_kernel, n_steps=n_steps)
    return pl.pallas_call(
        kern,
        grid_spec=grid_spec,
        out_shape=jax.ShapeDtypeStruct((parts, n_rows, piece), jnp.uint32),
        compiler_params=_params(("arbitrary", "arbitrary")),
        name="moe_ffn",
    )(block_expert, n_used, x_sorted, wg, wu, wd)


def _cast_kernel(x_ref, o_ref):
    o_ref[...] = x_ref[...].astype(o_ref.dtype)


def _to_bf16(w):
    shape = w.shape
    w2 = w.reshape(-1, shape[-1])
    pref = CAST_BLOCK_BYTES // (4 * shape[-1]) // LANES * LANES
    rows = _row_block(w2.shape[0], pref)
    out = pl.pallas_call(
        _cast_kernel,
        grid=(w2.shape[0] // rows,),
        in_specs=[pl.BlockSpec((rows, shape[-1]), lambda i: (i, 0))],
        out_specs=pl.BlockSpec((rows, shape[-1]), lambda i: (i, 0)),
        out_shape=jax.ShapeDtypeStruct(w2.shape, BF16),
        compiler_params=_params(("arbitrary",)),
        name="cast_bf16",
    )(w2)
    return out.reshape(shape)


def _qkv_kernel(h_ref, g_ref, pos_ref, w_ref, b_ref, invf_ref, mc_ref, m1_ref, m2_ref,
                q_ref, kv_ref, *, q_dim):
    x = h_ref[...]
    xn = _rms(x, g_ref[...]).astype(BF16)
    z = jnp.dot(xn, w_ref[...], preferred_element_type=F32) + b_ref[...]
    ang = pos_ref[...].astype(F32) * invf_ref[...]
    reps = LANES // ang.shape[0]
    cos = jnp.concatenate([jnp.cos(ang)] * reps, axis=0).T
    sin = jnp.concatenate([jnp.sin(ang)] * reps, axis=0).T
    cos = jnp.where(mc_ref[...] > 0.0, cos, 1.0)
    s_lo = sin * m1_ref[...]
    s_hi = sin * m2_ref[...]
    half = ROPE_DIM // 2

    def rope(t):
        return t * cos + pltpu.roll(t, LANES - half, 1) * s_lo + pltpu.roll(t, half, 1) * s_hi

    for j in range(q_dim // LANES):
        cs = slice(j * LANES, (j + 1) * LANES)
        q_ref[:, cs] = (rope(z[:, cs]) * Q_SCALE).astype(BF16)
    kv_ref[:, 0:LANES] = rope(z[:, q_dim:q_dim + LANES]).astype(BF16)
    kv_ref[:, LANES:2 * LANES] = z[:, q_dim + LANES:q_dim + 2 * LANES].astype(BF16)


def _rope_lane_masks():
    lane = np.arange(LANES) % HEAD_DIM
    half = ROPE_DIM // 2
    first = (lane < half).astype(np.float32)
    second = ((lane >= half) & (lane < ROPE_DIM)).astype(np.float32)
    return [jnp.asarray(m[None, :]) for m in (first + second, -first, second)]


def _qkv(h, g, positions, w_qkv, b_qkv, q_dim):
    n, d = h.shape
    qkv_dim = w_qkv.shape[1]
    assert qkv_dim == q_dim + 2 * LANES and HEAD_DIM % (ROPE_DIM // 2) == 0
    rows = _row_block(n, PROJ_ROWS)
    inv_freq = ROPE_THETA ** (-jnp.arange(0, ROPE_DIM, 2, dtype=F32) / ROPE_DIM)
    kern = functools.partial(_qkv_kernel, q_dim=q_dim)
    return pl.pallas_call(
        kern,
        grid=(n // rows,),
        in_specs=[
            pl.BlockSpec((rows, d), lambda i: (i, 0)),
            _full((1, d)),
            pl.BlockSpec((1, rows), lambda i: (0, i)),
            _full(w_qkv.shape),
            _full((1, qkv_dim)),
            _full((ROPE_DIM // 2, 1)),
            _full((1, LANES)),
            _full((1, LANES)),
            _full((1, LANES)),
        ],
        out_specs=[
            pl.BlockSpec((rows, q_dim), lambda i: (i, 0)),
            pl.BlockSpec((rows, 2 * LANES), lambda i: (i, 0)),
        ],
        out_shape=[
            jax.ShapeDtypeStruct((n, q_dim), BF16),
            jax.ShapeDtypeStruct((n, 2 * LANES), BF16),
        ],
        compiler_params=_params(("arbitrary",)),
        name="qkv_rope",
    )(h, g.reshape(1, d), positions.reshape(1, n), w_qkv.astype(BF16),
      b_qkv.reshape(1, qkv_dim), inv_freq.reshape(-1, 1), *_rope_lane_masks())


def _attn_kernel(sink_ref, q_ref, kvc_ref, kvp_ref, h_ref, wo_ref, bo_ref, out_ref,
                 kbuf, vbuf, o_buf, *, wpb):
    rows = q_ref.shape[0]
    tiles = q_ref.shape[1] // LANES // 2
    j = pl.program_id(1)
    kbuf[0:WINDOW, :] = kvp_ref[:, 0:LANES]
    kbuf[WINDOW:WINDOW + rows, :] = kvc_ref[:, 0:LANES]
    vbuf[0:WINDOW, :] = kvp_ref[:, LANES:2 * LANES]
    vbuf[WINDOW:WINDOW + rows, :] = kvc_ref[:, LANES:2 * LANES]

    qi = lax.broadcasted_iota(jnp.int32, (WINDOW, 2 * WINDOW), 0)
    sj = lax.broadcasted_iota(jnp.int32, (WINDOW, 2 * WINDOW), 1)
    dist = sj - qi
    local = (dist >= 1) & (dist <= WINDOW)
    first_half = lax.broadcasted_iota(jnp.int32, (2 * WINDOW, LANES), 1) < HEAD_DIM
    out_first_half = lax.broadcasted_iota(jnp.int32, (WINDOW, LANES), 1) < HEAD_DIM
    nt = (((1,), (1,)), ((), ()))

    def window(n, carry):
        r0 = pl.multiple_of(n * WINDOW, WINDOW)
        kt = kbuf[pl.ds(r0, 2 * WINDOW), :].astype(F32)
        vt = vbuf[pl.ds(r0, 2 * WINDOW), :].astype(F32)
        mask = local & ((sj >= WINDOW) | (j * wpb + n > 0))
        k0_lo = jnp.where(first_half, kt, 0.0)
        k1_hi = jnp.where(first_half, 0.0, kt)
        v0_lo = jnp.where(first_half, vt, 1.0)
        v1_hi = jnp.where(first_half, 1.0, vt)
        k_even = (k0_lo.astype(BF16), pltpu.roll(k1_hi, HEAD_DIM, 1).astype(BF16))
        k_odd = (pltpu.roll(k0_lo, HEAD_DIM, 1).astype(BF16), k1_hi.astype(BF16))
        v_even = (v0_lo.astype(BF16), pltpu.roll(v1_hi, HEAD_DIM, 1).astype(BF16))
        v_odd = (pltpu.roll(v0_lo, HEAD_DIM, 1).astype(BF16), v1_hi.astype(BF16))
        for kh in range(2):
            q_stack = jnp.concatenate(
                [q_ref[pl.ds(r0, WINDOW), (kh * tiles + t) * LANES:(kh * tiles + t + 1) * LANES]
                 for t in range(tiles)], axis=0)
            outs = []
            for parity, k_rhs, v_rhs in ((0, k_even[kh], v_even[kh]), (1, k_odd[kh], v_odd[kh])):
                s_all = lax.dot_general(q_stack, k_rhs, nt, preferred_element_type=F32)
                p_tiles, corr = [], []
                for t in range(tiles):
                    sink = sink_ref[(kh * tiles + t) * 2 + parity]
                    s = jnp.where(mask, s_all[t * WINDOW:(t + 1) * WINDOW], -jnp.inf)
                    m = jnp.maximum(jnp.max(s, axis=-1, keepdims=True), sink)
                    p_tiles.append(jnp.exp2(s - m).astype(BF16))
                    corr.append(jnp.exp2(sink - m))
                pv = jnp.dot(jnp.concatenate(p_tiles, axis=0), v_rhs, preferred_element_type=F32)
                outs.append((pv, corr))
            for t in range(tiles):
                pv_e = outs[0][0][t * WINDOW:(t + 1) * WINDOW]
                pv_o = outs[1][0][t * WINDOW:(t + 1) * WINDOW]
                num = jnp.where(out_first_half, pv_e, pv_o)
                sums = pltpu.roll(jnp.where(out_first_half, pv_o, pv_e), HEAD_DIM, 1)
                den = sums + jnp.where(out_first_half, outs[0][1][t], outs[1][1][t])
                o_buf[pl.ds(r0, WINDOW), (kh * tiles + t) * LANES:(kh * tiles + t + 1) * LANES] = (
                    (num / den).astype(BF16))
        return carry

    lax.fori_loop(0, rows // WINDOW, window, 0, unroll=2)
    out_ref[...] = (h_ref[...] + jnp.dot(o_buf[...], wo_ref[...], preferred_element_type=F32)
                    + bo_ref[...])


def _attention(h, q, kv, sinks, w_o, b_o, batch, seq):
    n, q_dim = q.shape
    d = h.shape[1]
    assert kv.shape[1] == 2 * LANES and (q_dim // HEAD_DIM) % 4 == 0
    rows = _row_block(seq, ATTN_ROWS)
    bps = seq // rows
    wpb = rows // WINDOW
    wps = seq // WINDOW

    grid_spec = pltpu.PrefetchScalarGridSpec(
        num_scalar_prefetch=1,
        grid=(batch, bps),
        in_specs=[
            pl.BlockSpec((rows, q_dim), lambda b, j, s: (b * bps + j, 0)),
            pl.BlockSpec((rows, 2 * LANES), lambda b, j, s: (b * bps + j, 0)),
            pl.BlockSpec((WINDOW, 2 * LANES),
                         lambda b, j, s: (b * wps + jnp.maximum(j * wpb - 1, 0), 0)),
            pl.BlockSpec((rows, d), lambda b, j, s: (b * bps + j, 0)),
            pl.BlockSpec((q_dim, d), lambda b, j, s: (0, 0)),
            pl.BlockSpec((1, d), lambda b, j, s: (0, 0)),
        ],
        out_specs=pl.BlockSpec((rows, d), lambda b, j, s: (b * bps + j, 0)),
        scratch_shapes=[pltpu.VMEM((rows + WINDOW, LANES), BF16),
                        pltpu.VMEM((rows + WINDOW, LANES), BF16),
                        pltpu.VMEM((rows, q_dim), BF16)],
    )
    kern = functools.partial(_attn_kernel, wpb=wpb)
    return pl.pallas_call(
        kern,
        grid_spec=grid_spec,
        out_shape=jax.ShapeDtypeStruct((n, d), F32),
        compiler_params=_params(("arbitrary", "arbitrary")),
        name="swa_attention",
    )(sinks.astype(F32) * LOG2_E, q, kv, kv, h, w_o.astype(BF16), b_o.reshape(1, d))


R_IDX0, R_IDX1, R_GATE0, R_GATE1, R_RANK0, R_RANK1 = range(6)


def _router_kernel(h_ref, g_ref, rw_ref, xpk_ref, route_ref, cnt_ref, tri_ref, carry_ref):
    rows = h_ref.shape[0]
    i = pl.program_id(0)

    @pl.when(i == 0)
    def _():
        r = lax.broadcasted_iota(jnp.int32, (rows, rows), 0)
        c = lax.broadcasted_iota(jnp.int32, (rows, rows), 1)
        tri_ref[...] = jnp.where(r > c, 1.0, 0.0).astype(BF16)
        carry_ref[...] = jnp.zeros_like(carry_ref)

    xn = _rms(h_ref[...], g_ref[...])
    xb = xn.astype(BF16)
    for p, words in enumerate(_pack_bf16_pairs(xn, xpk_ref.shape[2])):
        xpk_ref[p] = words

    logits = jnp.dot(xb, rw_ref[...], preferred_element_type=F32)
    lane = lax.broadcasted_iota(jnp.int32, logits.shape, 1)
    lg = jnp.where(lane < N_EXPERTS, logits, -jnp.inf)
    m1 = jnp.max(lg, axis=-1, keepdims=True)
    i1 = jnp.min(jnp.where(lg == m1, lane, LANES), axis=-1, keepdims=True)
    lg2 = jnp.where(lane == i1, -jnp.inf, lg)
    m2 = jnp.max(lg2, axis=-1, keepdims=True)
    i2 = jnp.min(jnp.where(lg2 == m2, lane, LANES), axis=-1, keepdims=True)
    e = jnp.exp(m2 - m1)
    g1 = 1.0 / (1.0 + e)
    g2 = e / (1.0 + e)

    sel = (lane == i1) | (lane == i2)
    sel_f = jnp.where(sel, 1.0, 0.0)
    carry = carry_ref[0:1, :]
    before = jnp.dot(tri_ref[...], sel_f.astype(BF16), preferred_element_type=F32) + carry
    r1 = jnp.sum(jnp.where(lane == i1, before, 0.0), axis=-1, keepdims=True)
    r2 = jnp.sum(jnp.where(lane == i2, before, 0.0), axis=-1, keepdims=True)
    carry = carry + jnp.sum(sel_f, axis=0, keepdims=True)
    carry_ref[...] = jnp.broadcast_to(carry, carry_ref.shape)
    cnt_ref[...] = jnp.broadcast_to(carry, cnt_ref.shape)

    route = jnp.zeros(logits.shape, F32)
    for k, val in ((R_IDX0, i1.astype(F32)), (R_IDX1, i2.astype(F32)), (R_GATE0, g1),
                   (R_GATE1, g2), (R_RANK0, r1), (R_RANK1, r2)):
        route = jnp.where(lane == k, val, route)
    route_ref[...] = route


def _router(h, g, router_w):
    n, d = h.shape
    rows = _row_block(n, ROUTER_ROWS)
    parts = d // 2 // SC_PIECE
    rw = jnp.zeros((d, LANES), BF16).at[:, 0:N_EXPERTS].set(router_w.astype(BF16))
    return pl.pallas_call(
        _router_kernel,
        grid=(n // rows,),
        in_specs=[
            pl.BlockSpec((rows, d), lambda i: (i, 0)),
            _full((1, d)),
            _full((d, LANES)),
        ],
        out_specs=[
            pl.BlockSpec((parts, rows, SC_PIECE), lambda i: (0, i, 0)),
            pl.BlockSpec((rows, LANES), lambda i: (i, 0)),
            _full((8, LANES)),
        ],
        out_shape=[
            jax.ShapeDtypeStruct((parts, n, SC_PIECE), jnp.uint32),
            jax.ShapeDtypeStruct((n, LANES), F32),
            jax.ShapeDtypeStruct((8, LANES), F32),
        ],
        scratch_shapes=[pltpu.VMEM((rows, rows), BF16), pltpu.VMEM((8, LANES), F32)],
        compiler_params=_params(("arbitrary",)),
        name="moe_router",
    )(h, g.reshape(1, d), rw)


def _sc_mesh():
    return plsc.VectorSubcoreMesh(core_axis_name="core", subcore_axis_name="subcore")


def _gather_pieces(src, idx):
    m = idx.shape[0]
    width = src.shape[1]
    assert m % (SC_WINDOW * SC_WORKERS) == 0

    @functools.partial(pl.kernel, out_type=jax.ShapeDtypeStruct((m, width), src.dtype),
                       mesh=_sc_mesh(), scratch_types=[])
    def gather_kernel(src_hbm, idx_hbm, out_hbm):
        def body(idx_vmem, out_vmem):
            pltpu.sync_copy(src_hbm.at[idx_vmem.at[0]], out_vmem)

        pltpu.emit_pipeline(
            body,
            grid=(m // SC_WINDOW,),
            in_specs=[pl.BlockSpec((1, SC_WINDOW), lambda i: (0, i))],
            out_specs=[pl.BlockSpec((SC_WINDOW, width), lambda i: (i, 0))],
            core_axis_name=("core", "subcore"),
            dimension_semantics=(pltpu.PARALLEL,),
        )(idx_hbm, out_hbm)

    return gather_kernel(src, idx.reshape(1, m))


def _scatter_pieces(src, idx, out_rows):
    m = idx.shape[0]
    width = src.shape[1]
    src_windows = src.shape[0] // SC_WINDOW
    assert m % (SC_WINDOW * SC_WORKERS) == 0 and src.shape[0] % SC_WINDOW == 0

    @functools.partial(pl.kernel, out_type=jax.ShapeDtypeStruct((out_rows, width), src.dtype),
                       mesh=_sc_mesh(), scratch_types=[])
    def scatter_kernel(src_hbm, idx_hbm, out_hbm):
        def body(src_vmem, idx_vmem):
            pltpu.sync_copy(src_vmem, out_hbm.at[idx_vmem.at[0]])

        pltpu.emit_pipeline(
            body,
            grid=(m // SC_WINDOW,),
            in_specs=[pl.BlockSpec((SC_WINDOW, width), lambda i: (i % src_windows, 0)),
                      pl.BlockSpec((1, SC_WINDOW), lambda i: (0, i))],
            out_specs=[],
            core_axis_name=("core", "subcore"),
            dimension_semantics=(pltpu.PARALLEL,),
        )(src_hbm, idx_hbm)

    return scatter_kernel(src, idx.reshape(1, m))


def _moe_combined(h_ref, y_ref, route_ref):
    parts, _, _, piece = y_ref.shape
    route = route_ref[...]
    g0 = route[:, R_GATE0:R_GATE0 + 1]
    g1 = route[:, R_GATE1:R_GATE1 + 1]
    lo, hi = [], []
    for p in range(parts):
        lo0, hi0 = _unpack_bf16_pair(y_ref[p, 0])
        lo1, hi1 = _unpack_bf16_pair(y_ref[p, 1])
        lo.append(g0 * lo0 + g1 * lo1)
        hi.append(g0 * hi0 + g1 * hi1)
    return h_ref[...] + jnp.concatenate(lo + hi, axis=1)


def _moe_specs(pending, rows):
    y_pairs, route = pending
    parts, _, _, piece = y_pairs.shape
    return [pl.BlockSpec((parts, 2, rows, piece), lambda i: (0, 0, i, 0)),
            pl.BlockSpec((rows, LANES), lambda i: (i, 0))]


def _final_kernel(h_ref, y_ref, route_ref, g_ref, o_ref):
    o_ref[...] = _rms(_moe_combined(h_ref, y_ref, route_ref), g_ref[...])


def _final(h, pending, final_g):
    n, d = h.shape
    rows = _row_block(n, PROJ_ROWS)
    return pl.pallas_call(
        _final_kernel,
        grid=(n // rows,),
        in_specs=[pl.BlockSpec((rows, d), lambda i: (i, 0))] + _moe_specs(pending, rows)
        + [_full((1, d))],
        out_specs=pl.BlockSpec((rows, d), lambda i: (i, 0)),
        out_shape=jax.ShapeDtypeStruct((n, d), F32),
        compiler_params=_params(("arbitrary",)),
        name="moe_combine_final_norm",
    )(h, *pending, final_g.reshape(1, d))


def _moe(h, g, router_w, layer, wg, wu, wd):
    n, d = h.shape
    xpk, route, cnt = _router(h, g, router_w)
    parts = xpk.shape[0]

    top_idx = route[:, R_IDX0:R_IDX1 + 1].astype(jnp.int32)
    rank = route[:, R_RANK0:R_RANK1 + 1].astype(jnp.int32)
    sizes = cnt[0, 0:N_EXPERTS].astype(jnp.int32)
    padded = ((sizes + MOE_ROWS - 1) // MOE_ROWS) * MOE_ROWS
    pends = jnp.cumsum(padded)
    pstarts = pends - padded
    dest = (pstarts[top_idx] + rank).T
    n_rows = 2 * n + N_EXPERTS * MOE_ROWS
    n_blocks = n_rows // MOE_ROWS
    block_start = jnp.arange(n_blocks, dtype=jnp.int32) * MOE_ROWS
    block_expert = jnp.minimum(
        jnp.sum((block_start[:, None] >= pends[None, :]).astype(jnp.int32), axis=1),
        N_EXPERTS - 1)
    n_used = (pends[N_EXPERTS - 1:] // MOE_ROWS).astype(jnp.int32)

    off = jnp.arange(parts, dtype=jnp.int32) * n_rows
    scatter_idx = (dest[:, None, :] + off[None, :, None]).reshape(-1)
    x_sorted = _scatter_pieces(xpk.reshape(parts * n, SC_PIECE), scatter_idx, parts * n_rows)
    y_rows = _moe_ffn(x_sorted.reshape(parts, n_rows, SC_PIECE), block_expert, n_used, layer,
                      wg, wu, wd)

    gather_idx = (dest[None, :, :] + off[:, None, None]).reshape(-1)
    y_pairs = _gather_pieces(y_rows.reshape(parts * n_rows, SC_PIECE), gather_idx)
    return y_pairs.reshape(parts, 2, n, SC_PIECE), route


def kernel(x, positions, final_norm_g, ev_norm1_g, ev_w_in, ev_conv_w, ev_ln_g, ev_ln_b, ev_spatial_w, ev_spatial_b, ev_w_out, ev_norm2_g, ev_ffn_wg, ev_ffn_wu, ev_ffn_wd, od_norm1_g, od_w_qkv, od_b_qkv, od_sinks, od_w_o, od_b_o, od_norm2_g, od_router_w, od_exp_wg, od_exp_wu, od_exp_wd):
    batch, seq, d = x.shape
    depth = ev_norm1_g.shape[0] + od_norm1_g.shape[0]
    assert depth % 2 == 0, "the final norm is fused into the last (odd) layer's MoE combine"
    n_q_heads = od_sinks.shape[1]
    h = x.reshape(batch * seq, d)
    experts = [od_exp_wg, od_exp_wu, od_exp_wd]
    ffn_steps = (batch * seq) // _row_block(batch * seq, FFN_ROWS)
    ride_along = all(_side_cast_ok(w, ffn_steps) for w in experts)
    if not ride_along:
        experts = [_to_bf16(w) for w in experts]
    pending = None
    for layer in range(depth):
        i = layer // 2
        if layer % 2 == 0:
            h = _mixer(h, pending, seq, ev_norm1_g[i], ev_w_in[i], ev_conv_w[i], ev_ln_g[i],
                       ev_ln_b[i], ev_spatial_w[i], ev_spatial_b[i], ev_w_out[i])
            side = experts if (ride_along and layer == 0) else ()
            h, cast = _dense_ffn(h, ev_norm2_g[i], ev_ffn_wg[i], ev_ffn_wu[i], ev_ffn_wd[i], side)
            if side:
                experts = cast
        else:
            q, kv = _qkv(h, od_norm1_g[i], positions, od_w_qkv[i], od_b_qkv[i],
                         n_q_heads * HEAD_DIM)
            h = _attention(h, q, kv, od_sinks[i], od_w_o[i], od_b_o[i], batch, seq)
            pending = _moe(h, od_norm2_g[i], od_router_w[i], i, *experts)
    return _final(h, pending, final_norm_g).reshape(batch, seq, d)
```

```python
import functools

import jax
import jax.numpy as jnp
import numpy as np
from jax import lax
from jax.experimental import pallas as pl
from jax.experimental.pallas import tpu as pltpu
from jax.experimental.pallas import tpu_sc as plsc

F32 = jnp.float32
BF16 = jnp.bfloat16

EPS = 1e-5
CHUNK = 128
GMLP_HEADS = 4
CONV_WIDTH = 3
HEAD_DIM = 64
WINDOW = 128
ROPE_DIM = HEAD_DIM // 4
ROPE_THETA = 500000.0
ATTN_SCALE = HEAD_DIM ** -0.5
LOG2_E = float(np.log2(np.e))
Q_SCALE = ATTN_SCALE * LOG2_E
N_EXPERTS = 8
LANES = 128
VMEM_LIMIT = 56 * 1024 * 1024

MIXER_ROWS = 512
FFN_ROWS = 512
CAST_BLOCK_BYTES = 8 * 1024 * 1024
PROJ_ROWS = 1024
ATTN_ROWS = 1024
ROUTER_ROWS = 1024
MOE_ROWS = 512
COMBINE_CHUNKS = 4
MOE_COL_SPLIT = 2
SC_WORKERS = 32
SC_PIECE = 256
SC_WINDOW = 128


def _row_block(n, pref):
    b = min(n, pref)
    while n % b:
        b -= LANES
    return b


def _col_block(f, pref):
    b = min(f, pref)
    b -= b % LANES
    while f % b:
        b -= LANES
    return b


def _params(sem):
    return pltpu.CompilerParams(dimension_semantics=sem, vmem_limit_bytes=VMEM_LIMIT)


def _rms(x, g):
    return x * lax.rsqrt(jnp.mean(x * x, axis=-1, keepdims=True) + EPS) * g


def _gelu(x):
    return 0.5 * x * (1.0 + lax.erf(x * np.float32(np.sqrt(0.5))))


def _full(shape):
    return pl.BlockSpec(shape, lambda *_: (0,) * len(shape))


def _mixer_kernel(*refs, blocks_per_seq, has_pending):
    if has_pending:
        h_ref, y_ref, route_ref = refs[:3]
        refs = refs[3:]
    else:
        h_ref = refs[0]
        refs = refs[1:]
    (g1_ref, win_ref, cw_ref, lng_ref, lnb_ref, ws_ref, bst_ref, wout_ref, o_ref,
     tail_ref, yb_ref) = refs
    rows = h_ref.shape[0]
    cd = cw_ref.shape[1]
    gd = lng_ref.shape[1]
    hd = gd // GMLP_HEADS
    i = pl.program_id(0)

    x = _moe_combined(h_ref, y_ref, route_ref) if has_pending else h_ref[...]
    xn = _rms(x, g1_ref[...]).astype(BF16)
    z = jnp.dot(xn, win_ref[...], preferred_element_type=F32)
    a_b = z[:, 0:cd]
    a_c = z[:, cd:2 * cd]
    a_x = z[:, 2 * cd:3 * cd]
    b_u = z[:, 3 * cd:3 * cd + gd]
    b_v = z[:, 3 * cd + gd:3 * cd + 2 * gd]

    g = a_c * a_x

    @pl.when(i % blocks_per_seq == 0)
    def _():
        tail_ref[...] = jnp.zeros_like(tail_ref)

    tail = tail_ref[...]
    row = lax.broadcasted_iota(jnp.int32, g.shape, 0)
    gm1 = jnp.where(row == 0, tail[7:8], pltpu.roll(g, 1, 0))
    gm2 = jnp.where(row == 0, tail[6:7], jnp.where(row == 1, tail[7:8], pltpu.roll(g, 2, 0)))
    tail_ref[...] = g[rows - 8:rows]
    cw = cw_ref[...]
    y_a = a_b * (gm2 * cw[0:1] + gm1 * cw[1:2] + g * cw[2:3])

    u = _gelu(b_u)
    v = _gelu(b_v)
    mu = jnp.mean(v, axis=-1, keepdims=True)
    vc = v - mu
    var = jnp.mean(vc * vc, axis=-1, keepdims=True)
    vn = (vc * lax.rsqrt(var + EPS) * lng_ref[...] + lnb_ref[...]).astype(BF16)
    ri = lax.broadcasted_iota(jnp.int32, (CHUNK, CHUNK), 0)
    ci = lax.broadcasted_iota(jnp.int32, (CHUNK, CHUNK), 1)
    causal = ri >= ci
    bst = bst_ref[...]
    for k in range(GMLP_HEADS):
        w_k = jnp.where(causal, ws_ref[k], 0.0).astype(BF16)
        b_k = bst[:, k:k + 1]
        for c in range(rows // CHUNK):
            rs = slice(c * CHUNK, (c + 1) * CHUNK)
            cs = slice(k * hd, (k + 1) * hd)
            mixed = jnp.dot(w_k, vn[rs, cs], preferred_element_type=F32) + b_k
            yb_ref[rs, cs] = (u[rs, cs] * mixed).astype(BF16)

    out = jnp.dot(y_a.astype(BF16), wout_ref[0:cd, :], preferred_element_type=F32)
    out = out + jnp.dot(yb_ref[...], wout_ref[cd:cd + gd, :], preferred_element_type=F32)
    o_ref[...] = x + out


def _mixer(h, pending, seq, *weights):
    if pending is None:
        return _mixer_call(h, None, 0, 1, seq, *weights)
    y_chunks, route = pending
    for c, y in enumerate(y_chunks):
        h = _mixer_call(h, (y, route), c, len(y_chunks), seq, *weights)
    return h


def _mixer_call(h, pending, chunk, n_chunks, seq, g1, w_in, conv_w, ln_g, ln_b, w_s, b_s, w_out):
    n, d = h.shape
    rows = _row_block(seq, MIXER_ROWS)
    cd = conv_w.shape[0]
    gd = ln_g.shape[0]
    has_pending = pending is not None
    steps = n // rows // n_chunks
    base = chunk * steps
    assert (steps * rows) % seq == 0
    kern = functools.partial(_mixer_kernel, blocks_per_seq=seq // rows, has_pending=has_pending)
    return pl.pallas_call(
        kern,
        grid=(steps,),
        in_specs=[pl.BlockSpec((rows, d), lambda i: (i + base, 0))]
        + (_moe_specs(pending, rows, base) if has_pending else [])
        + [
            _full((1, d)),
            _full(w_in.shape),
            _full((CONV_WIDTH, cd)),
            _full((1, gd)),
            _full((1, gd)),
            _full(w_s.shape),
            _full((CHUNK, GMLP_HEADS)),
            _full(w_out.shape),
        ],
        out_specs=pl.BlockSpec((rows, d), lambda i: (i + base, 0)),
        out_shape=jax.ShapeDtypeStruct((n, d), F32),
        scratch_shapes=[pltpu.VMEM((8, cd), F32), pltpu.VMEM((rows, gd), BF16)],
        input_output_aliases={0: 0} if n_chunks > 1 else {},
        compiler_params=_params(("arbitrary",)),
        name="mixer",
    )(h, *(pending or ()), g1.reshape(1, d), w_in.astype(BF16), conv_w.T, ln_g.reshape(1, gd),
      ln_b.reshape(1, gd), w_s, b_s.T, w_out.astype(BF16))


def _swiglu(xn, wg, wu, wd):
    h1 = jnp.dot(xn, wg, preferred_element_type=F32)
    h2 = jnp.dot(xn, wu, preferred_element_type=F32)
    a = (h1 / (1.0 + jnp.exp(-h1)) * h2).astype(BF16)
    return jnp.dot(a, wd, preferred_element_type=F32)


def _dense_ffn_kernel(x_ref, g_ref, wg_ref, wu_ref, wd_ref, *rest):
    n_side = (len(rest) - 1) // 2
    side_in, o_ref, side_out = rest[:n_side], rest[n_side], rest[n_side + 1:]
    x = x_ref[...]
    xn = _rms(x, g_ref[...]).astype(BF16)
    o_ref[...] = x + _swiglu(xn, wg_ref[...], wu_ref[...], wd_ref[...])
    for src, dst in zip(side_in, side_out):
        dst[...] = src[...].astype(dst.dtype)


def _side_cast_ok(w, steps):
    rows = int(np.prod(w.shape[:-1]))
    return rows % steps == 0 and (rows // steps) % 16 == 0 and w.shape[-1] % LANES == 0


def _dense_ffn(h, g, wg, wu, wd, side=()):
    n, d = h.shape
    rows = _row_block(n, FFN_ROWS)
    steps = n // rows
    side2d = [w.reshape(-1, w.shape[-1]) for w in side]
    side_specs = [pl.BlockSpec((w.shape[0] // steps, w.shape[1]), lambda i: (i, 0)) for w in side2d]
    outs = pl.pallas_call(
        _dense_ffn_kernel,
        grid=(steps,),
        in_specs=[
            pl.BlockSpec((rows, d), lambda i: (i, 0)),
            _full((1, d)),
            _full(wg.shape),
            _full(wu.shape),
            _full(wd.shape),
        ] + side_specs,
        out_specs=[pl.BlockSpec((rows, d), lambda i: (i, 0))] + side_specs,
        out_shape=[jax.ShapeDtypeStruct((n, d), F32)]
        + [jax.ShapeDtypeStruct(w.shape, BF16) for w in side2d],
        compiler_params=_params(("arbitrary",)),
        name="dense_ffn",
    )(h, g.reshape(1, d), wg.astype(BF16), wu.astype(BF16), wd.astype(BF16), *side2d)
    return outs[0], [o.reshape(w.shape) for o, w in zip(outs[1:], side)]


def _pack_bf16_pairs(x, piece):
    half = x.shape[1] // 2
    bits = lax.bitcast_convert_type(x.astype(BF16).astype(F32), jnp.uint32)
    return [(bits[:, half + p * piece:half + (p + 1) * piece] & jnp.uint32(0xFFFF0000))
            | (bits[:, p * piece:(p + 1) * piece] >> 16) for p in range(half // piece)]


def _unpack_bf16_pair(packed):
    lo = lax.bitcast_convert_type(packed << 16, F32)
    hi = lax.bitcast_convert_type(packed & jnp.uint32(0xFFFF0000), F32)
    return lo, hi


def _moe_ffn_kernel(be_ref, used_ref, x_ref, wg_ref, wu_ref, wd_ref, o_ref, xn_ref, acc_ref, *,
                    n_steps):
    del be_ref
    i = pl.program_id(0)
    f = pl.program_id(1)
    parts, _, piece = x_ref.shape
    half = parts * piece
    active = i < used_ref[0]

    def write_out(val):
        for p, words in enumerate(_pack_bf16_pairs(val, piece)):
            o_ref[p] = words

    def step(first, last):
        if first:
            for p in range(parts):
                lo, hi = _unpack_bf16_pair(x_ref[p])
                xn_ref[:, p * piece:(p + 1) * piece] = lo.astype(BF16)
                xn_ref[:, half + p * piece:half + (p + 1) * piece] = hi.astype(BF16)
        part = _swiglu(xn_ref[...], wg_ref[0, 0], wu_ref[0, 0], wd_ref[0, 0])
        if last:
            write_out(part if first else acc_ref[...] + part)
        elif first:
            acc_ref[...] = part
        else:
            acc_ref[...] += part

    if n_steps == 1:
        pl.when(active)(functools.partial(step, True, True))
    else:
        pl.when(active & (f == 0))(functools.partial(step, True, False))
        if n_steps > 2:
            pl.when(active & (f > 0) & (f < n_steps - 1))(functools.partial(step, False, False))
        pl.when(active & (f == n_steps - 1))(functools.partial(step, False, True))

    @pl.when(jnp.logical_not(active) & (f == n_steps - 1))
    def _():
        o_ref[...] = jnp.zeros_like(o_ref)


def _moe_ffn(x_sorted, block_expert, n_used, layer, wg, wu, wd):
    parts, n_rows, piece = x_sorted.shape
    d = 2 * parts * piece
    fdim = wg.shape[3]
    rows = MOE_ROWS
    cols = _col_block(fdim, fdim // MOE_COL_SPLIT)
    n_steps = fdim // cols

    def col(i, f, used):
        return jnp.where(i < used[0], f, n_steps - 1)

    grid_spec = pltpu.PrefetchScalarGridSpec(
        num_scalar_prefetch=2,
        grid=(n_rows // rows, n_steps),
        in_specs=[
            pl.BlockSpec((parts, rows, piece),
                         lambda i, f, be, used: (0, jnp.minimum(i, used[0] - 1), 0)),
            pl.BlockSpec((1, 1, d, cols), lambda i, f, be, used: (layer, be[i], 0, col(i, f, used))),
            pl.BlockSpec((1, 1, d, cols), lambda i, f, be, used: (layer, be[i], 0, col(i, f, used))),
            pl.BlockSpec((1, 1, cols, d), lambda i, f, be, used: (layer, be[i], col(i, f, used), 0)),
        ],
        out_specs=pl.BlockSpec((parts, rows, piece), lambda i, f, be, used: (0, i, 0)),
        scratch_shapes=[pltpu.VMEM((rows, d), BF16), pltpu.VMEM((rows, d), F32)],
    )
    kern = functools.partial(_moe_ffn_kernel, n_steps=n_steps)
    return pl.pallas_call(
        kern,
        grid_spec=grid_spec,
        out_shape=jax.ShapeDtypeStruct((parts, n_rows, piece), jnp.uint32),
        compiler_params=_params(("arbitrary", "arbitrary")),
        name="moe_ffn",
    )(block_expert, n_used, x_sorted, wg, wu, wd)


def _cast_kernel(x_ref, o_ref):
    o_ref[...] = x_ref[...].astype(o_ref.dtype)


def _to_bf16(w):
    shape = w.shape
    w2 = w.reshape(-1, shape[-1])
    pref = CAST_BLOCK_BYTES // (4 * shape[-1]) // LANES * LANES
    rows = _row_block(w2.shape[0], pref)
    out = pl.pallas_call(
        _cast_kernel,
        grid=(w2.shape[0] // rows,),
        in_specs=[pl.BlockSpec((rows, shape[-1]), lambda i: (i, 0))],
        out_specs=pl.BlockSpec((rows, shape[-1]), lambda i: (i, 0)),
        out_shape=jax.ShapeDtypeStruct(w2.shape, BF16),
        compiler_params=_params(("arbitrary",)),
        name="cast_bf16",
    )(w2)
    return out.reshape(shape)


def _qkv_kernel(h_ref, g_ref, pos_ref, w_ref, b_ref, invf_ref, mc_ref, m1_ref, m2_ref,
                q_ref, kv_ref, *, q_dim):
    x = h_ref[...]
    xn = _rms(x, g_ref[...]).astype(BF16)
    z = jnp.dot(xn, w_ref[...], preferred_element_type=F32) + b_ref[...]
    ang = pos_ref[...].astype(F32) * invf_ref[...]
    reps = LANES // ang.shape[0]
    cos = jnp.concatenate([jnp.cos(ang)] * reps, axis=0).T
    sin = jnp.concatenate([jnp.sin(ang)] * reps, axis=0).T
    cos = jnp.where(mc_ref[...] > 0.0, cos, 1.0)
    s_lo = sin * m1_ref[...]
    s_hi = sin * m2_ref[...]
    half = ROPE_DIM // 2

    def rope(t):
        return t * cos + pltpu.roll(t, LANES - half, 1) * s_lo + pltpu.roll(t, half, 1) * s_hi

    for j in range(q_dim // LANES):
        cs = slice(j * LANES, (j + 1) * LANES)
        q_ref[:, cs] = (rope(z[:, cs]) * Q_SCALE).astype(BF16)
    kv_ref[:, 0:LANES] = rope(z[:, q_dim:q_dim + LANES]).astype(BF16)
    kv_ref[:, LANES:2 * LANES] = z[:, q_dim + LANES:q_dim + 2 * LANES].astype(BF16)


def _rope_lane_masks():
    lane = np.arange(LANES) % HEAD_DIM
    half = ROPE_DIM // 2
    first = (lane < half).astype(np.float32)
    second = ((lane >= half) & (lane < ROPE_DIM)).astype(np.float32)
    return [jnp.asarray(m[None, :]) for m in (first + second, -first, second)]


def _qkv(h, g, positions, w_qkv, b_qkv, q_dim):
    n, d = h.shape
    qkv_dim = w_qkv.shape[1]
    assert qkv_dim == q_dim + 2 * LANES and HEAD_DIM % (ROPE_DIM // 2) == 0
    rows = _row_block(n, PROJ_ROWS)
    inv_freq = ROPE_THETA ** (-jnp.arange(0, ROPE_DIM, 2, dtype=F32) / ROPE_DIM)
    kern = functools.partial(_qkv_kernel, q_dim=q_dim)
    return pl.pallas_call(
        kern,
        grid=(n // rows,),
        in_specs=[
            pl.BlockSpec((rows, d), lambda i: (i, 0)),
            _full((1, d)),
            pl.BlockSpec((1, rows), lambda i: (0, i)),
            _full(w_qkv.shape),
            _full((1, qkv_dim)),
            _full((ROPE_DIM // 2, 1)),
            _full((1, LANES)),
            _full((1, LANES)),
            _full((1, LANES)),
        ],
        out_specs=[
            pl.BlockSpec((rows, q_dim), lambda i: (i, 0)),
            pl.BlockSpec((rows, 2 * LANES), lambda i: (i, 0)),
        ],
        out_shape=[
            jax.ShapeDtypeStruct((n, q_dim), BF16),
            jax.ShapeDtypeStruct((n, 2 * LANES), BF16),
        ],
        compiler_params=_params(("arbitrary",)),
        name="qkv_rope",
    )(h, g.reshape(1, d), positions.reshape(1, n), w_qkv.astype(BF16),
      b_qkv.reshape(1, qkv_dim), inv_freq.reshape(-1, 1), *_rope_lane_masks())


def _attn_kernel(sink_ref, q_ref, kvc_ref, kvp_ref, h_ref, wo_ref, bo_ref, out_ref,
                 kbuf, vbuf, o_buf, *, wpb):
    rows = q_ref.shape[0]
    tiles = q_ref.shape[1] // LANES // 2
    j = pl.program_id(1)
    kbuf[0:WINDOW, :] = kvp_ref[:, 0:LANES]
    kbuf[WINDOW:WINDOW + rows, :] = kvc_ref[:, 0:LANES]
    vbuf[0:WINDOW, :] = kvp_ref[:, LANES:2 * LANES]
    vbuf[WINDOW:WINDOW + rows, :] = kvc_ref[:, LANES:2 * LANES]

    qi = lax.broadcasted_iota(jnp.int32, (WINDOW, 2 * WINDOW), 0)
    sj = lax.broadcasted_iota(jnp.int32, (WINDOW, 2 * WINDOW), 1)
    dist = sj - qi
    local = (dist >= 1) & (dist <= WINDOW)
    first_half = lax.broadcasted_iota(jnp.int32, (2 * WINDOW, LANES), 1) < HEAD_DIM
    out_first_half = lax.broadcasted_iota(jnp.int32, (WINDOW, LANES), 1) < HEAD_DIM
    nt = (((1,), (1,)), ((), ()))

    def window(n, carry):
        r0 = pl.multiple_of(n * WINDOW, WINDOW)
        kt = kbuf[pl.ds(r0, 2 * WINDOW), :].astype(F32)
        vt = vbuf[pl.ds(r0, 2 * WINDOW), :].astype(F32)
        mask = local & ((sj >= WINDOW) | (j * wpb + n > 0))
        k0_lo = jnp.where(first_half, kt, 0.0)
        k1_hi = jnp.where(first_half, 0.0, kt)
        v0_lo = jnp.where(first_half, vt, 1.0)
        v1_hi = jnp.where(first_half, 1.0, vt)
        k_even = (k0_lo.astype(BF16), pltpu.roll(k1_hi, HEAD_DIM, 1).astype(BF16))
        k_odd = (pltpu.roll(k0_lo, HEAD_DIM, 1).astype(BF16), k1_hi.astype(BF16))
        v_even = (v0_lo.astype(BF16), pltpu.roll(v1_hi, HEAD_DIM, 1).astype(BF16))
        v_odd = (pltpu.roll(v0_lo, HEAD_DIM, 1).astype(BF16), v1_hi.astype(BF16))
        for kh in range(2):
            q_stack = jnp.concatenate(
                [q_ref[pl.ds(r0, WINDOW), (kh * tiles + t) * LANES:(kh * tiles + t + 1) * LANES]
                 for t in range(tiles)], axis=0)
            outs = []
            for parity, k_rhs, v_rhs in ((0, k_even[kh], v_even[kh]), (1, k_odd[kh], v_odd[kh])):
                s_all = lax.dot_general(q_stack, k_rhs, nt, preferred_element_type=F32)
                p_tiles, corr = [], []
                for t in range(tiles):
                    sink = sink_ref[(kh * tiles + t) * 2 + parity]
                    s = jnp.where(mask, s_all[t * WINDOW:(t + 1) * WINDOW], -jnp.inf)
                    m = jnp.maximum(jnp.max(s, axis=-1, keepdims=True), sink)
                    p_tiles.append(jnp.exp2(s - m).astype(BF16))
                    corr.append(jnp.exp2(sink - m))
                pv = jnp.dot(jnp.concatenate(p_tiles, axis=0), v_rhs, preferred_element_type=F32)
                outs.append((pv, corr))
            for t in range(tiles):
                pv_e = outs[0][0][t * WINDOW:(t + 1) * WINDOW]
                pv_o = outs[1][0][t * WINDOW:(t + 1) * WINDOW]
                num = jnp.where(out_first_half, pv_e, pv_o)
                sums = pltpu.roll(jnp.where(out_first_half, pv_o, pv_e), HEAD_DIM, 1)
                den = sums + jnp.where(out_first_half, outs[0][1][t], outs[1][1][t])
                o_buf[pl.ds(r0, WINDOW), (kh * tiles + t) * LANES:(kh * tiles + t + 1) * LANES] = (
                    (num / den).astype(BF16))
        return carry

    lax.fori_loop(0, rows // WINDOW, window, 0, unroll=2)
    out_ref[...] = (h_ref[...] + jnp.dot(o_buf[...], wo_ref[...], preferred_element_type=F32)
                    + bo_ref[...])


def _attention(h, q, kv, sinks, w_o, b_o, batch, seq):
    n, q_dim = q.shape
    d = h.shape[1]
    assert kv.shape[1] == 2 * LANES and (q_dim // HEAD_DIM) % 4 == 0
    rows = _row_block(seq, ATTN_ROWS)
    bps = seq // rows
    wpb = rows // WINDOW
    wps = seq // WINDOW

    grid_spec = pltpu.PrefetchScalarGridSpec(
        num_scalar_prefetch=1,
        grid=(batch, bps),
        in_specs=[
            pl.BlockSpec((rows, q_dim), lambda b, j, s: (b * bps + j, 0)),
            pl.BlockSpec((rows, 2 * LANES), lambda b, j, s: (b * bps + j, 0)),
            pl.BlockSpec((WINDOW, 2 * LANES),
                         lambda b, j, s: (b * wps + jnp.maximum(j * wpb - 1, 0), 0)),
            pl.BlockSpec((rows, d), lambda b, j, s: (b * bps + j, 0)),
            pl.BlockSpec((q_dim, d), lambda b, j, s: (0, 0)),
            pl.BlockSpec((1, d), lambda b, j, s: (0, 0)),
        ],
        out_specs=pl.BlockSpec((rows, d), lambda b, j, s: (b * bps + j, 0)),
        scratch_shapes=[pltpu.VMEM((rows + WINDOW, LANES), BF16),
                        pltpu.VMEM((rows + WINDOW, LANES), BF16),
                        pltpu.VMEM((rows, q_dim), BF16)],
    )
    kern = functools.partial(_attn_kernel, wpb=wpb)
    return pl.pallas_call(
        kern,
        grid_spec=grid_spec,
        out_shape=jax.ShapeDtypeStruct((n, d), F32),
        compiler_params=_params(("arbitrary", "arbitrary")),
        name="swa_attention",
    )(sinks.astype(F32) * LOG2_E, q, kv, kv, h, w_o.astype(BF16), b_o.reshape(1, d))


R_IDX0, R_IDX1, R_GATE0, R_GATE1, R_RANK0, R_RANK1 = range(6)


def _router_kernel(h_ref, g_ref, rw_ref, xpk_ref, route_ref, cnt_ref, tri_ref, carry_ref):
    rows = h_ref.shape[0]
    i = pl.program_id(0)

    @pl.when(i == 0)
    def _():
        r = lax.broadcasted_iota(jnp.int32, (rows, rows), 0)
        c = lax.broadcasted_iota(jnp.int32, (rows, rows), 1)
        tri_ref[...] = jnp.where(r > c, 1.0, 0.0).astype(BF16)
        carry_ref[...] = jnp.zeros_like(carry_ref)

    xn = _rms(h_ref[...], g_ref[...])
    xb = xn.astype(BF16)
    for p, words in enumerate(_pack_bf16_pairs(xn, xpk_ref.shape[2])):
        xpk_ref[p] = words

    logits = jnp.dot(xb, rw_ref[...], preferred_element_type=F32)
    lane = lax.broadcasted_iota(jnp.int32, logits.shape, 1)
    lg = jnp.where(lane < N_EXPERTS, logits, -jnp.inf)
    m1 = jnp.max(lg, axis=-1, keepdims=True)
    i1 = jnp.min(jnp.where(lg == m1, lane, LANES), axis=-1, keepdims=True)
    lg2 = jnp.where(lane == i1, -jnp.inf, lg)
    m2 = jnp.max(lg2, axis=-1, keepdims=True)
    i2 = jnp.min(jnp.where(lg2 == m2, lane, LANES), axis=-1, keepdims=True)
    e = jnp.exp(m2 - m1)
    g1 = 1.0 / (1.0 + e)
    g2 = e / (1.0 + e)

    sel = (lane == i1) | (lane == i2)
    sel_f = jnp.where(sel, 1.0, 0.0)
    carry = carry_ref[0:1, :]
    before = jnp.dot(tri_ref[...], sel_f.astype(BF16), preferred_element_type=F32) + carry
    r1 = jnp.sum(jnp.where(lane == i1, before, 0.0), axis=-1, keepdims=True)
    r2 = jnp.sum(jnp.where(lane == i2, before, 0.0), axis=-1, keepdims=True)
    carry = carry + jnp.sum(sel_f, axis=0, keepdims=True)
    carry_ref[...] = jnp.broadcast_to(carry, carry_ref.shape)
    cnt_ref[...] = jnp.broadcast_to(carry, cnt_ref.shape)

    route = jnp.zeros(logits.shape, F32)
    for k, val in ((R_IDX0, i1.astype(F32)), (R_IDX1, i2.astype(F32)), (R_GATE0, g1),
                   (R_GATE1, g2), (R_RANK0, r1), (R_RANK1, r2)):
        route = jnp.where(lane == k, val, route)
    route_ref[...] = route


def _router(h, g, router_w):
    n, d = h.shape
    rows = _row_block(n, ROUTER_ROWS)
    parts = d // 2 // SC_PIECE
    rw = jnp.zeros((d, LANES), BF16).at[:, 0:N_EXPERTS].set(router_w.astype(BF16))
    return pl.pallas_call(
        _router_kernel,
        grid=(n // rows,),
        in_specs=[
            pl.BlockSpec((rows, d), lambda i: (i, 0)),
            _full((1, d)),
            _full((d, LANES)),
        ],
        out_specs=[
            pl.BlockSpec((parts, rows, SC_PIECE), lambda i: (0, i, 0)),
            pl.BlockSpec((rows, LANES), lambda i: (i, 0)),
            _full((8, LANES)),
        ],
        out_shape=[
            jax.ShapeDtypeStruct((parts, n, SC_PIECE), jnp.uint32),
            jax.ShapeDtypeStruct((n, LANES), F32),
            jax.ShapeDtypeStruct((8, LANES), F32),
        ],
        scratch_shapes=[pltpu.VMEM((rows, rows), BF16), pltpu.VMEM((8, LANES), F32)],
        compiler_params=_params(("arbitrary",)),
        name="moe_router",
    )(h, g.reshape(1, d), rw)


def _sc_mesh():
    return plsc.VectorSubcoreMesh(core_axis_name="core", subcore_axis_name="subcore")


def _gather_pieces(src, idx):
    m = idx.shape[0]
    width = src.shape[1]
    assert m % (SC_WINDOW * SC_WORKERS) == 0

    @functools.partial(pl.kernel, out_type=jax.ShapeDtypeStruct((m, width), src.dtype),
                       mesh=_sc_mesh(), scratch_types=[])
    def gather_kernel(src_hbm, idx_hbm, out_hbm):
        def body(idx_vmem, out_vmem):
            pltpu.sync_copy(src_hbm.at[idx_vmem.at[0]], out_vmem)

        pltpu.emit_pipeline(
            body,
            grid=(m // SC_WINDOW,),
            in_specs=[pl.BlockSpec((1, SC_WINDOW), lambda i: (0, i))],
            out_specs=[pl.BlockSpec((SC_WINDOW, width), lambda i: (i, 0))],
            core_axis_name=("core", "subcore"),
            dimension_semantics=(pltpu.PARALLEL,),
        )(idx_hbm, out_hbm)

    return gather_kernel(src, idx.reshape(1, m))


def _scatter_pieces(src, idx, out_rows):
    m = idx.shape[0]
    width = src.shape[1]
    src_windows = src.shape[0] // SC_WINDOW
    assert m % (SC_WINDOW * SC_WORKERS) == 0 and src.shape[0] % SC_WINDOW == 0

    @functools.partial(pl.kernel, out_type=jax.ShapeDtypeStruct((out_rows, width), src.dtype),
                       mesh=_sc_mesh(), scratch_types=[])
    def scatter_kernel(src_hbm, idx_hbm, out_hbm):
        def body(src_vmem, idx_vmem):
            pltpu.sync_copy(src_vmem, out_hbm.at[idx_vmem.at[0]])

        pltpu.emit_pipeline(
            body,
            grid=(m // SC_WINDOW,),
            in_specs=[pl.BlockSpec((SC_WINDOW, width), lambda i: (i % src_windows, 0)),
                      pl.BlockSpec((1, SC_WINDOW), lambda i: (0, i))],
            out_specs=[],
            core_axis_name=("core", "subcore"),
            dimension_semantics=(pltpu.PARALLEL,),
        )(src_hbm, idx_hbm)

    return scatter_kernel(src, idx.reshape(1, m))


def _moe_combined(h_ref, y_ref, route_ref):
    parts, _, _, piece = y_ref.shape
    route = route_ref[...]
    g0 = route[:, R_GATE0:R_GATE0 + 1]
    g1 = route[:, R_GATE1:R_GATE1 + 1]
    lo, hi = [], []
    for p in range(parts):
        lo0, hi0 = _unpack_bf16_pair(y_ref[p, 0])
        lo1, hi1 = _unpack_bf16_pair(y_ref[p, 1])
        lo.append(g0 * lo0 + g1 * lo1)
        hi.append(g0 * hi0 + g1 * hi1)
    return h_ref[...] + jnp.concatenate(lo + hi, axis=1)


def _moe_specs(pending, rows, base):
    y_pairs, route = pending
    parts, _, _, piece = y_pairs.shape
    return [pl.BlockSpec((parts, 2, rows, piece), lambda i: (0, 0, i, 0)),
            pl.BlockSpec((rows, LANES), lambda i: (i + base, 0))]


def _final_kernel(h_ref, y_ref, route_ref, g_ref, o_ref):
    o_ref[...] = _rms(_moe_combined(h_ref, y_ref, route_ref), g_ref[...])


def _final(h, pending, final_g):
    n, d = h.shape
    y_chunks, route = pending
    n_chunks = len(y_chunks)
    rows = _row_block(n // n_chunks, PROJ_ROWS)
    steps = n // rows // n_chunks
    for c, y in enumerate(y_chunks):
        base = c * steps
        h = pl.pallas_call(
            _final_kernel,
            grid=(steps,),
            in_specs=[pl.BlockSpec((rows, d), lambda i, base=base: (i + base, 0))]
            + _moe_specs((y, route), rows, base) + [_full((1, d))],
            out_specs=pl.BlockSpec((rows, d), lambda i, base=base: (i + base, 0)),
            out_shape=jax.ShapeDtypeStruct((n, d), F32),
            input_output_aliases={0: 0} if n_chunks > 1 else {},
            compiler_params=_params(("arbitrary",)),
            name="moe_combine_final_norm",
        )(h, y, route, final_g.reshape(1, d))
    return h


def _moe(h, g, router_w, layer, wg, wu, wd, n_chunks):
    n, d = h.shape
    xpk, route, cnt = _router(h, g, router_w)
    parts = xpk.shape[0]

    top_idx = route[:, R_IDX0:R_IDX1 + 1].astype(jnp.int32)
    rank = route[:, R_RANK0:R_RANK1 + 1].astype(jnp.int32)
    sizes = cnt[0, 0:N_EXPERTS].astype(jnp.int32)
    padded = ((sizes + MOE_ROWS - 1) // MOE_ROWS) * MOE_ROWS
    pends = jnp.cumsum(padded)
    pstarts = pends - padded
    dest = (pstarts[top_idx] + rank).T
    n_rows = 2 * n + N_EXPERTS * MOE_ROWS
    n_blocks = n_rows // MOE_ROWS
    block_start = jnp.arange(n_blocks, dtype=jnp.int32) * MOE_ROWS
    block_expert = jnp.minimum(
        jnp.sum((block_start[:, None] >= pends[None, :]).astype(jnp.int32), axis=1),
        N_EXPERTS - 1)
    n_used = (pends[N_EXPERTS - 1:] // MOE_ROWS).astype(jnp.int32)

    off = jnp.arange(parts, dtype=jnp.int32) * n_rows
    scatter_idx = (dest[:, None, :] + off[None, :, None]).reshape(-1)
    x_sorted = _scatter_pieces(xpk.reshape(parts * n, SC_PIECE), scatter_idx, parts * n_rows)
    y_rows = _moe_ffn(x_sorted.reshape(parts, n_rows, SC_PIECE), block_expert, n_used, layer,
                      wg, wu, wd)

    y_flat = y_rows.reshape(parts * n_rows, SC_PIECE)
    nc = n // n_chunks
    y_chunks = []
    for c in range(n_chunks):
        gather_idx = (dest[None, :, c * nc:(c + 1) * nc] + off[:, None, None]).reshape(-1)
        y_chunks.append(_gather_pieces(y_flat, gather_idx).reshape(parts, 2, nc, SC_PIECE))
    return y_chunks, route


def kernel(x, positions, final_norm_g, ev_norm1_g, ev_w_in, ev_conv_w, ev_ln_g, ev_ln_b, ev_spatial_w, ev_spatial_b, ev_w_out, ev_norm2_g, ev_ffn_wg, ev_ffn_wu, ev_ffn_wd, od_norm1_g, od_w_qkv, od_b_qkv, od_sinks, od_w_o, od_b_o, od_norm2_g, od_router_w, od_exp_wg, od_exp_wu, od_exp_wd):
    batch, seq, d = x.shape
    depth = ev_norm1_g.shape[0] + od_norm1_g.shape[0]
    assert depth % 2 == 0, "the final norm is fused into the last (odd) layer's MoE combine"
    n_q_heads = od_sinks.shape[1]
    h = x.reshape(batch * seq, d)
    experts = [od_exp_wg, od_exp_wu, od_exp_wd]
    ffn_steps = (batch * seq) // _row_block(batch * seq, FFN_ROWS)
    ride_along = all(_side_cast_ok(w, ffn_steps) for w in experts)
    if not ride_along:
        experts = [_to_bf16(w) for w in experts]
    pending = None
    pieces_per_token = 2 * (d // 2 // SC_PIECE)
    n_chunks = max(c for c in range(1, COMBINE_CHUNKS + 1)
                   if batch % c == 0
                   and (batch // c * seq * pieces_per_token) % (SC_WINDOW * SC_WORKERS) == 0)
    for layer in range(depth):
        i = layer // 2
        if layer % 2 == 0:
            h = _mixer(h, pending, seq, ev_norm1_g[i], ev_w_in[i], ev_conv_w[i], ev_ln_g[i],
                       ev_ln_b[i], ev_spatial_w[i], ev_spatial_b[i], ev_w_out[i])
            side = experts if (ride_along and layer == 0) else ()
            h, cast = _dense_ffn(h, ev_norm2_g[i], ev_ffn_wg[i], ev_ffn_wu[i], ev_ffn_wd[i], side)
            if side:
                experts = cast
        else:
            q, kv = _qkv(h, od_norm1_g[i], positions, od_w_qkv[i], od_b_qkv[i],
                         n_q_heads * HEAD_DIM)
            h = _attention(h, q, kv, od_sinks[i], od_w_o[i], od_b_o[i], batch, seq)
            pending = _moe(h, od_norm2_g[i], od_router_w[i], i, *experts, n_chunks)
    return _final(h, pending, final_norm_g).reshape(batch, seq, d)
```

```python
import functools

import jax
import jax.numpy as jnp
import numpy as np
from jax import lax
from jax.experimental import pallas as pl
from jax.experimental.pallas import tpu as pltpu
from jax.experimental.pallas import tpu_sc as plsc

F32 = jnp.float32
BF16 = jnp.bfloat16

EPS = 1e-5
CHUNK = 128
GMLP_HEADS = 4
CONV_WIDTH = 3
HEAD_DIM = 64
WINDOW = 128
ROPE_DIM = HEAD_DIM // 4
ROPE_THETA = 500000.0
ATTN_SCALE = HEAD_DIM ** -0.5
LOG2_E = float(np.log2(np.e))
Q_SCALE = ATTN_SCALE * LOG2_E
N_EXPERTS = 8
LANES = 128
VMEM_LIMIT = 56 * 1024 * 1024

MIXER_ROWS = 512
FFN_ROWS = 512
CAST_BLOCK_BYTES = 8 * 1024 * 1024
PROJ_ROWS = 1024
ATTN_ROWS = 1024
ROUTER_ROWS = 1024
MOE_ROWS = 512
COMBINE_CHUNKS = 4
MOE_COL_SPLIT = 2
SC_WORKERS = 32
SC_PIECE = 256
SC_WINDOW = 128


def _row_block(n, pref):
    b = min(n, pref)
    while n % b:
        b -= LANES
    return b


def _col_block(f, pref):
    b = min(f, pref)
    b -= b % LANES
    while f % b:
        b -= LANES
    return b


def _params(sem):
    return pltpu.CompilerParams(dimension_semantics=sem, vmem_limit_bytes=VMEM_LIMIT)


def _rms(x, g):
    return x * lax.rsqrt(jnp.mean(x * x, axis=-1, keepdims=True) + EPS) * g


def _gelu(x):
    return 0.5 * x * (1.0 + lax.erf(x * np.float32(np.sqrt(0.5))))


def _full(shape):
    return pl.BlockSpec(shape, lambda *_: (0,) * len(shape))


def _mixer_kernel(*refs, blocks_per_seq, has_pending):
    if has_pending:
        h_ref, y_ref, route_ref = refs[:3]
        refs = refs[3:]
    else:
        h_ref = refs[0]
        refs = refs[1:]
    (g1_ref, win_ref, cw_ref, lng_ref, lnb_ref, ws_ref, bst_ref, wout_ref, o_ref,
     tail_ref, yb_ref) = refs
    rows = h_ref.shape[0]
    cd = cw_ref.shape[1]
    gd = lng_ref.shape[1]
    hd = gd // GMLP_HEADS
    i = pl.program_id(0)

    x = _moe_combined(h_ref, y_ref, route_ref) if has_pending else h_ref[...]
    xn = _rms(x, g1_ref[...]).astype(BF16)
    z = jnp.dot(xn, win_ref[...], preferred_element_type=F32)
    a_b = z[:, 0:cd]
    a_c = z[:, cd:2 * cd]
    a_x = z[:, 2 * cd:3 * cd]
    b_u = z[:, 3 * cd:3 * cd + gd]
    b_v = z[:, 3 * cd + gd:3 * cd + 2 * gd]

    g = a_c * a_x

    @pl.when(i % blocks_per_seq == 0)
    def _():
        tail_ref[...] = jnp.zeros_like(tail_ref)

    tail = tail_ref[...]
    row = lax.broadcasted_iota(jnp.int32, g.shape, 0)
    gm1 = jnp.where(row == 0, tail[7:8], pltpu.roll(g, 1, 0))
    gm2 = jnp.where(row == 0, tail[6:7], jnp.where(row == 1, tail[7:8], pltpu.roll(g, 2, 0)))
    tail_ref[...] = g[rows - 8:rows]
    cw = cw_ref[...]
    y_a = a_b * (gm2 * cw[0:1] + gm1 * cw[1:2] + g * cw[2:3])

    u = _gelu(b_u)
    v = _gelu(b_v)
    mu = jnp.mean(v, axis=-1, keepdims=True)
    vc = v - mu
    var = jnp.mean(vc * vc, axis=-1, keepdims=True)
    vn = (vc * lax.rsqrt(var + EPS) * lng_ref[...] + lnb_ref[...]).astype(BF16)
    ri = lax.broadcasted_iota(jnp.int32, (CHUNK, CHUNK), 0)
    ci = lax.broadcasted_iota(jnp.int32, (CHUNK, CHUNK), 1)
    causal = ri >= ci
    bst = bst_ref[...]
    for k in range(GMLP_HEADS):
        w_k = jnp.where(causal, ws_ref[k], 0.0).astype(BF16)
        b_k = bst[:, k:k + 1]
        for c in range(rows // CHUNK):
            rs = slice(c * CHUNK, (c + 1) * CHUNK)
            cs = slice(k * hd, (k + 1) * hd)
            mixed = jnp.dot(w_k, vn[rs, cs], preferred_element_type=F32) + b_k
            yb_ref[rs, cs] = (u[rs, cs] * mixed).astype(BF16)

    out = jnp.dot(y_a.astype(BF16), wout_ref[0:cd, :], preferred_element_type=F32)
    out = out + jnp.dot(yb_ref[...], wout_ref[cd:cd + gd, :], preferred_element_type=F32)
    o_ref[...] = x + out


def _mixer(h, pending, seq, *weights):
    if pending is None:
        return _mixer_call(h, None, 0, 1, seq, *weights)
    y_chunks, route = pending
    for c, y in enumerate(y_chunks):
        h = _mixer_call(h, (y, route), c, len(y_chunks), seq, *weights)
    return h


def _mixer_call(h, pending, chunk, n_chunks, seq, g1, w_in, conv_w, ln_g, ln_b, w_s, b_s, w_out):
    n, d = h.shape
    rows = _row_block(seq, MIXER_ROWS)
    cd = conv_w.shape[0]
    gd = ln_g.shape[0]
    has_pending = pending is not None
    steps = n // rows // n_chunks
    base = chunk * steps
    assert (steps * rows) % seq == 0
    kern = functools.partial(_mixer_kernel, blocks_per_seq=seq // rows, has_pending=has_pending)
    return pl.pallas_call(
        kern,
        grid=(steps,),
        in_specs=[pl.BlockSpec((rows, d), lambda i: (i + base, 0))]
        + (_moe_specs(pending, rows, base) if has_pending else [])
        + [
            _full((1, d)),
            _full(w_in.shape),
            _full((CONV_WIDTH, cd)),
            _full((1, gd)),
            _full((1, gd)),
            _full(w_s.shape),
            _full((CHUNK, GMLP_HEADS)),
            _full(w_out.shape),
        ],
        out_specs=pl.BlockSpec((rows, d), lambda i: (i + base, 0)),
        out_shape=jax.ShapeDtypeStruct((n, d), F32),
        scratch_shapes=[pltpu.VMEM((8, cd), F32), pltpu.VMEM((rows, gd), BF16)],
        input_output_aliases={0: 0} if n_chunks > 1 else {},
        compiler_params=_params(("arbitrary",)),
        name="mixer",
    )(h, *(pending or ()), g1.reshape(1, d), w_in.astype(BF16), conv_w.T, ln_g.reshape(1, gd),
      ln_b.reshape(1, gd), w_s, b_s.T, w_out.astype(BF16))


def _swiglu(xn, wg, wu, wd):
    h1 = jnp.dot(xn, wg, preferred_element_type=F32)
    h2 = jnp.dot(xn, wu, preferred_element_type=F32)
    a = (h1 / (1.0 + jnp.exp(-h1)) * h2).astype(BF16)
    return jnp.dot(a, wd, preferred_element_type=F32)


def _dense_ffn_kernel(x_ref, g_ref, wg_ref, wu_ref, wd_ref, *rest):
    n_side = (len(rest) - 1) // 2
    side_in, o_ref, side_out = rest[:n_side], rest[n_side], rest[n_side + 1:]
    x = x_ref[...]
    xn = _rms(x, g_ref[...]).astype(BF16)
    o_ref[...] = x + _swiglu(xn, wg_ref[...], wu_ref[...], wd_ref[...])
    for src, dst in zip(side_in, side_out):
        dst[...] = src[...].astype(dst.dtype)


def _side_cast_ok(w, steps):
    rows = int(np.prod(w.shape[:-1]))
    return rows % steps == 0 and (rows // steps) % 16 == 0 and w.shape[-1] % LANES == 0


def _dense_ffn(h, g, wg, wu, wd, side=()):
    n, d = h.shape
    rows = _row_block(n, FFN_ROWS)
    steps = n // rows
    side2d = [w.reshape(-1, w.shape[-1]) for w in side]
    side_specs = [pl.BlockSpec((w.shape[0] // steps, w.shape[1]), lambda i: (i, 0)) for w in side2d]
    outs = pl.pallas_call(
        _dense_ffn_kernel,
        grid=(steps,),
        in_specs=[
            pl.BlockSpec((rows, d), lambda i: (i, 0)),
            _full((1, d)),
            _full(wg.shape),
            _full(wu.shape),
            _full(wd.shape),
        ] + side_specs,
        out_specs=[pl.BlockSpec((rows, d), lambda i: (i, 0))] + side_specs,
        out_shape=[jax.ShapeDtypeStruct((n, d), F32)]
        + [jax.ShapeDtypeStruct(w.shape, BF16) for w in side2d],
        compiler_params=_params(("arbitrary",)),
        name="dense_ffn",
    )(h, g.reshape(1, d), wg.astype(BF16), wu.astype(BF16), wd.astype(BF16), *side2d)
    return outs[0], [o.reshape(w.shape) for o, w in zip(outs[1:], side)]


def _pack_bf16_pairs(x, piece):
    half = x.shape[1] // 2
    bits = lax.bitcast_convert_type(x.astype(BF16).astype(F32), jnp.uint32)
    return [(bits[:, half + p * piece:half + (p + 1) * piece] & jnp.uint32(0xFFFF0000))
            | (bits[:, p * piece:(p + 1) * piece] >> 16) for p in range(half // piece)]


def _unpack_bf16_pair(packed):
    lo = lax.bitcast_convert_type(packed << 16, F32)
    hi = lax.bitcast_convert_type(packed & jnp.uint32(0xFFFF0000), F32)
    return lo, hi


def _moe_ffn_kernel(be_ref, used_ref, x_ref, wg_ref, wu_ref, wd_ref, o_ref, xn_ref, acc_ref, *,
                    n_steps):
    del be_ref
    i = pl.program_id(0)
    f = pl.program_id(1)
    parts, _, piece = x_ref.shape
    half = parts * piece
    active = i < used_ref[0]

    def write_out(val):
        for p, words in enumerate(_pack_bf16_pairs(val, piece)):
            o_ref[p] = words

    def step(first, last):
        if first:
            for p in range(parts):
                lo, hi = _unpack_bf16_pair(x_ref[p])
                xn_ref[:, p * piece:(p + 1) * piece] = lo.astype(BF16)
                xn_ref[:, half + p * piece:half + (p + 1) * piece] = hi.astype(BF16)
        part = _swiglu(xn_ref[...], wg_ref[0, 0], wu_ref[0, 0], wd_ref[0, 0])
        if last:
            write_out(part if first else acc_ref[...] + part)
        elif first:
            acc_ref[...] = part
        else:
            acc_ref[...] += part

    if n_steps == 1:
        pl.when(active)(functools.partial(step, True, True))
    else:
        pl.when(active & (f == 0))(functools.partial(step, True, False))
        if n_steps > 2:
            pl.when(active & (f > 0) & (f < n_steps - 1))(functools.partial(step, False, False))
        pl.when(active & (f == n_steps - 1))(functools.partial(step, False, True))

    @pl.when(jnp.logical_not(active) & (f == n_steps - 1))
    def _():
        o_ref[...] = jnp.zeros_like(o_ref)


def _moe_ffn(x_sorted, block_expert, n_used, layer, wg, wu, wd):
    parts, n_rows, piece = x_sorted.shape
    d = 2 * parts * piece
    fdim = wg.shape[3]
    rows = MOE_ROWS
    cols = _col_block(fdim, fdim // MOE_COL_SPLIT)
    n_steps = fdim // cols

    def col(i, f, used):
        return jnp.where(i < used[0], f, n_steps - 1)

    grid_spec = pltpu.PrefetchScalarGridSpec(
        num_scalar_prefetch=2,
        grid=(n_rows // rows, n_steps),
        in_specs=[
            pl.BlockSpec((parts, rows, piece),
                         lambda i, f, be, used: (0, jnp.minimum(i, used[0] - 1), 0)),
            pl.BlockSpec((1, 1, d, cols), lambda i, f, be, used: (layer, be[i], 0, col(i, f, used))),
            pl.BlockSpec((1, 1, d, cols), lambda i, f, be, used: (layer, be[i], 0, col(i, f, used))),
            pl.BlockSpec((1, 1, cols, d), lambda i, f, be, used: (layer, be[i], col(i, f, used), 0)),
        ],
        out_specs=pl.BlockSpec((parts, rows, piece), lambda i, f, be, used: (0, i, 0)),
        scratch_shapes=[pltpu.VMEM((rows, d), BF16), pltpu.VMEM((rows, d), F32)],
    )
    kern = functools.partial(_moe_ffn_kernel, n_steps=n_steps)
    return pl.pallas_call(
        kern,
        grid_spec=grid_spec,
        out_shape=jax.ShapeDtypeStruct((parts, n_rows, piece), jnp.uint32),
        compiler_params=_params(("arbitrary", "arbitrary")),
        name="moe_ffn",
    )(block_expert, n_used, x_sorted, wg, wu, wd)


def _cast_kernel(x_ref, o_ref):
    o_ref[...] = x_ref[...].astype(o_ref.dtype)


def _to_bf16(w):
    shape = w.shape
    w2 = w.reshape(-1, shape[-1])
    pref = CAST_BLOCK_BYTES // (4 * shape[-1]) // LANES * LANES
    rows = _row_block(w2.shape[0], pref)
    out = pl.pallas_call(
        _cast_kernel,
        grid=(w2.shape[0] // rows,),
        in_specs=[pl.BlockSpec((rows, shape[-1]), lambda i: (i, 0))],
        out_specs=pl.BlockSpec((rows, shape[-1]), lambda i: (i, 0)),
        out_shape=jax.ShapeDtypeStruct(w2.shape, BF16),
        compiler_params=_params(("arbitrary",)),
        name="cast_bf16",
    )(w2)
    return out.reshape(shape)


def _qkv_kernel(h_ref, g_ref, pos_ref, w_ref, b_ref, invf_ref, mc_ref, m1_ref, m2_ref,
                q_ref, kv_ref, *, q_dim):
    x = h_ref[...]
    xn = _rms(x, g_ref[...]).astype(BF16)
    z = jnp.dot(xn, w_ref[...], preferred_element_type=F32) + b_ref[...]
    ang = pos_ref[...].astype(F32) * invf_ref[...]
    reps = LANES // ang.shape[0]
    cos = jnp.concatenate([jnp.cos(ang)] * reps, axis=0).T
    sin = jnp.concatenate([jnp.sin(ang)] * reps, axis=0).T
    cos = jnp.where(mc_ref[...] > 0.0, cos, 1.0)
    s_lo = sin * m1_ref[...]
    s_hi = sin * m2_ref[...]
    half = ROPE_DIM // 2

    def rope(t):
        return t * cos + pltpu.roll(t, LANES - half, 1) * s_lo + pltpu.roll(t, half, 1) * s_hi

    for j in range(q_dim // LANES):
        cs = slice(j * LANES, (j + 1) * LANES)
        q_ref[:, cs] = (rope(z[:, cs]) * Q_SCALE).astype(BF16)
    kv_ref[:, 0:LANES] = rope(z[:, q_dim:q_dim + LANES]).astype(BF16)
    kv_ref[:, LANES:2 * LANES] = z[:, q_dim + LANES:q_dim + 2 * LANES].astype(BF16)


def _rope_lane_masks():
    lane = np.arange(LANES) % HEAD_DIM
    half = ROPE_DIM // 2
    first = (lane < half).astype(np.float32)
    second = ((lane >= half) & (lane < ROPE_DIM)).astype(np.float32)
    return [jnp.asarray(m[None, :]) for m in (first + second, -first, second)]


def _qkv(h, g, positions, w_qkv, b_qkv, q_dim):
    n, d = h.shape
    qkv_dim = w_qkv.shape[1]
    assert qkv_dim == q_dim + 2 * LANES and HEAD_DIM % (ROPE_DIM // 2) == 0
    rows = _row_block(n, PROJ_ROWS)
    inv_freq = ROPE_THETA ** (-jnp.arange(0, ROPE_DIM, 2, dtype=F32) / ROPE_DIM)
    kern = functools.partial(_qkv_kernel, q_dim=q_dim)
    return pl.pallas_call(
        kern,
        grid=(n // rows,),
        in_specs=[
            pl.BlockSpec((rows, d), lambda i: (i, 0)),
            _full((1, d)),
            pl.BlockSpec((1, rows), lambda i: (0, i)),
            _full(w_qkv.shape),
            _full((1, qkv_dim)),
            _full((ROPE_DIM // 2, 1)),
            _full((1, LANES)),
            _full((1, LANES)),
            _full((1, LANES)),
        ],
        out_specs=[
            pl.BlockSpec((rows, q_dim), lambda i: (i, 0)),
            pl.BlockSpec((rows, 2 * LANES), lambda i: (i, 0)),
        ],
        out_shape=[
            jax.ShapeDtypeStruct((n, q_dim), BF16),
            jax.ShapeDtypeStruct((n, 2 * LANES), BF16),
        ],
        compiler_params=_params(("arbitrary",)),
        name="qkv_rope",
    )(h, g.reshape(1, d), positions.reshape(1, n), w_qkv.astype(BF16),
      b_qkv.reshape(1, qkv_dim), inv_freq.reshape(-1, 1), *_rope_lane_masks())


def _attn_kernel(sink_ref, q_ref, kvc_ref, kvp_ref, h_ref, wo_ref, bo_ref, out_ref,
                 kbuf, vbuf, o_buf, *, wpb):
    rows = q_ref.shape[0]
    tiles = q_ref.shape[1] // LANES // 2
    j = pl.program_id(1)
    kbuf[0:WINDOW, :] = kvp_ref[:, 0:LANES]
    kbuf[WINDOW:WINDOW + rows, :] = kvc_ref[:, 0:LANES]
    vbuf[0:WINDOW, :] = kvp_ref[:, LANES:2 * LANES]
    vbuf[WINDOW:WINDOW + rows, :] = kvc_ref[:, LANES:2 * LANES]

    qi = lax.broadcasted_iota(jnp.int32, (WINDOW, 2 * WINDOW), 0)
    sj = lax.broadcasted_iota(jnp.int32, (WINDOW, 2 * WINDOW), 1)
    dist = sj - qi
    local = (dist >= 1) & (dist <= WINDOW)
    first_half = lax.broadcasted_iota(jnp.int32, (2 * WINDOW, LANES), 1) < HEAD_DIM
    out_first_half = lax.broadcasted_iota(jnp.int32, (WINDOW, LANES), 1) < HEAD_DIM
    nt = (((1,), (1,)), ((), ()))

    def window(n, carry):
        r0 = pl.multiple_of(n * WINDOW, WINDOW)
        kt = kbuf[pl.ds(r0, 2 * WINDOW), :].astype(F32)
        vt = vbuf[pl.ds(r0, 2 * WINDOW), :].astype(F32)
        mask = local & ((sj >= WINDOW) | (j * wpb + n > 0))
        k0_lo = jnp.where(first_half, kt, 0.0)
        k1_hi = jnp.where(first_half, 0.0, kt)
        v0_lo = jnp.where(first_half, vt, 1.0)
        v1_hi = jnp.where(first_half, 1.0, vt)
        k_even = (k0_lo.astype(BF16), pltpu.roll(k1_hi, HEAD_DIM, 1).astype(BF16))
        k_odd = (pltpu.roll(k0_lo, HEAD_DIM, 1).astype(BF16), k1_hi.astype(BF16))
        v_even = (v0_lo.astype(BF16), pltpu.roll(v1_hi, HEAD_DIM, 1).astype(BF16))
        v_odd = (pltpu.roll(v0_lo, HEAD_DIM, 1).astype(BF16), v1_hi.astype(BF16))
        for kh in range(2):
            q_stack = jnp.concatenate(
                [q_ref[pl.ds(r0, WINDOW), (kh * tiles + t) * LANES:(kh * tiles + t + 1) * LANES]
                 for t in range(tiles)], axis=0)
            outs = []
            for parity, k_rhs, v_rhs in ((0, k_even[kh], v_even[kh]), (1, k_odd[kh], v_odd[kh])):
                s_all = lax.dot_general(q_stack, k_rhs, nt, preferred_element_type=F32)
                p_tiles, corr = [], []
                for t in range(tiles):
                    sink = sink_ref[(kh * tiles + t) * 2 + parity]
                    s = jnp.where(mask, s_all[t * WINDOW:(t + 1) * WINDOW], -jnp.inf)
                    m = jnp.maximum(jnp.max(s, axis=-1, keepdims=True), sink)
                    p_tiles.append(jnp.exp2(s - m).astype(BF16))
                    corr.append(jnp.exp2(sink - m))
                pv = jnp.dot(jnp.concatenate(p_tiles, axis=0), v_rhs, preferred_element_type=F32)
                outs.append((pv, corr))
            for t in range(tiles):
                pv_e = outs[0][0][t * WINDOW:(t + 1) * WINDOW]
                pv_o = outs[1][0][t * WINDOW:(t + 1) * WINDOW]
                num = jnp.where(out_first_half, pv_e, pv_o)
                sums = pltpu.roll(jnp.where(out_first_half, pv_o, pv_e), HEAD_DIM, 1)
                den = sums + jnp.where(out_first_half, outs[0][1][t], outs[1][1][t])
                o_buf[pl.ds(r0, WINDOW), (kh * tiles + t) * LANES:(kh * tiles + t + 1) * LANES] = (
                    (num / den).astype(BF16))
        return carry

    lax.fori_loop(0, rows // WINDOW, window, 0, unroll=2)
    out_ref[...] = (h_ref[...] + jnp.dot(o_buf[...], wo_ref[...], preferred_element_type=F32)
                    + bo_ref[...])


def _attention(h, q, kv, sinks, w_o, b_o, batch, seq):
    n, q_dim = q.shape
    d = h.shape[1]
    assert kv.shape[1] == 2 * LANES and (q_dim // HEAD_DIM) % 4 == 0
    rows = _row_block(seq, ATTN_ROWS)
    bps = seq // rows
    wpb = rows // WINDOW
    wps = seq // WINDOW

    grid_spec = pltpu.PrefetchScalarGridSpec(
        num_scalar_prefetch=1,
        grid=(batch, bps),
        in_specs=[
            pl.BlockSpec((rows, q_dim), lambda b, j, s: (b * bps + j, 0)),
            pl.BlockSpec((rows, 2 * LANES), lambda b, j, s: (b * bps + j, 0)),
            pl.BlockSpec((WINDOW, 2 * LANES),
                         lambda b, j, s: (b * wps + jnp.maximum(j * wpb - 1, 0), 0)),
            pl.BlockSpec((rows, d), lambda b, j, s: (b * bps + j, 0)),
            pl.BlockSpec((q_dim, d), lambda b, j, s: (0, 0)),
            pl.BlockSpec((1, d), lambda b, j, s: (0, 0)),
        ],
        out_specs=pl.BlockSpec((rows, d), lambda b, j, s: (b * bps + j, 0)),
        scratch_shapes=[pltpu.VMEM((rows + WINDOW, LANES), BF16),
                        pltpu.VMEM((rows + WINDOW, LANES), BF16),
                        pltpu.VMEM((rows, q_dim), BF16)],
    )
    kern = functools.partial(_attn_kernel, wpb=wpb)
    return pl.pallas_call(
        kern,
        grid_spec=grid_spec,
        out_shape=jax.ShapeDtypeStruct((n, d), F32),
        compiler_params=_params(("arbitrary", "arbitrary")),
        name="swa_attention",
    )(sinks.astype(F32) * LOG2_E, q, kv, kv, h, w_o.astype(BF16), b_o.reshape(1, d))


R_IDX0, R_IDX1, R_GATE0, R_GATE1, R_RANK0, R_RANK1 = range(6)


def _router_kernel(h_ref, g_ref, rw_ref, xpk_ref, route_ref, route_t_ref, cnt_ref, tri_ref,
                   carry_ref):
    rows = h_ref.shape[0]
    i = pl.program_id(0)

    @pl.when(i == 0)
    def _():
        r = lax.broadcasted_iota(jnp.int32, (rows, rows), 0)
        c = lax.broadcasted_iota(jnp.int32, (rows, rows), 1)
        tri_ref[...] = jnp.where(r > c, 1.0, 0.0).astype(BF16)
        carry_ref[...] = jnp.zeros_like(carry_ref)

    xn = _rms(h_ref[...], g_ref[...])
    xb = xn.astype(BF16)
    for p, words in enumerate(_pack_bf16_pairs(xn, xpk_ref.shape[2])):
        xpk_ref[p] = words

    logits = jnp.dot(xb, rw_ref[...], preferred_element_type=F32)
    lane = lax.broadcasted_iota(jnp.int32, logits.shape, 1)
    lg = jnp.where(lane < N_EXPERTS, logits, -jnp.inf)
    m1 = jnp.max(lg, axis=-1, keepdims=True)
    i1 = jnp.min(jnp.where(lg == m1, lane, LANES), axis=-1, keepdims=True)
    lg2 = jnp.where(lane == i1, -jnp.inf, lg)
    m2 = jnp.max(lg2, axis=-1, keepdims=True)
    i2 = jnp.min(jnp.where(lg2 == m2, lane, LANES), axis=-1, keepdims=True)
    e = jnp.exp(m2 - m1)
    g1 = 1.0 / (1.0 + e)
    g2 = e / (1.0 + e)

    sel = (lane == i1) | (lane == i2)
    sel_f = jnp.where(sel, 1.0, 0.0)
    carry = carry_ref[0:1, :]
    before = jnp.dot(tri_ref[...], sel_f.astype(BF16), preferred_element_type=F32) + carry
    r1 = jnp.sum(jnp.where(lane == i1, before, 0.0), axis=-1, keepdims=True)
    r2 = jnp.sum(jnp.where(lane == i2, before, 0.0), axis=-1, keepdims=True)
    carry = carry + jnp.sum(sel_f, axis=0, keepdims=True)
    carry_ref[...] = jnp.broadcast_to(carry, carry_ref.shape)
    cnt_ref[...] = jnp.broadcast_to(carry, cnt_ref.shape)

    route = jnp.zeros(logits.shape, F32)
    for k, val in ((R_IDX0, i1.astype(F32)), (R_IDX1, i2.astype(F32)), (R_GATE0, g1),
                   (R_GATE1, g2), (R_RANK0, r1), (R_RANK1, r2)):
        route = jnp.where(lane == k, val, route)
    route_ref[...] = route
    route_t_ref[...] = route.T[0:route_t_ref.shape[0], :]


def _router(h, g, router_w):
    n, d = h.shape
    rows = _row_block(n, ROUTER_ROWS)
    parts = d // 2 // SC_PIECE
    rw = jnp.zeros((d, LANES), BF16).at[:, 0:N_EXPERTS].set(router_w.astype(BF16))
    return pl.pallas_call(
        _router_kernel,
        grid=(n // rows,),
        in_specs=[
            pl.BlockSpec((rows, d), lambda i: (i, 0)),
            _full((1, d)),
            _full((d, LANES)),
        ],
        out_specs=[
            pl.BlockSpec((parts, rows, SC_PIECE), lambda i: (0, i, 0)),
            pl.BlockSpec((rows, LANES), lambda i: (i, 0)),
            pl.BlockSpec((8, rows), lambda i: (0, i)),
            _full((8, LANES)),
        ],
        out_shape=[
            jax.ShapeDtypeStruct((parts, n, SC_PIECE), jnp.uint32),
            jax.ShapeDtypeStruct((n, LANES), F32),
            jax.ShapeDtypeStruct((8, n), F32),
            jax.ShapeDtypeStruct((8, LANES), F32),
        ],
        scratch_shapes=[pltpu.VMEM((rows, rows), BF16), pltpu.VMEM((8, LANES), F32)],
        compiler_params=_params(("arbitrary",)),
        name="moe_router",
    )(h, g.reshape(1, d), rw)


def _sc_mesh():
    return plsc.VectorSubcoreMesh(core_axis_name="core", subcore_axis_name="subcore")


def _gather_pieces(src, idx):
    m = idx.shape[0]
    width = src.shape[1]
    assert m % (SC_WINDOW * SC_WORKERS) == 0

    @functools.partial(pl.kernel, out_type=jax.ShapeDtypeStruct((m, width), src.dtype),
                       mesh=_sc_mesh(), scratch_types=[])
    def gather_kernel(src_hbm, idx_hbm, out_hbm):
        def body(idx_vmem, out_vmem):
            pltpu.sync_copy(src_hbm.at[idx_vmem.at[0]], out_vmem)

        pltpu.emit_pipeline(
            body,
            grid=(m // SC_WINDOW,),
            in_specs=[pl.BlockSpec((1, SC_WINDOW), lambda i: (0, i))],
            out_specs=[pl.BlockSpec((SC_WINDOW, width), lambda i: (i, 0))],
            core_axis_name=("core", "subcore"),
            dimension_semantics=(pltpu.PARALLEL,),
        )(idx_hbm, out_hbm)

    return gather_kernel(src, idx.reshape(1, m))


def _scatter_pieces(src, idx, out_rows):
    m = idx.shape[0]
    width = src.shape[1]
    src_windows = src.shape[0] // SC_WINDOW
    assert m % (SC_WINDOW * SC_WORKERS) == 0 and src.shape[0] % SC_WINDOW == 0

    @functools.partial(pl.kernel, out_type=jax.ShapeDtypeStruct((out_rows, width), src.dtype),
                       mesh=_sc_mesh(), scratch_types=[])
    def scatter_kernel(src_hbm, idx_hbm, out_hbm):
        def body(src_vmem, idx_vmem):
            pltpu.sync_copy(src_vmem, out_hbm.at[idx_vmem.at[0]])

        pltpu.emit_pipeline(
            body,
            grid=(m // SC_WINDOW,),
            in_specs=[pl.BlockSpec((SC_WINDOW, width), lambda i: (i % src_windows, 0)),
                      pl.BlockSpec((1, SC_WINDOW), lambda i: (0, i))],
            out_specs=[],
            core_axis_name=("core", "subcore"),
            dimension_semantics=(pltpu.PARALLEL,),
        )(src_hbm, idx_hbm)

    return scatter_kernel(src, idx.reshape(1, m))


def _moe_combined(h_ref, y_ref, route_ref):
    parts, _, _, piece = y_ref.shape
    route = route_ref[...]
    g0 = route[:, R_GATE0:R_GATE0 + 1]
    g1 = route[:, R_GATE1:R_GATE1 + 1]
    lo, hi = [], []
    for p in range(parts):
        lo0, hi0 = _unpack_bf16_pair(y_ref[p, 0])
        lo1, hi1 = _unpack_bf16_pair(y_ref[p, 1])
        lo.append(g0 * lo0 + g1 * lo1)
        hi.append(g0 * hi0 + g1 * hi1)
    return h_ref[...] + jnp.concatenate(lo + hi, axis=1)


def _moe_specs(pending, rows, base):
    y_pairs, route = pending
    parts, _, _, piece = y_pairs.shape
    return [pl.BlockSpec((parts, 2, rows, piece), lambda i: (0, 0, i, 0)),
            pl.BlockSpec((rows, LANES), lambda i: (i + base, 0))]


def _final_kernel(h_ref, y_ref, route_ref, g_ref, o_ref):
    o_ref[...] = _rms(_moe_combined(h_ref, y_ref, route_ref), g_ref[...])


def _final(h, pending, final_g):
    n, d = h.shape
    y_chunks, route = pending
    n_chunks = len(y_chunks)
    rows = _row_block(n // n_chunks, PROJ_ROWS)
    steps = n // rows // n_chunks
    for c, y in enumerate(y_chunks):
        base = c * steps
        h = pl.pallas_call(
            _final_kernel,
            grid=(steps,),
            in_specs=[pl.BlockSpec((rows, d), lambda i, base=base: (i + base, 0))]
            + _moe_specs((y, route), rows, base) + [_full((1, d))],
            out_specs=pl.BlockSpec((rows, d), lambda i, base=base: (i + base, 0)),
            out_shape=jax.ShapeDtypeStruct((n, d), F32),
            input_output_aliases={0: 0} if n_chunks > 1 else {},
            compiler_params=_params(("arbitrary",)),
            name="moe_combine_final_norm",
        )(h, y, route, final_g.reshape(1, d))
    return h


def _moe(h, g, router_w, layer, wg, wu, wd, n_chunks):
    n, d = h.shape
    xpk, route, route_t, cnt = _router(h, g, router_w)
    parts = xpk.shape[0]

    top_idx = route_t[R_IDX0:R_IDX1 + 1].astype(jnp.int32)
    rank = route_t[R_RANK0:R_RANK1 + 1].astype(jnp.int32)
    sizes = cnt[0, 0:N_EXPERTS].astype(jnp.int32)
    padded = ((sizes + MOE_ROWS - 1) // MOE_ROWS) * MOE_ROWS
    pends = jnp.cumsum(padded)
    pstarts = pends - padded
    dest = rank
    for e in range(N_EXPERTS):
        dest = dest + jnp.where(top_idx == e, pstarts[e], 0)
    n_rows = 2 * n + N_EXPERTS * MOE_ROWS
    n_blocks = n_rows // MOE_ROWS
    block_start = jnp.arange(n_blocks, dtype=jnp.int32) * MOE_ROWS
    block_expert = jnp.minimum(
        jnp.sum((block_start[:, None] >= pends[None, :]).astype(jnp.int32), axis=1),
        N_EXPERTS - 1)
    n_used = (pends[N_EXPERTS - 1:] // MOE_ROWS).astype(jnp.int32)

    off = jnp.arange(parts, dtype=jnp.int32) * n_rows
    scatter_idx = (dest[:, None, :] + off[None, :, None]).reshape(-1)
    x_sorted = _scatter_pieces(xpk.reshape(parts * n, SC_PIECE), scatter_idx, parts * n_rows)
    y_rows = _moe_ffn(x_sorted.reshape(parts, n_rows, SC_PIECE), block_expert, n_used, layer,
                      wg, wu, wd)

    y_flat = y_rows.reshape(parts * n_rows, SC_PIECE)
    nc = n // n_chunks
    chunk_dest = dest.reshape(2, n_chunks, nc).transpose(1, 0, 2)
    gather_idx = (chunk_dest[:, None] + off[None, :, None, None]).reshape(n_chunks, -1)
    y_chunks = [_gather_pieces(y_flat, gather_idx[c]).reshape(parts, 2, nc, SC_PIECE)
                for c in range(n_chunks)]
    return y_chunks, route


def kernel(x, positions, final_norm_g, ev_norm1_g, ev_w_in, ev_conv_w, ev_ln_g, ev_ln_b, ev_spatial_w, ev_spatial_b, ev_w_out, ev_norm2_g, ev_ffn_wg, ev_ffn_wu, ev_ffn_wd, od_norm1_g, od_w_qkv, od_b_qkv, od_sinks, od_w_o, od_b_o, od_norm2_g, od_router_w, od_exp_wg, od_exp_wu, od_exp_wd):
    batch, seq, d = x.shape
    depth = ev_norm1_g.shape[0] + od_norm1_g.shape[0]
    assert depth % 2 == 0, "the final norm is fused into the last (odd) layer's MoE combine"
    n_q_heads = od_sinks.shape[1]
    h = x.reshape(batch * seq, d)
    experts = [od_exp_wg, od_exp_wu, od_exp_wd]
    ffn_steps = (batch * seq) // _row_block(batch * seq, FFN_ROWS)
    ride_along = all(_side_cast_ok(w, ffn_steps) for w in experts)
    if not ride_along:
        experts = [_to_bf16(w) for w in experts]
    pending = None
    pieces_per_token = 2 * (d // 2 // SC_PIECE)
    n_chunks = max(c for c in range(1, COMBINE_CHUNKS + 1)
                   if batch % c == 0
                   and (batch // c * seq * pieces_per_token) % (SC_WINDOW * SC_WORKERS) == 0)
    for layer in range(depth):
        i = layer // 2
        if layer % 2 == 0:
            h = _mixer(h, pending, seq, ev_norm1_g[i], ev_w_in[i], ev_conv_w[i], ev_ln_g[i],
                       ev_ln_b[i], ev_spatial_w[i], ev_spatial_b[i], ev_w_out[i])
            side = experts if (ride_along and layer == 0) else ()
            h, cast = _dense_ffn(h, ev_norm2_g[i], ev_ffn_wg[i], ev_ffn_wu[i], ev_ffn_wd[i], side)
            if side:
                experts = cast
        else:
            q, kv = _qkv(h, od_norm1_g[i], positions, od_w_qkv[i], od_b_qkv[i],
                         n_q_heads * HEAD_DIM)
            h = _attention(h, q, kv, od_sinks[i], od_w_o[i], od_b_o[i], batch, seq)
            pending = _moe(h, od_norm2_g[i], od_router_w[i], i, *experts, n_chunks)
    return _final(h, pending, final_norm_g).reshape(batch, seq, d)
```

```python
import functools

import jax
import jax.numpy as jnp
import numpy as np
from jax import lax
from jax.experimental import pallas as pl
from jax.experimental.pallas import tpu as pltpu
from jax.experimental.pallas import tpu_sc as plsc

F32 = jnp.float32
BF16 = jnp.bfloat16

EPS = 1e-5
CHUNK = 128
GMLP_HEADS = 4
CONV_WIDTH = 3
HEAD_DIM = 64
WINDOW = 128
ROPE_DIM = HEAD_DIM // 4
ROPE_THETA = 500000.0
ATTN_SCALE = HEAD_DIM ** -0.5
LOG2_E = float(np.log2(np.e))
Q_SCALE = ATTN_SCALE * LOG2_E
N_EXPERTS = 8
LANES = 128
VMEM_LIMIT = 56 * 1024 * 1024

MIXER_ROWS = 512
FFN_ROWS = 512
CAST_BLOCK_BYTES = 8 * 1024 * 1024
PROJ_ROWS = 1024
ATTN_ROWS = 1024
ATTN_STACK = 2
ROUTER_ROWS = 1024
MOE_ROWS = 512
COMBINE_CHUNKS = 4
MOE_COL_SPLIT = 2
SC_WORKERS = 32
SC_PIECE = 256
SC_WINDOW = 128


def _row_block(n, pref):
    b = min(n, pref)
    while n % b:
        b -= LANES
    return b


def _col_block(f, pref):
    b = min(f, pref)
    b -= b % LANES
    while f % b:
        b -= LANES
    return b


def _params(sem):
    return pltpu.CompilerParams(dimension_semantics=sem, vmem_limit_bytes=VMEM_LIMIT)


def _rms(x, g):
    return x * lax.rsqrt(jnp.mean(x * x, axis=-1, keepdims=True) + EPS) * g


def _gelu(x):
    return 0.5 * x * (1.0 + lax.erf(x * np.float32(np.sqrt(0.5))))


def _full(shape):
    return pl.BlockSpec(shape, lambda *_: (0,) * len(shape))


def _mixer_kernel(*refs, blocks_per_seq, has_pending):
    if has_pending:
        h_ref, y_ref, route_ref = refs[:3]
        refs = refs[3:]
    else:
        h_ref = refs[0]
        refs = refs[1:]
    (g1_ref, win_ref, cw_ref, lng_ref, lnb_ref, ws_ref, bst_ref, wout_ref, o_ref,
     tail_ref, yb_ref) = refs
    rows = h_ref.shape[0]
    cd = cw_ref.shape[1]
    gd = lng_ref.shape[1]
    hd = gd // GMLP_HEADS
    i = pl.program_id(0)

    x = _moe_combined(h_ref, y_ref, route_ref) if has_pending else h_ref[...]
    xn = _rms(x, g1_ref[...]).astype(BF16)
    z = jnp.dot(xn, win_ref[...], preferred_element_type=F32)
    a_b = z[:, 0:cd]
    a_c = z[:, cd:2 * cd]
    a_x = z[:, 2 * cd:3 * cd]
    b_u = z[:, 3 * cd:3 * cd + gd]
    b_v = z[:, 3 * cd + gd:3 * cd + 2 * gd]

    g = a_c * a_x

    @pl.when(i % blocks_per_seq == 0)
    def _():
        tail_ref[...] = jnp.zeros_like(tail_ref)

    tail = tail_ref[...]
    row = lax.broadcasted_iota(jnp.int32, g.shape, 0)
    gm1 = jnp.where(row == 0, tail[7:8], pltpu.roll(g, 1, 0))
    gm2 = jnp.where(row == 0, tail[6:7], jnp.where(row == 1, tail[7:8], pltpu.roll(g, 2, 0)))
    tail_ref[...] = g[rows - 8:rows]
    cw = cw_ref[...]
    y_a = a_b * (gm2 * cw[0:1] + gm1 * cw[1:2] + g * cw[2:3])

    u = _gelu(b_u)
    v = _gelu(b_v)
    mu = jnp.mean(v, axis=-1, keepdims=True)
    vc = v - mu
    var = jnp.mean(vc * vc, axis=-1, keepdims=True)
    vn = (vc * lax.rsqrt(var + EPS) * lng_ref[...] + lnb_ref[...]).astype(BF16)
    ri = lax.broadcasted_iota(jnp.int32, (CHUNK, CHUNK), 0)
    ci = lax.broadcasted_iota(jnp.int32, (CHUNK, CHUNK), 1)
    causal = ri >= ci
    bst = bst_ref[...]
    for k in range(GMLP_HEADS):
        w_k = jnp.where(causal, ws_ref[k], 0.0).astype(BF16)
        b_k = bst[:, k:k + 1]
        for c in range(rows // CHUNK):
            rs = slice(c * CHUNK, (c + 1) * CHUNK)
            cs = slice(k * hd, (k + 1) * hd)
            mixed = jnp.dot(w_k, vn[rs, cs], preferred_element_type=F32) + b_k
            yb_ref[rs, cs] = (u[rs, cs] * mixed).astype(BF16)

    out = jnp.dot(y_a.astype(BF16), wout_ref[0:cd, :], preferred_element_type=F32)
    out = out + jnp.dot(yb_ref[...], wout_ref[cd:cd + gd, :], preferred_element_type=F32)
    o_ref[...] = x + out


def _mixer(h, pending, seq, *weights):
    if pending is None:
        return _mixer_call(h, None, 0, 1, seq, *weights)
    y_chunks, route = pending
    for c, y in enumerate(y_chunks):
        h = _mixer_call(h, (y, route), c, len(y_chunks), seq, *weights)
    return h


def _mixer_call(h, pending, chunk, n_chunks, seq, g1, w_in, conv_w, ln_g, ln_b, w_s, b_s, w_out):
    n, d = h.shape
    rows = _row_block(seq, MIXER_ROWS)
    cd = conv_w.shape[0]
    gd = ln_g.shape[0]
    has_pending = pending is not None
    steps = n // rows // n_chunks
    base = chunk * steps
    assert (steps * rows) % seq == 0
    kern = functools.partial(_mixer_kernel, blocks_per_seq=seq // rows, has_pending=has_pending)
    return pl.pallas_call(
        kern,
        grid=(steps,),
        in_specs=[pl.BlockSpec((rows, d), lambda i: (i + base, 0))]
        + (_moe_specs(pending, rows, base) if has_pending else [])
        + [
            _full((1, d)),
            _full(w_in.shape),
            _full((CONV_WIDTH, cd)),
            _full((1, gd)),
            _full((1, gd)),
            _full(w_s.shape),
            _full((CHUNK, GMLP_HEADS)),
            _full(w_out.shape),
        ],
        out_specs=pl.BlockSpec((rows, d), lambda i: (i + base, 0)),
        out_shape=jax.ShapeDtypeStruct((n, d), F32),
        scratch_shapes=[pltpu.VMEM((8, cd), F32), pltpu.VMEM((rows, gd), BF16)],
        input_output_aliases={0: 0} if n_chunks > 1 else {},
        compiler_params=_params(("arbitrary",)),
        name="mixer",
    )(h, *(pending or ()), g1.reshape(1, d), w_in.astype(BF16), conv_w.T, ln_g.reshape(1, gd),
      ln_b.reshape(1, gd), w_s, b_s.T, w_out.astype(BF16))


def _swiglu(xn, wg, wu, wd):
    h1 = jnp.dot(xn, wg, preferred_element_type=F32)
    h2 = jnp.dot(xn, wu, preferred_element_type=F32)
    a = (h1 / (1.0 + jnp.exp(-h1)) * h2).astype(BF16)
    return jnp.dot(a, wd, preferred_element_type=F32)


def _dense_ffn_kernel(x_ref, g_ref, wg_ref, wu_ref, wd_ref, *rest):
    n_side = (len(rest) - 1) // 2
    side_in, o_ref, side_out = rest[:n_side], rest[n_side], rest[n_side + 1:]
    x = x_ref[...]
    xn = _rms(x, g_ref[...]).astype(BF16)
    o_ref[...] = x + _swiglu(xn, wg_ref[...], wu_ref[...], wd_ref[...])
    for src, dst in zip(side_in, side_out):
        dst[...] = src[...].astype(dst.dtype)


def _side_cast_ok(w, steps):
    rows = int(np.prod(w.shape[:-1]))
    return rows % steps == 0 and (rows // steps) % 16 == 0 and w.shape[-1] % LANES == 0


def _dense_ffn(h, g, wg, wu, wd, side=()):
    n, d = h.shape
    rows = _row_block(n, FFN_ROWS)
    steps = n // rows
    side2d = [w.reshape(-1, w.shape[-1]) for w in side]
    side_specs = [pl.BlockSpec((w.shape[0] // steps, w.shape[1]), lambda i: (i, 0)) for w in side2d]
    outs = pl.pallas_call(
        _dense_ffn_kernel,
        grid=(steps,),
        in_specs=[
            pl.BlockSpec((rows, d), lambda i: (i, 0)),
            _full((1, d)),
            _full(wg.shape),
            _full(wu.shape),
            _full(wd.shape),
        ] + side_specs,
        out_specs=[pl.BlockSpec((rows, d), lambda i: (i, 0))] + side_specs,
        out_shape=[jax.ShapeDtypeStruct((n, d), F32)]
        + [jax.ShapeDtypeStruct(w.shape, BF16) for w in side2d],
        compiler_params=_params(("arbitrary",)),
        name="dense_ffn",
    )(h, g.reshape(1, d), wg.astype(BF16), wu.astype(BF16), wd.astype(BF16), *side2d)
    return outs[0], [o.reshape(w.shape) for o, w in zip(outs[1:], side)]


def _pack_bf16_pairs(x, piece):
    half = x.shape[1] // 2
    bits = lax.bitcast_convert_type(x.astype(BF16).astype(F32), jnp.uint32)
    return [(bits[:, half + p * piece:half + (p + 1) * piece] & jnp.uint32(0xFFFF0000))
            | (bits[:, p * piece:(p + 1) * piece] >> 16) for p in range(half // piece)]


def _unpack_bf16_pair(packed):
    lo = lax.bitcast_convert_type(packed << 16, F32)
    hi = lax.bitcast_convert_type(packed & jnp.uint32(0xFFFF0000), F32)
    return lo, hi


def _moe_ffn_kernel(be_ref, used_ref, x_ref, wg_ref, wu_ref, wd_ref, o_ref, xn_ref, acc_ref, *,
                    n_steps):
    del be_ref
    i = pl.program_id(0)
    f = pl.program_id(1)
    parts, _, piece = x_ref.shape
    half = parts * piece
    active = i < used_ref[0]

    def write_out(val):
        for p, words in enumerate(_pack_bf16_pairs(val, piece)):
            o_ref[p] = words

    def step(first, last):
        if first:
            for p in range(parts):
                lo, hi = _unpack_bf16_pair(x_ref[p])
                xn_ref[:, p * piece:(p + 1) * piece] = lo.astype(BF16)
                xn_ref[:, half + p * piece:half + (p + 1) * piece] = hi.astype(BF16)
        part = _swiglu(xn_ref[...], wg_ref[0, 0], wu_ref[0, 0], wd_ref[0, 0])
        if last:
            write_out(part if first else acc_ref[...] + part)
        elif first:
            acc_ref[...] = part
        else:
            acc_ref[...] += part

    if n_steps == 1:
        pl.when(active)(functools.partial(step, True, True))
    else:
        pl.when(active & (f == 0))(functools.partial(step, True, False))
        if n_steps > 2:
            pl.when(active & (f > 0) & (f < n_steps - 1))(functools.partial(step, False, False))
        pl.when(active & (f == n_steps - 1))(functools.partial(step, False, True))

    @pl.when(jnp.logical_not(active) & (f == n_steps - 1))
    def _():
        o_ref[...] = jnp.zeros_like(o_ref)


def _moe_ffn(x_sorted, block_expert, n_used, layer, wg, wu, wd):
    parts, n_rows, piece = x_sorted.shape
    d = 2 * parts * piece
    fdim = wg.shape[3]
    rows = MOE_ROWS
    cols = _col_block(fdim, fdim // MOE_COL_SPLIT)
    n_steps = fdim // cols

    def col(i, f, used):
        return jnp.where(i < used[0], f, n_steps - 1)

    grid_spec = pltpu.PrefetchScalarGridSpec(
        num_scalar_prefetch=2,
        grid=(n_rows // rows, n_steps),
        in_specs=[
            pl.BlockSpec((parts, rows, piece),
                         lambda i, f, be, used: (0, jnp.minimum(i, used[0] - 1), 0)),
            pl.BlockSpec((1, 1, d, cols), lambda i, f, be, used: (layer, be[i], 0, col(i, f, used))),
            pl.BlockSpec((1, 1, d, cols), lambda i, f, be, used: (layer, be[i], 0, col(i, f, used))),
            pl.BlockSpec((1, 1, cols, d), lambda i, f, be, used: (layer, be[i], col(i, f, used), 0)),
        ],
        out_specs=pl.BlockSpec((parts, rows, piece), lambda i, f, be, used: (0, i, 0)),
        scratch_shapes=[pltpu.VMEM((rows, d), BF16), pltpu.VMEM((rows, d), F32)],
    )
    kern = functools.partial(_moe_ffn_kernel, n_steps=n_steps)
    return pl.pallas_call(
        kern,
        grid_spec=grid_spec,
        out_shape=jax.ShapeDtypeStruct((parts, n_rows, piece), jnp.uint32),
        compiler_params=_params(("arbitrary", "arbitrary")),
        name="moe_ffn",
    )(block_expert, n_used, x_sorted, wg, wu, wd)


def _cast_kernel(x_ref, o_ref):
    o_ref[...] = x_ref[...].astype(o_ref.dtype)


def _to_bf16(w):
    shape = w.shape
    w2 = w.reshape(-1, shape[-1])
    pref = CAST_BLOCK_BYTES // (4 * shape[-1]) // LANES * LANES
    rows = _row_block(w2.shape[0], pref)
    out = pl.pallas_call(
        _cast_kernel,
        grid=(w2.shape[0] // rows,),
        in_specs=[pl.BlockSpec((rows, shape[-1]), lambda i: (i, 0))],
        out_specs=pl.BlockSpec((rows, shape[-1]), lambda i: (i, 0)),
        out_shape=jax.ShapeDtypeStruct(w2.shape, BF16),
        compiler_params=_params(("arbitrary",)),
        name="cast_bf16",
    )(w2)
    return out.reshape(shape)


def _qkv_kernel(h_ref, g_ref, pos_ref, w_ref, b_ref, invf_ref, mc_ref, m1_ref, m2_ref,
                q_ref, kv_ref, *, q_dim):
    x = h_ref[...]
    xn = _rms(x, g_ref[...]).astype(BF16)
    z = jnp.dot(xn, w_ref[...], preferred_element_type=F32) + b_ref[...]
    ang = pos_ref[...].astype(F32) * invf_ref[...]
    reps = LANES // ang.shape[0]
    cos = jnp.concatenate([jnp.cos(ang)] * reps, axis=0).T
    sin = jnp.concatenate([jnp.sin(ang)] * reps, axis=0).T
    cos = jnp.where(mc_ref[...] > 0.0, cos, 1.0)
    s_lo = sin * m1_ref[...]
    s_hi = sin * m2_ref[...]
    half = ROPE_DIM // 2

    def rope(t):
        return t * cos + pltpu.roll(t, LANES - half, 1) * s_lo + pltpu.roll(t, half, 1) * s_hi

    for j in range(q_dim // LANES):
        cs = slice(j * LANES, (j + 1) * LANES)
        q_ref[:, cs] = (rope(z[:, cs]) * Q_SCALE).astype(BF16)
    kv_ref[:, 0:LANES] = rope(z[:, q_dim:q_dim + LANES]).astype(BF16)
    kv_ref[:, LANES:2 * LANES] = z[:, q_dim + LANES:q_dim + 2 * LANES].astype(BF16)


def _rope_lane_masks():
    lane = np.arange(LANES) % HEAD_DIM
    half = ROPE_DIM // 2
    first = (lane < half).astype(np.float32)
    second = ((lane >= half) & (lane < ROPE_DIM)).astype(np.float32)
    return [jnp.asarray(m[None, :]) for m in (first + second, -first, second)]


def _qkv(h, g, positions, w_qkv, b_qkv, q_dim):
    n, d = h.shape
    qkv_dim = w_qkv.shape[1]
    assert qkv_dim == q_dim + 2 * LANES and HEAD_DIM % (ROPE_DIM // 2) == 0
    rows = _row_block(n, PROJ_ROWS)
    inv_freq = ROPE_THETA ** (-jnp.arange(0, ROPE_DIM, 2, dtype=F32) / ROPE_DIM)
    kern = functools.partial(_qkv_kernel, q_dim=q_dim)
    return pl.pallas_call(
        kern,
        grid=(n // rows,),
        in_specs=[
            pl.BlockSpec((rows, d), lambda i: (i, 0)),
            _full((1, d)),
            pl.BlockSpec((1, rows), lambda i: (0, i)),
            _full(w_qkv.shape),
            _full((1, qkv_dim)),
            _full((ROPE_DIM // 2, 1)),
            _full((1, LANES)),
            _full((1, LANES)),
            _full((1, LANES)),
        ],
        out_specs=[
            pl.BlockSpec((rows, q_dim), lambda i: (i, 0)),
            pl.BlockSpec((rows, 2 * LANES), lambda i: (i, 0)),
        ],
        out_shape=[
            jax.ShapeDtypeStruct((n, q_dim), BF16),
            jax.ShapeDtypeStruct((n, 2 * LANES), BF16),
        ],
        compiler_params=_params(("arbitrary",)),
        name="qkv_rope",
    )(h, g.reshape(1, d), positions.reshape(1, n), w_qkv.astype(BF16),
      b_qkv.reshape(1, qkv_dim), inv_freq.reshape(-1, 1), *_rope_lane_masks())


def _attn_kernel(sink_ref, q_ref, kvc_ref, kvp_ref, h_ref, wo_ref, bo_ref, out_ref,
                 kbuf, vbuf, o_buf, *, wpb):
    rows = q_ref.shape[0]
    tiles = q_ref.shape[1] // LANES // 2
    j = pl.program_id(1)
    kbuf[0:WINDOW, :] = kvp_ref[:, 0:LANES]
    kbuf[WINDOW:WINDOW + rows, :] = kvc_ref[:, 0:LANES]
    vbuf[0:WINDOW, :] = kvp_ref[:, LANES:2 * LANES]
    vbuf[WINDOW:WINDOW + rows, :] = kvc_ref[:, LANES:2 * LANES]

    from_prev = (lax.broadcasted_iota(jnp.int32, (WINDOW, WINDOW), 1)
                 > lax.broadcasted_iota(jnp.int32, (WINDOW, WINDOW), 0))
    first_half = lax.broadcasted_iota(jnp.int32, (2 * WINDOW, LANES), 1) < HEAD_DIM
    out_first_half = lax.broadcasted_iota(jnp.int32, (WINDOW, LANES), 1) < HEAD_DIM
    nt = (((1,), (1,)), ((), ()))

    def window(n, carry):
        r0 = pl.multiple_of(n * WINDOW, WINDOW)
        kt = kbuf[pl.ds(r0, 2 * WINDOW), :].astype(F32)
        vt = vbuf[pl.ds(r0, 2 * WINDOW), :].astype(F32)
        prev_bias = jnp.where(j * wpb + n > 0, 0.0, -jnp.inf)
        k0_lo = jnp.where(first_half, kt, 0.0)
        k1_hi = jnp.where(first_half, 0.0, kt)
        v0_lo = jnp.where(first_half, vt, 1.0)
        v1_hi = jnp.where(first_half, 1.0, vt)
        k_even = (k0_lo.astype(BF16), pltpu.roll(k1_hi, HEAD_DIM, 1).astype(BF16))
        k_odd = (pltpu.roll(k0_lo, HEAD_DIM, 1).astype(BF16), k1_hi.astype(BF16))
        v_even = (v0_lo.astype(BF16), pltpu.roll(v1_hi, HEAD_DIM, 1).astype(BF16))
        v_odd = (pltpu.roll(v0_lo, HEAD_DIM, 1).astype(BF16), v1_hi.astype(BF16))
        for kh in range(2):
            for t0 in range(0, tiles, ATTN_STACK):
                group = [kh * tiles + t0 + u for u in range(ATTN_STACK)]
                q_stack = jnp.concatenate(
                    [q_ref[pl.ds(r0, WINDOW), t * LANES:(t + 1) * LANES] for t in group], axis=0)
                outs = []
                for parity, k_rhs, v_rhs in ((0, k_even[kh], v_even[kh]), (1, k_odd[kh], v_odd[kh])):
                    s_all = lax.dot_general(q_stack, k_rhs, nt, preferred_element_type=F32)
                    p_tiles, corr = [], []
                    for u, t in enumerate(group):
                        sink = sink_ref[t * 2 + parity]
                        s_prev = s_all[u * WINDOW:(u + 1) * WINDOW, 0:WINDOW] + prev_bias
                        s_cur = s_all[u * WINDOW:(u + 1) * WINDOW, WINDOW:2 * WINDOW]
                        s = jnp.where(from_prev, s_prev, s_cur)
                        m = jnp.maximum(jnp.max(s, axis=-1, keepdims=True), sink)
                        p = jnp.exp2(s - m)
                        p_tiles.append(jnp.concatenate(
                            [jnp.where(from_prev, p, 0.0), jnp.where(from_prev, 0.0, p)],
                            axis=1).astype(BF16))
                        corr.append(jnp.exp2(sink - m))
                    pv = jnp.dot(jnp.concatenate(p_tiles, axis=0), v_rhs,
                                 preferred_element_type=F32)
                    outs.append((pv, corr))
                for u, t in enumerate(group):
                    pv_e = outs[0][0][u * WINDOW:(u + 1) * WINDOW]
                    pv_o = outs[1][0][u * WINDOW:(u + 1) * WINDOW]
                    num = jnp.where(out_first_half, pv_e, pv_o)
                    sums = pltpu.roll(jnp.where(out_first_half, pv_o, pv_e), HEAD_DIM, 1)
                    den = sums + jnp.where(out_first_half, outs[0][1][u], outs[1][1][u])
                    o_buf[pl.ds(r0, WINDOW), t * LANES:(t + 1) * LANES] = (num / den).astype(BF16)
        return carry

    lax.fori_loop(0, rows // WINDOW, window, 0, unroll=2)
    out_ref[...] = (h_ref[...] + jnp.dot(o_buf[...], wo_ref[...], preferred_element_type=F32)
                    + bo_ref[...])


def _attention(h, q, kv, sinks, w_o, b_o, batch, seq):
    n, q_dim = q.shape
    d = h.shape[1]
    assert kv.shape[1] == 2 * LANES and (q_dim // HEAD_DIM) % 4 == 0
    rows = _row_block(seq, ATTN_ROWS)
    bps = seq // rows
    wpb = rows // WINDOW
    wps = seq // WINDOW

    grid_spec = pltpu.PrefetchScalarGridSpec(
        num_scalar_prefetch=1,
        grid=(batch, bps),
        in_specs=[
            pl.BlockSpec((rows, q_dim), lambda b, j, s: (b * bps + j, 0)),
            pl.BlockSpec((rows, 2 * LANES), lambda b, j, s: (b * bps + j, 0)),
            pl.BlockSpec((WINDOW, 2 * LANES),
                         lambda b, j, s: (b * wps + jnp.maximum(j * wpb - 1, 0), 0)),
            pl.BlockSpec((rows, d), lambda b, j, s: (b * bps + j, 0)),
            pl.BlockSpec((q_dim, d), lambda b, j, s: (0, 0)),
            pl.BlockSpec((1, d), lambda b, j, s: (0, 0)),
        ],
        out_specs=pl.BlockSpec((rows, d), lambda b, j, s: (b * bps + j, 0)),
        scratch_shapes=[pltpu.VMEM((rows + WINDOW, LANES), BF16),
                        pltpu.VMEM((rows + WINDOW, LANES), BF16),
                        pltpu.VMEM((rows, q_dim), BF16)],
    )
    kern = functools.partial(_attn_kernel, wpb=wpb)
    return pl.pallas_call(
        kern,
        grid_spec=grid_spec,
        out_shape=jax.ShapeDtypeStruct((n, d), F32),
        compiler_params=_params(("arbitrary", "arbitrary")),
        name="swa_attention",
    )(sinks.astype(F32) * LOG2_E, q, kv, kv, h, w_o.astype(BF16), b_o.reshape(1, d))


R_IDX0, R_IDX1, R_GATE0, R_GATE1, R_RANK0, R_RANK1 = range(6)


def _router_kernel(h_ref, g_ref, rw_ref, xpk_ref, route_ref, route_t_ref, cnt_ref, tri_ref,
                   carry_ref):
    rows = h_ref.shape[0]
    i = pl.program_id(0)

    @pl.when(i == 0)
    def _():
        r = lax.broadcasted_iota(jnp.int32, (rows, rows), 0)
        c = lax.broadcasted_iota(jnp.int32, (rows, rows), 1)
        tri_ref[...] = jnp.where(r > c, 1.0, 0.0).astype(BF16)
        carry_ref[...] = jnp.zeros_like(carry_ref)

    xn = _rms(h_ref[...], g_ref[...])
    xb = xn.astype(BF16)
    for p, words in enumerate(_pack_bf16_pairs(xn, xpk_ref.shape[2])):
        xpk_ref[p] = words

    logits = jnp.dot(xb, rw_ref[...], preferred_element_type=F32)
    lane = lax.broadcasted_iota(jnp.int32, logits.shape, 1)
    lg = jnp.where(lane < N_EXPERTS, logits, -jnp.inf)
    m1 = jnp.max(lg, axis=-1, keepdims=True)
    i1 = jnp.min(jnp.where(lg == m1, lane, LANES), axis=-1, keepdims=True)
    lg2 = jnp.where(lane == i1, -jnp.inf, lg)
    m2 = jnp.max(lg2, axis=-1, keepdims=True)
    i2 = jnp.min(jnp.where(lg2 == m2, lane, LANES), axis=-1, keepdims=True)
    e = jnp.exp(m2 - m1)
    g1 = 1.0 / (1.0 + e)
    g2 = e / (1.0 + e)

    sel = (lane == i1) | (lane == i2)
    sel_f = jnp.where(sel, 1.0, 0.0)
    carry = carry_ref[0:1, :]
    before = jnp.dot(tri_ref[...], sel_f.astype(BF16), preferred_element_type=F32) + carry
    r1 = jnp.sum(jnp.where(lane == i1, before, 0.0), axis=-1, keepdims=True)
    r2 = jnp.sum(jnp.where(lane == i2, before, 0.0), axis=-1, keepdims=True)
    carry = carry + jnp.sum(sel_f, axis=0, keepdims=True)
    carry_ref[...] = jnp.broadcast_to(carry, carry_ref.shape)
    cnt_ref[...] = jnp.broadcast_to(carry, cnt_ref.shape)

    route = jnp.zeros(logits.shape, F32)
    for k, val in ((R_IDX0, i1.astype(F32)), (R_IDX1, i2.astype(F32)), (R_GATE0, g1),
                   (R_GATE1, g2), (R_RANK0, r1), (R_RANK1, r2)):
        route = jnp.where(lane == k, val, route)
    route_ref[...] = route
    route_t_ref[...] = route.T[0:route_t_ref.shape[0], :]


def _router(h, g, router_w):
    n, d = h.shape
    rows = _row_block(n, ROUTER_ROWS)
    parts = d // 2 // SC_PIECE
    rw = jnp.zeros((d, LANES), BF16).at[:, 0:N_EXPERTS].set(router_w.astype(BF16))
    return pl.pallas_call(
        _router_kernel,
        grid=(n // rows,),
        in_specs=[
            pl.BlockSpec((rows, d), lambda i: (i, 0)),
            _full((1, d)),
            _full((d, LANES)),
        ],
        out_specs=[
            pl.BlockSpec((parts, rows, SC_PIECE), lambda i: (0, i, 0)),
            pl.BlockSpec((rows, LANES), lambda i: (i, 0)),
            pl.BlockSpec((8, rows), lambda i: (0, i)),
            _full((8, LANES)),
        ],
        out_shape=[
            jax.ShapeDtypeStruct((parts, n, SC_PIECE), jnp.uint32),
            jax.ShapeDtypeStruct((n, LANES), F32),
            jax.ShapeDtypeStruct((8, n), F32),
            jax.ShapeDtypeStruct((8, LANES), F32),
        ],
        scratch_shapes=[pltpu.VMEM((rows, rows), BF16), pltpu.VMEM((8, LANES), F32)],
        compiler_params=_params(("arbitrary",)),
        name="moe_router",
    )(h, g.reshape(1, d), rw)


def _sc_mesh():
    return plsc.VectorSubcoreMesh(core_axis_name="core", subcore_axis_name="subcore")


def _gather_pieces(src, idx):
    m = idx.shape[0]
    width = src.shape[1]
    assert m % (SC_WINDOW * SC_WORKERS) == 0

    @functools.partial(pl.kernel, out_type=jax.ShapeDtypeStruct((m, width), src.dtype),
                       mesh=_sc_mesh(), scratch_types=[])
    def gather_kernel(src_hbm, idx_hbm, out_hbm):
        def body(idx_vmem, out_vmem):
            pltpu.sync_copy(src_hbm.at[idx_vmem.at[0]], out_vmem)

        pltpu.emit_pipeline(
            body,
            grid=(m // SC_WINDOW,),
            in_specs=[pl.BlockSpec((1, SC_WINDOW), lambda i: (0, i))],
            out_specs=[pl.BlockSpec((SC_WINDOW, width), lambda i: (i, 0))],
            core_axis_name=("core", "subcore"),
            dimension_semantics=(pltpu.PARALLEL,),
        )(idx_hbm, out_hbm)

    return gather_kernel(src, idx.reshape(1, m))


def _scatter_pieces(src, idx, out_rows):
    m = idx.shape[0]
    width = src.shape[1]
    src_windows = src.shape[0] // SC_WINDOW
    assert m % (SC_WINDOW * SC_WORKERS) == 0 and src.shape[0] % SC_WINDOW == 0

    @functools.partial(pl.kernel, out_type=jax.ShapeDtypeStruct((out_rows, width), src.dtype),
                       mesh=_sc_mesh(), scratch_types=[])
    def scatter_kernel(src_hbm, idx_hbm, out_hbm):
        def body(src_vmem, idx_vmem):
            pltpu.sync_copy(src_vmem, out_hbm.at[idx_vmem.at[0]])

        pltpu.emit_pipeline(
            body,
            grid=(m // SC_WINDOW,),
            in_specs=[pl.BlockSpec((SC_WINDOW, width), lambda i: (i % src_windows, 0)),
                      pl.BlockSpec((1, SC_WINDOW), lambda i: (0, i))],
            out_specs=[],
            core_axis_name=("core", "subcore"),
            dimension_semantics=(pltpu.PARALLEL,),
        )(src_hbm, idx_hbm)

    return scatter_kernel(src, idx.reshape(1, m))


def _moe_combined(h_ref, y_ref, route_ref):
    parts, _, _, piece = y_ref.shape
    route = route_ref[...]
    g0 = route[:, R_GATE0:R_GATE0 + 1]
    g1 = route[:, R_GATE1:R_GATE1 + 1]
    lo, hi = [], []
    for p in range(parts):
        lo0, hi0 = _unpack_bf16_pair(y_ref[p, 0])
        lo1, hi1 = _unpack_bf16_pair(y_ref[p, 1])
        lo.append(g0 * lo0 + g1 * lo1)
        hi.append(g0 * hi0 + g1 * hi1)
    return h_ref[...] + jnp.concatenate(lo + hi, axis=1)


def _moe_specs(pending, rows, base):
    y_pairs, route = pending
    parts, _, _, piece = y_pairs.shape
    return [pl.BlockSpec((parts, 2, rows, piece), lambda i: (0, 0, i, 0)),
            pl.BlockSpec((rows, LANES), lambda i: (i + base, 0))]


def _final_kernel(h_ref, y_ref, route_ref, g_ref, o_ref):
    o_ref[...] = _rms(_moe_combined(h_ref, y_ref, route_ref), g_ref[...])


def _final(h, pending, final_g):
    n, d = h.shape
    y_chunks, route = pending
    n_chunks = len(y_chunks)
    rows = _row_block(n // n_chunks, PROJ_ROWS)
    steps = n // rows // n_chunks
    for c, y in enumerate(y_chunks):
        base = c * steps
        h = pl.pallas_call(
            _final_kernel,
            grid=(steps,),
            in_specs=[pl.BlockSpec((rows, d), lambda i, base=base: (i + base, 0))]
            + _moe_specs((y, route), rows, base) + [_full((1, d))],
            out_specs=pl.BlockSpec((rows, d), lambda i, base=base: (i + base, 0)),
            out_shape=jax.ShapeDtypeStruct((n, d), F32),
            input_output_aliases={0: 0} if n_chunks > 1 else {},
            compiler_params=_params(("arbitrary",)),
            name="moe_combine_final_norm",
        )(h, y, route, final_g.reshape(1, d))
    return h


def _moe(h, g, router_w, layer, wg, wu, wd, n_chunks):
    n, d = h.shape
    xpk, route, route_t, cnt = _router(h, g, router_w)
    parts = xpk.shape[0]

    top_idx = route_t[R_IDX0:R_IDX1 + 1].astype(jnp.int32)
    rank = route_t[R_RANK0:R_RANK1 + 1].astype(jnp.int32)
    sizes = cnt[0, 0:N_EXPERTS].astype(jnp.int32)
    padded = ((sizes + MOE_ROWS - 1) // MOE_ROWS) * MOE_ROWS
    pends = jnp.cumsum(padded)
    pstarts = pends - padded
    dest = rank
    for e in range(N_EXPERTS):
        dest = dest + jnp.where(top_idx == e, pstarts[e], 0)
    n_rows = 2 * n + N_EXPERTS * MOE_ROWS
    n_blocks = n_rows // MOE_ROWS
    block_start = jnp.arange(n_blocks, dtype=jnp.int32) * MOE_ROWS
    block_expert = jnp.minimum(
        jnp.sum((block_start[:, None] >= pends[None, :]).astype(jnp.int32), axis=1),
        N_EXPERTS - 1)
    n_used = (pends[N_EXPERTS - 1:] // MOE_ROWS).astype(jnp.int32)

    off = jnp.arange(parts, dtype=jnp.int32) * n_rows
    scatter_idx = (dest[:, None, :] + off[None, :, None]).reshape(-1)
    x_sorted = _scatter_pieces(xpk.reshape(parts * n, SC_PIECE), scatter_idx, parts * n_rows)
    y_rows = _moe_ffn(x_sorted.reshape(parts, n_rows, SC_PIECE), block_expert, n_used, layer,
                      wg, wu, wd)

    y_flat = y_rows.reshape(parts * n_rows, SC_PIECE)
    nc = n // n_chunks
    chunk_dest = dest.reshape(2, n_chunks, nc).transpose(1, 0, 2)
    gather_idx = (chunk_dest[:, None] + off[None, :, None, None]).reshape(n_chunks, -1)
    y_chunks = [_gather_pieces(y_flat, gather_idx[c]).reshape(parts, 2, nc, SC_PIECE)
                for c in range(n_chunks)]
    return y_chunks, route


def kernel(x, positions, final_norm_g, ev_norm1_g, ev_w_in, ev_conv_w, ev_ln_g, ev_ln_b, ev_spatial_w, ev_spatial_b, ev_w_out, ev_norm2_g, ev_ffn_wg, ev_ffn_wu, ev_ffn_wd, od_norm1_g, od_w_qkv, od_b_qkv, od_sinks, od_w_o, od_b_o, od_norm2_g, od_router_w, od_exp_wg, od_exp_wu, od_exp_wd):
    batch, seq, d = x.shape
    depth = ev_norm1_g.shape[0] + od_norm1_g.shape[0]
    assert depth % 2 == 0, "the final norm is fused into the last (odd) layer's MoE combine"
    n_q_heads = od_sinks.shape[1]
    h = x.reshape(batch * seq, d)
    experts = [od_exp_wg, od_exp_wu, od_exp_wd]
    ffn_steps = (batch * seq) // _row_block(batch * seq, FFN_ROWS)
    ride_along = all(_side_cast_ok(w, ffn_steps) for w in experts)
    if not ride_along:
        experts = [_to_bf16(w) for w in experts]
    pending = None
    pieces_per_token = 2 * (d // 2 // SC_PIECE)
    n_chunks = max(c for c in range(1, COMBINE_CHUNKS + 1)
                   if batch % c == 0
                   and (batch // c * seq * pieces_per_token) % (SC_WINDOW * SC_WORKERS) == 0)
    for layer in range(depth):
        i = layer // 2
        if layer % 2 == 0:
            h = _mixer(h, pending, seq, ev_norm1_g[i], ev_w_in[i], ev_conv_w[i], ev_ln_g[i],
                       ev_ln_b[i], ev_spatial_w[i], ev_spatial_b[i], ev_w_out[i])
            side = experts if (ride_along and layer == 0) else ()
            h, cast = _dense_ffn(h, ev_norm2_g[i], ev_ffn_wg[i], ev_ffn_wu[i], ev_ffn_wd[i], side)
            if side:
                experts = cast
        else:
            q, kv = _qkv(h, od_norm1_g[i], positions, od_w_qkv[i], od_b_qkv[i],
                         n_q_heads * HEAD_DIM)
            h = _attention(h, q, kv, od_sinks[i], od_w_o[i], od_b_o[i], batch, seq)
            pending = _moe(h, od_norm2_g[i], od_router_w[i], i, *experts, n_chunks)
    return _final(h, pending, final_norm_g).reshape(batch, seq, d)
```

```python
import functools

import jax
import jax.numpy as jnp
import numpy as np
from jax import lax
from jax.experimental import pallas as pl
from jax.experimental.pallas import tpu as pltpu
from jax.experimental.pallas import tpu_sc as plsc

F32 = jnp.float32
BF16 = jnp.bfloat16

EPS = 1e-5
CHUNK = 128
GMLP_HEADS = 4
CONV_WIDTH = 3
HEAD_DIM = 64
WINDOW = 128
ROPE_DIM = HEAD_DIM // 4
ROPE_THETA = 500000.0
ATTN_SCALE = HEAD_DIM ** -0.5
LOG2_E = float(np.log2(np.e))
Q_SCALE = ATTN_SCALE * LOG2_E
N_EXPERTS = 8
LANES = 128
VMEM_LIMIT = 56 * 1024 * 1024

MIXER_ROWS = 512
FFN_ROWS = 512
CAST_BLOCK_BYTES = 8 * 1024 * 1024
PROJ_ROWS = 1024
ATTN_ROWS = 1024
ATTN_STACK = 2
ROUTER_ROWS = 1024
MOE_ROWS = 1024
COMBINE_CHUNKS = 4
MOE_COL_SPLIT = 2
SC_WORKERS = 32
SC_PIECE = 256
SC_WINDOW = 128


def _row_block(n, pref):
    b = min(n, pref)
    while n % b:
        b -= LANES
    return b


def _col_block(f, pref):
    b = min(f, pref)
    b -= b % LANES
    while f % b:
        b -= LANES
    return b


def _params(sem):
    return pltpu.CompilerParams(dimension_semantics=sem, vmem_limit_bytes=VMEM_LIMIT)


def _rms(x, g):
    return x * lax.rsqrt(jnp.mean(x * x, axis=-1, keepdims=True) + EPS) * g


def _gelu(x):
    return 0.5 * x * (1.0 + lax.erf(x * np.float32(np.sqrt(0.5))))


def _full(shape):
    return pl.BlockSpec(shape, lambda *_: (0,) * len(shape))


def _mixer_kernel(*refs, blocks_per_seq, has_pending):
    if has_pending:
        h_ref, y_ref, route_ref = refs[:3]
        refs = refs[3:]
    else:
        h_ref = refs[0]
        refs = refs[1:]
    (g1_ref, win_ref, cw_ref, lng_ref, lnb_ref, ws_ref, bst_ref, wout_ref, o_ref,
     tail_ref, yb_ref) = refs
    rows = h_ref.shape[0]
    cd = cw_ref.shape[1]
    gd = lng_ref.shape[1]
    hd = gd // GMLP_HEADS
    i = pl.program_id(0)

    x = _moe_combined(h_ref, y_ref, route_ref) if has_pending else h_ref[...]
    xn = _rms(x, g1_ref[...]).astype(BF16)
    z = jnp.dot(xn, win_ref[...], preferred_element_type=F32)
    a_b = z[:, 0:cd]
    a_c = z[:, cd:2 * cd]
    a_x = z[:, 2 * cd:3 * cd]
    b_u = z[:, 3 * cd:3 * cd + gd]
    b_v = z[:, 3 * cd + gd:3 * cd + 2 * gd]

    g = a_c * a_x

    @pl.when(i % blocks_per_seq == 0)
    def _():
        tail_ref[...] = jnp.zeros_like(tail_ref)

    tail = tail_ref[...]
    row = lax.broadcasted_iota(jnp.int32, g.shape, 0)
    gm1 = jnp.where(row == 0, tail[7:8], pltpu.roll(g, 1, 0))
    gm2 = jnp.where(row == 0, tail[6:7], jnp.where(row == 1, tail[7:8], pltpu.roll(g, 2, 0)))
    tail_ref[...] = g[rows - 8:rows]
    cw = cw_ref[...]
    y_a = a_b * (gm2 * cw[0:1] + gm1 * cw[1:2] + g * cw[2:3])

    u = _gelu(b_u)
    v = _gelu(b_v)
    mu = jnp.mean(v, axis=-1, keepdims=True)
    vc = v - mu
    var = jnp.mean(vc * vc, axis=-1, keepdims=True)
    vn = (vc * lax.rsqrt(var + EPS) * lng_ref[...] + lnb_ref[...]).astype(BF16)
    ri = lax.broadcasted_iota(jnp.int32, (CHUNK, CHUNK), 0)
    ci = lax.broadcasted_iota(jnp.int32, (CHUNK, CHUNK), 1)
    causal = ri >= ci
    bst = bst_ref[...]
    for k in range(GMLP_HEADS):
        w_k = jnp.where(causal, ws_ref[k], 0.0).astype(BF16)
        b_k = bst[:, k:k + 1]
        for c in range(rows // CHUNK):
            rs = slice(c * CHUNK, (c + 1) * CHUNK)
            cs = slice(k * hd, (k + 1) * hd)
            mixed = jnp.dot(w_k, vn[rs, cs], preferred_element_type=F32) + b_k
            yb_ref[rs, cs] = (u[rs, cs] * mixed).astype(BF16)

    out = jnp.dot(y_a.astype(BF16), wout_ref[0:cd, :], preferred_element_type=F32)
    out = out + jnp.dot(yb_ref[...], wout_ref[cd:cd + gd, :], preferred_element_type=F32)
    o_ref[...] = x + out


def _mixer(h, pending, seq, *weights):
    if pending is None:
        return _mixer_call(h, None, 0, 1, seq, *weights)
    y_chunks, route = pending
    for c, y in enumerate(y_chunks):
        h = _mixer_call(h, (y, route), c, len(y_chunks), seq, *weights)
    return h


def _mixer_call(h, pending, chunk, n_chunks, seq, g1, w_in, conv_w, ln_g, ln_b, w_s, b_s, w_out):
    n, d = h.shape
    rows = _row_block(seq, MIXER_ROWS)
    cd = conv_w.shape[0]
    gd = ln_g.shape[0]
    has_pending = pending is not None
    steps = n // rows // n_chunks
    base = chunk * steps
    assert (steps * rows) % seq == 0
    kern = functools.partial(_mixer_kernel, blocks_per_seq=seq // rows, has_pending=has_pending)
    return pl.pallas_call(
        kern,
        grid=(steps,),
        in_specs=[pl.BlockSpec((rows, d), lambda i: (i + base, 0))]
        + (_moe_specs(pending, rows, base) if has_pending else [])
        + [
            _full((1, d)),
            _full(w_in.shape),
            _full((CONV_WIDTH, cd)),
            _full((1, gd)),
            _full((1, gd)),
            _full(w_s.shape),
            _full((CHUNK, GMLP_HEADS)),
            _full(w_out.shape),
        ],
        out_specs=pl.BlockSpec((rows, d), lambda i: (i + base, 0)),
        out_shape=jax.ShapeDtypeStruct((n, d), F32),
        scratch_shapes=[pltpu.VMEM((8, cd), F32), pltpu.VMEM((rows, gd), BF16)],
        input_output_aliases={0: 0} if n_chunks > 1 else {},
        compiler_params=_params(("arbitrary",)),
        name="mixer",
    )(h, *(pending or ()), g1.reshape(1, d), w_in.astype(BF16), conv_w.T, ln_g.reshape(1, gd),
      ln_b.reshape(1, gd), w_s, b_s.T, w_out.astype(BF16))


def _swiglu(xn, wg, wu, wd):
    h1 = jnp.dot(xn, wg, preferred_element_type=F32)
    h2 = jnp.dot(xn, wu, preferred_element_type=F32)
    a = (h1 / (1.0 + jnp.exp(-h1)) * h2).astype(BF16)
    return jnp.dot(a, wd, preferred_element_type=F32)


def _dense_ffn_kernel(x_ref, g_ref, wg_ref, wu_ref, wd_ref, *rest):
    n_side = (len(rest) - 1) // 2
    side_in, o_ref, side_out = rest[:n_side], rest[n_side], rest[n_side + 1:]
    x = x_ref[...]
    xn = _rms(x, g_ref[...]).astype(BF16)
    o_ref[...] = x + _swiglu(xn, wg_ref[...], wu_ref[...], wd_ref[...])
    for src, dst in zip(side_in, side_out):
        dst[...] = src[...].astype(dst.dtype)


def _side_cast_ok(w, steps):
    rows = int(np.prod(w.shape[:-1]))
    return rows % steps == 0 and (rows // steps) % 16 == 0 and w.shape[-1] % LANES == 0


def _dense_ffn(h, g, wg, wu, wd, side=()):
    n, d = h.shape
    rows = _row_block(n, FFN_ROWS)
    steps = n // rows
    side2d = [w.reshape(-1, w.shape[-1]) for w in side]
    side_specs = [pl.BlockSpec((w.shape[0] // steps, w.shape[1]), lambda i: (i, 0)) for w in side2d]
    outs = pl.pallas_call(
        _dense_ffn_kernel,
        grid=(steps,),
        in_specs=[
            pl.BlockSpec((rows, d), lambda i: (i, 0)),
            _full((1, d)),
            _full(wg.shape),
            _full(wu.shape),
            _full(wd.shape),
        ] + side_specs,
        out_specs=[pl.BlockSpec((rows, d), lambda i: (i, 0))] + side_specs,
        out_shape=[jax.ShapeDtypeStruct((n, d), F32)]
        + [jax.ShapeDtypeStruct(w.shape, BF16) for w in side2d],
        compiler_params=_params(("arbitrary",)),
        name="dense_ffn",
    )(h, g.reshape(1, d), wg.astype(BF16), wu.astype(BF16), wd.astype(BF16), *side2d)
    return outs[0], [o.reshape(w.shape) for o, w in zip(outs[1:], side)]


def _pack_bf16_pairs(x, piece):
    half = x.shape[1] // 2
    bits = lax.bitcast_convert_type(x.astype(BF16).astype(F32), jnp.uint32)
    return [(bits[:, half + p * piece:half + (p + 1) * piece] & jnp.uint32(0xFFFF0000))
            | (bits[:, p * piece:(p + 1) * piece] >> 16) for p in range(half // piece)]


def _unpack_bf16_pair(packed):
    lo = lax.bitcast_convert_type(packed << 16, F32)
    hi = lax.bitcast_convert_type(packed & jnp.uint32(0xFFFF0000), F32)
    return lo, hi


def _moe_ffn_kernel(be_ref, used_ref, x_ref, wg_ref, wu_ref, wd_ref, o_ref, xn_ref, acc_ref, *,
                    n_steps):
    del be_ref
    i = pl.program_id(0)
    f = pl.program_id(1)
    parts, _, piece = x_ref.shape
    half = parts * piece
    active = i < used_ref[0]

    def write_out(val):
        for p, words in enumerate(_pack_bf16_pairs(val, piece)):
            o_ref[p] = words

    def step(first, last):
        if first:
            for p in range(parts):
                lo, hi = _unpack_bf16_pair(x_ref[p])
                xn_ref[:, p * piece:(p + 1) * piece] = lo.astype(BF16)
                xn_ref[:, half + p * piece:half + (p + 1) * piece] = hi.astype(BF16)
        part = _swiglu(xn_ref[...], wg_ref[0, 0], wu_ref[0, 0], wd_ref[0, 0])
        if last:
            write_out(part if first else acc_ref[...] + part)
        elif first:
            acc_ref[...] = part
        else:
            acc_ref[...] += part

    if n_steps == 1:
        pl.when(active)(functools.partial(step, True, True))
    else:
        pl.when(active & (f == 0))(functools.partial(step, True, False))
        if n_steps > 2:
            pl.when(active & (f > 0) & (f < n_steps - 1))(functools.partial(step, False, False))
        pl.when(active & (f == n_steps - 1))(functools.partial(step, False, True))

    @pl.when(jnp.logical_not(active) & (f == n_steps - 1))
    def _():
        o_ref[...] = jnp.zeros_like(o_ref)


def _moe_ffn(x_sorted, block_expert, n_used, layer, wg, wu, wd):
    parts, n_rows, piece = x_sorted.shape
    d = 2 * parts * piece
    fdim = wg.shape[3]
    rows = MOE_ROWS
    cols = _col_block(fdim, fdim // MOE_COL_SPLIT)
    n_steps = fdim // cols

    def col(i, f, used):
        return jnp.where(i < used[0], f, n_steps - 1)

    grid_spec = pltpu.PrefetchScalarGridSpec(
        num_scalar_prefetch=2,
        grid=(n_rows // rows, n_steps),
        in_specs=[
            pl.BlockSpec((parts, rows, piece),
                         lambda i, f, be, used: (0, jnp.minimum(i, used[0] - 1), 0)),
            pl.BlockSpec((1, 1, d, cols), lambda i, f, be, used: (layer, be[i], 0, col(i, f, used))),
            pl.BlockSpec((1, 1, d, cols), lambda i, f, be, used: (layer, be[i], 0, col(i, f, used))),
            pl.BlockSpec((1, 1, cols, d), lambda i, f, be, used: (layer, be[i], col(i, f, used), 0)),
        ],
        out_specs=pl.BlockSpec((parts, rows, piece), lambda i, f, be, used: (0, i, 0)),
        scratch_shapes=[pltpu.VMEM((rows, d), BF16), pltpu.VMEM((rows, d), F32)],
    )
    kern = functools.partial(_moe_ffn_kernel, n_steps=n_steps)
    return pl.pallas_call(
        kern,
        grid_spec=grid_spec,
        out_shape=jax.ShapeDtypeStruct((parts, n_rows, piece), jnp.uint32),
        compiler_params=_params(("arbitrary", "arbitrary")),
        name="moe_ffn",
    )(block_expert, n_used, x_sorted, wg, wu, wd)


def _cast_kernel(x_ref, o_ref):
    o_ref[...] = x_ref[...].astype(o_ref.dtype)


def _to_bf16(w):
    shape = w.shape
    w2 = w.reshape(-1, shape[-1])
    pref = CAST_BLOCK_BYTES // (4 * shape[-1]) // LANES * LANES
    rows = _row_block(w2.shape[0], pref)
    out = pl.pallas_call(
        _cast_kernel,
        grid=(w2.shape[0] // rows,),
        in_specs=[pl.BlockSpec((rows, shape[-1]), lambda i: (i, 0))],
        out_specs=pl.BlockSpec((rows, shape[-1]), lambda i: (i, 0)),
        out_shape=jax.ShapeDtypeStruct(w2.shape, BF16),
        compiler_params=_params(("arbitrary",)),
        name="cast_bf16",
    )(w2)
    return out.reshape(shape)


def _qkv_kernel(h_ref, g_ref, pos_ref, w_ref, b_ref, invf_ref, mc_ref, m1_ref, m2_ref,
                q_ref, kv_ref, *, q_dim):
    x = h_ref[...]
    xn = _rms(x, g_ref[...]).astype(BF16)
    z = jnp.dot(xn, w_ref[...], preferred_element_type=F32) + b_ref[...]
    ang = pos_ref[...].astype(F32) * invf_ref[...]
    reps = LANES // ang.shape[0]
    cos = jnp.concatenate([jnp.cos(ang)] * reps, axis=0).T
    sin = jnp.concatenate([jnp.sin(ang)] * reps, axis=0).T
    cos = jnp.where(mc_ref[...] > 0.0, cos, 1.0)
    s_lo = sin * m1_ref[...]
    s_hi = sin * m2_ref[...]
    half = ROPE_DIM // 2

    def rope(t):
        return t * cos + pltpu.roll(t, LANES - half, 1) * s_lo + pltpu.roll(t, half, 1) * s_hi

    for j in range(q_dim // LANES):
        cs = slice(j * LANES, (j + 1) * LANES)
        q_ref[:, cs] = (rope(z[:, cs]) * Q_SCALE).astype(BF16)
    kv_ref[:, 0:LANES] = rope(z[:, q_dim:q_dim + LANES]).astype(BF16)
    kv_ref[:, LANES:2 * LANES] = z[:, q_dim + LANES:q_dim + 2 * LANES].astype(BF16)


def _rope_lane_masks():
    lane = np.arange(LANES) % HEAD_DIM
    half = ROPE_DIM // 2
    first = (lane < half).astype(np.float32)
    second = ((lane >= half) & (lane < ROPE_DIM)).astype(np.float32)
    return [jnp.asarray(m[None, :]) for m in (first + second, -first, second)]


def _qkv(h, g, positions, w_qkv, b_qkv, q_dim):
    n, d = h.shape
    qkv_dim = w_qkv.shape[1]
    assert qkv_dim == q_dim + 2 * LANES and HEAD_DIM % (ROPE_DIM // 2) == 0
    rows = _row_block(n, PROJ_ROWS)
    inv_freq = ROPE_THETA ** (-jnp.arange(0, ROPE_DIM, 2, dtype=F32) / ROPE_DIM)
    kern = functools.partial(_qkv_kernel, q_dim=q_dim)
    return pl.pallas_call(
        kern,
        grid=(n // rows,),
        in_specs=[
            pl.BlockSpec((rows, d), lambda i: (i, 0)),
            _full((1, d)),
            pl.BlockSpec((1, rows), lambda i: (0, i)),
            _full(w_qkv.shape),
            _full((1, qkv_dim)),
            _full((ROPE_DIM // 2, 1)),
            _full((1, LANES)),
            _full((1, LANES)),
            _full((1, LANES)),
        ],
        out_specs=[
            pl.BlockSpec((rows, q_dim), lambda i: (i, 0)),
            pl.BlockSpec((rows, 2 * LANES), lambda i: (i, 0)),
        ],
        out_shape=[
            jax.ShapeDtypeStruct((n, q_dim), BF16),
            jax.ShapeDtypeStruct((n, 2 * LANES), BF16),
        ],
        compiler_params=_params(("arbitrary",)),
        name="qkv_rope",
    )(h, g.reshape(1, d), positions.reshape(1, n), w_qkv.astype(BF16),
      b_qkv.reshape(1, qkv_dim), inv_freq.reshape(-1, 1), *_rope_lane_masks())


def _attn_kernel(sink_ref, q_ref, kvc_ref, kvp_ref, h_ref, wo_ref, bo_ref, out_ref,
                 kbuf, vbuf, o_buf, *, wpb):
    rows = q_ref.shape[0]
    tiles = q_ref.shape[1] // LANES // 2
    j = pl.program_id(1)
    kbuf[0:WINDOW, :] = kvp_ref[:, 0:LANES]
    kbuf[WINDOW:WINDOW + rows, :] = kvc_ref[:, 0:LANES]
    vbuf[0:WINDOW, :] = kvp_ref[:, LANES:2 * LANES]
    vbuf[WINDOW:WINDOW + rows, :] = kvc_ref[:, LANES:2 * LANES]

    from_prev = (lax.broadcasted_iota(jnp.int32, (WINDOW, WINDOW), 1)
                 > lax.broadcasted_iota(jnp.int32, (WINDOW, WINDOW), 0))
    first_half = lax.broadcasted_iota(jnp.int32, (2 * WINDOW, LANES), 1) < HEAD_DIM
    out_first_half = lax.broadcasted_iota(jnp.int32, (WINDOW, LANES), 1) < HEAD_DIM
    nt = (((1,), (1,)), ((), ()))

    def window(n, carry):
        r0 = pl.multiple_of(n * WINDOW, WINDOW)
        kt = kbuf[pl.ds(r0, 2 * WINDOW), :].astype(F32)
        vt = vbuf[pl.ds(r0, 2 * WINDOW), :].astype(F32)
        prev_bias = jnp.where(j * wpb + n > 0, 0.0, -jnp.inf)
        k0_lo = jnp.where(first_half, kt, 0.0)
        k1_hi = jnp.where(first_half, 0.0, kt)
        v0_lo = jnp.where(first_half, vt, 1.0)
        v1_hi = jnp.where(first_half, 1.0, vt)
        k_even = (k0_lo.astype(BF16), pltpu.roll(k1_hi, HEAD_DIM, 1).astype(BF16))
        k_odd = (pltpu.roll(k0_lo, HEAD_DIM, 1).astype(BF16), k1_hi.astype(BF16))
        v_even = (v0_lo.astype(BF16), pltpu.roll(v1_hi, HEAD_DIM, 1).astype(BF16))
        v_odd = (pltpu.roll(v0_lo, HEAD_DIM, 1).astype(BF16), v1_hi.astype(BF16))
        for kh in range(2):
            for t0 in range(0, tiles, ATTN_STACK):
                group = [kh * tiles + t0 + u for u in range(ATTN_STACK)]
                q_stack = jnp.concatenate(
                    [q_ref[pl.ds(r0, WINDOW), t * LANES:(t + 1) * LANES] for t in group], axis=0)
                outs = []
                for parity, k_rhs, v_rhs in ((0, k_even[kh], v_even[kh]), (1, k_odd[kh], v_odd[kh])):
                    s_all = lax.dot_general(q_stack, k_rhs, nt, preferred_element_type=F32)
                    p_tiles, corr = [], []
                    for u, t in enumerate(group):
                        sink = sink_ref[t * 2 + parity]
                        s_prev = s_all[u * WINDOW:(u + 1) * WINDOW, 0:WINDOW] + prev_bias
                        s_cur = s_all[u * WINDOW:(u + 1) * WINDOW, WINDOW:2 * WINDOW]
                        s = jnp.where(from_prev, s_prev, s_cur)
                        m = jnp.maximum(jnp.max(s, axis=-1, keepdims=True), sink)
                        p = jnp.exp2(s - m)
                        p_tiles.append(jnp.concatenate(
                            [jnp.where(from_prev, p, 0.0), jnp.where(from_prev, 0.0, p)],
                            axis=1).astype(BF16))
                        corr.append(jnp.exp2(sink - m))
                    pv = jnp.dot(jnp.concatenate(p_tiles, axis=0), v_rhs,
                                 preferred_element_type=F32)
                    outs.append((pv, corr))
                for u, t in enumerate(group):
                    pv_e = outs[0][0][u * WINDOW:(u + 1) * WINDOW]
                    pv_o = outs[1][0][u * WINDOW:(u + 1) * WINDOW]
                    num = jnp.where(out_first_half, pv_e, pv_o)
                    sums = pltpu.roll(jnp.where(out_first_half, pv_o, pv_e), HEAD_DIM, 1)
                    den = sums + jnp.where(out_first_half, outs[0][1][u], outs[1][1][u])
                    o_buf[pl.ds(r0, WINDOW), t * LANES:(t + 1) * LANES] = (num / den).astype(BF16)
        return carry

    lax.fori_loop(0, rows // WINDOW, window, 0, unroll=2)
    out_ref[...] = (h_ref[...] + jnp.dot(o_buf[...], wo_ref[...], preferred_element_type=F32)
                    + bo_ref[...])


def _attention(h, q, kv, sinks, w_o, b_o, batch, seq):
    n, q_dim = q.shape
    d = h.shape[1]
    assert kv.shape[1] == 2 * LANES and (q_dim // HEAD_DIM) % 4 == 0
    rows = _row_block(seq, ATTN_ROWS)
    bps = seq // rows
    wpb = rows // WINDOW
    wps = seq // WINDOW

    grid_spec = pltpu.PrefetchScalarGridSpec(
        num_scalar_prefetch=1,
        grid=(batch, bps),
        in_specs=[
            pl.BlockSpec((rows, q_dim), lambda b, j, s: (b * bps + j, 0)),
            pl.BlockSpec((rows, 2 * LANES), lambda b, j, s: (b * bps + j, 0)),
            pl.BlockSpec((WINDOW, 2 * LANES),
                         lambda b, j, s: (b * wps + jnp.maximum(j * wpb - 1, 0), 0)),
            pl.BlockSpec((rows, d), lambda b, j, s: (b * bps + j, 0)),
            pl.BlockSpec((q_dim, d), lambda b, j, s: (0, 0)),
            pl.BlockSpec((1, d), lambda b, j, s: (0, 0)),
        ],
        out_specs=pl.BlockSpec((rows, d), lambda b, j, s: (b * bps + j, 0)),
        scratch_shapes=[pltpu.VMEM((rows + WINDOW, LANES), BF16),
                        pltpu.VMEM((rows + WINDOW, LANES), BF16),
                        pltpu.VMEM((rows, q_dim), BF16)],
    )
    kern = functools.partial(_attn_kernel, wpb=wpb)
    return pl.pallas_call(
        kern,
        grid_spec=grid_spec,
        out_shape=jax.ShapeDtypeStruct((n, d), F32),
        compiler_params=_params(("arbitrary", "arbitrary")),
        name="swa_attention",
    )(sinks.astype(F32) * LOG2_E, q, kv, kv, h, w_o.astype(BF16), b_o.reshape(1, d))


R_IDX0, R_IDX1, R_GATE0, R_GATE1, R_RANK0, R_RANK1 = range(6)


def _router_kernel(h_ref, g_ref, rw_ref, xpk_ref, route_ref, route_t_ref, cnt_ref, tri_ref,
                   carry_ref):
    rows = h_ref.shape[0]
    i = pl.program_id(0)

    @pl.when(i == 0)
    def _():
        r = lax.broadcasted_iota(jnp.int32, (rows, rows), 0)
        c = lax.broadcasted_iota(jnp.int32, (rows, rows), 1)
        tri_ref[...] = jnp.where(r > c, 1.0, 0.0).astype(BF16)
        carry_ref[...] = jnp.zeros_like(carry_ref)

    xn = _rms(h_ref[...], g_ref[...])
    xb = xn.astype(BF16)
    for p, words in enumerate(_pack_bf16_pairs(xn, xpk_ref.shape[2])):
        xpk_ref[p] = words

    logits = jnp.dot(xb, rw_ref[...], preferred_element_type=F32)
    lane = lax.broadcasted_iota(jnp.int32, logits.shape, 1)
    lg = jnp.where(lane < N_EXPERTS, logits, -jnp.inf)
    m1 = jnp.max(lg, axis=-1, keepdims=True)
    i1 = jnp.min(jnp.where(lg == m1, lane, LANES), axis=-1, keepdims=True)
    lg2 = jnp.where(lane == i1, -jnp.inf, lg)
    m2 = jnp.max(lg2, axis=-1, keepdims=True)
    i2 = jnp.min(jnp.where(lg2 == m2, lane, LANES), axis=-1, keepdims=True)
    e = jnp.exp(m2 - m1)
    g1 = 1.0 / (1.0 + e)
    g2 = e / (1.0 + e)

    sel = (lane == i1) | (lane == i2)
    sel_f = jnp.where(sel, 1.0, 0.0)
    carry = carry_ref[0:1, :]
    before = jnp.dot(tri_ref[...], sel_f.astype(BF16), preferred_element_type=F32) + carry
    r1 = jnp.sum(jnp.where(lane == i1, before, 0.0), axis=-1, keepdims=True)
    r2 = jnp.sum(jnp.where(lane == i2, before, 0.0), axis=-1, keepdims=True)
    carry = carry + jnp.sum(sel_f, axis=0, keepdims=True)
    carry_ref[...] = jnp.broadcast_to(carry, carry_ref.shape)
    cnt_ref[...] = jnp.broadcast_to(carry, cnt_ref.shape)

    route = jnp.zeros(logits.shape, F32)
    for k, val in ((R_IDX0, i1.astype(F32)), (R_IDX1, i2.astype(F32)), (R_GATE0, g1),
                   (R_GATE1, g2), (R_RANK0, r1), (R_RANK1, r2)):
        route = jnp.where(lane == k, val, route)
    route_ref[...] = route
    route_t_ref[...] = route.T[0:route_t_ref.shape[0], :]


def _router(h, g, router_w):
    n, d = h.shape
    rows = _row_block(n, ROUTER_ROWS)
    parts = d // 2 // SC_PIECE
    rw = jnp.zeros((d, LANES), BF16).at[:, 0:N_EXPERTS].set(router_w.astype(BF16))
    return pl.pallas_call(
        _router_kernel,
        grid=(n // rows,),
        in_specs=[
            pl.BlockSpec((rows, d), lambda i: (i, 0)),
            _full((1, d)),
            _full((d, LANES)),
        ],
        out_specs=[
            pl.BlockSpec((parts, rows, SC_PIECE), lambda i: (0, i, 0)),
            pl.BlockSpec((rows, LANES), lambda i: (i, 0)),
            pl.BlockSpec((8, rows), lambda i: (0, i)),
            _full((8, LANES)),
        ],
        out_shape=[
            jax.ShapeDtypeStruct((parts, n, SC_PIECE), jnp.uint32),
            jax.ShapeDtypeStruct((n, LANES), F32),
            jax.ShapeDtypeStruct((8, n), F32),
            jax.ShapeDtypeStruct((8, LANES), F32),
        ],
        scratch_shapes=[pltpu.VMEM((rows, rows), BF16), pltpu.VMEM((8, LANES), F32)],
        compiler_params=_params(("arbitrary",)),
        name="moe_router",
    )(h, g.reshape(1, d), rw)


def _sc_mesh():
    return plsc.VectorSubcoreMesh(core_axis_name="core", subcore_axis_name="subcore")


def _gather_pieces(src, idx):
    m = idx.shape[0]
    width = src.shape[1]
    assert m % (SC_WINDOW * SC_WORKERS) == 0

    @functools.partial(pl.kernel, out_type=jax.ShapeDtypeStruct((m, width), src.dtype),
                       mesh=_sc_mesh(), scratch_types=[])
    def gather_kernel(src_hbm, idx_hbm, out_hbm):
        def body(idx_vmem, out_vmem):
            pltpu.sync_copy(src_hbm.at[idx_vmem.at[0]], out_vmem)

        pltpu.emit_pipeline(
            body,
            grid=(m // SC_WINDOW,),
            in_specs=[pl.BlockSpec((1, SC_WINDOW), lambda i: (0, i))],
            out_specs=[pl.BlockSpec((SC_WINDOW, width), lambda i: (i, 0))],
            core_axis_name=("core", "subcore"),
            dimension_semantics=(pltpu.PARALLEL,),
        )(idx_hbm, out_hbm)

    return gather_kernel(src, idx.reshape(1, m))


def _scatter_pieces(src, idx, out_rows):
    m = idx.shape[0]
    width = src.shape[1]
    src_windows = src.shape[0] // SC_WINDOW
    assert m % (SC_WINDOW * SC_WORKERS) == 0 and src.shape[0] % SC_WINDOW == 0

    @functools.partial(pl.kernel, out_type=jax.ShapeDtypeStruct((out_rows, width), src.dtype),
                       mesh=_sc_mesh(), scratch_types=[])
    def scatter_kernel(src_hbm, idx_hbm, out_hbm):
        def body(src_vmem, idx_vmem):
            pltpu.sync_copy(src_vmem, out_hbm.at[idx_vmem.at[0]])

        pltpu.emit_pipeline(
            body,
            grid=(m // SC_WINDOW,),
            in_specs=[pl.BlockSpec((SC_WINDOW, width), lambda i: (i % src_windows, 0)),
                      pl.BlockSpec((1, SC_WINDOW), lambda i: (0, i))],
            out_specs=[],
            core_axis_name=("core", "subcore"),
            dimension_semantics=(pltpu.PARALLEL,),
        )(src_hbm, idx_hbm)

    return scatter_kernel(src, idx.reshape(1, m))


def _moe_combined(h_ref, y_ref, route_ref):
    parts, _, _, piece = y_ref.shape
    route = route_ref[...]
    g0 = route[:, R_GATE0:R_GATE0 + 1]
    g1 = route[:, R_GATE1:R_GATE1 + 1]
    lo, hi = [], []
    for p in range(parts):
        lo0, hi0 = _unpack_bf16_pair(y_ref[p, 0])
        lo1, hi1 = _unpack_bf16_pair(y_ref[p, 1])
        lo.append(g0 * lo0 + g1 * lo1)
        hi.append(g0 * hi0 + g1 * hi1)
    return h_ref[...] + jnp.concatenate(lo + hi, axis=1)


def _moe_specs(pending, rows, base):
    y_pairs, route = pending
    parts, _, _, piece = y_pairs.shape
    return [pl.BlockSpec((parts, 2, rows, piece), lambda i: (0, 0, i, 0)),
            pl.BlockSpec((rows, LANES), lambda i: (i + base, 0))]


def _final_kernel(h_ref, y_ref, route_ref, g_ref, o_ref):
    o_ref[...] = _rms(_moe_combined(h_ref, y_ref, route_ref), g_ref[...])


def _final(h, pending, final_g):
    n, d = h.shape
    y_chunks, route = pending
    n_chunks = len(y_chunks)
    rows = _row_block(n // n_chunks, PROJ_ROWS)
    steps = n // rows // n_chunks
    for c, y in enumerate(y_chunks):
        base = c * steps
        h = pl.pallas_call(
            _final_kernel,
            grid=(steps,),
            in_specs=[pl.BlockSpec((rows, d), lambda i, base=base: (i + base, 0))]
            + _moe_specs((y, route), rows, base) + [_full((1, d))],
            out_specs=pl.BlockSpec((rows, d), lambda i, base=base: (i + base, 0)),
            out_shape=jax.ShapeDtypeStruct((n, d), F32),
            input_output_aliases={0: 0} if n_chunks > 1 else {},
            compiler_params=_params(("arbitrary",)),
            name="moe_combine_final_norm",
        )(h, y, route, final_g.reshape(1, d))
    return h


def _moe(h, g, router_w, layer, wg, wu, wd, n_chunks):
    n, d = h.shape
    xpk, route, route_t, cnt = _router(h, g, router_w)
    parts = xpk.shape[0]

    top_idx = route_t[R_IDX0:R_IDX1 + 1].astype(jnp.int32)
    rank = route_t[R_RANK0:R_RANK1 + 1].astype(jnp.int32)
    sizes = cnt[0, 0:N_EXPERTS].astype(jnp.int32)
    padded = ((sizes + MOE_ROWS - 1) // MOE_ROWS) * MOE_ROWS
    pends = jnp.cumsum(padded)
    pstarts = pends - padded
    dest = rank
    for e in range(N_EXPERTS):
        dest = dest + jnp.where(top_idx == e, pstarts[e], 0)
    n_rows = 2 * n + N_EXPERTS * MOE_ROWS
    n_blocks = n_rows // MOE_ROWS
    block_start = jnp.arange(n_blocks, dtype=jnp.int32) * MOE_ROWS
    block_expert = jnp.minimum(
        jnp.sum((block_start[:, None] >= pends[None, :]).astype(jnp.int32), axis=1),
        N_EXPERTS - 1)
    n_used = (pends[N_EXPERTS - 1:] // MOE_ROWS).astype(jnp.int32)

    off = jnp.arange(parts, dtype=jnp.int32) * n_rows
    scatter_idx = (dest[:, None, :] + off[None, :, None]).reshape(-1)
    x_sorted = _scatter_pieces(xpk.reshape(parts * n, SC_PIECE), scatter_idx, parts * n_rows)
    y_rows = _moe_ffn(x_sorted.reshape(parts, n_rows, SC_PIECE), block_expert, n_used, layer,
                      wg, wu, wd)

    y_flat = y_rows.reshape(parts * n_rows, SC_PIECE)
    nc = n // n_chunks
    chunk_dest = dest.reshape(2, n_chunks, nc).transpose(1, 0, 2)
    gather_idx = (chunk_dest[:, None] + off[None, :, None, None]).reshape(n_chunks, -1)
    y_chunks = [_gather_pieces(y_flat, gather_idx[c]).reshape(parts, 2, nc, SC_PIECE)
                for c in range(n_chunks)]
    return y_chunks, route


def kernel(x, positions, final_norm_g, ev_norm1_g, ev_w_in, ev_conv_w, ev_ln_g, ev_ln_b, ev_spatial_w, ev_spatial_b, ev_w_out, ev_norm2_g, ev_ffn_wg, ev_ffn_wu, ev_ffn_wd, od_norm1_g, od_w_qkv, od_b_qkv, od_sinks, od_w_o, od_b_o, od_norm2_g, od_router_w, od_exp_wg, od_exp_wu, od_exp_wd):
    batch, seq, d = x.shape
    depth = ev_norm1_g.shape[0] + od_norm1_g.shape[0]
    assert depth % 2 == 0, "the final norm is fused into the last (odd) layer's MoE combine"
    n_q_heads = od_sinks.shape[1]
    h = x.reshape(batch * seq, d)
    experts = [od_exp_wg, od_exp_wu, od_exp_wd]
    ffn_steps = (batch * seq) // _row_block(batch * seq, FFN_ROWS)
    ride_along = all(_side_cast_ok(w, ffn_steps) for w in experts)
    if not ride_along:
        experts = [_to_bf16(w) for w in experts]
    pending = None
    pieces_per_token = 2 * (d // 2 // SC_PIECE)
    n_chunks = max(c for c in range(1, COMBINE_CHUNKS + 1)
                   if batch % c == 0
                   and (batch // c * seq * pieces_per_token) % (SC_WINDOW * SC_WORKERS) == 0)
    for layer in range(depth):
        i = layer // 2
        if layer % 2 == 0:
            h = _mixer(h, pending, seq, ev_norm1_g[i], ev_w_in[i], ev_conv_w[i], ev_ln_g[i],
                       ev_ln_b[i], ev_spatial_w[i], ev_spatial_b[i], ev_w_out[i])
            side = experts if (ride_along and layer == 0) else ()
            h, cast = _dense_ffn(h, ev_norm2_g[i], ev_ffn_wg[i], ev_ffn_wu[i], ev_ffn_wd[i], side)
            if side:
                experts = cast
        else:
            q, kv = _qkv(h, od_norm1_g[i], positions, od_w_qkv[i], od_b_qkv[i],
                         n_q_heads * HEAD_DIM)
            h = _attention(h, q, kv, od_sinks[i], od_w_o[i], od_b_o[i], batch, seq)
            pending = _moe(h, od_norm2_g[i], od_router_w[i], i, *experts, n_chunks)
    return _final(h, pending, final_norm_g).reshape(batch, seq, d)
```

```python
import functools

import jax
import jax.numpy as jnp
import numpy as np
from jax import lax
from jax.experimental import pallas as pl
from jax.experimental.pallas import tpu as pltpu
from jax.experimental.pallas import tpu_sc as plsc

F32 = jnp.float32
BF16 = jnp.bfloat16

EPS = 1e-5
CHUNK = 128
GMLP_HEADS = 4
CONV_WIDTH = 3
HEAD_DIM = 64
WINDOW = 128
ROPE_DIM = HEAD_DIM // 4
ROPE_THETA = 500000.0
ATTN_SCALE = HEAD_DIM ** -0.5
LOG2_E = float(np.log2(np.e))
Q_SCALE = ATTN_SCALE * LOG2_E
N_EXPERTS = 8
LANES = 128
VMEM_LIMIT = 56 * 1024 * 1024

MIXER_ROWS = 512
FFN_ROWS = 512
CAST_BLOCK_BYTES = 8 * 1024 * 1024
PROJ_ROWS = 1024
ATTN_ROWS = 1024
ATTN_STACK = 2
ROUTER_ROWS = 1024
MOE_ROWS = 1024
COMBINE_CHUNKS = 8
MOE_COL_SPLIT = 2
SC_WORKERS = 32
SC_PIECE = 256
SC_WINDOW = 128


def _row_block(n, pref):
    b = min(n, pref)
    while n % b:
        b -= LANES
    return b


def _col_block(f, pref):
    b = min(f, pref)
    b -= b % LANES
    while f % b:
        b -= LANES
    return b


def _params(sem):
    return pltpu.CompilerParams(dimension_semantics=sem, vmem_limit_bytes=VMEM_LIMIT)


def _rms(x, g):
    return x * lax.rsqrt(jnp.mean(x * x, axis=-1, keepdims=True) + EPS) * g


def _gelu(x):
    return 0.5 * x * (1.0 + lax.erf(x * np.float32(np.sqrt(0.5))))


def _full(shape):
    return pl.BlockSpec(shape, lambda *_: (0,) * len(shape))


def _mixer_kernel(*refs, blocks_per_seq, has_pending):
    if has_pending:
        h_ref, y_ref, route_ref = refs[:3]
        refs = refs[3:]
    else:
        h_ref = refs[0]
        refs = refs[1:]
    (g1_ref, win_ref, cw_ref, lng_ref, lnb_ref, ws_ref, bst_ref, wout_ref, o_ref,
     tail_ref, yb_ref) = refs
    rows = h_ref.shape[0]
    cd = cw_ref.shape[1]
    gd = lng_ref.shape[1]
    hd = gd // GMLP_HEADS
    i = pl.program_id(0)

    x = _moe_combined(h_ref, y_ref, route_ref) if has_pending else h_ref[...]
    xn = _rms(x, g1_ref[...]).astype(BF16)
    z = jnp.dot(xn, win_ref[...], preferred_element_type=F32)
    a_b = z[:, 0:cd]
    a_c = z[:, cd:2 * cd]
    a_x = z[:, 2 * cd:3 * cd]
    b_u = z[:, 3 * cd:3 * cd + gd]
    b_v = z[:, 3 * cd + gd:3 * cd + 2 * gd]

    g = a_c * a_x

    @pl.when(i % blocks_per_seq == 0)
    def _():
        tail_ref[...] = jnp.zeros_like(tail_ref)

    tail = tail_ref[...]
    row = lax.broadcasted_iota(jnp.int32, g.shape, 0)
    gm1 = jnp.where(row == 0, tail[7:8], pltpu.roll(g, 1, 0))
    gm2 = jnp.where(row == 0, tail[6:7], jnp.where(row == 1, tail[7:8], pltpu.roll(g, 2, 0)))
    tail_ref[...] = g[rows - 8:rows]
    cw = cw_ref[...]
    y_a = a_b * (gm2 * cw[0:1] + gm1 * cw[1:2] + g * cw[2:3])

    u = _gelu(b_u)
    v = _gelu(b_v)
    mu = jnp.mean(v, axis=-1, keepdims=True)
    vc = v - mu
    var = jnp.mean(vc * vc, axis=-1, keepdims=True)
    vn = (vc * lax.rsqrt(var + EPS) * lng_ref[...] + lnb_ref[...]).astype(BF16)
    ri = lax.broadcasted_iota(jnp.int32, (CHUNK, CHUNK), 0)
    ci = lax.broadcasted_iota(jnp.int32, (CHUNK, CHUNK), 1)
    causal = ri >= ci
    bst = bst_ref[...]
    for k in range(GMLP_HEADS):
        w_k = jnp.where(causal, ws_ref[k], 0.0).astype(BF16)
        b_k = bst[:, k:k + 1]
        for c in range(rows // CHUNK):
            rs = slice(c * CHUNK, (c + 1) * CHUNK)
            cs = slice(k * hd, (k + 1) * hd)
            mixed = jnp.dot(w_k, vn[rs, cs], preferred_element_type=F32) + b_k
            yb_ref[rs, cs] = (u[rs, cs] * mixed).astype(BF16)

    out = jnp.dot(y_a.astype(BF16), wout_ref[0:cd, :], preferred_element_type=F32)
    out = out + jnp.dot(yb_ref[...], wout_ref[cd:cd + gd, :], preferred_element_type=F32)
    o_ref[...] = x + out


def _mixer(h, pending, seq, *weights):
    if pending is None:
        return _mixer_call(h, None, 0, 1, seq, *weights)
    y_chunks, route = pending
    for c, y in enumerate(y_chunks):
        h = _mixer_call(h, (y, route), c, len(y_chunks), seq, *weights)
    return h


def _mixer_call(h, pending, chunk, n_chunks, seq, g1, w_in, conv_w, ln_g, ln_b, w_s, b_s, w_out):
    n, d = h.shape
    rows = _row_block(seq, MIXER_ROWS)
    cd = conv_w.shape[0]
    gd = ln_g.shape[0]
    has_pending = pending is not None
    steps = n // rows // n_chunks
    base = chunk * steps
    assert (steps * rows) % seq == 0
    kern = functools.partial(_mixer_kernel, blocks_per_seq=seq // rows, has_pending=has_pending)
    return pl.pallas_call(
        kern,
        grid=(steps,),
        in_specs=[pl.BlockSpec((rows, d), lambda i: (i + base, 0))]
        + (_moe_specs(pending, rows, base) if has_pending else [])
        + [
            _full((1, d)),
            _full(w_in.shape),
            _full((CONV_WIDTH, cd)),
            _full((1, gd)),
            _full((1, gd)),
            _full(w_s.shape),
            _full((CHUNK, GMLP_HEADS)),
            _full(w_out.shape),
        ],
        out_specs=pl.BlockSpec((rows, d), lambda i: (i + base, 0)),
        out_shape=jax.ShapeDtypeStruct((n, d), F32),
        scratch_shapes=[pltpu.VMEM((8, cd), F32), pltpu.VMEM((rows, gd), BF16)],
        input_output_aliases={0: 0} if n_chunks > 1 else {},
        compiler_params=_params(("arbitrary",)),
        name="mixer",
    )(h, *(pending or ()), g1.reshape(1, d), w_in.astype(BF16), conv_w.T, ln_g.reshape(1, gd),
      ln_b.reshape(1, gd), w_s, b_s.T, w_out.astype(BF16))


def _swiglu(xn, wg, wu, wd):
    h1 = jnp.dot(xn, wg, preferred_element_type=F32)
    h2 = jnp.dot(xn, wu, preferred_element_type=F32)
    a = (h1 / (1.0 + jnp.exp(-h1)) * h2).astype(BF16)
    return jnp.dot(a, wd, preferred_element_type=F32)


def _dense_ffn_kernel(x_ref, g_ref, wg_ref, wu_ref, wd_ref, *rest):
    n_side = (len(rest) - 1) // 2
    side_in, o_ref, side_out = rest[:n_side], rest[n_side], rest[n_side + 1:]
    x = x_ref[...]
    xn = _rms(x, g_ref[...]).astype(BF16)
    o_ref[...] = x + _swiglu(xn, wg_ref[...], wu_ref[...], wd_ref[...])
    for src, dst in zip(side_in, side_out):
        dst[...] = src[...].astype(dst.dtype)


def _side_cast_ok(w, steps):
    rows = int(np.prod(w.shape[:-1]))
    return rows % steps == 0 and (rows // steps) % 16 == 0 and w.shape[-1] % LANES == 0


def _dense_ffn(h, g, wg, wu, wd, side=()):
    n, d = h.shape
    rows = _row_block(n, FFN_ROWS)
    steps = n // rows
    side2d = [w.reshape(-1, w.shape[-1]) for w in side]
    side_specs = [pl.BlockSpec((w.shape[0] // steps, w.shape[1]), lambda i: (i, 0)) for w in side2d]
    outs = pl.pallas_call(
        _dense_ffn_kernel,
        grid=(steps,),
        in_specs=[
            pl.BlockSpec((rows, d), lambda i: (i, 0)),
            _full((1, d)),
            _full(wg.shape),
            _full(wu.shape),
            _full(wd.shape),
        ] + side_specs,
        out_specs=[pl.BlockSpec((rows, d), lambda i: (i, 0))] + side_specs,
        out_shape=[jax.ShapeDtypeStruct((n, d), F32)]
        + [jax.ShapeDtypeStruct(w.shape, BF16) for w in side2d],
        compiler_params=_params(("arbitrary",)),
        name="dense_ffn",
    )(h, g.reshape(1, d), wg.astype(BF16), wu.astype(BF16), wd.astype(BF16), *side2d)
    return outs[0], [o.reshape(w.shape) for o, w in zip(outs[1:], side)]


def _pack_bf16_pairs(x, piece):
    half = x.shape[1] // 2
    bits = lax.bitcast_convert_type(x.astype(BF16).astype(F32), jnp.uint32)
    return [(bits[:, half + p * piece:half + (p + 1) * piece] & jnp.uint32(0xFFFF0000))
            | (bits[:, p * piece:(p + 1) * piece] >> 16) for p in range(half // piece)]


def _unpack_bf16_pair(packed):
    lo = lax.bitcast_convert_type(packed << 16, F32)
    hi = lax.bitcast_convert_type(packed & jnp.uint32(0xFFFF0000), F32)
    return lo, hi


def _moe_ffn_kernel(be_ref, used_ref, x_ref, wg_ref, wu_ref, wd_ref, o_ref, xn_ref, acc_ref, *,
                    n_steps):
    del be_ref
    i = pl.program_id(0)
    f = pl.program_id(1)
    parts, _, piece = x_ref.shape
    half = parts * piece
    active = i < used_ref[0]

    def write_out(val):
        for p, words in enumerate(_pack_bf16_pairs(val, piece)):
            o_ref[p] = words

    def step(first, last):
        if first:
            for p in range(parts):
                lo, hi = _unpack_bf16_pair(x_ref[p])
                xn_ref[:, p * piece:(p + 1) * piece] = lo.astype(BF16)
                xn_ref[:, half + p * piece:half + (p + 1) * piece] = hi.astype(BF16)
        part = _swiglu(xn_ref[...], wg_ref[0, 0], wu_ref[0, 0], wd_ref[0, 0])
        if last:
            write_out(part if first else acc_ref[...] + part)
        elif first:
            acc_ref[...] = part
        else:
            acc_ref[...] += part

    if n_steps == 1:
        pl.when(active)(functools.partial(step, True, True))
    else:
        pl.when(active & (f == 0))(functools.partial(step, True, False))
        if n_steps > 2:
            pl.when(active & (f > 0) & (f < n_steps - 1))(functools.partial(step, False, False))
        pl.when(active & (f == n_steps - 1))(functools.partial(step, False, True))

    @pl.when(jnp.logical_not(active) & (f == n_steps - 1))
    def _():
        o_ref[...] = jnp.zeros_like(o_ref)


def _moe_ffn(x_sorted, block_expert, n_used, layer, wg, wu, wd):
    parts, n_rows, piece = x_sorted.shape
    d = 2 * parts * piece
    fdim = wg.shape[3]
    rows = MOE_ROWS
    cols = _col_block(fdim, fdim // MOE_COL_SPLIT)
    n_steps = fdim // cols

    def col(i, f, used):
        return jnp.where(i < used[0], f, n_steps - 1)

    grid_spec = pltpu.PrefetchScalarGridSpec(
        num_scalar_prefetch=2,
        grid=(n_rows // rows, n_steps),
        in_specs=[
            pl.BlockSpec((parts, rows, piece),
                         lambda i, f, be, used: (0, jnp.minimum(i, used[0] - 1), 0)),
            pl.BlockSpec((1, 1, d, cols), lambda i, f, be, used: (layer, be[i], 0, col(i, f, used))),
            pl.BlockSpec((1, 1, d, cols), lambda i, f, be, used: (layer, be[i], 0, col(i, f, used))),
            pl.BlockSpec((1, 1, cols, d), lambda i, f, be, used: (layer, be[i], col(i, f, used), 0)),
        ],
        out_specs=pl.BlockSpec((parts, rows, piece), lambda i, f, be, used: (0, i, 0)),
        scratch_shapes=[pltpu.VMEM((rows, d), BF16), pltpu.VMEM((rows, d), F32)],
    )
    kern = functools.partial(_moe_ffn_kernel, n_steps=n_steps)
    return pl.pallas_call(
        kern,
        grid_spec=grid_spec,
        out_shape=jax.ShapeDtypeStruct((parts, n_rows, piece), jnp.uint32),
        compiler_params=_params(("arbitrary", "arbitrary")),
        name="moe_ffn",
    )(block_expert, n_used, x_sorted, wg, wu, wd)


def _cast_kernel(x_ref, o_ref):
    o_ref[...] = x_ref[...].astype(o_ref.dtype)


def _to_bf16(w):
    shape = w.shape
    w2 = w.reshape(-1, shape[-1])
    pref = CAST_BLOCK_BYTES // (4 * shape[-1]) // LANES * LANES
    rows = _row_block(w2.shape[0], pref)
    out = pl.pallas_call(
        _cast_kernel,
        grid=(w2.shape[0] // rows,),
        in_specs=[pl.BlockSpec((rows, shape[-1]), lambda i: (i, 0))],
        out_specs=pl.BlockSpec((rows, shape[-1]), lambda i: (i, 0)),
        out_shape=jax.ShapeDtypeStruct(w2.shape, BF16),
        compiler_params=_params(("arbitrary",)),
        name="cast_bf16",
    )(w2)
    return out.reshape(shape)


def _qkv_kernel(h_ref, g_ref, pos_ref, w_ref, b_ref, invf_ref, mc_ref, q_ref, kv_ref, *, q_dim):
    x = h_ref[...]
    xn = _rms(x, g_ref[...]).astype(BF16)
    z = jnp.dot(xn, w_ref[...], preferred_element_type=F32) + b_ref[...]
    ang = pos_ref[...].astype(F32) * invf_ref[...]
    reps = LANES // ang.shape[0]
    cos = jnp.concatenate([jnp.cos(ang)] * reps, axis=0).T
    sin = jnp.concatenate([jnp.sin(ang)] * reps, axis=0).T
    cos = jnp.where(mc_ref[...] != 0.0, cos, 1.0)
    sin = sin * mc_ref[...]

    def rope(t):
        return t * cos + pltpu.roll(t, LANES // 2, 1) * sin

    for j in range(q_dim // LANES):
        cs = slice(j * LANES, (j + 1) * LANES)
        q_ref[:, cs] = (rope(z[:, cs]) * Q_SCALE).astype(BF16)
    kv_ref[:, 0:LANES] = rope(z[:, q_dim:q_dim + LANES]).astype(BF16)
    kv_ref[:, LANES:2 * LANES] = z[:, q_dim + LANES:q_dim + 2 * LANES].astype(BF16)


QK_GROUP = LANES // 4


def _qk_tile_layout():
    lane = np.arange(LANES)
    group, off = lane // QK_GROUP, lane % QK_GROUP
    half = ROPE_DIM // 2
    head = group % 2
    second = group // 2
    rest = QK_GROUP - half
    dim = np.where(off < half, second * half + off, ROPE_DIM + second * rest + (off - half))
    return head, dim


def _qk_column_order(n_cols):
    head, dim = _qk_tile_layout()
    tile = np.arange(n_cols) // LANES
    return tile * LANES + np.tile(head * HEAD_DIM + dim, n_cols // LANES)


def _rope_sign_lanes():
    _, dim = _qk_tile_layout()
    half = ROPE_DIM // 2
    sign = np.where(dim < half, -1.0, np.where(dim < ROPE_DIM, 1.0, 0.0)).astype(np.float32)
    return jnp.asarray(sign[None, :])


def _qkv(h, g, positions, w_qkv, b_qkv, q_dim):
    n, d = h.shape
    qkv_dim = w_qkv.shape[1]
    assert qkv_dim == q_dim + 2 * LANES and QK_GROUP % (ROPE_DIM // 2) == 0
    rows = _row_block(n, PROJ_ROWS)
    inv_freq = ROPE_THETA ** (-jnp.arange(0, ROPE_DIM, 2, dtype=F32) / ROPE_DIM)
    order = np.concatenate([_qk_column_order(q_dim + LANES), np.arange(q_dim + LANES, qkv_dim)])
    w_qkv = w_qkv[:, order]
    b_qkv = b_qkv[order]
    kern = functools.partial(_qkv_kernel, q_dim=q_dim)
    return pl.pallas_call(
        kern,
        grid=(n // rows,),
        in_specs=[
            pl.BlockSpec((rows, d), lambda i: (i, 0)),
            _full((1, d)),
            pl.BlockSpec((1, rows), lambda i: (0, i)),
            _full(w_qkv.shape),
            _full((1, qkv_dim)),
            _full((ROPE_DIM // 2, 1)),
            _full((1, LANES)),
        ],
        out_specs=[
            pl.BlockSpec((rows, q_dim), lambda i: (i, 0)),
            pl.BlockSpec((rows, 2 * LANES), lambda i: (i, 0)),
        ],
        out_shape=[
            jax.ShapeDtypeStruct((n, q_dim), BF16),
            jax.ShapeDtypeStruct((n, 2 * LANES), BF16),
        ],
        compiler_params=_params(("arbitrary",)),
        name="qkv_rope",
    )(h, g.reshape(1, d), positions.reshape(1, n), w_qkv.astype(BF16),
      b_qkv.reshape(1, qkv_dim), inv_freq.reshape(-1, 1), _rope_sign_lanes())


def _attn_kernel(sink_ref, q_ref, kvc_ref, kvp_ref, h_ref, wo_ref, bo_ref, out_ref,
                 kbuf, vbuf, o_buf, *, wpb):
    rows = q_ref.shape[0]
    tiles = q_ref.shape[1] // LANES // 2
    j = pl.program_id(1)
    kbuf[0:WINDOW, :] = kvp_ref[:, 0:LANES]
    kbuf[WINDOW:WINDOW + rows, :] = kvc_ref[:, 0:LANES]
    vbuf[0:WINDOW, :] = kvp_ref[:, LANES:2 * LANES]
    vbuf[WINDOW:WINDOW + rows, :] = kvc_ref[:, LANES:2 * LANES]

    from_prev = (lax.broadcasted_iota(jnp.int32, (WINDOW, WINDOW), 1)
                 > lax.broadcasted_iota(jnp.int32, (WINDOW, WINDOW), 0))
    kv_lane = lax.broadcasted_iota(jnp.int32, (2 * WINDOW, LANES), 1)
    first_half = kv_lane < HEAD_DIM
    head0_lanes = (kv_lane // QK_GROUP) % 2 == 0
    out_first_half = lax.broadcasted_iota(jnp.int32, (WINDOW, LANES), 1) < HEAD_DIM
    nt = (((1,), (1,)), ((), ()))

    def window(n, carry):
        r0 = pl.multiple_of(n * WINDOW, WINDOW)
        kt = kbuf[pl.ds(r0, 2 * WINDOW), :].astype(F32)
        vt = vbuf[pl.ds(r0, 2 * WINDOW), :].astype(F32)
        prev_bias = jnp.where(j * wpb + n > 0, 0.0, -jnp.inf)
        k0_lo = jnp.where(head0_lanes, kt, 0.0)
        k1_hi = jnp.where(head0_lanes, 0.0, kt)
        v0_lo = jnp.where(first_half, vt, 1.0)
        v1_hi = jnp.where(first_half, 1.0, vt)
        k_even = (k0_lo.astype(BF16), pltpu.roll(k1_hi, LANES - QK_GROUP, 1).astype(BF16))
        k_odd = (pltpu.roll(k0_lo, QK_GROUP, 1).astype(BF16), k1_hi.astype(BF16))
        v_even = (v0_lo.astype(BF16), pltpu.roll(v1_hi, HEAD_DIM, 1).astype(BF16))
        v_odd = (pltpu.roll(v0_lo, HEAD_DIM, 1).astype(BF16), v1_hi.astype(BF16))
        for kh in range(2):
            for t0 in range(0, tiles, ATTN_STACK):
                group = [kh * tiles + t0 + u for u in range(ATTN_STACK)]
                q_stack = jnp.concatenate(
                    [q_ref[pl.ds(r0, WINDOW), t * LANES:(t + 1) * LANES] for t in group], axis=0)
                outs = []
                for parity, k_rhs, v_rhs in ((0, k_even[kh], v_even[kh]), (1, k_odd[kh], v_odd[kh])):
                    s_all = lax.dot_general(q_stack, k_rhs, nt, preferred_element_type=F32)
                    p_tiles, corr = [], []
                    for u, t in enumerate(group):
                        sink = sink_ref[t * 2 + parity]
                        s_prev = s_all[u * WINDOW:(u + 1) * WINDOW, 0:WINDOW] + prev_bias
                        s_cur = s_all[u * WINDOW:(u + 1) * WINDOW, WINDOW:2 * WINDOW]
                        s = jnp.where(from_prev, s_prev, s_cur)
                        m = jnp.maximum(jnp.max(s, axis=-1, keepdims=True), sink)
                        p = jnp.exp2(s - m)
                        p_tiles.append(jnp.concatenate(
                            [jnp.where(from_prev, p, 0.0), jnp.where(from_prev, 0.0, p)],
                            axis=1).astype(BF16))
                        corr.append(jnp.exp2(sink - m))
                    pv = jnp.dot(jnp.concatenate(p_tiles, axis=0), v_rhs,
                                 preferred_element_type=F32)
                    outs.append((pv, corr))
                for u, t in enumerate(group):
                    pv_e = outs[0][0][u * WINDOW:(u + 1) * WINDOW]
                    pv_o = outs[1][0][u * WINDOW:(u + 1) * WINDOW]
                    num = jnp.where(out_first_half, pv_e, pv_o)
                    sums = pltpu.roll(jnp.where(out_first_half, pv_o, pv_e), HEAD_DIM, 1)
                    den = sums + jnp.where(out_first_half, outs[0][1][u], outs[1][1][u])
                    o_buf[pl.ds(r0, WINDOW), t * LANES:(t + 1) * LANES] = (num / den).astype(BF16)
        return carry

    lax.fori_loop(0, rows // WINDOW, window, 0, unroll=2)
    out_ref[...] = (h_ref[...] + jnp.dot(o_buf[...], wo_ref[...], preferred_element_type=F32)
                    + bo_ref[...])


def _attention(h, q, kv, sinks, w_o, b_o, batch, seq):
    n, q_dim = q.shape
    d = h.shape[1]
    assert kv.shape[1] == 2 * LANES and (q_dim // HEAD_DIM) % 4 == 0
    rows = _row_block(seq, ATTN_ROWS)
    bps = seq // rows
    wpb = rows // WINDOW
    wps = seq // WINDOW

    grid_spec = pltpu.PrefetchScalarGridSpec(
        num_scalar_prefetch=1,
        grid=(batch, bps),
        in_specs=[
            pl.BlockSpec((rows, q_dim), lambda b, j, s: (b * bps + j, 0)),
            pl.BlockSpec((rows, 2 * LANES), lambda b, j, s: (b * bps + j, 0)),
            pl.BlockSpec((WINDOW, 2 * LANES),
                         lambda b, j, s: (b * wps + jnp.maximum(j * wpb - 1, 0), 0)),
            pl.BlockSpec((rows, d), lambda b, j, s: (b * bps + j, 0)),
            pl.BlockSpec((q_dim, d), lambda b, j, s: (0, 0)),
            pl.BlockSpec((1, d), lambda b, j, s: (0, 0)),
        ],
        out_specs=pl.BlockSpec((rows, d), lambda b, j, s: (b * bps + j, 0)),
        scratch_shapes=[pltpu.VMEM((rows + WINDOW, LANES), BF16),
                        pltpu.VMEM((rows + WINDOW, LANES), BF16),
                        pltpu.VMEM((rows, q_dim), BF16)],
    )
    kern = functools.partial(_attn_kernel, wpb=wpb)
    return pl.pallas_call(
        kern,
        grid_spec=grid_spec,
        out_shape=jax.ShapeDtypeStruct((n, d), F32),
        compiler_params=_params(("arbitrary", "arbitrary")),
        name="swa_attention",
    )(sinks.astype(F32) * LOG2_E, q, kv, kv, h, w_o.astype(BF16), b_o.reshape(1, d))


R_IDX0, R_IDX1, R_GATE0, R_GATE1, R_RANK0, R_RANK1 = range(6)


def _router_kernel(h_ref, g_ref, rw_ref, xpk_ref, route_ref, route_t_ref, cnt_ref, tri_ref,
                   carry_ref):
    rows = h_ref.shape[0]
    i = pl.program_id(0)

    @pl.when(i == 0)
    def _():
        r = lax.broadcasted_iota(jnp.int32, (rows, rows), 0)
        c = lax.broadcasted_iota(jnp.int32, (rows, rows), 1)
        tri_ref[...] = jnp.where(r > c, 1.0, 0.0).astype(BF16)
        carry_ref[...] = jnp.zeros_like(carry_ref)

    xn = _rms(h_ref[...], g_ref[...])
    xb = xn.astype(BF16)
    for p, words in enumerate(_pack_bf16_pairs(xn, xpk_ref.shape[2])):
        xpk_ref[p] = words

    logits = jnp.dot(xb, rw_ref[...], preferred_element_type=F32)
    lane = lax.broadcasted_iota(jnp.int32, logits.shape, 1)
    lg = jnp.where(lane < N_EXPERTS, logits, -jnp.inf)
    m1 = jnp.max(lg, axis=-1, keepdims=True)
    i1 = jnp.min(jnp.where(lg == m1, lane, LANES), axis=-1, keepdims=True)
    lg2 = jnp.where(lane == i1, -jnp.inf, lg)
    m2 = jnp.max(lg2, axis=-1, keepdims=True)
    i2 = jnp.min(jnp.where(lg2 == m2, lane, LANES), axis=-1, keepdims=True)
    e = jnp.exp(m2 - m1)
    g1 = 1.0 / (1.0 + e)
    g2 = e / (1.0 + e)

    sel = (lane == i1) | (lane == i2)
    sel_f = jnp.where(sel, 1.0, 0.0)
    carry = carry_ref[0:1, :]
    before = jnp.dot(tri_ref[...], sel_f.astype(BF16), preferred_element_type=F32) + carry
    r1 = jnp.sum(jnp.where(lane == i1, before, 0.0), axis=-1, keepdims=True)
    r2 = jnp.sum(jnp.where(lane == i2, before, 0.0), axis=-1, keepdims=True)
    carry = carry + jnp.sum(sel_f, axis=0, keepdims=True)
    carry_ref[...] = jnp.broadcast_to(carry, carry_ref.shape)
    cnt_ref[...] = jnp.broadcast_to(carry, cnt_ref.shape)

    route = jnp.zeros(logits.shape, F32)
    for k, val in ((R_IDX0, i1.astype(F32)), (R_IDX1, i2.astype(F32)), (R_GATE0, g1),
                   (R_GATE1, g2), (R_RANK0, r1), (R_RANK1, r2)):
        route = jnp.where(lane == k, val, route)
    route_ref[...] = route
    route_t_ref[...] = route.T[0:route_t_ref.shape[0], :]


def _router(h, g, router_w):
    n, d = h.shape
    rows = _row_block(n, ROUTER_ROWS)
    parts = d // 2 // SC_PIECE
    rw = jnp.zeros((d, LANES), BF16).at[:, 0:N_EXPERTS].set(router_w.astype(BF16))
    return pl.pallas_call(
        _router_kernel,
        grid=(n // rows,),
        in_specs=[
            pl.BlockSpec((rows, d), lambda i: (i, 0)),
            _full((1, d)),
            _full((d, LANES)),
        ],
        out_specs=[
            pl.BlockSpec((parts, rows, SC_PIECE), lambda i: (0, i, 0)),
            pl.BlockSpec((rows, LANES), lambda i: (i, 0)),
            pl.BlockSpec((8, rows), lambda i: (0, i)),
            _full((8, LANES)),
        ],
        out_shape=[
            jax.ShapeDtypeStruct((parts, n, SC_PIECE), jnp.uint32),
            jax.ShapeDtypeStruct((n, LANES), F32),
            jax.ShapeDtypeStruct((8, n), F32),
            jax.ShapeDtypeStruct((8, LANES), F32),
        ],
        scratch_shapes=[pltpu.VMEM((rows, rows), BF16), pltpu.VMEM((8, LANES), F32)],
        compiler_params=_params(("arbitrary",)),
        name="moe_router",
    )(h, g.reshape(1, d), rw)


def _sc_mesh():
    return plsc.VectorSubcoreMesh(core_axis_name="core", subcore_axis_name="subcore")


def _gather_pieces(src, idx):
    m = idx.shape[0]
    width = src.shape[1]
    assert m % (SC_WINDOW * SC_WORKERS) == 0

    @functools.partial(pl.kernel, out_type=jax.ShapeDtypeStruct((m, width), src.dtype),
                       mesh=_sc_mesh(), scratch_types=[])
    def gather_kernel(src_hbm, idx_hbm, out_hbm):
        def body(idx_vmem, out_vmem):
            pltpu.sync_copy(src_hbm.at[idx_vmem.at[0]], out_vmem)

        pltpu.emit_pipeline(
            body,
            grid=(m // SC_WINDOW,),
            in_specs=[pl.BlockSpec((1, SC_WINDOW), lambda i: (0, i))],
            out_specs=[pl.BlockSpec((SC_WINDOW, width), lambda i: (i, 0))],
            core_axis_name=("core", "subcore"),
            dimension_semantics=(pltpu.PARALLEL,),
        )(idx_hbm, out_hbm)

    return gather_kernel(src, idx.reshape(1, m))


def _scatter_pieces(src, idx, out_rows):
    m = idx.shape[0]
    width = src.shape[1]
    src_windows = src.shape[0] // SC_WINDOW
    assert m % (SC_WINDOW * SC_WORKERS) == 0 and src.shape[0] % SC_WINDOW == 0

    @functools.partial(pl.kernel, out_type=jax.ShapeDtypeStruct((out_rows, width), src.dtype),
                       mesh=_sc_mesh(), scratch_types=[])
    def scatter_kernel(src_hbm, idx_hbm, out_hbm):
        def body(src_vmem, idx_vmem):
            pltpu.sync_copy(src_vmem, out_hbm.at[idx_vmem.at[0]])

        pltpu.emit_pipeline(
            body,
            grid=(m // SC_WINDOW,),
            in_specs=[pl.BlockSpec((SC_WINDOW, width), lambda i: (i % src_windows, 0)),
                      pl.BlockSpec((1, SC_WINDOW), lambda i: (0, i))],
            out_specs=[],
            core_axis_name=("core", "subcore"),
            dimension_semantics=(pltpu.PARALLEL,),
        )(src_hbm, idx_hbm)

    return scatter_kernel(src, idx.reshape(1, m))


def _moe_combined(h_ref, y_ref, route_ref):
    parts, _, _, piece = y_ref.shape
    route = route_ref[...]
    g0 = route[:, R_GATE0:R_GATE0 + 1]
    g1 = route[:, R_GATE1:R_GATE1 + 1]
    lo, hi = [], []
    for p in range(parts):
        lo0, hi0 = _unpack_bf16_pair(y_ref[p, 0])
        lo1, hi1 = _unpack_bf16_pair(y_ref[p, 1])
        lo.append(g0 * lo0 + g1 * lo1)
        hi.append(g0 * hi0 + g1 * hi1)
    return h_ref[...] + jnp.concatenate(lo + hi, axis=1)


def _moe_specs(pending, rows, base):
    y_pairs, route = pending
    parts, _, _, piece = y_pairs.shape
    return [pl.BlockSpec((parts, 2, rows, piece), lambda i: (0, 0, i, 0)),
            pl.BlockSpec((rows, LANES), lambda i: (i + base, 0))]


def _final_kernel(h_ref, y_ref, route_ref, g_ref, o_ref):
    o_ref[...] = _rms(_moe_combined(h_ref, y_ref, route_ref), g_ref[...])


def _final(h, pending, final_g):
    n, d = h.shape
    y_chunks, route = pending
    n_chunks = len(y_chunks)
    rows = _row_block(n // n_chunks, PROJ_ROWS)
    steps = n // rows // n_chunks
    for c, y in enumerate(y_chunks):
        base = c * steps
        h = pl.pallas_call(
            _final_kernel,
            grid=(steps,),
            in_specs=[pl.BlockSpec((rows, d), lambda i, base=base: (i + base, 0))]
            + _moe_specs((y, route), rows, base) + [_full((1, d))],
            out_specs=pl.BlockSpec((rows, d), lambda i, base=base: (i + base, 0)),
            out_shape=jax.ShapeDtypeStruct((n, d), F32),
            input_output_aliases={0: 0} if n_chunks > 1 else {},
            compiler_params=_params(("arbitrary",)),
            name="moe_combine_final_norm",
        )(h, y, route, final_g.reshape(1, d))
    return h


def _moe(h, g, router_w, layer, wg, wu, wd, n_chunks):
    n, d = h.shape
    xpk, route, route_t, cnt = _router(h, g, router_w)
    parts = xpk.shape[0]

    top_idx = route_t[R_IDX0:R_IDX1 + 1].astype(jnp.int32)
    rank = route_t[R_RANK0:R_RANK1 + 1].astype(jnp.int32)
    sizes = cnt[0, 0:N_EXPERTS].astype(jnp.int32)
    padded = ((sizes + MOE_ROWS - 1) // MOE_ROWS) * MOE_ROWS
    pends = jnp.cumsum(padded)
    pstarts = pends - padded
    dest = rank
    for e in range(N_EXPERTS):
        dest = dest + jnp.where(top_idx == e, pstarts[e], 0)
    n_rows = 2 * n + N_EXPERTS * MOE_ROWS
    n_blocks = n_rows // MOE_ROWS
    block_start = jnp.arange(n_blocks, dtype=jnp.int32) * MOE_ROWS
    block_expert = jnp.minimum(
        jnp.sum((block_start[:, None] >= pends[None, :]).astype(jnp.int32), axis=1),
        N_EXPERTS - 1)
    n_used = (pends[N_EXPERTS - 1:] // MOE_ROWS).astype(jnp.int32)

    off = jnp.arange(parts, dtype=jnp.int32) * n_rows
    scatter_idx = (dest[:, None, :] + off[None, :, None]).reshape(-1)
    x_sorted = _scatter_pieces(xpk.reshape(parts * n, SC_PIECE), scatter_idx, parts * n_rows)
    y_rows = _moe_ffn(x_sorted.reshape(parts, n_rows, SC_PIECE), block_expert, n_used, layer,
                      wg, wu, wd)

    y_flat = y_rows.reshape(parts * n_rows, SC_PIECE)
    nc = n // n_chunks
    chunk_dest = dest.reshape(2, n_chunks, nc).transpose(1, 0, 2)
    gather_idx = (chunk_dest[:, None] + off[None, :, None, None]).reshape(n_chunks, -1)
    y_chunks = [_gather_pieces(y_flat, gather_idx[c]).reshape(parts, 2, nc, SC_PIECE)
                for c in range(n_chunks)]
    return y_chunks, route


def kernel(x, positions, final_norm_g, ev_norm1_g, ev_w_in, ev_conv_w, ev_ln_g, ev_ln_b, ev_spatial_w, ev_spatial_b, ev_w_out, ev_norm2_g, ev_ffn_wg, ev_ffn_wu, ev_ffn_wd, od_norm1_g, od_w_qkv, od_b_qkv, od_sinks, od_w_o, od_b_o, od_norm2_g, od_router_w, od_exp_wg, od_exp_wu, od_exp_wd):
    batch, seq, d = x.shape
    depth = ev_norm1_g.shape[0] + od_norm1_g.shape[0]
    assert depth % 2 == 0, "the final norm is fused into the last (odd) layer's MoE combine"
    n_q_heads = od_sinks.shape[1]
    h = x.reshape(batch * seq, d)
    experts = [od_exp_wg, od_exp_wu, od_exp_wd]
    ffn_steps = (batch * seq) // _row_block(batch * seq, FFN_ROWS)
    ride_along = all(_side_cast_ok(w, ffn_steps) for w in experts)
    if not ride_along:
        experts = [_to_bf16(w) for w in experts]
    pending = None
    pieces_per_token = 2 * (d // 2 // SC_PIECE)
    n_chunks = max(c for c in range(1, COMBINE_CHUNKS + 1)
                   if batch % c == 0
                   and (batch // c * seq * pieces_per_token) % (SC_WINDOW * SC_WORKERS) == 0)
    for layer in range(depth):
        i = layer // 2
        if layer % 2 == 0:
            h = _mixer(h, pending, seq, ev_norm1_g[i], ev_w_in[i], ev_conv_w[i], ev_ln_g[i],
                       ev_ln_b[i], ev_spatial_w[i], ev_spatial_b[i], ev_w_out[i])
            side = experts if (ride_along and layer == 0) else ()
            h, cast = _dense_ffn(h, ev_norm2_g[i], ev_ffn_wg[i], ev_ffn_wu[i], ev_ffn_wd[i], side)
            if side:
                experts = cast
        else:
            q, kv = _qkv(h, od_norm1_g[i], positions, od_w_qkv[i], od_b_qkv[i],
                         n_q_heads * HEAD_DIM)
            h = _attention(h, q, kv, od_sinks[i], od_w_o[i], od_b_o[i], batch, seq)
            pending = _moe(h, od_norm2_g[i], od_router_w[i], i, *experts, n_chunks)
    return _final(h, pending, final_norm_g).reshape(batch, seq, d)
```

```python
import functools

import jax
import jax.numpy as jnp
import numpy as np
from jax import lax
from jax.experimental import pallas as pl
from jax.experimental.pallas import tpu as pltpu
from jax.experimental.pallas import tpu_sc as plsc

F32 = jnp.float32
BF16 = jnp.bfloat16

EPS = 1e-5
CHUNK = 128
GMLP_HEADS = 4
CONV_WIDTH = 3
HEAD_DIM = 64
WINDOW = 128
ROPE_DIM = HEAD_DIM // 4
ROPE_THETA = 500000.0
ATTN_SCALE = HEAD_DIM ** -0.5
LOG2_E = float(np.log2(np.e))
Q_SCALE = ATTN_SCALE * LOG2_E
N_EXPERTS = 8
LANES = 128
VMEM_LIMIT = 56 * 1024 * 1024

MIXER_ROWS = 512
FFN_ROWS = 512
CAST_BLOCK_BYTES = 8 * 1024 * 1024
PROJ_ROWS = 1024
ATTN_ROWS = 1024
ATTN_STACK = 2
ROUTER_ROWS = 1024
MOE_ROWS = 1024
COMBINE_CHUNKS = 4
MOE_COL_SPLIT = 2
SC_WORKERS = 32
SC_PIECE = 256
SC_WINDOW = 128


def _row_block(n, pref):
    b = min(n, pref)
    while n % b:
        b -= LANES
    return b


def _col_block(f, pref):
    b = min(f, pref)
    b -= b % LANES
    while f % b:
        b -= LANES
    return b


def _params(sem):
    return pltpu.CompilerParams(dimension_semantics=sem, vmem_limit_bytes=VMEM_LIMIT)


def _rms(x, g):
    return x * lax.rsqrt(jnp.mean(x * x, axis=-1, keepdims=True) + EPS) * g


def _gelu(x):
    return 0.5 * x * (1.0 + lax.erf(x * np.float32(np.sqrt(0.5))))


def _full(shape):
    return pl.BlockSpec(shape, lambda *_: (0,) * len(shape))


def _mixer_kernel(*refs, blocks_per_seq, has_pending):
    if has_pending:
        h_ref, y_ref, route_ref = refs[:3]
        refs = refs[3:]
    else:
        h_ref = refs[0]
        refs = refs[1:]
    (g1_ref, win_ref, cw_ref, lng_ref, lnb_ref, ws_ref, bst_ref, wout_ref, o_ref,
     tail_ref, yb_ref) = refs
    rows = h_ref.shape[0]
    cd = cw_ref.shape[1]
    gd = lng_ref.shape[1]
    hd = gd // GMLP_HEADS
    i = pl.program_id(0)

    x = _moe_combined(h_ref, y_ref, route_ref) if has_pending else h_ref[...]
    xn = _rms(x, g1_ref[...]).astype(BF16)
    z = jnp.dot(xn, win_ref[...], preferred_element_type=F32)
    a_b = z[:, 0:cd]
    a_c = z[:, cd:2 * cd]
    a_x = z[:, 2 * cd:3 * cd]
    b_u = z[:, 3 * cd:3 * cd + gd]
    b_v = z[:, 3 * cd + gd:3 * cd + 2 * gd]

    g = a_c * a_x

    @pl.when(i % blocks_per_seq == 0)
    def _():
        tail_ref[...] = jnp.zeros_like(tail_ref)

    tail = tail_ref[...]
    row = lax.broadcasted_iota(jnp.int32, g.shape, 0)
    gm1 = jnp.where(row == 0, tail[7:8], pltpu.roll(g, 1, 0))
    gm2 = jnp.where(row == 0, tail[6:7], jnp.where(row == 1, tail[7:8], pltpu.roll(g, 2, 0)))
    tail_ref[...] = g[rows - 8:rows]
    cw = cw_ref[...]
    y_a = a_b * (gm2 * cw[0:1] + gm1 * cw[1:2] + g * cw[2:3])

    u = _gelu(b_u)
    v = _gelu(b_v)
    mu = jnp.mean(v, axis=-1, keepdims=True)
    vc = v - mu
    var = jnp.mean(vc * vc, axis=-1, keepdims=True)
    vn = (vc * lax.rsqrt(var + EPS) * lng_ref[...] + lnb_ref[...]).astype(BF16)
    ri = lax.broadcasted_iota(jnp.int32, (CHUNK, CHUNK), 0)
    ci = lax.broadcasted_iota(jnp.int32, (CHUNK, CHUNK), 1)
    causal = ri >= ci
    bst = bst_ref[...]
    for k in range(GMLP_HEADS):
        w_k = jnp.where(causal, ws_ref[k], 0.0).astype(BF16)
        b_k = bst[:, k:k + 1]
        for c in range(rows // CHUNK):
            rs = slice(c * CHUNK, (c + 1) * CHUNK)
            cs = slice(k * hd, (k + 1) * hd)
            mixed = jnp.dot(w_k, vn[rs, cs], preferred_element_type=F32) + b_k
            yb_ref[rs, cs] = (u[rs, cs] * mixed).astype(BF16)

    out = jnp.dot(y_a.astype(BF16), wout_ref[0:cd, :], preferred_element_type=F32)
    out = out + jnp.dot(yb_ref[...], wout_ref[cd:cd + gd, :], preferred_element_type=F32)
    o_ref[...] = x + out


def _mixer(h, pending, seq, *weights):
    if pending is None:
        return _mixer_call(h, None, 0, 1, seq, *weights)
    y_chunks, route = pending
    for c, y in enumerate(y_chunks):
        h = _mixer_call(h, (y, route), c, len(y_chunks), seq, *weights)
    return h


def _mixer_call(h, pending, chunk, n_chunks, seq, g1, w_in, conv_w, ln_g, ln_b, w_s, b_s, w_out):
    n, d = h.shape
    rows = _row_block(seq, MIXER_ROWS)
    cd = conv_w.shape[0]
    gd = ln_g.shape[0]
    has_pending = pending is not None
    steps = n // rows // n_chunks
    base = chunk * steps
    assert (steps * rows) % seq == 0
    kern = functools.partial(_mixer_kernel, blocks_per_seq=seq // rows, has_pending=has_pending)
    return pl.pallas_call(
        kern,
        grid=(steps,),
        in_specs=[pl.BlockSpec((rows, d), lambda i: (i + base, 0))]
        + (_moe_specs(pending, rows, base) if has_pending else [])
        + [
            _full((1, d)),
            _full(w_in.shape),
            _full((CONV_WIDTH, cd)),
            _full((1, gd)),
            _full((1, gd)),
            _full(w_s.shape),
            _full((CHUNK, GMLP_HEADS)),
            _full(w_out.shape),
        ],
        out_specs=pl.BlockSpec((rows, d), lambda i: (i + base, 0)),
        out_shape=jax.ShapeDtypeStruct((n, d), F32),
        scratch_shapes=[pltpu.VMEM((8, cd), F32), pltpu.VMEM((rows, gd), BF16)],
        input_output_aliases={0: 0} if n_chunks > 1 else {},
        compiler_params=_params(("arbitrary",)),
        name="mixer",
    )(h, *(pending or ()), g1.reshape(1, d), w_in.astype(BF16), conv_w.T, ln_g.reshape(1, gd),
      ln_b.reshape(1, gd), w_s, b_s.T, w_out.astype(BF16))


def _swiglu(xn, wg, wu, wd):
    h1 = jnp.dot(xn, wg, preferred_element_type=F32)
    h2 = jnp.dot(xn, wu, preferred_element_type=F32)
    a = (h1 / (1.0 + jnp.exp(-h1)) * h2).astype(BF16)
    return jnp.dot(a, wd, preferred_element_type=F32)


def _dense_ffn_kernel(x_ref, g_ref, wg_ref, wu_ref, wd_ref, *rest):
    n_side = (len(rest) - 1) // 2
    side_in, o_ref, side_out = rest[:n_side], rest[n_side], rest[n_side + 1:]
    x = x_ref[...]
    xn = _rms(x, g_ref[...]).astype(BF16)
    o_ref[...] = x + _swiglu(xn, wg_ref[...], wu_ref[...], wd_ref[...])
    for src, dst in zip(side_in, side_out):
        dst[...] = src[...].astype(dst.dtype)


def _side_cast_ok(w, steps):
    rows = int(np.prod(w.shape[:-1]))
    return rows % steps == 0 and (rows // steps) % 16 == 0 and w.shape[-1] % LANES == 0


def _dense_ffn(h, g, wg, wu, wd, side=()):
    n, d = h.shape
    rows = _row_block(n, FFN_ROWS)
    steps = n // rows
    side2d = [w.reshape(-1, w.shape[-1]) for w in side]
    side_specs = [pl.BlockSpec((w.shape[0] // steps, w.shape[1]), lambda i: (i, 0)) for w in side2d]
    outs = pl.pallas_call(
        _dense_ffn_kernel,
        grid=(steps,),
        in_specs=[
            pl.BlockSpec((rows, d), lambda i: (i, 0)),
            _full((1, d)),
            _full(wg.shape),
            _full(wu.shape),
            _full(wd.shape),
        ] + side_specs,
        out_specs=[pl.BlockSpec((rows, d), lambda i: (i, 0))] + side_specs,
        out_shape=[jax.ShapeDtypeStruct((n, d), F32)]
        + [jax.ShapeDtypeStruct(w.shape, BF16) for w in side2d],
        compiler_params=_params(("arbitrary",)),
        name="dense_ffn",
    )(h, g.reshape(1, d), wg.astype(BF16), wu.astype(BF16), wd.astype(BF16), *side2d)
    return outs[0], [o.reshape(w.shape) for o, w in zip(outs[1:], side)]


def _pack_bf16_pairs(x, piece):
    half = x.shape[1] // 2
    bits = lax.bitcast_convert_type(x.astype(BF16).astype(F32), jnp.uint32)
    return [(bits[:, half + p * piece:half + (p + 1) * piece] & jnp.uint32(0xFFFF0000))
            | (bits[:, p * piece:(p + 1) * piece] >> 16) for p in range(half // piece)]


def _unpack_bf16_pair(packed):
    lo = lax.bitcast_convert_type(packed << 16, F32)
    hi = lax.bitcast_convert_type(packed & jnp.uint32(0xFFFF0000), F32)
    return lo, hi


def _moe_ffn_kernel(be_ref, used_ref, x_ref, wg_ref, wu_ref, wd_ref, o_ref, xn_ref, acc_ref, *,
                    n_steps):
    del be_ref
    i = pl.program_id(0)
    f = pl.program_id(1)
    parts, _, piece = x_ref.shape
    half = parts * piece
    active = i < used_ref[0]

    def write_out(val):
        for p, words in enumerate(_pack_bf16_pairs(val, piece)):
            o_ref[p] = words

    def step(first, last):
        if first:
            for p in range(parts):
                lo, hi = _unpack_bf16_pair(x_ref[p])
                xn_ref[:, p * piece:(p + 1) * piece] = lo.astype(BF16)
                xn_ref[:, half + p * piece:half + (p + 1) * piece] = hi.astype(BF16)
        part = _swiglu(xn_ref[...], wg_ref[0, 0], wu_ref[0, 0], wd_ref[0, 0])
        if last:
            write_out(part if first else acc_ref[...] + part)
        elif first:
            acc_ref[...] = part
        else:
            acc_ref[...] += part

    if n_steps == 1:
        pl.when(active)(functools.partial(step, True, True))
    else:
        pl.when(active & (f == 0))(functools.partial(step, True, False))
        if n_steps > 2:
            pl.when(active & (f > 0) & (f < n_steps - 1))(functools.partial(step, False, False))
        pl.when(active & (f == n_steps - 1))(functools.partial(step, False, True))

    @pl.when(jnp.logical_not(active) & (f == n_steps - 1))
    def _():
        o_ref[...] = jnp.zeros_like(o_ref)


def _moe_ffn(x_sorted, block_expert, n_used, layer, wg, wu, wd):
    parts, n_rows, piece = x_sorted.shape
    d = 2 * parts * piece
    fdim = wg.shape[3]
    rows = MOE_ROWS
    cols = _col_block(fdim, fdim // MOE_COL_SPLIT)
    n_steps = fdim // cols

    def col(i, f, used):
        return jnp.where(i < used[0], f, n_steps - 1)

    grid_spec = pltpu.PrefetchScalarGridSpec(
        num_scalar_prefetch=2,
        grid=(n_rows // rows, n_steps),
        in_specs=[
            pl.BlockSpec((parts, rows, piece),
                         lambda i, f, be, used: (0, jnp.minimum(i, used[0] - 1), 0)),
            pl.BlockSpec((1, 1, d, cols), lambda i, f, be, used: (layer, be[i], 0, col(i, f, used))),
            pl.BlockSpec((1, 1, d, cols), lambda i, f, be, used: (layer, be[i], 0, col(i, f, used))),
            pl.BlockSpec((1, 1, cols, d), lambda i, f, be, used: (layer, be[i], col(i, f, used), 0)),
        ],
        out_specs=pl.BlockSpec((parts, rows, piece), lambda i, f, be, used: (0, i, 0)),
        scratch_shapes=[pltpu.VMEM((rows, d), BF16), pltpu.VMEM((rows, d), F32)],
    )
    kern = functools.partial(_moe_ffn_kernel, n_steps=n_steps)
    return pl.pallas_call(
        kern,
        grid_spec=grid_spec,
        out_shape=jax.ShapeDtypeStruct((parts, n_rows, piece), jnp.uint32),
        compiler_params=_params(("arbitrary", "arbitrary")),
        name="moe_ffn",
    )(block_expert, n_used, x_sorted, wg, wu, wd)


def _cast_kernel(x_ref, o_ref):
    o_ref[...] = x_ref[...].astype(o_ref.dtype)


def _to_bf16(w):
    shape = w.shape
    w2 = w.reshape(-1, shape[-1])
    pref = CAST_BLOCK_BYTES // (4 * shape[-1]) // LANES * LANES
    rows = _row_block(w2.shape[0], pref)
    out = pl.pallas_call(
        _cast_kernel,
        grid=(w2.shape[0] // rows,),
        in_specs=[pl.BlockSpec((rows, shape[-1]), lambda i: (i, 0))],
        out_specs=pl.BlockSpec((rows, shape[-1]), lambda i: (i, 0)),
        out_shape=jax.ShapeDtypeStruct(w2.shape, BF16),
        compiler_params=_params(("arbitrary",)),
        name="cast_bf16",
    )(w2)
    return out.reshape(shape)


def _qkv_kernel(h_ref, g_ref, pos_ref, w_ref, b_ref, invf_ref, mc_ref, q_ref, kv_ref, *, q_dim):
    x = h_ref[...]
    xn = _rms(x, g_ref[...]).astype(BF16)
    z = jnp.dot(xn, w_ref[...], preferred_element_type=F32) + b_ref[...]
    ang = pos_ref[...].astype(F32) * invf_ref[...]
    reps = LANES // ang.shape[0]
    cos = jnp.concatenate([jnp.cos(ang)] * reps, axis=0).T
    sin = jnp.concatenate([jnp.sin(ang)] * reps, axis=0).T
    cos = jnp.where(mc_ref[...] != 0.0, cos, 1.0)
    sin = sin * mc_ref[...]

    def rope(t):
        return t * cos + pltpu.roll(t, LANES // 2, 1) * sin

    for j in range(q_dim // LANES):
        cs = slice(j * LANES, (j + 1) * LANES)
        q_ref[:, cs] = (rope(z[:, cs]) * Q_SCALE).astype(BF16)
    kv_ref[:, 0:LANES] = rope(z[:, q_dim:q_dim + LANES]).astype(BF16)
    kv_ref[:, LANES:2 * LANES] = z[:, q_dim + LANES:q_dim + 2 * LANES].astype(BF16)


QK_GROUP = LANES // 4


def _qk_tile_layout():
    lane = np.arange(LANES)
    group, off = lane // QK_GROUP, lane % QK_GROUP
    half = ROPE_DIM // 2
    head = group % 2
    second = group // 2
    rest = QK_GROUP - half
    dim = np.where(off < half, second * half + off, ROPE_DIM + second * rest + (off - half))
    return head, dim


def _qk_column_order(n_cols):
    head, dim = _qk_tile_layout()
    tile = np.arange(n_cols) // LANES
    return tile * LANES + np.tile(head * HEAD_DIM + dim, n_cols // LANES)


def _rope_sign_lanes():
    _, dim = _qk_tile_layout()
    half = ROPE_DIM // 2
    sign = np.where(dim < half, -1.0, np.where(dim < ROPE_DIM, 1.0, 0.0)).astype(np.float32)
    return jnp.asarray(sign[None, :])


def _qkv(h, g, positions, w_qkv, b_qkv, q_dim):
    n, d = h.shape
    qkv_dim = w_qkv.shape[1]
    assert qkv_dim == q_dim + 2 * LANES and QK_GROUP % (ROPE_DIM // 2) == 0
    rows = _row_block(n, PROJ_ROWS)
    inv_freq = ROPE_THETA ** (-jnp.arange(0, ROPE_DIM, 2, dtype=F32) / ROPE_DIM)
    order = np.concatenate([_qk_column_order(q_dim + LANES), np.arange(q_dim + LANES, qkv_dim)])
    w_qkv = w_qkv[:, order]
    b_qkv = b_qkv[order]
    kern = functools.partial(_qkv_kernel, q_dim=q_dim)
    return pl.pallas_call(
        kern,
        grid=(n // rows,),
        in_specs=[
            pl.BlockSpec((rows, d), lambda i: (i, 0)),
            _full((1, d)),
            pl.BlockSpec((1, rows), lambda i: (0, i)),
            _full(w_qkv.shape),
            _full((1, qkv_dim)),
            _full((ROPE_DIM // 2, 1)),
            _full((1, LANES)),
        ],
        out_specs=[
            pl.BlockSpec((rows, q_dim), lambda i: (i, 0)),
            pl.BlockSpec((rows, 2 * LANES), lambda i: (i, 0)),
        ],
        out_shape=[
            jax.ShapeDtypeStruct((n, q_dim), BF16),
            jax.ShapeDtypeStruct((n, 2 * LANES), BF16),
        ],
        compiler_params=_params(("arbitrary",)),
        name="qkv_rope",
    )(h, g.reshape(1, d), positions.reshape(1, n), w_qkv.astype(BF16),
      b_qkv.reshape(1, qkv_dim), inv_freq.reshape(-1, 1), _rope_sign_lanes())


def _attn_kernel(sink_ref, q_ref, kvc_ref, kvp_ref, h_ref, wo_ref, bo_ref, out_ref,
                 kbuf, vbuf, o_buf, *, wpb):
    rows = q_ref.shape[0]
    tiles = q_ref.shape[1] // LANES // 2
    j = pl.program_id(1)
    kbuf[0:WINDOW, :] = kvp_ref[:, 0:LANES]
    kbuf[WINDOW:WINDOW + rows, :] = kvc_ref[:, 0:LANES]
    vbuf[0:WINDOW, :] = kvp_ref[:, LANES:2 * LANES]
    vbuf[WINDOW:WINDOW + rows, :] = kvc_ref[:, LANES:2 * LANES]

    from_prev = (lax.broadcasted_iota(jnp.int32, (WINDOW, WINDOW), 1)
                 > lax.broadcasted_iota(jnp.int32, (WINDOW, WINDOW), 0))
    kv_lane = lax.broadcasted_iota(jnp.int32, (2 * WINDOW, LANES), 1)
    first_half = kv_lane < HEAD_DIM
    head0_lanes = (kv_lane // QK_GROUP) % 2 == 0
    out_first_half = lax.broadcasted_iota(jnp.int32, (WINDOW, LANES), 1) < HEAD_DIM
    nt = (((1,), (1,)), ((), ()))

    def window(n, carry):
        r0 = pl.multiple_of(n * WINDOW, WINDOW)
        kt = kbuf[pl.ds(r0, 2 * WINDOW), :].astype(F32)
        vt = vbuf[pl.ds(r0, 2 * WINDOW), :].astype(F32)
        prev_bias = jnp.where(j * wpb + n > 0, 0.0, -jnp.inf)
        k0_lo = jnp.where(head0_lanes, kt, 0.0)
        k1_hi = jnp.where(head0_lanes, 0.0, kt)
        v0_lo = jnp.where(first_half, vt, 1.0)
        v1_hi = jnp.where(first_half, 1.0, vt)
        k_even = (k0_lo.astype(BF16), pltpu.roll(k1_hi, LANES - QK_GROUP, 1).astype(BF16))
        k_odd = (pltpu.roll(k0_lo, QK_GROUP, 1).astype(BF16), k1_hi.astype(BF16))
        v_even = (v0_lo.astype(BF16), pltpu.roll(v1_hi, HEAD_DIM, 1).astype(BF16))
        v_odd = (pltpu.roll(v0_lo, HEAD_DIM, 1).astype(BF16), v1_hi.astype(BF16))
        for kh in range(2):
            for t0 in range(0, tiles, ATTN_STACK):
                group = [kh * tiles + t0 + u for u in range(ATTN_STACK)]
                q_stack = jnp.concatenate(
                    [q_ref[pl.ds(r0, WINDOW), t * LANES:(t + 1) * LANES] for t in group], axis=0)
                outs = []
                for parity, k_rhs, v_rhs in ((0, k_even[kh], v_even[kh]), (1, k_odd[kh], v_odd[kh])):
                    s_all = lax.dot_general(q_stack, k_rhs, nt, preferred_element_type=F32)
                    p_tiles, corr = [], []
                    for u, t in enumerate(group):
                        sink = sink_ref[t * 2 + parity]
                        s_prev = s_all[u * WINDOW:(u + 1) * WINDOW, 0:WINDOW] + prev_bias
                        s_cur = s_all[u * WINDOW:(u + 1) * WINDOW, WINDOW:2 * WINDOW]
                        s = jnp.where(from_prev, s_prev, s_cur)
                        m = jnp.maximum(jnp.max(s, axis=-1, keepdims=True), sink)
                        p = jnp.exp2(s - m)
                        p_tiles.append(jnp.concatenate(
                            [jnp.where(from_prev, p, 0.0), jnp.where(from_prev, 0.0, p)],
                            axis=1).astype(BF16))
                        corr.append(jnp.exp2(sink - m))
                    pv = jnp.dot(jnp.concatenate(p_tiles, axis=0), v_rhs,
                                 preferred_element_type=F32)
                    outs.append((pv, corr))
                for u, t in enumerate(group):
                    pv_e = outs[0][0][u * WINDOW:(u + 1) * WINDOW]
                    pv_o = outs[1][0][u * WINDOW:(u + 1) * WINDOW]
                    num = jnp.where(out_first_half, pv_e, pv_o)
                    sums = pltpu.roll(jnp.where(out_first_half, pv_o, pv_e), HEAD_DIM, 1)
                    den = sums + jnp.where(out_first_half, outs[0][1][u], outs[1][1][u])
                    o_buf[pl.ds(r0, WINDOW), t * LANES:(t + 1) * LANES] = (num / den).astype(BF16)
        return carry

    lax.fori_loop(0, rows // WINDOW, window, 0, unroll=2)
    out_ref[...] = (h_ref[...] + jnp.dot(o_buf[...], wo_ref[...], preferred_element_type=F32)
                    + bo_ref[...])


def _attention(h, q, kv, sinks, w_o, b_o, batch, seq):
    n, q_dim = q.shape
    d = h.shape[1]
    assert kv.shape[1] == 2 * LANES and (q_dim // HEAD_DIM) % 4 == 0
    rows = _row_block(seq, ATTN_ROWS)
    bps = seq // rows
    wpb = rows // WINDOW
    wps = seq // WINDOW

    grid_spec = pltpu.PrefetchScalarGridSpec(
        num_scalar_prefetch=1,
        grid=(batch, bps),
        in_specs=[
            pl.BlockSpec((rows, q_dim), lambda b, j, s: (b * bps + j, 0)),
            pl.BlockSpec((rows, 2 * LANES), lambda b, j, s: (b * bps + j, 0)),
            pl.BlockSpec((WINDOW, 2 * LANES),
                         lambda b, j, s: (b * wps + jnp.maximum(j * wpb - 1, 0), 0)),
            pl.BlockSpec((rows, d), lambda b, j, s: (b * bps + j, 0)),
            pl.BlockSpec((q_dim, d), lambda b, j, s: (0, 0)),
            pl.BlockSpec((1, d), lambda b, j, s: (0, 0)),
        ],
        out_specs=pl.BlockSpec((rows, d), lambda b, j, s: (b * bps + j, 0)),
        scratch_shapes=[pltpu.VMEM((rows + WINDOW, LANES), BF16),
                        pltpu.VMEM((rows + WINDOW, LANES), BF16),
                        pltpu.VMEM((rows, q_dim), BF16)],
    )
    kern = functools.partial(_attn_kernel, wpb=wpb)
    return pl.pallas_call(
        kern,
        grid_spec=grid_spec,
        out_shape=jax.ShapeDtypeStruct((n, d), F32),
        compiler_params=_params(("arbitrary", "arbitrary")),
        name="swa_attention",
    )(sinks.astype(F32) * LOG2_E, q, kv, kv, h, w_o.astype(BF16), b_o.reshape(1, d))


R_IDX0, R_IDX1, R_GATE0, R_GATE1, R_RANK0, R_RANK1 = range(6)


def _router_kernel(h_ref, g_ref, rw_ref, xpk_ref, route_ref, route_t_ref, cnt_ref, tri_ref,
                   carry_ref):
    rows = h_ref.shape[0]
    i = pl.program_id(0)

    @pl.when(i == 0)
    def _():
        r = lax.broadcasted_iota(jnp.int32, (rows, rows), 0)
        c = lax.broadcasted_iota(jnp.int32, (rows, rows), 1)
        tri_ref[...] = jnp.where(r > c, 1.0, 0.0).astype(BF16)
        carry_ref[...] = jnp.zeros_like(carry_ref)

    xn = _rms(h_ref[...], g_ref[...])
    xb = xn.astype(BF16)
    for p, words in enumerate(_pack_bf16_pairs(xn, xpk_ref.shape[2])):
        xpk_ref[p] = words

    logits = jnp.dot(xb, rw_ref[...], preferred_element_type=F32)
    lane = lax.broadcasted_iota(jnp.int32, logits.shape, 1)
    lg = jnp.where(lane < N_EXPERTS, logits, -jnp.inf)
    m1 = jnp.max(lg, axis=-1, keepdims=True)
    i1 = jnp.min(jnp.where(lg == m1, lane, LANES), axis=-1, keepdims=True)
    lg2 = jnp.where(lane == i1, -jnp.inf, lg)
    m2 = jnp.max(lg2, axis=-1, keepdims=True)
    i2 = jnp.min(jnp.where(lg2 == m2, lane, LANES), axis=-1, keepdims=True)
    e = jnp.exp(m2 - m1)
    g1 = 1.0 / (1.0 + e)
    g2 = e / (1.0 + e)

    sel = (lane == i1) | (lane == i2)
    sel_f = jnp.where(sel, 1.0, 0.0)
    carry = carry_ref[0:1, :]
    before = jnp.dot(tri_ref[...], sel_f.astype(BF16), preferred_element_type=F32) + carry
    r1 = jnp.sum(jnp.where(lane == i1, before, 0.0), axis=-1, keepdims=True)
    r2 = jnp.sum(jnp.where(lane == i2, before, 0.0), axis=-1, keepdims=True)
    carry = carry + jnp.sum(sel_f, axis=0, keepdims=True)
    carry_ref[...] = jnp.broadcast_to(carry, carry_ref.shape)
    cnt_ref[...] = jnp.broadcast_to(carry, cnt_ref.shape)

    route = jnp.zeros(logits.shape, F32)
    for k, val in ((R_IDX0, i1.astype(F32)), (R_IDX1, i2.astype(F32)), (R_GATE0, g1),
                   (R_GATE1, g2), (R_RANK0, r1), (R_RANK1, r2)):
        route = jnp.where(lane == k, val, route)
    route_ref[...] = route
    route_t_ref[...] = route.T[0:route_t_ref.shape[0], :]


def _router(h, g, router_w):
    n, d = h.shape
    rows = _row_block(n, ROUTER_ROWS)
    parts = d // 2 // SC_PIECE
    rw = jnp.zeros((d, LANES), BF16).at[:, 0:N_EXPERTS].set(router_w.astype(BF16))
    return pl.pallas_call(
        _router_kernel,
        grid=(n // rows,),
        in_specs=[
            pl.BlockSpec((rows, d), lambda i: (i, 0)),
            _full((1, d)),
            _full((d, LANES)),
        ],
        out_specs=[
            pl.BlockSpec((parts, rows, SC_PIECE), lambda i: (0, i, 0)),
            pl.BlockSpec((rows, LANES), lambda i: (i, 0)),
            pl.BlockSpec((8, rows), lambda i: (0, i)),
            _full((8, LANES)),
        ],
        out_shape=[
            jax.ShapeDtypeStruct((parts, n, SC_PIECE), jnp.uint32),
            jax.ShapeDtypeStruct((n, LANES), F32),
            jax.ShapeDtypeStruct((8, n), F32),
            jax.ShapeDtypeStruct((8, LANES), F32),
        ],
        scratch_shapes=[pltpu.VMEM((rows, rows), BF16), pltpu.VMEM((8, LANES), F32)],
        compiler_params=_params(("arbitrary",)),
        name="moe_router",
    )(h, g.reshape(1, d), rw)


def _sc_mesh():
    return plsc.VectorSubcoreMesh(core_axis_name="core", subcore_axis_name="subcore")


def _gather_pieces(src, idx):
    m = idx.shape[0]
    width = src.shape[1]
    assert m % (SC_WINDOW * SC_WORKERS) == 0

    @functools.partial(pl.kernel, out_type=jax.ShapeDtypeStruct((m, width), src.dtype),
                       mesh=_sc_mesh(), scratch_types=[])
    def gather_kernel(src_hbm, idx_hbm, out_hbm):
        def body(idx_vmem, out_vmem):
            pltpu.sync_copy(src_hbm.at[idx_vmem.at[0]], out_vmem)

        pltpu.emit_pipeline(
            body,
            grid=(m // SC_WINDOW,),
            in_specs=[pl.BlockSpec((1, SC_WINDOW), lambda i: (0, i))],
            out_specs=[pl.BlockSpec((SC_WINDOW, width), lambda i: (i, 0))],
            core_axis_name=("core", "subcore"),
            dimension_semantics=(pltpu.PARALLEL,),
        )(idx_hbm, out_hbm)

    return gather_kernel(src, idx.reshape(1, m))


def _scatter_pieces(src, idx, out_rows):
    m = idx.shape[0]
    width = src.shape[1]
    src_windows = src.shape[0] // SC_WINDOW
    assert m % (SC_WINDOW * SC_WORKERS) == 0 and src.shape[0] % SC_WINDOW == 0

    @functools.partial(pl.kernel, out_type=jax.ShapeDtypeStruct((out_rows, width), src.dtype),
                       mesh=_sc_mesh(), scratch_types=[])
    def scatter_kernel(src_hbm, idx_hbm, out_hbm):
        def body(src_vmem, idx_vmem):
            pltpu.sync_copy(src_vmem, out_hbm.at[idx_vmem.at[0]])

        pltpu.emit_pipeline(
            body,
            grid=(m // SC_WINDOW,),
            in_specs=[pl.BlockSpec((SC_WINDOW, width), lambda i: (i % src_windows, 0)),
                      pl.BlockSpec((1, SC_WINDOW), lambda i: (0, i))],
            out_specs=[],
            core_axis_name=("core", "subcore"),
            dimension_semantics=(pltpu.PARALLEL,),
        )(src_hbm, idx_hbm)

    return scatter_kernel(src, idx.reshape(1, m))


def _moe_combined(h_ref, y_ref, route_ref):
    parts, _, _, piece = y_ref.shape
    route = route_ref[...]
    g0 = route[:, R_GATE0:R_GATE0 + 1]
    g1 = route[:, R_GATE1:R_GATE1 + 1]
    lo, hi = [], []
    for p in range(parts):
        lo0, hi0 = _unpack_bf16_pair(y_ref[p, 0])
        lo1, hi1 = _unpack_bf16_pair(y_ref[p, 1])
        lo.append(g0 * lo0 + g1 * lo1)
        hi.append(g0 * hi0 + g1 * hi1)
    return h_ref[...] + jnp.concatenate(lo + hi, axis=1)


def _moe_specs(pending, rows, base):
    y_pairs, route = pending
    parts, _, _, piece = y_pairs.shape
    return [pl.BlockSpec((parts, 2, rows, piece), lambda i: (0, 0, i, 0)),
            pl.BlockSpec((rows, LANES), lambda i: (i + base, 0))]


def _final_kernel(h_ref, y_ref, route_ref, g_ref, o_ref):
    o_ref[...] = _rms(_moe_combined(h_ref, y_ref, route_ref), g_ref[...])


def _final(h, pending, final_g):
    n, d = h.shape
    y_chunks, route = pending
    n_chunks = len(y_chunks)
    rows = _row_block(n // n_chunks, PROJ_ROWS)
    steps = n // rows // n_chunks
    for c, y in enumerate(y_chunks):
        base = c * steps
        h = pl.pallas_call(
            _final_kernel,
            grid=(steps,),
            in_specs=[pl.BlockSpec((rows, d), lambda i, base=base: (i + base, 0))]
            + _moe_specs((y, route), rows, base) + [_full((1, d))],
            out_specs=pl.BlockSpec((rows, d), lambda i, base=base: (i + base, 0)),
            out_shape=jax.ShapeDtypeStruct((n, d), F32),
            input_output_aliases={0: 0} if n_chunks > 1 else {},
            compiler_params=_params(("arbitrary",)),
            name="moe_combine_final_norm",
        )(h, y, route, final_g.reshape(1, d))
    return h


def _moe(h, g, router_w, layer, wg, wu, wd, n_chunks):
    n, d = h.shape
    xpk, route, route_t, cnt = _router(h, g, router_w)
    parts = xpk.shape[0]

    top_idx = route_t[R_IDX0:R_IDX1 + 1].astype(jnp.int32)
    rank = route_t[R_RANK0:R_RANK1 + 1].astype(jnp.int32)
    sizes = cnt[0, 0:N_EXPERTS].astype(jnp.int32)
    padded = ((sizes + MOE_ROWS - 1) // MOE_ROWS) * MOE_ROWS
    pends = jnp.cumsum(padded)
    pstarts = pends - padded
    dest = rank
    for e in range(N_EXPERTS):
        dest = dest + jnp.where(top_idx == e, pstarts[e], 0)
    n_rows = 2 * n + N_EXPERTS * MOE_ROWS
    n_blocks = n_rows // MOE_ROWS
    block_start = jnp.arange(n_blocks, dtype=jnp.int32) * MOE_ROWS
    block_expert = jnp.minimum(
        jnp.sum((block_start[:, None] >= pends[None, :]).astype(jnp.int32), axis=1),
        N_EXPERTS - 1)
    n_used = (pends[N_EXPERTS - 1:] // MOE_ROWS).astype(jnp.int32)

    off = jnp.arange(parts, dtype=jnp.int32) * n_rows
    scatter_idx = (dest[:, None, :] + off[None, :, None]).reshape(-1)
    x_sorted = _scatter_pieces(xpk.reshape(parts * n, SC_PIECE), scatter_idx, parts * n_rows)
    y_rows = _moe_ffn(x_sorted.reshape(parts, n_rows, SC_PIECE), block_expert, n_used, layer,
                      wg, wu, wd)

    y_flat = y_rows.reshape(parts * n_rows, SC_PIECE)
    nc = n // n_chunks
    chunk_dest = dest.reshape(2, n_chunks, nc).transpose(1, 0, 2)
    gather_idx = (chunk_dest[:, None] + off[None, :, None, None]).reshape(n_chunks, -1)
    y_chunks = [_gather_pieces(y_flat, gather_idx[c]).reshape(parts, 2, nc, SC_PIECE)
                for c in range(n_chunks)]
    return y_chunks, route


def kernel(x, positions, final_norm_g, ev_norm1_g, ev_w_in, ev_conv_w, ev_ln_g, ev_ln_b, ev_spatial_w, ev_spatial_b, ev_w_out, ev_norm2_g, ev_ffn_wg, ev_ffn_wu, ev_ffn_wd, od_norm1_g, od_w_qkv, od_b_qkv, od_sinks, od_w_o, od_b_o, od_norm2_g, od_router_w, od_exp_wg, od_exp_wu, od_exp_wd):
    batch, seq, d = x.shape
    depth = ev_norm1_g.shape[0] + od_norm1_g.shape[0]
    assert depth % 2 == 0, "the final norm is fused into the last (odd) layer's MoE combine"
    n_q_heads = od_sinks.shape[1]
    h = x.reshape(batch * seq, d)
    experts = [od_exp_wg, od_exp_wu, od_exp_wd]
    ffn_steps = (batch * seq) // _row_block(batch * seq, FFN_ROWS)
    ride_along = all(_side_cast_ok(w, ffn_steps) for w in experts)
    if not ride_along:
        experts = [_to_bf16(w) for w in experts]
    pending = None
    pieces_per_token = 2 * (d // 2 // SC_PIECE)
    n_chunks = max(c for c in range(1, COMBINE_CHUNKS + 1)
                   if batch % c == 0
                   and (batch // c * seq * pieces_per_token) % (SC_WINDOW * SC_WORKERS) == 0)
    for layer in range(depth):
        i = layer // 2
        if layer % 2 == 0:
            h = _mixer(h, pending, seq, ev_norm1_g[i], ev_w_in[i], ev_conv_w[i], ev_ln_g[i],
                       ev_ln_b[i], ev_spatial_w[i], ev_spatial_b[i], ev_w_out[i])
            side = experts if (ride_along and layer == 0) else ()
            h, cast = _dense_ffn(h, ev_norm2_g[i], ev_ffn_wg[i], ev_ffn_wu[i], ev_ffn_wd[i], side)
            if side:
                experts = cast
        else:
            q, kv = _qkv(h, od_norm1_g[i], positions, od_w_qkv[i], od_b_qkv[i],
                         n_q_heads * HEAD_DIM)
            h = _attention(h, q, kv, od_sinks[i], od_w_o[i], od_b_o[i], batch, seq)
            pending = _moe(h, od_norm2_g[i], od_router_w[i], i, *experts, n_chunks)
    return _final(h, pending, final_norm_g).reshape(batch, seq, d)
```

```python
import functools

import jax
import jax.numpy as jnp
import numpy as np
from jax import lax
from jax.experimental import pallas as pl
from jax.experimental.pallas import tpu as pltpu
from jax.experimental.pallas import tpu_sc as plsc

F32 = jnp.float32
BF16 = jnp.bfloat16

EPS = 1e-5
CHUNK = 128
GMLP_HEADS = 4
CONV_WIDTH = 3
HEAD_DIM = 64
WINDOW = 128
ROPE_DIM = HEAD_DIM // 4
ROPE_THETA = 500000.0
ATTN_SCALE = HEAD_DIM ** -0.5
LOG2_E = float(np.log2(np.e))
Q_SCALE = ATTN_SCALE * LOG2_E
N_EXPERTS = 8
LANES = 128
VMEM_LIMIT = 56 * 1024 * 1024

MIXER_ROWS = 512
FFN_ROWS = 512
CAST_BLOCK_BYTES = 8 * 1024 * 1024
PROJ_ROWS = 1024
ATTN_ROWS = 1024
ATTN_STACK = 2
ROUTER_ROWS = 1024
MOE_ROWS = 1024
COMBINE_CHUNKS = 4
MOE_COL_SPLIT = 2
SC_WORKERS = 32
SC_PIECE = 256
SC_WINDOW = 128


def _row_block(n, pref):
    b = min(n, pref)
    while n % b:
        b -= LANES
    return b


def _col_block(f, pref):
    b = min(f, pref)
    b -= b % LANES
    while f % b:
        b -= LANES
    return b


def _params(sem):
    return pltpu.CompilerParams(dimension_semantics=sem, vmem_limit_bytes=VMEM_LIMIT)


def _rms(x, g):
    return x * lax.rsqrt(jnp.mean(x * x, axis=-1, keepdims=True) + EPS) * g


def _gelu(x):
    return 0.5 * x * (1.0 + lax.erf(x * np.float32(np.sqrt(0.5))))


def _full(shape):
    return pl.BlockSpec(shape, lambda *_: (0,) * len(shape))


def _mixer_kernel(*refs, blocks_per_seq, has_pending):
    if has_pending:
        h_ref, y_ref, route_ref = refs[:3]
        refs = refs[3:]
    else:
        h_ref = refs[0]
        refs = refs[1:]
    (g1_ref, win_ref, cw_ref, lng_ref, lnb_ref, ws_ref, bst_ref, wout_ref, o_ref,
     tail_ref, yb_ref) = refs
    rows = h_ref.shape[0]
    cd = cw_ref.shape[1]
    gd = lng_ref.shape[1]
    hd = gd // GMLP_HEADS
    i = pl.program_id(0)

    x = _moe_combined(h_ref, y_ref, route_ref) if has_pending else h_ref[...]
    xn = _rms(x, g1_ref[...]).astype(BF16)
    z = jnp.dot(xn, win_ref[...], preferred_element_type=F32)
    a_b = z[:, 0:cd]
    a_c = z[:, cd:2 * cd]
    a_x = z[:, 2 * cd:3 * cd]
    b_u = z[:, 3 * cd:3 * cd + gd]
    b_v = z[:, 3 * cd + gd:3 * cd + 2 * gd]

    g = a_c * a_x

    @pl.when(i % blocks_per_seq == 0)
    def _():
        tail_ref[...] = jnp.zeros_like(tail_ref)

    tail = tail_ref[...]
    row = lax.broadcasted_iota(jnp.int32, g.shape, 0)
    gm1 = jnp.where(row == 0, tail[7:8], pltpu.roll(g, 1, 0))
    gm2 = jnp.where(row == 0, tail[6:7], jnp.where(row == 1, tail[7:8], pltpu.roll(g, 2, 0)))
    tail_ref[...] = g[rows - 8:rows]
    cw = cw_ref[...]
    y_a = a_b * (gm2 * cw[0:1] + gm1 * cw[1:2] + g * cw[2:3])

    u = _gelu(b_u)
    v = _gelu(b_v)
    mu = jnp.mean(v, axis=-1, keepdims=True)
    vc = v - mu
    var = jnp.mean(vc * vc, axis=-1, keepdims=True)
    vn = (vc * lax.rsqrt(var + EPS) * lng_ref[...] + lnb_ref[...]).astype(BF16)
    ri = lax.broadcasted_iota(jnp.int32, (CHUNK, CHUNK), 0)
    ci = lax.broadcasted_iota(jnp.int32, (CHUNK, CHUNK), 1)
    causal = ri >= ci
    bst = bst_ref[...]
    for k in range(GMLP_HEADS):
        w_k = jnp.where(causal, ws_ref[k], 0.0).astype(BF16)
        b_k = bst[:, k:k + 1]
        for c in range(rows // CHUNK):
            rs = slice(c * CHUNK, (c + 1) * CHUNK)
            cs = slice(k * hd, (k + 1) * hd)
            mixed = jnp.dot(w_k, vn[rs, cs], preferred_element_type=F32) + b_k
            yb_ref[rs, cs] = (u[rs, cs] * mixed).astype(BF16)

    out = jnp.dot(y_a.astype(BF16), wout_ref[0:cd, :], preferred_element_type=F32)
    out = out + jnp.dot(yb_ref[...], wout_ref[cd:cd + gd, :], preferred_element_type=F32)
    o_ref[...] = x + out


def _mixer(h, pending, seq, *weights):
    if pending is None:
        return _mixer_call(h, None, 0, 1, seq, *weights)
    y_chunks, route = pending
    for c, y in enumerate(y_chunks):
        h = _mixer_call(h, (y, route), c, len(y_chunks), seq, *weights)
    return h


def _mixer_call(h, pending, chunk, n_chunks, seq, g1, w_in, conv_w, ln_g, ln_b, w_s, b_s, w_out):
    n, d = h.shape
    rows = _row_block(seq, MIXER_ROWS)
    cd = conv_w.shape[0]
    gd = ln_g.shape[0]
    has_pending = pending is not None
    steps = n // rows // n_chunks
    base = chunk * steps
    assert (steps * rows) % seq == 0
    kern = functools.partial(_mixer_kernel, blocks_per_seq=seq // rows, has_pending=has_pending)
    return pl.pallas_call(
        kern,
        grid=(steps,),
        in_specs=[pl.BlockSpec((rows, d), lambda i: (i + base, 0))]
        + (_moe_specs(pending, rows, base) if has_pending else [])
        + [
            _full((1, d)),
            _full(w_in.shape),
            _full((CONV_WIDTH, cd)),
            _full((1, gd)),
            _full((1, gd)),
            _full(w_s.shape),
            _full((CHUNK, GMLP_HEADS)),
            _full(w_out.shape),
        ],
        out_specs=pl.BlockSpec((rows, d), lambda i: (i + base, 0)),
        out_shape=jax.ShapeDtypeStruct((n, d), F32),
        scratch_shapes=[pltpu.VMEM((8, cd), F32), pltpu.VMEM((rows, gd), BF16)],
        input_output_aliases={0: 0} if n_chunks > 1 else {},
        compiler_params=_params(("arbitrary",)),
        name="mixer",
    )(h, *(pending or ()), g1.reshape(1, d), w_in.astype(BF16), conv_w.T, ln_g.reshape(1, gd),
      ln_b.reshape(1, gd), w_s, b_s.T, w_out.astype(BF16))


def _swiglu(xn, wg, wu, wd):
    h1 = jnp.dot(xn, wg, preferred_element_type=F32)
    h2 = jnp.dot(xn, wu, preferred_element_type=F32)
    a = (h1 / (1.0 + jnp.exp(-h1)) * h2).astype(BF16)
    return jnp.dot(a, wd, preferred_element_type=F32)


def _dense_ffn_kernel(x_ref, g_ref, wg_ref, wu_ref, wd_ref, *rest):
    n_side = (len(rest) - 1) // 2
    side_in, o_ref, side_out = rest[:n_side], rest[n_side], rest[n_side + 1:]
    x = x_ref[...]
    xn = _rms(x, g_ref[...]).astype(BF16)
    o_ref[...] = x + _swiglu(xn, wg_ref[...], wu_ref[...], wd_ref[...])
    for src, dst in zip(side_in, side_out):
        dst[...] = src[...].astype(dst.dtype)


def _side_cast_ok(w, steps):
    rows = int(np.prod(w.shape[:-1]))
    return rows % steps == 0 and (rows // steps) % 16 == 0 and w.shape[-1] % LANES == 0


def _dense_ffn(h, g, wg, wu, wd, side=()):
    n, d = h.shape
    rows = _row_block(n, FFN_ROWS)
    steps = n // rows
    side2d = [w.reshape(-1, w.shape[-1]) for w in side]
    side_specs = [pl.BlockSpec((w.shape[0] // steps, w.shape[1]), lambda i: (i, 0)) for w in side2d]
    outs = pl.pallas_call(
        _dense_ffn_kernel,
        grid=(steps,),
        in_specs=[
            pl.BlockSpec((rows, d), lambda i: (i, 0)),
            _full((1, d)),
            _full(wg.shape),
            _full(wu.shape),
            _full(wd.shape),
        ] + side_specs,
        out_specs=[pl.BlockSpec((rows, d), lambda i: (i, 0))] + side_specs,
        out_shape=[jax.ShapeDtypeStruct((n, d), F32)]
        + [jax.ShapeDtypeStruct(w.shape, BF16) for w in side2d],
        compiler_params=_params(("arbitrary",)),
        name="dense_ffn",
    )(h, g.reshape(1, d), wg.astype(BF16), wu.astype(BF16), wd.astype(BF16), *side2d)
    return outs[0], [o.reshape(w.shape) for o, w in zip(outs[1:], side)]


def _pack_bf16_pairs(x, piece):
    half = x.shape[1] // 2
    bits = lax.bitcast_convert_type(x.astype(BF16).astype(F32), jnp.uint32)
    return [(bits[:, half + p * piece:half + (p + 1) * piece] & jnp.uint32(0xFFFF0000))
            | (bits[:, p * piece:(p + 1) * piece] >> 16) for p in range(half // piece)]


def _unpack_bf16_pair(packed):
    lo = lax.bitcast_convert_type(packed << 16, F32)
    hi = lax.bitcast_convert_type(packed & jnp.uint32(0xFFFF0000), F32)
    return lo, hi


def _moe_ffn_kernel(be_ref, used_ref, x_ref, wg_ref, wu_ref, wd_ref, o_ref, xn_ref, acc_ref, *,
                    n_steps):
    del be_ref
    i = pl.program_id(0)
    f = pl.program_id(1)
    parts, _, piece = x_ref.shape
    half = parts * piece
    active = i < used_ref[0]

    def write_out(val):
        for p, words in enumerate(_pack_bf16_pairs(val, piece)):
            o_ref[p] = words

    def step(first, last):
        if first:
            for p in range(parts):
                lo, hi = _unpack_bf16_pair(x_ref[p])
                xn_ref[:, p * piece:(p + 1) * piece] = lo.astype(BF16)
                xn_ref[:, half + p * piece:half + (p + 1) * piece] = hi.astype(BF16)
        part = _swiglu(xn_ref[...], wg_ref[0, 0], wu_ref[0, 0], wd_ref[0, 0])
        if last:
            write_out(part if first else acc_ref[...] + part)
        elif first:
            acc_ref[...] = part
        else:
            acc_ref[...] += part

    if n_steps == 1:
        pl.when(active)(functools.partial(step, True, True))
    else:
        pl.when(active & (f == 0))(functools.partial(step, True, False))
        if n_steps > 2:
            pl.when(active & (f > 0) & (f < n_steps - 1))(functools.partial(step, False, False))
        pl.when(active & (f == n_steps - 1))(functools.partial(step, False, True))

    @pl.when(jnp.logical_not(active) & (f == n_steps - 1))
    def _():
        o_ref[...] = jnp.zeros_like(o_ref)


def _moe_ffn(x_sorted, block_expert, n_used, layer, wg, wu, wd):
    parts, n_rows, piece = x_sorted.shape
    d = 2 * parts * piece
    fdim = wg.shape[3]
    rows = MOE_ROWS
    cols = _col_block(fdim, fdim // MOE_COL_SPLIT)
    n_steps = fdim // cols

    def col(i, f, used):
        return jnp.where(i < used[0], f, n_steps - 1)

    grid_spec = pltpu.PrefetchScalarGridSpec(
        num_scalar_prefetch=2,
        grid=(n_rows // rows, n_steps),
        in_specs=[
            pl.BlockSpec((parts, rows, piece),
                         lambda i, f, be, used: (0, jnp.minimum(i, used[0] - 1), 0)),
            pl.BlockSpec((1, 1, d, cols), lambda i, f, be, used: (layer, be[i], 0, col(i, f, used))),
            pl.BlockSpec((1, 1, d, cols), lambda i, f, be, used: (layer, be[i], 0, col(i, f, used))),
            pl.BlockSpec((1, 1, cols, d), lambda i, f, be, used: (layer, be[i], col(i, f, used), 0)),
        ],
        out_specs=pl.BlockSpec((parts, rows, piece), lambda i, f, be, used: (0, i, 0)),
        scratch_shapes=[pltpu.VMEM((rows, d), BF16), pltpu.VMEM((rows, d), F32)],
    )
    kern = functools.partial(_moe_ffn_kernel, n_steps=n_steps)
    return pl.pallas_call(
        kern,
        grid_spec=grid_spec,
        out_shape=jax.ShapeDtypeStruct((parts, n_rows, piece), jnp.uint32),
        compiler_params=_params(("arbitrary", "arbitrary")),
        name="moe_ffn",
    )(block_expert, n_used, x_sorted, wg, wu, wd)


def _cast_kernel(x_ref, o_ref):
    o_ref[...] = x_ref[...].astype(o_ref.dtype)


def _to_bf16(w):
    shape = w.shape
    w2 = w.reshape(-1, shape[-1])
    pref = CAST_BLOCK_BYTES // (4 * shape[-1]) // LANES * LANES
    rows = _row_block(w2.shape[0], pref)
    out = pl.pallas_call(
        _cast_kernel,
        grid=(w2.shape[0] // rows,),
        in_specs=[pl.BlockSpec((rows, shape[-1]), lambda i: (i, 0))],
        out_specs=pl.BlockSpec((rows, shape[-1]), lambda i: (i, 0)),
        out_shape=jax.ShapeDtypeStruct(w2.shape, BF16),
        compiler_params=_params(("arbitrary",)),
        name="cast_bf16",
    )(w2)
    return out.reshape(shape)


def _qkv_kernel(h_ref, g_ref, pos_ref, w_ref, b_ref, invf_ref, mc_ref, q_ref, kv_ref, *, q_dim):
    x = h_ref[...]
    xn = _rms(x, g_ref[...]).astype(BF16)
    z = jnp.dot(xn, w_ref[...], preferred_element_type=F32) + b_ref[...]
    ang = pos_ref[...].astype(F32) * invf_ref[...]
    reps = LANES // ang.shape[0]
    cos = jnp.concatenate([jnp.cos(ang)] * reps, axis=0).T
    sin = jnp.concatenate([jnp.sin(ang)] * reps, axis=0).T
    cos = jnp.where(mc_ref[...] != 0.0, cos, 1.0)
    sin = sin * mc_ref[...]

    def rope(t):
        return t * cos + pltpu.roll(t, LANES // 2, 1) * sin

    for j in range(q_dim // LANES):
        cs = slice(j * LANES, (j + 1) * LANES)
        q_ref[:, cs] = (rope(z[:, cs]) * Q_SCALE).astype(BF16)
    kv_ref[:, 0:LANES] = rope(z[:, q_dim:q_dim + LANES]).astype(BF16)
    kv_ref[:, LANES:2 * LANES] = z[:, q_dim + LANES:q_dim + 2 * LANES].astype(BF16)


QK_GROUP = LANES // 4


def _qk_tile_layout():
    lane = np.arange(LANES)
    group, off = lane // QK_GROUP, lane % QK_GROUP
    half = ROPE_DIM // 2
    head = group % 2
    second = group // 2
    rest = QK_GROUP - half
    dim = np.where(off < half, second * half + off, ROPE_DIM + second * rest + (off - half))
    return head, dim


def _qk_column_order(n_cols):
    head, dim = _qk_tile_layout()
    tile = np.arange(n_cols) // LANES
    return tile * LANES + np.tile(head * HEAD_DIM + dim, n_cols // LANES)


def _rope_sign_lanes():
    _, dim = _qk_tile_layout()
    half = ROPE_DIM // 2
    sign = np.where(dim < half, -1.0, np.where(dim < ROPE_DIM, 1.0, 0.0)).astype(np.float32)
    return jnp.asarray(sign[None, :])


def _qkv(h, g, positions, w_qkv, b_qkv, q_dim):
    n, d = h.shape
    qkv_dim = w_qkv.shape[1]
    assert qkv_dim == q_dim + 2 * LANES and QK_GROUP % (ROPE_DIM // 2) == 0
    rows = _row_block(n, PROJ_ROWS)
    inv_freq = ROPE_THETA ** (-jnp.arange(0, ROPE_DIM, 2, dtype=F32) / ROPE_DIM)
    order = np.concatenate([_qk_column_order(q_dim + LANES), np.arange(q_dim + LANES, qkv_dim)])
    w_qkv = w_qkv[:, order]
    b_qkv = b_qkv[order]
    kern = functools.partial(_qkv_kernel, q_dim=q_dim)
    return pl.pallas_call(
        kern,
        grid=(n // rows,),
        in_specs=[
            pl.BlockSpec((rows, d), lambda i: (i, 0)),
            _full((1, d)),
            pl.BlockSpec((1, rows), lambda i: (0, i)),
            _full(w_qkv.shape),
            _full((1, qkv_dim)),
            _full((ROPE_DIM // 2, 1)),
            _full((1, LANES)),
        ],
        out_specs=[
            pl.BlockSpec((rows, q_dim), lambda i: (i, 0)),
            pl.BlockSpec((rows, 2 * LANES), lambda i: (i, 0)),
        ],
        out_shape=[
            jax.ShapeDtypeStruct((n, q_dim), BF16),
            jax.ShapeDtypeStruct((n, 2 * LANES), BF16),
        ],
        compiler_params=_params(("arbitrary",)),
        name="qkv_rope",
    )(h, g.reshape(1, d), positions.reshape(1, n), w_qkv.astype(BF16),
      b_qkv.reshape(1, qkv_dim), inv_freq.reshape(-1, 1), _rope_sign_lanes())


def _attn_kernel(sink_ref, q_ref, kvc_ref, kvp_ref, h_ref, wo_ref, bo_ref, out_ref,
                 kbuf, vbuf, o_buf, *, wpb):
    rows = q_ref.shape[0]
    tiles = q_ref.shape[1] // LANES // 2
    j = pl.program_id(1)
    kbuf[0:WINDOW, :] = kvp_ref[:, 0:LANES]
    kbuf[WINDOW:WINDOW + rows, :] = kvc_ref[:, 0:LANES]
    vbuf[0:WINDOW, :] = kvp_ref[:, LANES:2 * LANES]
    vbuf[WINDOW:WINDOW + rows, :] = kvc_ref[:, LANES:2 * LANES]

    from_prev = (lax.broadcasted_iota(jnp.int32, (WINDOW, WINDOW), 1)
                 > lax.broadcasted_iota(jnp.int32, (WINDOW, WINDOW), 0))
    kv_lane = lax.broadcasted_iota(jnp.int32, (2 * WINDOW, LANES), 1)
    first_half = kv_lane < HEAD_DIM
    head0_lanes = (kv_lane // QK_GROUP) % 2 == 0
    out_first_half = lax.broadcasted_iota(jnp.int32, (WINDOW, LANES), 1) < HEAD_DIM
    nt = (((1,), (1,)), ((), ()))

    def window(n, carry):
        r0 = pl.multiple_of(n * WINDOW, WINDOW)
        kt = kbuf[pl.ds(r0, 2 * WINDOW), :].astype(F32)
        vt = vbuf[pl.ds(r0, 2 * WINDOW), :].astype(F32)
        prev_bias = jnp.where(j * wpb + n > 0, 0.0, -jnp.inf)
        k0_lo = jnp.where(head0_lanes, kt, 0.0)
        k1_hi = jnp.where(head0_lanes, 0.0, kt)
        v0_lo = jnp.where(first_half, vt, 1.0)
        v1_hi = jnp.where(first_half, 1.0, vt)
        k_even = (k0_lo.astype(BF16), pltpu.roll(k1_hi, LANES - QK_GROUP, 1).astype(BF16))
        k_odd = (pltpu.roll(k0_lo, QK_GROUP, 1).astype(BF16), k1_hi.astype(BF16))
        v_even = (v0_lo.astype(BF16), pltpu.roll(v1_hi, HEAD_DIM, 1).astype(BF16))
        v_odd = (pltpu.roll(v0_lo, HEAD_DIM, 1).astype(BF16), v1_hi.astype(BF16))
        for kh in range(2):
            for t0 in range(0, tiles, ATTN_STACK):
                group = [kh * tiles + t0 + u for u in range(ATTN_STACK)]
                q_stack = jnp.concatenate(
                    [q_ref[pl.ds(r0, WINDOW), t * LANES:(t + 1) * LANES] for t in group], axis=0)
                outs = []
                for parity, k_rhs, v_rhs in ((0, k_even[kh], v_even[kh]), (1, k_odd[kh], v_odd[kh])):
                    s_all = lax.dot_general(q_stack, k_rhs, nt, preferred_element_type=F32)
                    p_tiles, corr = [], []
                    for u, t in enumerate(group):
                        sink = sink_ref[t * 2 + parity]
                        s_prev = s_all[u * WINDOW:(u + 1) * WINDOW, 0:WINDOW] + prev_bias
                        s_cur = s_all[u * WINDOW:(u + 1) * WINDOW, WINDOW:2 * WINDOW]
                        s = jnp.where(from_prev, s_prev, s_cur)
                        m = jnp.maximum(jnp.max(s, axis=-1, keepdims=True), sink)
                        p = jnp.exp2(s - m)
                        p_tiles.append(jnp.concatenate(
                            [jnp.where(from_prev, p, 0.0), jnp.where(from_prev, 0.0, p)],
                            axis=1).astype(BF16))
                        corr.append(jnp.exp2(sink - m))
                    pv = jnp.dot(jnp.concatenate(p_tiles, axis=0), v_rhs,
                                 preferred_element_type=F32)
                    outs.append((pv, corr))
                for u, t in enumerate(group):
                    pv_e = outs[0][0][u * WINDOW:(u + 1) * WINDOW]
                    pv_o = outs[1][0][u * WINDOW:(u + 1) * WINDOW]
                    num = jnp.where(out_first_half, pv_e, pv_o)
                    sums = pltpu.roll(jnp.where(out_first_half, pv_o, pv_e), HEAD_DIM, 1)
                    den = sums + jnp.where(out_first_half, outs[0][1][u], outs[1][1][u])
                    o_buf[pl.ds(r0, WINDOW), t * LANES:(t + 1) * LANES] = (num / den).astype(BF16)
        return carry

    lax.fori_loop(0, rows // WINDOW, window, 0, unroll=2)
    out_ref[...] = (h_ref[...] + jnp.dot(o_buf[...], wo_ref[...], preferred_element_type=F32)
                    + bo_ref[...])


def _attention(h, q, kv, sinks, w_o, b_o, batch, seq):
    n, q_dim = q.shape
    d = h.shape[1]
    assert kv.shape[1] == 2 * LANES and (q_dim // HEAD_DIM) % 4 == 0
    rows = _row_block(seq, ATTN_ROWS)
    bps = seq // rows
    wpb = rows // WINDOW
    wps = seq // WINDOW

    grid_spec = pltpu.PrefetchScalarGridSpec(
        num_scalar_prefetch=1,
        grid=(batch, bps),
        in_specs=[
            pl.BlockSpec((rows, q_dim), lambda b, j, s: (b * bps + j, 0)),
            pl.BlockSpec((rows, 2 * LANES), lambda b, j, s: (b * bps + j, 0)),
            pl.BlockSpec((WINDOW, 2 * LANES),
                         lambda b, j, s: (b * wps + jnp.maximum(j * wpb - 1, 0), 0)),
            pl.BlockSpec((rows, d), lambda b, j, s: (b * bps + j, 0)),
            pl.BlockSpec((q_dim, d), lambda b, j, s: (0, 0)),
            pl.BlockSpec((1, d), lambda b, j, s: (0, 0)),
        ],
        out_specs=pl.BlockSpec((rows, d), lambda b, j, s: (b * bps + j, 0)),
        scratch_shapes=[pltpu.VMEM((rows + WINDOW, LANES), BF16),
                        pltpu.VMEM((rows + WINDOW, LANES), BF16),
                        pltpu.VMEM((rows, q_dim), BF16)],
    )
    kern = functools.partial(_attn_kernel, wpb=wpb)
    return pl.pallas_call(
        kern,
        grid_spec=grid_spec,
        out_shape=jax.ShapeDtypeStruct((n, d), F32),
        compiler_params=_params(("arbitrary", "arbitrary")),
        name="swa_attention",
    )(sinks.astype(F32) * LOG2_E, q, kv, kv, h, w_o.astype(BF16), b_o.reshape(1, d))


R_IDX0, R_IDX1, R_GATE0, R_GATE1, R_RANK0, R_RANK1 = range(6)


def _router_kernel(h_ref, g_ref, rwt_ref, xpk_ref, route_ref, route_t_ref, cnt_ref, tri_ref,
                   carry_ref):
    rows = h_ref.shape[0]
    i = pl.program_id(0)

    @pl.when(i == 0)
    def _():
        r = lax.broadcasted_iota(jnp.int32, (rows, rows), 0)
        c = lax.broadcasted_iota(jnp.int32, (rows, rows), 1)
        tri_ref[...] = jnp.where(r < c, 1.0, 0.0).astype(BF16)
        carry_ref[...] = jnp.zeros_like(carry_ref)

    xn = _rms(h_ref[...], g_ref[...])
    xb = xn.astype(BF16)
    for p, words in enumerate(_pack_bf16_pairs(xn, xpk_ref.shape[2])):
        xpk_ref[p] = words

    nt = (((1,), (1,)), ((), ()))
    logits = lax.dot_general(rwt_ref[...], xb, nt, preferred_element_type=F32)[0:N_EXPERTS]
    ex = lax.broadcasted_iota(jnp.int32, logits.shape, 0)
    m1 = jnp.max(logits, axis=0, keepdims=True)
    i1 = jnp.min(jnp.where(logits == m1, ex, N_EXPERTS), axis=0, keepdims=True)
    lg2 = jnp.where(ex == i1, -jnp.inf, logits)
    m2 = jnp.max(lg2, axis=0, keepdims=True)
    i2 = jnp.min(jnp.where(lg2 == m2, ex, N_EXPERTS), axis=0, keepdims=True)
    e = jnp.exp(m2 - m1)
    g1 = 1.0 / (1.0 + e)
    g2 = e / (1.0 + e)

    sel_f = jnp.where((ex == i1) | (ex == i2), 1.0, 0.0)
    sel_pad = jnp.concatenate([sel_f, jnp.zeros_like(sel_f)], axis=0).astype(BF16)
    carry = carry_ref[:, 0:1]
    before = jnp.dot(sel_pad, tri_ref[...], preferred_element_type=F32)[0:N_EXPERTS] + carry
    r1 = jnp.sum(jnp.where(ex == i1, before, 0.0), axis=0, keepdims=True)
    r2 = jnp.sum(jnp.where(ex == i2, before, 0.0), axis=0, keepdims=True)
    carry = carry + jnp.sum(sel_f, axis=1, keepdims=True)
    carry_ref[...] = jnp.broadcast_to(carry, carry_ref.shape)
    cnt_ref[...] = jnp.broadcast_to(carry, cnt_ref.shape)

    fields = [None] * 8
    for k, val in ((R_IDX0, i1.astype(F32)), (R_IDX1, i2.astype(F32)), (R_GATE0, g1),
                   (R_GATE1, g2), (R_RANK0, r1), (R_RANK1, r2)):
        fields[k] = val
    route_t = jnp.concatenate([f if f is not None else jnp.zeros_like(g1) for f in fields], axis=0)
    route_t_ref[...] = route_t
    route_ref[...] = jnp.concatenate([route_t] * (LANES // 8), axis=0).T


def _router(h, g, router_w):
    n, d = h.shape
    rows = _row_block(n, ROUTER_ROWS)
    parts = d // 2 // SC_PIECE
    rwt = jnp.zeros((16, d), BF16).at[0:N_EXPERTS, :].set(router_w.T.astype(BF16))
    return pl.pallas_call(
        _router_kernel,
        grid=(n // rows,),
        in_specs=[
            pl.BlockSpec((rows, d), lambda i: (i, 0)),
            _full((1, d)),
            _full((16, d)),
        ],
        out_specs=[
            pl.BlockSpec((parts, rows, SC_PIECE), lambda i: (0, i, 0)),
            pl.BlockSpec((rows, LANES), lambda i: (i, 0)),
            pl.BlockSpec((8, rows), lambda i: (0, i)),
            _full((8, LANES)),
        ],
        out_shape=[
            jax.ShapeDtypeStruct((parts, n, SC_PIECE), jnp.uint32),
            jax.ShapeDtypeStruct((n, LANES), F32),
            jax.ShapeDtypeStruct((8, n), F32),
            jax.ShapeDtypeStruct((8, LANES), F32),
        ],
        scratch_shapes=[pltpu.VMEM((rows, rows), BF16), pltpu.VMEM((8, LANES), F32)],
        compiler_params=_params(("arbitrary",)),
        name="moe_router",
    )(h, g.reshape(1, d), rwt)


def _sc_mesh():
    return plsc.VectorSubcoreMesh(core_axis_name="core", subcore_axis_name="subcore")


def _gather_pieces(src, idx):
    m = idx.shape[0]
    width = src.shape[1]
    assert m % (SC_WINDOW * SC_WORKERS) == 0

    @functools.partial(pl.kernel, out_type=jax.ShapeDtypeStruct((m, width), src.dtype),
                       mesh=_sc_mesh(), scratch_types=[])
    def gather_kernel(src_hbm, idx_hbm, out_hbm):
        def body(idx_vmem, out_vmem):
            pltpu.sync_copy(src_hbm.at[idx_vmem.at[0]], out_vmem)

        pltpu.emit_pipeline(
            body,
            grid=(m // SC_WINDOW,),
            in_specs=[pl.BlockSpec((1, SC_WINDOW), lambda i: (0, i))],
            out_specs=[pl.BlockSpec((SC_WINDOW, width), lambda i: (i, 0))],
            core_axis_name=("core", "subcore"),
            dimension_semantics=(pltpu.PARALLEL,),
        )(idx_hbm, out_hbm)

    return gather_kernel(src, idx.reshape(1, m))


def _scatter_pieces(src, idx, out_rows):
    m = idx.shape[0]
    width = src.shape[1]
    src_windows = src.shape[0] // SC_WINDOW
    assert m % (SC_WINDOW * SC_WORKERS) == 0 and src.shape[0] % SC_WINDOW == 0

    @functools.partial(pl.kernel, out_type=jax.ShapeDtypeStruct((out_rows, width), src.dtype),
                       mesh=_sc_mesh(), scratch_types=[])
    def scatter_kernel(src_hbm, idx_hbm, out_hbm):
        def body(src_vmem, idx_vmem):
            pltpu.sync_copy(src_vmem, out_hbm.at[idx_vmem.at[0]])

        pltpu.emit_pipeline(
            body,
            grid=(m // SC_WINDOW,),
            in_specs=[pl.BlockSpec((SC_WINDOW, width), lambda i: (i % src_windows, 0)),
                      pl.BlockSpec((1, SC_WINDOW), lambda i: (0, i))],
            out_specs=[],
            core_axis_name=("core", "subcore"),
            dimension_semantics=(pltpu.PARALLEL,),
        )(src_hbm, idx_hbm)

    return scatter_kernel(src, idx.reshape(1, m))


def _moe_combined(h_ref, y_ref, route_ref):
    parts, _, _, piece = y_ref.shape
    route = route_ref[...]
    g0 = route[:, R_GATE0:R_GATE0 + 1]
    g1 = route[:, R_GATE1:R_GATE1 + 1]
    lo, hi = [], []
    for p in range(parts):
        lo0, hi0 = _unpack_bf16_pair(y_ref[p, 0])
        lo1, hi1 = _unpack_bf16_pair(y_ref[p, 1])
        lo.append(g0 * lo0 + g1 * lo1)
        hi.append(g0 * hi0 + g1 * hi1)
    return h_ref[...] + jnp.concatenate(lo + hi, axis=1)


def _moe_specs(pending, rows, base):
    y_pairs, route = pending
    parts, _, _, piece = y_pairs.shape
    return [pl.BlockSpec((parts, 2, rows, piece), lambda i: (0, 0, i, 0)),
            pl.BlockSpec((rows, LANES), lambda i: (i + base, 0))]


def _final_kernel(h_ref, y_ref, route_ref, g_ref, o_ref):
    o_ref[...] = _rms(_moe_combined(h_ref, y_ref, route_ref), g_ref[...])


def _final(h, pending, final_g):
    n, d = h.shape
    y_chunks, route = pending
    n_chunks = len(y_chunks)
    rows = _row_block(n // n_chunks, PROJ_ROWS)
    steps = n // rows // n_chunks
    for c, y in enumerate(y_chunks):
        base = c * steps
        h = pl.pallas_call(
            _final_kernel,
            grid=(steps,),
            in_specs=[pl.BlockSpec((rows, d), lambda i, base=base: (i + base, 0))]
            + _moe_specs((y, route), rows, base) + [_full((1, d))],
            out_specs=pl.BlockSpec((rows, d), lambda i, base=base: (i + base, 0)),
            out_shape=jax.ShapeDtypeStruct((n, d), F32),
            input_output_aliases={0: 0} if n_chunks > 1 else {},
            compiler_params=_params(("arbitrary",)),
            name="moe_combine_final_norm",
        )(h, y, route, final_g.reshape(1, d))
    return h


def _moe(h, g, router_w, layer, wg, wu, wd, n_chunks):
    n, d = h.shape
    xpk, route, route_t, cnt = _router(h, g, router_w)
    parts = xpk.shape[0]

    top_idx = route_t[R_IDX0:R_IDX1 + 1].astype(jnp.int32)
    rank = route_t[R_RANK0:R_RANK1 + 1].astype(jnp.int32)
    sizes = cnt[0:N_EXPERTS, 0].astype(jnp.int32)
    padded = ((sizes + MOE_ROWS - 1) // MOE_ROWS) * MOE_ROWS
    pends = jnp.cumsum(padded)
    pstarts = pends - padded
    dest = rank
    for e in range(N_EXPERTS):
        dest = dest + jnp.where(top_idx == e, pstarts[e], 0)
    n_rows = 2 * n + N_EXPERTS * MOE_ROWS
    n_blocks = n_rows // MOE_ROWS
    block_start = jnp.arange(n_blocks, dtype=jnp.int32) * MOE_ROWS
    block_expert = jnp.minimum(
        jnp.sum((block_start[:, None] >= pends[None, :]).astype(jnp.int32), axis=1),
        N_EXPERTS - 1)
    n_used = (pends[N_EXPERTS - 1:] // MOE_ROWS).astype(jnp.int32)

    off = jnp.arange(parts, dtype=jnp.int32) * n_rows
    scatter_idx = (dest[:, None, :] + off[None, :, None]).reshape(-1)
    x_sorted = _scatter_pieces(xpk.reshape(parts * n, SC_PIECE), scatter_idx, parts * n_rows)
    y_rows = _moe_ffn(x_sorted.reshape(parts, n_rows, SC_PIECE), block_expert, n_used, layer,
                      wg, wu, wd)

    y_flat = y_rows.reshape(parts * n_rows, SC_PIECE)
    nc = n // n_chunks
    chunk_dest = dest.reshape(2, n_chunks, nc).transpose(1, 0, 2)
    gather_idx = (chunk_dest[:, None] + off[None, :, None, None]).reshape(n_chunks, -1)
    y_chunks = [_gather_pieces(y_flat, gather_idx[c]).reshape(parts, 2, nc, SC_PIECE)
                for c in range(n_chunks)]
    return y_chunks, route


def kernel(x, positions, final_norm_g, ev_norm1_g, ev_w_in, ev_conv_w, ev_ln_g, ev_ln_b, ev_spatial_w, ev_spatial_b, ev_w_out, ev_norm2_g, ev_ffn_wg, ev_ffn_wu, ev_ffn_wd, od_norm1_g, od_w_qkv, od_b_qkv, od_sinks, od_w_o, od_b_o, od_norm2_g, od_router_w, od_exp_wg, od_exp_wu, od_exp_wd):
    batch, seq, d = x.shape
    depth = ev_norm1_g.shape[0] + od_norm1_g.shape[0]
    assert depth % 2 == 0, "the final norm is fused into the last (odd) layer's MoE combine"
    n_q_heads = od_sinks.shape[1]
    h = x.reshape(batch * seq, d)
    experts = [od_exp_wg, od_exp_wu, od_exp_wd]
    ffn_steps = (batch * seq) // _row_block(batch * seq, FFN_ROWS)
    ride_along = all(_side_cast_ok(w, ffn_steps) for w in experts)
    if not ride_along:
        experts = [_to_bf16(w) for w in experts]
    pending = None
    pieces_per_token = 2 * (d // 2 // SC_PIECE)
    n_chunks = max(c for c in range(1, COMBINE_CHUNKS + 1)
                   if batch % c == 0
                   and (batch // c * seq * pieces_per_token) % (SC_WINDOW * SC_WORKERS) == 0)
    for layer in range(depth):
        i = layer // 2
        if layer % 2 == 0:
            h = _mixer(h, pending, seq, ev_norm1_g[i], ev_w_in[i], ev_conv_w[i], ev_ln_g[i],
                       ev_ln_b[i], ev_spatial_w[i], ev_spatial_b[i], ev_w_out[i])
            side = experts if (ride_along and layer == 0) else ()
            h, cast = _dense_ffn(h, ev_norm2_g[i], ev_ffn_wg[i], ev_ffn_wu[i], ev_ffn_wd[i], side)
            if side:
                experts = cast
        else:
            q, kv = _qkv(h, od_norm1_g[i], positions, od_w_qkv[i], od_b_qkv[i],
                         n_q_heads * HEAD_DIM)
            h = _attention(h, q, kv, od_sinks[i], od_w_o[i], od_b_o[i], batch, seq)
            pending = _moe(h, od_norm2_g[i], od_router_w[i], i, *experts, n_chunks)
    return _final(h, pending, final_norm_g).reshape(batch, seq, d)
```

```python
import functools

import jax
import jax.numpy as jnp
import numpy as np
from jax import lax
from jax.experimental import pallas as pl
from jax.experimental.pallas import tpu as pltpu
from jax.experimental.pallas import tpu_sc as plsc

F32 = jnp.float32
BF16 = jnp.bfloat16

EPS = 1e-5
CHUNK = 128
GMLP_HEADS = 4
CONV_WIDTH = 3
HEAD_DIM = 64
WINDOW = 128
ROPE_DIM = HEAD_DIM // 4
ROPE_THETA = 500000.0
ATTN_SCALE = HEAD_DIM ** -0.5
LOG2_E = float(np.log2(np.e))
Q_SCALE = ATTN_SCALE * LOG2_E
N_EXPERTS = 8
LANES = 128
VMEM_LIMIT = 56 * 1024 * 1024

MIXER_ROWS = 512
FFN_ROWS = 512
CAST_BLOCK_BYTES = 8 * 1024 * 1024
PROJ_ROWS = 1024
ATTN_ROWS = 1024
ATTN_STACK = 2
ROUTER_ROWS = 1024
MOE_ROWS = 1024
COMBINE_CHUNKS = 4
MOE_COL_SPLIT = 2
SC_WORKERS = 32
SC_PIECE = 256
SC_WINDOW = 128


def _row_block(n, pref):
    b = min(n, pref)
    while n % b:
        b -= LANES
    return b


def _col_block(f, pref):
    b = min(f, pref)
    b -= b % LANES
    while f % b:
        b -= LANES
    return b


def _params(sem):
    return pltpu.CompilerParams(dimension_semantics=sem, vmem_limit_bytes=VMEM_LIMIT)


def _rms(x, g):
    return x * lax.rsqrt(jnp.mean(x * x, axis=-1, keepdims=True) + EPS) * g


def _gelu(x):
    return 0.5 * x * (1.0 + lax.erf(x * np.float32(np.sqrt(0.5))))


def _full(shape):
    return pl.BlockSpec(shape, lambda *_: (0,) * len(shape))


def _mixer_kernel(*refs, blocks_per_seq, has_pending):
    if has_pending:
        h_ref, y_ref, route_ref = refs[:3]
        refs = refs[3:]
    else:
        h_ref = refs[0]
        refs = refs[1:]
    (g1_ref, win_ref, cw_ref, lng_ref, lnb_ref, ws_ref, bst_ref, wout_ref, o_ref,
     tail_ref, yb_ref) = refs
    rows = h_ref.shape[0]
    cd = cw_ref.shape[1]
    gd = lng_ref.shape[1]
    hd = gd // GMLP_HEADS
    i = pl.program_id(0)

    x = _moe_combined(h_ref, y_ref, route_ref) if has_pending else h_ref[...]
    xn = _rms(x, g1_ref[...]).astype(BF16)
    z = jnp.dot(xn, win_ref[...], preferred_element_type=F32)
    a_b = z[:, 0:cd]
    a_c = z[:, cd:2 * cd]
    a_x = z[:, 2 * cd:3 * cd]
    b_u = z[:, 3 * cd:3 * cd + gd]
    b_v = z[:, 3 * cd + gd:3 * cd + 2 * gd]

    g = a_c * a_x

    @pl.when(i % blocks_per_seq == 0)
    def _():
        tail_ref[...] = jnp.zeros_like(tail_ref)

    tail = tail_ref[...]
    row = lax.broadcasted_iota(jnp.int32, g.shape, 0)
    gm1 = jnp.where(row == 0, tail[7:8], pltpu.roll(g, 1, 0))
    gm2 = jnp.where(row == 0, tail[6:7], jnp.where(row == 1, tail[7:8], pltpu.roll(g, 2, 0)))
    tail_ref[...] = g[rows - 8:rows]
    cw = cw_ref[...]
    y_a = a_b * (gm2 * cw[0:1] + gm1 * cw[1:2] + g * cw[2:3])

    u = _gelu(b_u)
    v = _gelu(b_v)
    mu = jnp.mean(v, axis=-1, keepdims=True)
    vc = v - mu
    var = jnp.mean(vc * vc, axis=-1, keepdims=True)
    vn = (vc * lax.rsqrt(var + EPS) * lng_ref[...] + lnb_ref[...]).astype(BF16)
    ri = lax.broadcasted_iota(jnp.int32, (CHUNK, CHUNK), 0)
    ci = lax.broadcasted_iota(jnp.int32, (CHUNK, CHUNK), 1)
    causal = ri >= ci
    bst = bst_ref[...]
    for k in range(GMLP_HEADS):
        w_k = jnp.where(causal, ws_ref[k], 0.0).astype(BF16)
        b_k = bst[:, k:k + 1]
        for c in range(rows // CHUNK):
            rs = slice(c * CHUNK, (c + 1) * CHUNK)
            cs = slice(k * hd, (k + 1) * hd)
            mixed = jnp.dot(w_k, vn[rs, cs], preferred_element_type=F32) + b_k
            yb_ref[rs, cs] = (u[rs, cs] * mixed).astype(BF16)

    out = jnp.dot(y_a.astype(BF16), wout_ref[0:cd, :], preferred_element_type=F32)
    out = out + jnp.dot(yb_ref[...], wout_ref[cd:cd + gd, :], preferred_element_type=F32)
    o_ref[...] = x + out


def _mixer(h, pending, seq, *weights):
    if pending is None:
        return _mixer_call(h, None, 0, 1, seq, *weights)
    y_chunks, route = pending
    for c, y in enumerate(y_chunks):
        h = _mixer_call(h, (y, route), c, len(y_chunks), seq, *weights)
    return h


def _mixer_call(h, pending, chunk, n_chunks, seq, g1, w_in, conv_w, ln_g, ln_b, w_s, b_s, w_out):
    n, d = h.shape
    rows = _row_block(seq, MIXER_ROWS)
    cd = conv_w.shape[0]
    gd = ln_g.shape[0]
    has_pending = pending is not None
    steps = n // rows // n_chunks
    base = chunk * steps
    assert (steps * rows) % seq == 0
    kern = functools.partial(_mixer_kernel, blocks_per_seq=seq // rows, has_pending=has_pending)
    return pl.pallas_call(
        kern,
        grid=(steps,),
        in_specs=[pl.BlockSpec((rows, d), lambda i: (i + base, 0))]
        + (_moe_specs(pending, rows, base) if has_pending else [])
        + [
            _full((1, d)),
            _full(w_in.shape),
            _full((CONV_WIDTH, cd)),
            _full((1, gd)),
            _full((1, gd)),
            _full(w_s.shape),
            _full((CHUNK, GMLP_HEADS)),
            _full(w_out.shape),
        ],
        out_specs=pl.BlockSpec((rows, d), lambda i: (i + base, 0)),
        out_shape=jax.ShapeDtypeStruct((n, d), F32),
        scratch_shapes=[pltpu.VMEM((8, cd), F32), pltpu.VMEM((rows, gd), BF16)],
        input_output_aliases={0: 0} if n_chunks > 1 else {},
        compiler_params=_params(("arbitrary",)),
        name="mixer",
    )(h, *(pending or ()), g1.reshape(1, d), w_in.astype(BF16), conv_w.T, ln_g.reshape(1, gd),
      ln_b.reshape(1, gd), w_s, b_s.T, w_out.astype(BF16))


def _swiglu(xn, wg, wu, wd):
    h1 = jnp.dot(xn, wg, preferred_element_type=F32)
    h2 = jnp.dot(xn, wu, preferred_element_type=F32)
    a = (h1 / (1.0 + jnp.exp(-h1)) * h2).astype(BF16)
    return jnp.dot(a, wd, preferred_element_type=F32)


def _dense_ffn_kernel(x_ref, g_ref, wg_ref, wu_ref, wd_ref, *rest):
    n_side = (len(rest) - 1) // 2
    side_in, o_ref, side_out = rest[:n_side], rest[n_side], rest[n_side + 1:]
    x = x_ref[...]
    xn = _rms(x, g_ref[...]).astype(BF16)
    o_ref[...] = x + _swiglu(xn, wg_ref[...], wu_ref[...], wd_ref[...])
    for src, dst in zip(side_in, side_out):
        dst[...] = src[...].astype(dst.dtype)


def _side_cast_ok(w, steps):
    rows = int(np.prod(w.shape[:-1]))
    return rows % steps == 0 and (rows // steps) % 16 == 0 and w.shape[-1] % LANES == 0


def _dense_ffn(h, g, wg, wu, wd, side=()):
    n, d = h.shape
    rows = _row_block(n, FFN_ROWS)
    steps = n // rows
    side2d = [w.reshape(-1, w.shape[-1]) for w in side]
    side_specs = [pl.BlockSpec((w.shape[0] // steps, w.shape[1]), lambda i: (i, 0)) for w in side2d]
    outs = pl.pallas_call(
        _dense_ffn_kernel,
        grid=(steps,),
        in_specs=[
            pl.BlockSpec((rows, d), lambda i: (i, 0)),
            _full((1, d)),
            _full(wg.shape),
            _full(wu.shape),
            _full(wd.shape),
        ] + side_specs,
        out_specs=[pl.BlockSpec((rows, d), lambda i: (i, 0))] + side_specs,
        out_shape=[jax.ShapeDtypeStruct((n, d), F32)]
        + [jax.ShapeDtypeStruct(w.shape, BF16) for w in side2d],
        compiler_params=_params(("arbitrary",)),
        name="dense_ffn",
    )(h, g.reshape(1, d), wg.astype(BF16), wu.astype(BF16), wd.astype(BF16), *side2d)
    return outs[0], [o.reshape(w.shape) for o, w in zip(outs[1:], side)]


def _pack_bf16_pairs(x, piece):
    half = x.shape[1] // 2
    bits = lax.bitcast_convert_type(x.astype(BF16).astype(F32), jnp.uint32)
    return [(bits[:, half + p * piece:half + (p + 1) * piece] & jnp.uint32(0xFFFF0000))
            | (bits[:, p * piece:(p + 1) * piece] >> 16) for p in range(half // piece)]


def _unpack_bf16_pair(packed):
    lo = lax.bitcast_convert_type(packed << 16, F32)
    hi = lax.bitcast_convert_type(packed & jnp.uint32(0xFFFF0000), F32)
    return lo, hi


def _moe_ffn_kernel(be_ref, used_ref, x_ref, wg_ref, wu_ref, wd_ref, o_ref, xn_ref, acc_ref, *,
                    n_steps):
    del be_ref
    i = pl.program_id(0)
    f = pl.program_id(1)
    parts, _, piece = x_ref.shape
    half = parts * piece
    active = i < used_ref[0]

    def write_out(val):
        for p, words in enumerate(_pack_bf16_pairs(val, piece)):
            o_ref[p] = words

    def step(first, last):
        if first:
            for p in range(parts):
                lo, hi = _unpack_bf16_pair(x_ref[p])
                xn_ref[:, p * piece:(p + 1) * piece] = lo.astype(BF16)
                xn_ref[:, half + p * piece:half + (p + 1) * piece] = hi.astype(BF16)
        part = _swiglu(xn_ref[...], wg_ref[0, 0], wu_ref[0, 0], wd_ref[0, 0])
        if last:
            write_out(part if first else acc_ref[...] + part)
        elif first:
            acc_ref[...] = part
        else:
            acc_ref[...] += part

    if n_steps == 1:
        pl.when(active)(functools.partial(step, True, True))
    else:
        pl.when(active & (f == 0))(functools.partial(step, True, False))
        if n_steps > 2:
            pl.when(active & (f > 0) & (f < n_steps - 1))(functools.partial(step, False, False))
        pl.when(active & (f == n_steps - 1))(functools.partial(step, False, True))

    @pl.when(jnp.logical_not(active) & (f == n_steps - 1))
    def _():
        o_ref[...] = jnp.zeros_like(o_ref)


def _moe_ffn(x_sorted, block_expert, n_used, layer, wg, wu, wd):
    parts, n_rows, piece = x_sorted.shape
    d = 2 * parts * piece
    fdim = wg.shape[3]
    rows = MOE_ROWS
    cols = _col_block(fdim, fdim // MOE_COL_SPLIT)
    n_steps = fdim // cols

    def col(i, f, used):
        return jnp.where(i < used[0], f, n_steps - 1)

    grid_spec = pltpu.PrefetchScalarGridSpec(
        num_scalar_prefetch=2,
        grid=(n_rows // rows, n_steps),
        in_specs=[
            pl.BlockSpec((parts, rows, piece),
                         lambda i, f, be, used: (0, jnp.minimum(i, used[0] - 1), 0)),
            pl.BlockSpec((1, 1, d, cols), lambda i, f, be, used: (layer, be[i], 0, col(i, f, used))),
            pl.BlockSpec((1, 1, d, cols), lambda i, f, be, used: (layer, be[i], 0, col(i, f, used))),
            pl.BlockSpec((1, 1, cols, d), lambda i, f, be, used: (layer, be[i], col(i, f, used), 0)),
        ],
        out_specs=pl.BlockSpec((parts, rows, piece), lambda i, f, be, used: (0, i, 0)),
        scratch_shapes=[pltpu.VMEM((rows, d), BF16), pltpu.VMEM((rows, d), F32)],
    )
    kern = functools.partial(_moe_ffn_kernel, n_steps=n_steps)
    return pl.pallas_call(
        kern,
        grid_spec=grid_spec,
        out_shape=jax.ShapeDtypeStruct((parts, n_rows, piece), jnp.uint32),
        compiler_params=_params(("arbitrary", "arbitrary")),
        name="moe_ffn",
    )(block_expert, n_used, x_sorted, wg, wu, wd)


def _cast_kernel(x_ref, o_ref):
    o_ref[...] = x_ref[...].astype(o_ref.dtype)


def _to_bf16(w):
    shape = w.shape
    w2 = w.reshape(-1, shape[-1])
    pref = CAST_BLOCK_BYTES // (4 * shape[-1]) // LANES * LANES
    rows = _row_block(w2.shape[0], pref)
    out = pl.pallas_call(
        _cast_kernel,
        grid=(w2.shape[0] // rows,),
        in_specs=[pl.BlockSpec((rows, shape[-1]), lambda i: (i, 0))],
        out_specs=pl.BlockSpec((rows, shape[-1]), lambda i: (i, 0)),
        out_shape=jax.ShapeDtypeStruct(w2.shape, BF16),
        compiler_params=_params(("arbitrary",)),
        name="cast_bf16",
    )(w2)
    return out.reshape(shape)


def _qkv_kernel(h_ref, g_ref, pos_ref, w_ref, b_ref, invf_ref, mc_ref, q_ref, kv_ref, *, q_dim):
    x = h_ref[...]
    xn = _rms(x, g_ref[...]).astype(BF16)
    z = jnp.dot(xn, w_ref[...], preferred_element_type=F32) + b_ref[...]
    ang = pos_ref[...].astype(F32) * invf_ref[...]
    reps = LANES // ang.shape[0]
    cos = jnp.concatenate([jnp.cos(ang)] * reps, axis=0).T
    sin = jnp.concatenate([jnp.sin(ang)] * reps, axis=0).T
    cos = jnp.where(mc_ref[...] != 0.0, cos, 1.0)
    sin = sin * mc_ref[...]

    def rope(t):
        return t * cos + pltpu.roll(t, LANES // 2, 1) * sin

    for j in range(q_dim // LANES):
        cs = slice(j * LANES, (j + 1) * LANES)
        q_ref[:, cs] = (rope(z[:, cs]) * Q_SCALE).astype(BF16)
    kv_ref[:, 0:LANES] = rope(z[:, q_dim:q_dim + LANES]).astype(BF16)
    kv_ref[:, LANES:2 * LANES] = z[:, q_dim + LANES:q_dim + 2 * LANES].astype(BF16)


QK_GROUP = LANES // 4


def _qk_tile_layout():
    lane = np.arange(LANES)
    group, off = lane // QK_GROUP, lane % QK_GROUP
    half = ROPE_DIM // 2
    head = group % 2
    second = group // 2
    rest = QK_GROUP - half
    dim = np.where(off < half, second * half + off, ROPE_DIM + second * rest + (off - half))
    return head, dim


def _qk_column_order(n_cols):
    head, dim = _qk_tile_layout()
    tile = np.arange(n_cols) // LANES
    return tile * LANES + np.tile(head * HEAD_DIM + dim, n_cols // LANES)


def _rope_sign_lanes():
    _, dim = _qk_tile_layout()
    half = ROPE_DIM // 2
    sign = np.where(dim < half, -1.0, np.where(dim < ROPE_DIM, 1.0, 0.0)).astype(np.float32)
    return jnp.asarray(sign[None, :])


def _qkv(h, g, positions, w_qkv, b_qkv, q_dim):
    n, d = h.shape
    qkv_dim = w_qkv.shape[1]
    assert qkv_dim == q_dim + 2 * LANES and QK_GROUP % (ROPE_DIM // 2) == 0
    rows = _row_block(n, PROJ_ROWS)
    inv_freq = ROPE_THETA ** (-jnp.arange(0, ROPE_DIM, 2, dtype=F32) / ROPE_DIM)
    order = np.concatenate([_qk_column_order(q_dim + LANES), np.arange(q_dim + LANES, qkv_dim)])
    w_qkv = w_qkv[:, order]
    b_qkv = b_qkv[order]
    kern = functools.partial(_qkv_kernel, q_dim=q_dim)
    return pl.pallas_call(
        kern,
        grid=(n // rows,),
        in_specs=[
            pl.BlockSpec((rows, d), lambda i: (i, 0)),
            _full((1, d)),
            pl.BlockSpec((1, rows), lambda i: (0, i)),
            _full(w_qkv.shape),
            _full((1, qkv_dim)),
            _full((ROPE_DIM // 2, 1)),
            _full((1, LANES)),
        ],
        out_specs=[
            pl.BlockSpec((rows, q_dim), lambda i: (i, 0)),
            pl.BlockSpec((rows, 2 * LANES), lambda i: (i, 0)),
        ],
        out_shape=[
            jax.ShapeDtypeStruct((n, q_dim), BF16),
            jax.ShapeDtypeStruct((n, 2 * LANES), BF16),
        ],
        compiler_params=_params(("arbitrary",)),
        name="qkv_rope",
    )(h, g.reshape(1, d), positions.reshape(1, n), w_qkv.astype(BF16),
      b_qkv.reshape(1, qkv_dim), inv_freq.reshape(-1, 1), _rope_sign_lanes())


def _attn_kernel(sink_ref, q_ref, kvc_ref, kvp_ref, h_ref, wo_ref, bo_ref, out_ref,
                 kbuf, vbuf, o_buf, *, wpb):
    rows = q_ref.shape[0]
    tiles = q_ref.shape[1] // LANES // 2
    j = pl.program_id(1)
    kbuf[0:WINDOW, :] = kvp_ref[:, 0:LANES]
    kbuf[WINDOW:WINDOW + rows, :] = kvc_ref[:, 0:LANES]
    vbuf[0:WINDOW, :] = kvp_ref[:, LANES:2 * LANES]
    vbuf[WINDOW:WINDOW + rows, :] = kvc_ref[:, LANES:2 * LANES]

    from_prev = (lax.broadcasted_iota(jnp.int32, (WINDOW, WINDOW), 1)
                 > lax.broadcasted_iota(jnp.int32, (WINDOW, WINDOW), 0))
    kv_lane = lax.broadcasted_iota(jnp.int32, (2 * WINDOW, LANES), 1)
    first_half = kv_lane < HEAD_DIM
    head0_lanes = (kv_lane // QK_GROUP) % 2 == 0
    out_first_half = lax.broadcasted_iota(jnp.int32, (WINDOW, LANES), 1) < HEAD_DIM
    nt = (((1,), (1,)), ((), ()))

    def window(n, carry):
        r0 = pl.multiple_of(n * WINDOW, WINDOW)
        kt = kbuf[pl.ds(r0, 2 * WINDOW), :].astype(F32)
        vt = vbuf[pl.ds(r0, 2 * WINDOW), :].astype(F32)
        prev_bias = jnp.where(j * wpb + n > 0, 0.0, -jnp.inf)
        k0_lo = jnp.where(head0_lanes, kt, 0.0)
        k1_hi = jnp.where(head0_lanes, 0.0, kt)
        v0_lo = jnp.where(first_half, vt, 1.0)
        v1_hi = jnp.where(first_half, 1.0, vt)
        k_even = (k0_lo.astype(BF16), pltpu.roll(k1_hi, LANES - QK_GROUP, 1).astype(BF16))
        k_odd = (pltpu.roll(k0_lo, QK_GROUP, 1).astype(BF16), k1_hi.astype(BF16))
        v_even = (v0_lo.astype(BF16), pltpu.roll(v1_hi, HEAD_DIM, 1).astype(BF16))
        v_odd = (pltpu.roll(v0_lo, HEAD_DIM, 1).astype(BF16), v1_hi.astype(BF16))
        for kh in range(2):
            for t0 in range(0, tiles, ATTN_STACK):
                group = [kh * tiles + t0 + u for u in range(ATTN_STACK)]
                q_stack = jnp.concatenate(
                    [q_ref[pl.ds(r0, WINDOW), t * LANES:(t + 1) * LANES] for t in group], axis=0)
                outs = []
                for parity, k_rhs, v_rhs in ((0, k_even[kh], v_even[kh]), (1, k_odd[kh], v_odd[kh])):
                    s_all = lax.dot_general(q_stack, k_rhs, nt, preferred_element_type=F32)
                    p_tiles, corr = [], []
                    for u, t in enumerate(group):
                        sink = sink_ref[t * 2 + parity]
                        s_prev = s_all[u * WINDOW:(u + 1) * WINDOW, 0:WINDOW] + prev_bias
                        s_cur = s_all[u * WINDOW:(u + 1) * WINDOW, WINDOW:2 * WINDOW]
                        s = jnp.where(from_prev, s_prev, s_cur)
                        m = jnp.maximum(jnp.max(s, axis=-1, keepdims=True), sink)
                        p = jnp.exp2(s - m)
                        p_tiles.append(jnp.concatenate(
                            [jnp.where(from_prev, p, 0.0), jnp.where(from_prev, 0.0, p)],
                            axis=1).astype(BF16))
                        corr.append(jnp.exp2(sink - m))
                    pv = jnp.dot(jnp.concatenate(p_tiles, axis=0), v_rhs,
                                 preferred_element_type=F32)
                    outs.append((pv, corr))
                for u, t in enumerate(group):
                    pv_e = outs[0][0][u * WINDOW:(u + 1) * WINDOW]
                    pv_o = outs[1][0][u * WINDOW:(u + 1) * WINDOW]
                    num = jnp.where(out_first_half, pv_e, pv_o)
                    sums = pltpu.roll(jnp.where(out_first_half, pv_o, pv_e), HEAD_DIM, 1)
                    den = sums + jnp.where(out_first_half, outs[0][1][u], outs[1][1][u])
                    o_buf[pl.ds(r0, WINDOW), t * LANES:(t + 1) * LANES] = (num / den).astype(BF16)
        return carry

    lax.fori_loop(0, rows // WINDOW, window, 0, unroll=2)
    out_ref[...] = (h_ref[...] + jnp.dot(o_buf[...], wo_ref[...], preferred_element_type=F32)
                    + bo_ref[...])


def _attention(h, q, kv, sinks, w_o, b_o, batch, seq):
    n, q_dim = q.shape
    d = h.shape[1]
    assert kv.shape[1] == 2 * LANES and (q_dim // HEAD_DIM) % 4 == 0
    rows = _row_block(seq, ATTN_ROWS)
    bps = seq // rows
    wpb = rows // WINDOW
    wps = seq // WINDOW

    grid_spec = pltpu.PrefetchScalarGridSpec(
        num_scalar_prefetch=1,
        grid=(batch, bps),
        in_specs=[
            pl.BlockSpec((rows, q_dim), lambda b, j, s: (b * bps + j, 0)),
            pl.BlockSpec((rows, 2 * LANES), lambda b, j, s: (b * bps + j, 0)),
            pl.BlockSpec((WINDOW, 2 * LANES),
                         lambda b, j, s: (b * wps + jnp.maximum(j * wpb - 1, 0), 0)),
            pl.BlockSpec((rows, d), lambda b, j, s: (b * bps + j, 0)),
            pl.BlockSpec((q_dim, d), lambda b, j, s: (0, 0)),
            pl.BlockSpec((1, d), lambda b, j, s: (0, 0)),
        ],
        out_specs=pl.BlockSpec((rows, d), lambda b, j, s: (b * bps + j, 0)),
        scratch_shapes=[pltpu.VMEM((rows + WINDOW, LANES), BF16),
                        pltpu.VMEM((rows + WINDOW, LANES), BF16),
                        pltpu.VMEM((rows, q_dim), BF16)],
    )
    kern = functools.partial(_attn_kernel, wpb=wpb)
    return pl.pallas_call(
        kern,
        grid_spec=grid_spec,
        out_shape=jax.ShapeDtypeStruct((n, d), F32),
        compiler_params=_params(("arbitrary", "arbitrary")),
        name="swa_attention",
    )(sinks.astype(F32) * LOG2_E, q, kv, kv, h, w_o.astype(BF16), b_o.reshape(1, d))


R_IDX0, R_IDX1, R_GATE0, R_GATE1, R_RANK0, R_RANK1 = range(6)


def _router_kernel(h_ref, g_ref, rwt_ref, xpk_ref, route_ref, route_t_ref, cnt_ref, tri_ref,
                   carry_ref):
    rows = h_ref.shape[0]
    i = pl.program_id(0)

    @pl.when(i == 0)
    def _():
        r = lax.broadcasted_iota(jnp.int32, (rows, rows), 0)
        c = lax.broadcasted_iota(jnp.int32, (rows, rows), 1)
        tri_ref[...] = jnp.where(r < c, 1.0, 0.0).astype(BF16)
        carry_ref[...] = jnp.zeros_like(carry_ref)

    xn = _rms(h_ref[...], g_ref[...])
    xb = xn.astype(BF16)
    for p, words in enumerate(_pack_bf16_pairs(xn, xpk_ref.shape[2])):
        xpk_ref[p] = words

    nt = (((1,), (1,)), ((), ()))
    logits = lax.dot_general(rwt_ref[...], xb, nt, preferred_element_type=F32)[0:N_EXPERTS]
    ex = lax.broadcasted_iota(jnp.int32, logits.shape, 0)
    m1 = jnp.max(logits, axis=0, keepdims=True)
    i1 = jnp.min(jnp.where(logits == m1, ex, N_EXPERTS), axis=0, keepdims=True)
    lg2 = jnp.where(ex == i1, -jnp.inf, logits)
    m2 = jnp.max(lg2, axis=0, keepdims=True)
    i2 = jnp.min(jnp.where(lg2 == m2, ex, N_EXPERTS), axis=0, keepdims=True)
    e = jnp.exp(m2 - m1)
    g1 = 1.0 / (1.0 + e)
    g2 = e / (1.0 + e)

    sel_f = jnp.where((ex == i1) | (ex == i2), 1.0, 0.0)
    sel_pad = jnp.concatenate([sel_f, jnp.zeros_like(sel_f)], axis=0).astype(BF16)
    carry = carry_ref[:, 0:1]
    before = jnp.dot(sel_pad, tri_ref[...], preferred_element_type=F32)[0:N_EXPERTS] + carry
    r1 = jnp.sum(jnp.where(ex == i1, before, 0.0), axis=0, keepdims=True)
    r2 = jnp.sum(jnp.where(ex == i2, before, 0.0), axis=0, keepdims=True)
    carry = carry + jnp.sum(sel_f, axis=1, keepdims=True)
    carry_ref[...] = jnp.broadcast_to(carry, carry_ref.shape)
    cnt_ref[...] = jnp.broadcast_to(carry, cnt_ref.shape)

    fields = [None] * 8
    for k, val in ((R_IDX0, i1.astype(F32)), (R_IDX1, i2.astype(F32)), (R_GATE0, g1),
                   (R_GATE1, g2), (R_RANK0, r1), (R_RANK1, r2)):
        fields[k] = val
    route_t = jnp.concatenate([f if f is not None else jnp.zeros_like(g1) for f in fields], axis=0)
    route_t_ref[...] = route_t
    route_ref[...] = jnp.concatenate([route_t] * (LANES // 8), axis=0).T


def _router(h, g, router_w):
    n, d = h.shape
    rows = _row_block(n, ROUTER_ROWS)
    parts = d // 2 // SC_PIECE
    rwt = jnp.zeros((16, d), BF16).at[0:N_EXPERTS, :].set(router_w.T.astype(BF16))
    return pl.pallas_call(
        _router_kernel,
        grid=(n // rows,),
        in_specs=[
            pl.BlockSpec((rows, d), lambda i: (i, 0)),
            _full((1, d)),
            _full((16, d)),
        ],
        out_specs=[
            pl.BlockSpec((parts, rows, SC_PIECE), lambda i: (0, i, 0)),
            pl.BlockSpec((rows, LANES), lambda i: (i, 0)),
            pl.BlockSpec((8, rows), lambda i: (0, i)),
            _full((8, LANES)),
        ],
        out_shape=[
            jax.ShapeDtypeStruct((parts, n, SC_PIECE), jnp.uint32),
            jax.ShapeDtypeStruct((n, LANES), F32),
            jax.ShapeDtypeStruct((8, n), F32),
            jax.ShapeDtypeStruct((8, LANES), F32),
        ],
        scratch_shapes=[pltpu.VMEM((rows, rows), BF16), pltpu.VMEM((8, LANES), F32)],
        compiler_params=_params(("arbitrary",)),
        name="moe_router",
    )(h, g.reshape(1, d), rwt)


def _sc_mesh():
    return plsc.VectorSubcoreMesh(core_axis_name="core", subcore_axis_name="subcore")


def _gather_pieces(src, idx):
    m = idx.shape[0]
    width = src.shape[1]
    assert m % (SC_WINDOW * SC_WORKERS) == 0

    @functools.partial(pl.kernel, out_type=jax.ShapeDtypeStruct((m, width), src.dtype),
                       mesh=_sc_mesh(), scratch_types=[])
    def gather_kernel(src_hbm, idx_hbm, out_hbm):
        def body(idx_vmem, out_vmem):
            pltpu.sync_copy(src_hbm.at[idx_vmem.at[0]], out_vmem)

        pltpu.emit_pipeline(
            body,
            grid=(m // SC_WINDOW,),
            in_specs=[pl.BlockSpec((1, SC_WINDOW), lambda i: (0, i))],
            out_specs=[pl.BlockSpec((SC_WINDOW, width), lambda i: (i, 0))],
            core_axis_name=("core", "subcore"),
            dimension_semantics=(pltpu.PARALLEL,),
        )(idx_hbm, out_hbm)

    return gather_kernel(src, idx.reshape(1, m))


def _scatter_pieces(src, idx, out_rows):
    copies, m = idx.shape
    width = src.shape[1]
    assert m == src.shape[0] and m % (SC_WINDOW * SC_WORKERS) == 0

    @functools.partial(pl.kernel, out_type=jax.ShapeDtypeStruct((out_rows, width), src.dtype),
                       mesh=_sc_mesh(), scratch_types=[])
    def scatter_kernel(src_hbm, *refs):
        idx_hbm, out_hbm = refs[:copies], refs[copies]

        def body(src_vmem, *idx_vmem):
            for iv in idx_vmem:
                pltpu.sync_copy(src_vmem, out_hbm.at[iv.at[0]])

        pltpu.emit_pipeline(
            body,
            grid=(m // SC_WINDOW,),
            in_specs=[pl.BlockSpec((SC_WINDOW, width), lambda i: (i, 0))]
            + [pl.BlockSpec((1, SC_WINDOW), lambda i: (0, i))] * copies,
            out_specs=[],
            core_axis_name=("core", "subcore"),
            dimension_semantics=(pltpu.PARALLEL,),
        )(src_hbm, *idx_hbm)

    return scatter_kernel(src, *[idx[j].reshape(1, m) for j in range(copies)])


def _moe_combined(h_ref, y_ref, route_ref):
    parts, _, _, piece = y_ref.shape
    route = route_ref[...]
    g0 = route[:, R_GATE0:R_GATE0 + 1]
    g1 = route[:, R_GATE1:R_GATE1 + 1]
    lo, hi = [], []
    for p in range(parts):
        lo0, hi0 = _unpack_bf16_pair(y_ref[p, 0])
        lo1, hi1 = _unpack_bf16_pair(y_ref[p, 1])
        lo.append(g0 * lo0 + g1 * lo1)
        hi.append(g0 * hi0 + g1 * hi1)
    return h_ref[...] + jnp.concatenate(lo + hi, axis=1)


def _moe_specs(pending, rows, base):
    y_pairs, route = pending
    parts, _, _, piece = y_pairs.shape
    return [pl.BlockSpec((parts, 2, rows, piece), lambda i: (0, 0, i, 0)),
            pl.BlockSpec((rows, LANES), lambda i: (i + base, 0))]


def _final_kernel(h_ref, y_ref, route_ref, g_ref, o_ref):
    o_ref[...] = _rms(_moe_combined(h_ref, y_ref, route_ref), g_ref[...])


def _final(h, pending, final_g):
    n, d = h.shape
    y_chunks, route = pending
    n_chunks = len(y_chunks)
    rows = _row_block(n // n_chunks, PROJ_ROWS)
    steps = n // rows // n_chunks
    for c, y in enumerate(y_chunks):
        base = c * steps
        h = pl.pallas_call(
            _final_kernel,
            grid=(steps,),
            in_specs=[pl.BlockSpec((rows, d), lambda i, base=base: (i + base, 0))]
            + _moe_specs((y, route), rows, base) + [_full((1, d))],
            out_specs=pl.BlockSpec((rows, d), lambda i, base=base: (i + base, 0)),
            out_shape=jax.ShapeDtypeStruct((n, d), F32),
            input_output_aliases={0: 0} if n_chunks > 1 else {},
            compiler_params=_params(("arbitrary",)),
            name="moe_combine_final_norm",
        )(h, y, route, final_g.reshape(1, d))
    return h


def _moe(h, g, router_w, layer, wg, wu, wd, n_chunks):
    n, d = h.shape
    xpk, route, route_t, cnt = _router(h, g, router_w)
    parts = xpk.shape[0]

    top_idx = route_t[R_IDX0:R_IDX1 + 1].astype(jnp.int32)
    rank = route_t[R_RANK0:R_RANK1 + 1].astype(jnp.int32)
    sizes = cnt[0:N_EXPERTS, 0].astype(jnp.int32)
    padded = ((sizes + MOE_ROWS - 1) // MOE_ROWS) * MOE_ROWS
    pends = jnp.cumsum(padded)
    pstarts = pends - padded
    dest = rank
    for e in range(N_EXPERTS):
        dest = dest + jnp.where(top_idx == e, pstarts[e], 0)
    n_rows = 2 * n + N_EXPERTS * MOE_ROWS
    n_blocks = n_rows // MOE_ROWS
    block_start = jnp.arange(n_blocks, dtype=jnp.int32) * MOE_ROWS
    block_expert = jnp.minimum(
        jnp.sum((block_start[:, None] >= pends[None, :]).astype(jnp.int32), axis=1),
        N_EXPERTS - 1)
    n_used = (pends[N_EXPERTS - 1:] // MOE_ROWS).astype(jnp.int32)

    off = jnp.arange(parts, dtype=jnp.int32) * n_rows
    scatter_idx = (dest[:, None, :] + off[None, :, None]).reshape(2, parts * n)
    x_sorted = _scatter_pieces(xpk.reshape(parts * n, SC_PIECE), scatter_idx, parts * n_rows)
    y_rows = _moe_ffn(x_sorted.reshape(parts, n_rows, SC_PIECE), block_expert, n_used, layer,
                      wg, wu, wd)

    y_flat = y_rows.reshape(parts * n_rows, SC_PIECE)
    nc = n // n_chunks
    chunk_dest = dest.reshape(2, n_chunks, nc).transpose(1, 0, 2)
    gather_idx = (chunk_dest[:, None] + off[None, :, None, None]).reshape(n_chunks, -1)
    y_chunks = [_gather_pieces(y_flat, gather_idx[c]).reshape(parts, 2, nc, SC_PIECE)
                for c in range(n_chunks)]
    return y_chunks, route


def kernel(x, positions, final_norm_g, ev_norm1_g, ev_w_in, ev_conv_w, ev_ln_g, ev_ln_b, ev_spatial_w, ev_spatial_b, ev_w_out, ev_norm2_g, ev_ffn_wg, ev_ffn_wu, ev_ffn_wd, od_norm1_g, od_w_qkv, od_b_qkv, od_sinks, od_w_o, od_b_o, od_norm2_g, od_router_w, od_exp_wg, od_exp_wu, od_exp_wd):
    batch, seq, d = x.shape
    depth = ev_norm1_g.shape[0] + od_norm1_g.shape[0]
    assert depth % 2 == 0, "the final norm is fused into the last (odd) layer's MoE combine"
    n_q_heads = od_sinks.shape[1]
    h = x.reshape(batch * seq, d)
    experts = [od_exp_wg, od_exp_wu, od_exp_wd]
    ffn_steps = (batch * seq) // _row_block(batch * seq, FFN_ROWS)
    ride_along = all(_side_cast_ok(w, ffn_steps) for w in experts)
    if not ride_along:
        experts = [_to_bf16(w) for w in experts]
    pending = None
    pieces_per_token = 2 * (d // 2 // SC_PIECE)
    n_chunks = max(c for c in range(1, COMBINE_CHUNKS + 1)
                   if batch % c == 0
                   and (batch // c * seq * pieces_per_token) % (SC_WINDOW * SC_WORKERS) == 0)
    for layer in range(depth):
        i = layer // 2
        if layer % 2 == 0:
            h = _mixer(h, pending, seq, ev_norm1_g[i], ev_w_in[i], ev_conv_w[i], ev_ln_g[i],
                       ev_ln_b[i], ev_spatial_w[i], ev_spatial_b[i], ev_w_out[i])
            side = experts if (ride_along and layer == 0) else ()
            h, cast = _dense_ffn(h, ev_norm2_g[i], ev_ffn_wg[i], ev_ffn_wu[i], ev_ffn_wd[i], side)
            if side:
                experts = cast
        else:
            q, kv = _qkv(h, od_norm1_g[i], positions, od_w_qkv[i], od_b_qkv[i],
                         n_q_heads * HEAD_DIM)
            h = _attention(h, q, kv, od_sinks[i], od_w_o[i], od_b_o[i], batch, seq)
            pending = _moe(h, od_norm2_g[i], od_router_w[i], i, *experts, n_chunks)
    return _final(h, pending, final_norm_g).reshape(batch, seq, d)
```

```python
import functools

import jax
import jax.numpy as jnp
import numpy as np
from jax import lax
from jax.experimental import pallas as pl
from jax.experimental.pallas import tpu as pltpu
from jax.experimental.pallas import tpu_sc as plsc

F32 = jnp.float32
BF16 = jnp.bfloat16

EPS = 1e-5
CHUNK = 128
GMLP_HEADS = 4
CONV_WIDTH = 3
HEAD_DIM = 64
WINDOW = 128
ROPE_DIM = HEAD_DIM // 4
ROPE_THETA = 500000.0
ATTN_SCALE = HEAD_DIM ** -0.5
LOG2_E = float(np.log2(np.e))
Q_SCALE = ATTN_SCALE * LOG2_E
N_EXPERTS = 8
LANES = 128
VMEM_LIMIT = 56 * 1024 * 1024

MIXER_ROWS = 512
FFN_ROWS = 512
CAST_BLOCK_BYTES = 8 * 1024 * 1024
PROJ_ROWS = 1024
ATTN_ROWS = 1024
ATTN_STACK = 2
ROUTER_ROWS = 1024
MOE_ROWS = 1024
COMBINE_CHUNKS = 4
MOE_COL_SPLIT = 2
SC_WORKERS = 32
SC_PIECE = 256
SC_WINDOW = 128


def _row_block(n, pref):
    b = min(n, pref)
    while n % b:
        b -= LANES
    return b


def _col_block(f, pref):
    b = min(f, pref)
    b -= b % LANES
    while f % b:
        b -= LANES
    return b


def _params(sem):
    return pltpu.CompilerParams(dimension_semantics=sem, vmem_limit_bytes=VMEM_LIMIT)


def _rms(x, g):
    return x * lax.rsqrt(jnp.mean(x * x, axis=-1, keepdims=True) + EPS) * g


def _gelu(x):
    return 0.5 * x * (1.0 + lax.erf(x * np.float32(np.sqrt(0.5))))


def _full(shape):
    return pl.BlockSpec(shape, lambda *_: (0,) * len(shape))


def _mixer_kernel(*refs, blocks_per_seq, has_pending):
    if has_pending:
        h_ref, y_ref, route_ref = refs[:3]
        refs = refs[3:]
    else:
        h_ref = refs[0]
        refs = refs[1:]
    (g1_ref, win_ref, cw_ref, lng_ref, lnb_ref, ws_ref, bst_ref, wout_ref, o_ref,
     tail_ref, yb_ref) = refs
    rows = h_ref.shape[0]
    cd = cw_ref.shape[1]
    gd = lng_ref.shape[1]
    hd = gd // GMLP_HEADS
    i = pl.program_id(0)

    x = _moe_combined(h_ref, y_ref, route_ref) if has_pending else h_ref[...]
    xn = _rms(x, g1_ref[...]).astype(BF16)
    z = jnp.dot(xn, win_ref[...], preferred_element_type=F32)
    a_b = z[:, 0:cd]
    a_c = z[:, cd:2 * cd]
    a_x = z[:, 2 * cd:3 * cd]
    b_u = z[:, 3 * cd:3 * cd + gd]
    b_v = z[:, 3 * cd + gd:3 * cd + 2 * gd]

    g = a_c * a_x

    @pl.when(i % blocks_per_seq == 0)
    def _():
        tail_ref[...] = jnp.zeros_like(tail_ref)

    tail = tail_ref[...]
    row = lax.broadcasted_iota(jnp.int32, g.shape, 0)
    gm1 = jnp.where(row == 0, tail[7:8], pltpu.roll(g, 1, 0))
    gm2 = jnp.where(row == 0, tail[6:7], jnp.where(row == 1, tail[7:8], pltpu.roll(g, 2, 0)))
    tail_ref[...] = g[rows - 8:rows]
    cw = cw_ref[...]
    y_a = a_b * (gm2 * cw[0:1] + gm1 * cw[1:2] + g * cw[2:3])

    u = _gelu(b_u)
    v = _gelu(b_v)
    mu = jnp.mean(v, axis=-1, keepdims=True)
    vc = v - mu
    var = jnp.mean(vc * vc, axis=-1, keepdims=True)
    vn = (vc * lax.rsqrt(var + EPS) * lng_ref[...] + lnb_ref[...]).astype(BF16)
    ri = lax.broadcasted_iota(jnp.int32, (CHUNK, CHUNK), 0)
    ci = lax.broadcasted_iota(jnp.int32, (CHUNK, CHUNK), 1)
    causal = ri >= ci
    bst = bst_ref[...]
    for k in range(GMLP_HEADS):
        w_k = jnp.where(causal, ws_ref[k], 0.0).astype(BF16)
        b_k = bst[:, k:k + 1]
        for c in range(rows // CHUNK):
            rs = slice(c * CHUNK, (c + 1) * CHUNK)
            cs = slice(k * hd, (k + 1) * hd)
            mixed = jnp.dot(w_k, vn[rs, cs], preferred_element_type=F32) + b_k
            yb_ref[rs, cs] = (u[rs, cs] * mixed).astype(BF16)

    out = jnp.dot(y_a.astype(BF16), wout_ref[0:cd, :], preferred_element_type=F32)
    out = out + jnp.dot(yb_ref[...], wout_ref[cd:cd + gd, :], preferred_element_type=F32)
    o_ref[...] = x + out


def _mixer(h, pending, seq, *weights):
    if pending is None:
        return _mixer_call(h, None, 0, 1, seq, *weights)
    y_chunks, route = pending
    for c, y in enumerate(y_chunks):
        h = _mixer_call(h, (y, route), c, len(y_chunks), seq, *weights)
    return h


def _mixer_call(h, pending, chunk, n_chunks, seq, g1, w_in, conv_w, ln_g, ln_b, w_s, b_s, w_out):
    n, d = h.shape
    rows = _row_block(seq, MIXER_ROWS)
    cd = conv_w.shape[0]
    gd = ln_g.shape[0]
    has_pending = pending is not None
    steps = n // rows // n_chunks
    base = chunk * steps
    assert (steps * rows) % seq == 0
    kern = functools.partial(_mixer_kernel, blocks_per_seq=seq // rows, has_pending=has_pending)
    return pl.pallas_call(
        kern,
        grid=(steps,),
        in_specs=[pl.BlockSpec((rows, d), lambda i: (i + base, 0))]
        + (_moe_specs(pending, rows, base) if has_pending else [])
        + [
            _full((1, d)),
            _full(w_in.shape),
            _full((CONV_WIDTH, cd)),
            _full((1, gd)),
            _full((1, gd)),
            _full(w_s.shape),
            _full((CHUNK, GMLP_HEADS)),
            _full(w_out.shape),
        ],
        out_specs=pl.BlockSpec((rows, d), lambda i: (i + base, 0)),
        out_shape=jax.ShapeDtypeStruct((n, d), F32),
        scratch_shapes=[pltpu.VMEM((8, cd), F32), pltpu.VMEM((rows, gd), BF16)],
        input_output_aliases={0: 0} if n_chunks > 1 else {},
        compiler_params=_params(("arbitrary",)),
        name="mixer",
    )(h, *(pending or ()), g1.reshape(1, d), w_in.astype(BF16), conv_w.T, ln_g.reshape(1, gd),
      ln_b.reshape(1, gd), w_s, b_s.T, w_out.astype(BF16))


def _swiglu(xn, wg, wu, wd):
    h1 = jnp.dot(xn, wg, preferred_element_type=F32)
    h2 = jnp.dot(xn, wu, preferred_element_type=F32)
    a = (h1 / (1.0 + jnp.exp(-h1)) * h2).astype(BF16)
    return jnp.dot(a, wd, preferred_element_type=F32)


def _dense_ffn_kernel(x_ref, g_ref, wg_ref, wu_ref, wd_ref, *rest):
    n_side = (len(rest) - 1) // 2
    side_in, o_ref, side_out = rest[:n_side], rest[n_side], rest[n_side + 1:]
    x = x_ref[...]
    xn = _rms(x, g_ref[...]).astype(BF16)
    o_ref[...] = x + _swiglu(xn, wg_ref[...], wu_ref[...], wd_ref[...])
    for src, dst in zip(side_in, side_out):
        dst[...] = src[...].astype(dst.dtype)


def _side_cast_ok(w, steps):
    rows = int(np.prod(w.shape[:-1]))
    return rows % steps == 0 and (rows // steps) % 16 == 0 and w.shape[-1] % LANES == 0


def _dense_ffn(h, g, wg, wu, wd, side=()):
    n, d = h.shape
    rows = _row_block(n, FFN_ROWS)
    steps = n // rows
    side2d = [w.reshape(-1, w.shape[-1]) for w in side]
    side_specs = [pl.BlockSpec((w.shape[0] // steps, w.shape[1]), lambda i: (i, 0)) for w in side2d]
    outs = pl.pallas_call(
        _dense_ffn_kernel,
        grid=(steps,),
        in_specs=[
            pl.BlockSpec((rows, d), lambda i: (i, 0)),
            _full((1, d)),
            _full(wg.shape),
            _full(wu.shape),
            _full(wd.shape),
        ] + side_specs,
        out_specs=[pl.BlockSpec((rows, d), lambda i: (i, 0))] + side_specs,
        out_shape=[jax.ShapeDtypeStruct((n, d), F32)]
        + [jax.ShapeDtypeStruct(w.shape, BF16) for w in side2d],
        compiler_params=_params(("arbitrary",)),
        name="dense_ffn",
    )(h, g.reshape(1, d), wg.astype(BF16), wu.astype(BF16), wd.astype(BF16), *side2d)
    return outs[0], [o.reshape(w.shape) for o, w in zip(outs[1:], side)]


def _pack_bf16_pairs(x, piece):
    half = x.shape[1] // 2
    bits = lax.bitcast_convert_type(x.astype(BF16).astype(F32), jnp.uint32)
    return [(bits[:, half + p * piece:half + (p + 1) * piece] & jnp.uint32(0xFFFF0000))
            | (bits[:, p * piece:(p + 1) * piece] >> 16) for p in range(half // piece)]


def _unpack_bf16_pair(packed):
    lo = lax.bitcast_convert_type(packed << 16, F32)
    hi = lax.bitcast_convert_type(packed & jnp.uint32(0xFFFF0000), F32)
    return lo, hi


def _moe_ffn_kernel(be_ref, used_ref, x_ref, wg_ref, wu_ref, wd_ref, o_ref, xn_ref, acc_ref, *,
                    n_steps):
    del be_ref
    i = pl.program_id(0)
    f = pl.program_id(1)
    parts, _, piece = x_ref.shape
    half = parts * piece
    active = i < used_ref[0]

    def write_out(val):
        for p, words in enumerate(_pack_bf16_pairs(val, piece)):
            o_ref[p] = words

    def step(first, last):
        if first:
            for p in range(parts):
                lo, hi = _unpack_bf16_pair(x_ref[p])
                xn_ref[:, p * piece:(p + 1) * piece] = lo.astype(BF16)
                xn_ref[:, half + p * piece:half + (p + 1) * piece] = hi.astype(BF16)
        part = _swiglu(xn_ref[...], wg_ref[0, 0], wu_ref[0, 0], wd_ref[0, 0])
        if last:
            write_out(part if first else acc_ref[...] + part)
        elif first:
            acc_ref[...] = part
        else:
            acc_ref[...] += part

    if n_steps == 1:
        pl.when(active)(functools.partial(step, True, True))
    else:
        pl.when(active & (f == 0))(functools.partial(step, True, False))
        if n_steps > 2:
            pl.when(active & (f > 0) & (f < n_steps - 1))(functools.partial(step, False, False))
        pl.when(active & (f == n_steps - 1))(functools.partial(step, False, True))

    @pl.when(jnp.logical_not(active) & (f == n_steps - 1))
    def _():
        o_ref[...] = jnp.zeros_like(o_ref)


def _moe_ffn(x_sorted, block_expert, n_used, layer, wg, wu, wd):
    parts, n_rows, piece = x_sorted.shape
    d = 2 * parts * piece
    fdim = wg.shape[3]
    rows = MOE_ROWS
    cols = _col_block(fdim, fdim // MOE_COL_SPLIT)
    n_steps = fdim // cols

    def col(i, f, used):
        return jnp.where(i < used[0], f, n_steps - 1)

    grid_spec = pltpu.PrefetchScalarGridSpec(
        num_scalar_prefetch=2,
        grid=(n_rows // rows, n_steps),
        in_specs=[
            pl.BlockSpec((parts, rows, piece),
                         lambda i, f, be, used: (0, jnp.minimum(i, used[0] - 1), 0)),
            pl.BlockSpec((1, 1, d, cols), lambda i, f, be, used: (layer, be[i], 0, col(i, f, used))),
            pl.BlockSpec((1, 1, d, cols), lambda i, f, be, used: (layer, be[i], 0, col(i, f, used))),
            pl.BlockSpec((1, 1, cols, d), lambda i, f, be, used: (layer, be[i], col(i, f, used), 0)),
        ],
        out_specs=pl.BlockSpec((parts, rows, piece), lambda i, f, be, used: (0, i, 0)),
        scratch_shapes=[pltpu.VMEM((rows, d), BF16), pltpu.VMEM((rows, d), F32)],
    )
    kern = functools.partial(_moe_ffn_kernel, n_steps=n_steps)
    return pl.pallas_call(
        kern,
        grid_spec=grid_spec,
        out_shape=jax.ShapeDtypeStruct((parts, n_rows, piece), jnp.uint32),
        compiler_params=_params(("arbitrary", "arbitrary")),
        name="moe_ffn",
    )(block_expert, n_used, x_sorted, wg, wu, wd)


def _cast_kernel(x_ref, o_ref):
    o_ref[...] = x_ref[...].astype(o_ref.dtype)


def _to_bf16(w):
    shape = w.shape
    w2 = w.reshape(-1, shape[-1])
    pref = CAST_BLOCK_BYTES // (4 * shape[-1]) // LANES * LANES
    rows = _row_block(w2.shape[0], pref)
    out = pl.pallas_call(
        _cast_kernel,
        grid=(w2.shape[0] // rows,),
        in_specs=[pl.BlockSpec((rows, shape[-1]), lambda i: (i, 0))],
        out_specs=pl.BlockSpec((rows, shape[-1]), lambda i: (i, 0)),
        out_shape=jax.ShapeDtypeStruct(w2.shape, BF16),
        compiler_params=_params(("arbitrary",)),
        name="cast_bf16",
    )(w2)
    return out.reshape(shape)


def _qkv_kernel(h_ref, g_ref, pos_ref, w_ref, b_ref, invf_ref, mc_ref, q_ref, kv_ref, *, q_dim):
    x = h_ref[...]
    xn = _rms(x, g_ref[...]).astype(BF16)
    z = jnp.dot(xn, w_ref[...], preferred_element_type=F32) + b_ref[...]
    ang = pos_ref[...].astype(F32) * invf_ref[...]
    reps = LANES // ang.shape[0]
    cos = jnp.concatenate([jnp.cos(ang)] * reps, axis=0).T
    sin = jnp.concatenate([jnp.sin(ang)] * reps, axis=0).T
    cos = jnp.where(mc_ref[...] != 0.0, cos, 1.0)
    sin = sin * mc_ref[...]

    def rope(t):
        return t * cos + pltpu.roll(t, LANES // 2, 1) * sin

    for j in range(q_dim // LANES):
        cs = slice(j * LANES, (j + 1) * LANES)
        q_ref[:, cs] = (rope(z[:, cs]) * Q_SCALE).astype(BF16)
    kv_ref[:, 0:LANES] = rope(z[:, q_dim:q_dim + LANES]).astype(BF16)
    kv_ref[:, LANES:2 * LANES] = z[:, q_dim + LANES:q_dim + 2 * LANES].astype(BF16)


QK_GROUP = LANES // 4


def _qk_tile_layout():
    lane = np.arange(LANES)
    group, off = lane // QK_GROUP, lane % QK_GROUP
    half = ROPE_DIM // 2
    head = group % 2
    second = group // 2
    rest = QK_GROUP - half
    dim = np.where(off < half, second * half + off, ROPE_DIM + second * rest + (off - half))
    return head, dim


def _qk_column_order(n_cols):
    head, dim = _qk_tile_layout()
    tile = np.arange(n_cols) // LANES
    return tile * LANES + np.tile(head * HEAD_DIM + dim, n_cols // LANES)


def _rope_sign_lanes():
    _, dim = _qk_tile_layout()
    half = ROPE_DIM // 2
    sign = np.where(dim < half, -1.0, np.where(dim < ROPE_DIM, 1.0, 0.0)).astype(np.float32)
    return jnp.asarray(sign[None, :])


def _qkv(h, g, positions, w_qkv, b_qkv, q_dim):
    n, d = h.shape
    qkv_dim = w_qkv.shape[1]
    assert qkv_dim == q_dim + 2 * LANES and QK_GROUP % (ROPE_DIM // 2) == 0
    rows = _row_block(n, PROJ_ROWS)
    inv_freq = ROPE_THETA ** (-jnp.arange(0, ROPE_DIM, 2, dtype=F32) / ROPE_DIM)
    order = np.concatenate([_qk_column_order(q_dim + LANES), np.arange(q_dim + LANES, qkv_dim)])
    w_qkv = w_qkv[:, order]
    b_qkv = b_qkv[order]
    kern = functools.partial(_qkv_kernel, q_dim=q_dim)
    return pl.pallas_call(
        kern,
        grid=(n // rows,),
        in_specs=[
            pl.BlockSpec((rows, d), lambda i: (i, 0)),
            _full((1, d)),
            pl.BlockSpec((1, rows), lambda i: (0, i)),
            _full(w_qkv.shape),
            _full((1, qkv_dim)),
            _full((ROPE_DIM // 2, 1)),
            _full((1, LANES)),
        ],
        out_specs=[
            pl.BlockSpec((rows, q_dim), lambda i: (i, 0)),
            pl.BlockSpec((rows, 2 * LANES), lambda i: (i, 0)),
        ],
        out_shape=[
            jax.ShapeDtypeStruct((n, q_dim), BF16),
            jax.ShapeDtypeStruct((n, 2 * LANES), BF16),
        ],
        compiler_params=_params(("arbitrary",)),
        name="qkv_rope",
    )(h, g.reshape(1, d), positions.reshape(1, n), w_qkv.astype(BF16),
      b_qkv.reshape(1, qkv_dim), inv_freq.reshape(-1, 1), _rope_sign_lanes())


def _attn_kernel(sink_ref, q_ref, kvc_ref, kvp_ref, h_ref, wo_ref, bo_ref, out_ref,
                 kbuf, vbuf, o_buf, *, wpb):
    rows = q_ref.shape[0]
    tiles = q_ref.shape[1] // LANES // 2
    j = pl.program_id(1)
    kbuf[0:WINDOW, :] = kvp_ref[:, 0:LANES]
    kbuf[WINDOW:WINDOW + rows, :] = kvc_ref[:, 0:LANES]
    vbuf[0:WINDOW, :] = kvp_ref[:, LANES:2 * LANES]
    vbuf[WINDOW:WINDOW + rows, :] = kvc_ref[:, LANES:2 * LANES]

    from_prev = (lax.broadcasted_iota(jnp.int32, (WINDOW, WINDOW), 1)
                 > lax.broadcasted_iota(jnp.int32, (WINDOW, WINDOW), 0))
    kv_lane = lax.broadcasted_iota(jnp.int32, (2 * WINDOW, LANES), 1)
    first_half = kv_lane < HEAD_DIM
    head0_lanes = (kv_lane // QK_GROUP) % 2 == 0
    out_first_half = lax.broadcasted_iota(jnp.int32, (WINDOW, LANES), 1) < HEAD_DIM
    ones_lo = jnp.where(first_half, 1.0, 0.0)
    ones_hi = jnp.where(first_half, 0.0, 1.0)
    nt = (((1,), (1,)), ((), ()))

    def window(n, carry):
        r0 = pl.multiple_of(n * WINDOW, WINDOW)
        kt = kbuf[pl.ds(r0, 2 * WINDOW), :].astype(F32)
        vt = vbuf[pl.ds(r0, 2 * WINDOW), :].astype(F32)
        prev_bias = jnp.where(j * wpb + n > 0, 0.0, -jnp.inf)
        k0_lo = jnp.where(head0_lanes, kt, 0.0)
        k1_hi = jnp.where(head0_lanes, 0.0, kt)
        v0_lo = jnp.where(first_half, vt, 0.0)
        v1_hi = jnp.where(first_half, 0.0, vt)
        k_both = (
            jnp.concatenate([k0_lo, pltpu.roll(k0_lo, QK_GROUP, 1)], axis=0).astype(BF16),
            jnp.concatenate([pltpu.roll(k1_hi, LANES - QK_GROUP, 1), k1_hi], axis=0).astype(BF16))
        v_both = (
            jnp.concatenate([jnp.concatenate([v0_lo, ones_lo], axis=1),
                             jnp.concatenate([pltpu.roll(v0_lo, HEAD_DIM, 1), ones_hi], axis=1)],
                            axis=0).astype(BF16),
            jnp.concatenate([jnp.concatenate([pltpu.roll(v1_hi, HEAD_DIM, 1), ones_lo], axis=1),
                             jnp.concatenate([v1_hi, ones_hi], axis=1)], axis=0).astype(BF16))
        for kh in range(2):
            for t0 in range(0, tiles, ATTN_STACK):
                group = [kh * tiles + t0 + u for u in range(ATTN_STACK)]
                q_stack = jnp.concatenate(
                    [q_ref[pl.ds(r0, WINDOW), t * LANES:(t + 1) * LANES] for t in group], axis=0)
                s_all = lax.dot_general(q_stack, k_both[kh], nt, preferred_element_type=F32)
                p_rows, corr = [], []
                for u, t in enumerate(group):
                    p_cols, corr_t = [], []
                    for parity in range(2):
                        sink = sink_ref[t * 2 + parity]
                        c0 = parity * 2 * WINDOW
                        s_prev = s_all[u * WINDOW:(u + 1) * WINDOW, c0:c0 + WINDOW] + prev_bias
                        s_cur = s_all[u * WINDOW:(u + 1) * WINDOW, c0 + WINDOW:c0 + 2 * WINDOW]
                        s = jnp.where(from_prev, s_prev, s_cur)
                        m = jnp.maximum(jnp.max(s, axis=-1, keepdims=True), sink)
                        p = jnp.exp2(s - m)
                        p_cols += [jnp.where(from_prev, p, 0.0), jnp.where(from_prev, 0.0, p)]
                        corr_t.append(jnp.exp2(sink - m))
                    p_rows.append(jnp.concatenate(p_cols, axis=1).astype(BF16))
                    corr.append(corr_t)
                pv = jnp.dot(jnp.concatenate(p_rows, axis=0), v_both[kh],
                             preferred_element_type=F32)
                for u, t in enumerate(group):
                    num = pv[u * WINDOW:(u + 1) * WINDOW, 0:LANES]
                    den = (pv[u * WINDOW:(u + 1) * WINDOW, LANES:2 * LANES]
                           + jnp.where(out_first_half, corr[u][0], corr[u][1]))
                    o_buf[pl.ds(r0, WINDOW), t * LANES:(t + 1) * LANES] = (num / den).astype(BF16)
        return carry

    lax.fori_loop(0, rows // WINDOW, window, 0, unroll=2)
    out_ref[...] = (h_ref[...] + jnp.dot(o_buf[...], wo_ref[...], preferred_element_type=F32)
                    + bo_ref[...])


def _attention(h, q, kv, sinks, w_o, b_o, batch, seq):
    n, q_dim = q.shape
    d = h.shape[1]
    assert kv.shape[1] == 2 * LANES and (q_dim // HEAD_DIM) % 4 == 0
    rows = _row_block(seq, ATTN_ROWS)
    bps = seq // rows
    wpb = rows // WINDOW
    wps = seq // WINDOW

    grid_spec = pltpu.PrefetchScalarGridSpec(
        num_scalar_prefetch=1,
        grid=(batch, bps),
        in_specs=[
            pl.BlockSpec((rows, q_dim), lambda b, j, s: (b * bps + j, 0)),
            pl.BlockSpec((rows, 2 * LANES), lambda b, j, s: (b * bps + j, 0)),
            pl.BlockSpec((WINDOW, 2 * LANES),
                         lambda b, j, s: (b * wps + jnp.maximum(j * wpb - 1, 0), 0)),
            pl.BlockSpec((rows, d), lambda b, j, s: (b * bps + j, 0)),
            pl.BlockSpec((q_dim, d), lambda b, j, s: (0, 0)),
            pl.BlockSpec((1, d), lambda b, j, s: (0, 0)),
        ],
        out_specs=pl.BlockSpec((rows, d), lambda b, j, s: (b * bps + j, 0)),
        scratch_shapes=[pltpu.VMEM((rows + WINDOW, LANES), BF16),
                        pltpu.VMEM((rows + WINDOW, LANES), BF16),
                        pltpu.VMEM((rows, q_dim), BF16)],
    )
    kern = functools.partial(_attn_kernel, wpb=wpb)
    return pl.pallas_call(
        kern,
        grid_spec=grid_spec,
        out_shape=jax.ShapeDtypeStruct((n, d), F32),
        compiler_params=_params(("arbitrary", "arbitrary")),
        name="swa_attention",
    )(sinks.astype(F32) * LOG2_E, q, kv, kv, h, w_o.astype(BF16), b_o.reshape(1, d))


R_IDX0, R_IDX1, R_GATE0, R_GATE1, R_RANK0, R_RANK1 = range(6)


def _router_kernel(h_ref, g_ref, rwt_ref, xpk_ref, route_ref, route_t_ref, cnt_ref, tri_ref,
                   carry_ref):
    rows = h_ref.shape[0]
    i = pl.program_id(0)

    @pl.when(i == 0)
    def _():
        r = lax.broadcasted_iota(jnp.int32, (rows, rows), 0)
        c = lax.broadcasted_iota(jnp.int32, (rows, rows), 1)
        tri_ref[...] = jnp.where(r < c, 1.0, 0.0).astype(BF16)
        carry_ref[...] = jnp.zeros_like(carry_ref)

    xn = _rms(h_ref[...], g_ref[...])
    xb = xn.astype(BF16)
    for p, words in enumerate(_pack_bf16_pairs(xn, xpk_ref.shape[2])):
        xpk_ref[p] = words

    nt = (((1,), (1,)), ((), ()))
    logits = lax.dot_general(rwt_ref[...], xb, nt, preferred_element_type=F32)[0:N_EXPERTS]
    ex = lax.broadcasted_iota(jnp.int32, logits.shape, 0)
    m1 = jnp.max(logits, axis=0, keepdims=True)
    i1 = jnp.min(jnp.where(logits == m1, ex, N_EXPERTS), axis=0, keepdims=True)
    lg2 = jnp.where(ex == i1, -jnp.inf, logits)
    m2 = jnp.max(lg2, axis=0, keepdims=True)
    i2 = jnp.min(jnp.where(lg2 == m2, ex, N_EXPERTS), axis=0, keepdims=True)
    e = jnp.exp(m2 - m1)
    g1 = 1.0 / (1.0 + e)
    g2 = e / (1.0 + e)

    sel_f = jnp.where((ex == i1) | (ex == i2), 1.0, 0.0)
    sel_pad = jnp.concatenate([sel_f, jnp.zeros_like(sel_f)], axis=0).astype(BF16)
    carry = carry_ref[:, 0:1]
    before = jnp.dot(sel_pad, tri_ref[...], preferred_element_type=F32)[0:N_EXPERTS] + carry
    r1 = jnp.sum(jnp.where(ex == i1, before, 0.0), axis=0, keepdims=True)
    r2 = jnp.sum(jnp.where(ex == i2, before, 0.0), axis=0, keepdims=True)
    carry = carry + jnp.sum(sel_f, axis=1, keepdims=True)
    carry_ref[...] = jnp.broadcast_to(carry, carry_ref.shape)
    cnt_ref[...] = jnp.broadcast_to(carry, cnt_ref.shape)

    fields = [None] * 8
    for k, val in ((R_IDX0, i1.astype(F32)), (R_IDX1, i2.astype(F32)), (R_GATE0, g1),
                   (R_GATE1, g2), (R_RANK0, r1), (R_RANK1, r2)):
        fields[k] = val
    route_t = jnp.concatenate([f if f is not None else jnp.zeros_like(g1) for f in fields], axis=0)
    route_t_ref[...] = route_t
    route_ref[...] = jnp.concatenate([route_t] * (LANES // 8), axis=0).T


def _router(h, g, router_w):
    n, d = h.shape
    rows = _row_block(n, ROUTER_ROWS)
    parts = d // 2 // SC_PIECE
    rwt = jnp.zeros((16, d), BF16).at[0:N_EXPERTS, :].set(router_w.T.astype(BF16))
    return pl.pallas_call(
        _router_kernel,
        grid=(n // rows,),
        in_specs=[
            pl.BlockSpec((rows, d), lambda i: (i, 0)),
            _full((1, d)),
            _full((16, d)),
        ],
        out_specs=[
            pl.BlockSpec((parts, rows, SC_PIECE), lambda i: (0, i, 0)),
            pl.BlockSpec((rows, LANES), lambda i: (i, 0)),
            pl.BlockSpec((8, rows), lambda i: (0, i)),
            _full((8, LANES)),
        ],
        out_shape=[
            jax.ShapeDtypeStruct((parts, n, SC_PIECE), jnp.uint32),
            jax.ShapeDtypeStruct((n, LANES), F32),
            jax.ShapeDtypeStruct((8, n), F32),
            jax.ShapeDtypeStruct((8, LANES), F32),
        ],
        scratch_shapes=[pltpu.VMEM((rows, rows), BF16), pltpu.VMEM((8, LANES), F32)],
        compiler_params=_params(("arbitrary",)),
        name="moe_router",
    )(h, g.reshape(1, d), rwt)


def _sc_mesh():
    return plsc.VectorSubcoreMesh(core_axis_name="core", subcore_axis_name="subcore")


def _gather_pieces(src, idx):
    m = idx.shape[0]
    width = src.shape[1]
    assert m % (SC_WINDOW * SC_WORKERS) == 0

    @functools.partial(pl.kernel, out_type=jax.ShapeDtypeStruct((m, width), src.dtype),
                       mesh=_sc_mesh(), scratch_types=[])
    def gather_kernel(src_hbm, idx_hbm, out_hbm):
        def body(idx_vmem, out_vmem):
            pltpu.sync_copy(src_hbm.at[idx_vmem.at[0]], out_vmem)

        pltpu.emit_pipeline(
            body,
            grid=(m // SC_WINDOW,),
            in_specs=[pl.BlockSpec((1, SC_WINDOW), lambda i: (0, i))],
            out_specs=[pl.BlockSpec((SC_WINDOW, width), lambda i: (i, 0))],
            core_axis_name=("core", "subcore"),
            dimension_semantics=(pltpu.PARALLEL,),
        )(idx_hbm, out_hbm)

    return gather_kernel(src, idx.reshape(1, m))


def _scatter_pieces(src, idx, out_rows):
    copies, m = idx.shape
    width = src.shape[1]
    assert m == src.shape[0] and m % (SC_WINDOW * SC_WORKERS) == 0

    @functools.partial(pl.kernel, out_type=jax.ShapeDtypeStruct((out_rows, width), src.dtype),
                       mesh=_sc_mesh(), scratch_types=[])
    def scatter_kernel(src_hbm, *refs):
        idx_hbm, out_hbm = refs[:copies], refs[copies]

        def body(src_vmem, *idx_vmem):
            for iv in idx_vmem:
                pltpu.sync_copy(src_vmem, out_hbm.at[iv.at[0]])

        pltpu.emit_pipeline(
            body,
            grid=(m // SC_WINDOW,),
            in_specs=[pl.BlockSpec((SC_WINDOW, width), lambda i: (i, 0))]
            + [pl.BlockSpec((1, SC_WINDOW), lambda i: (0, i))] * copies,
            out_specs=[],
            core_axis_name=("core", "subcore"),
            dimension_semantics=(pltpu.PARALLEL,),
        )(src_hbm, *idx_hbm)

    return scatter_kernel(src, *[idx[j].reshape(1, m) for j in range(copies)])


def _moe_combined(h_ref, y_ref, route_ref):
    parts, _, _, piece = y_ref.shape
    route = route_ref[...]
    g0 = route[:, R_GATE0:R_GATE0 + 1]
    g1 = route[:, R_GATE1:R_GATE1 + 1]
    lo, hi = [], []
    for p in range(parts):
        lo0, hi0 = _unpack_bf16_pair(y_ref[p, 0])
        lo1, hi1 = _unpack_bf16_pair(y_ref[p, 1])
        lo.append(g0 * lo0 + g1 * lo1)
        hi.append(g0 * hi0 + g1 * hi1)
    return h_ref[...] + jnp.concatenate(lo + hi, axis=1)


def _moe_specs(pending, rows, base):
    y_pairs, route = pending
    parts, _, _, piece = y_pairs.shape
    return [pl.BlockSpec((parts, 2, rows, piece), lambda i: (0, 0, i, 0)),
            pl.BlockSpec((rows, LANES), lambda i: (i + base, 0))]


def _final_kernel(h_ref, y_ref, route_ref, g_ref, o_ref):
    o_ref[...] = _rms(_moe_combined(h_ref, y_ref, route_ref), g_ref[...])


def _final(h, pending, final_g):
    n, d = h.shape
    y_chunks, route = pending
    n_chunks = len(y_chunks)
    rows = _row_block(n // n_chunks, PROJ_ROWS)
    steps = n // rows // n_chunks
    for c, y in enumerate(y_chunks):
        base = c * steps
        h = pl.pallas_call(
            _final_kernel,
            grid=(steps,),
            in_specs=[pl.BlockSpec((rows, d), lambda i, base=base: (i + base, 0))]
            + _moe_specs((y, route), rows, base) + [_full((1, d))],
            out_specs=pl.BlockSpec((rows, d), lambda i, base=base: (i + base, 0)),
            out_shape=jax.ShapeDtypeStruct((n, d), F32),
            input_output_aliases={0: 0} if n_chunks > 1 else {},
            compiler_params=_params(("arbitrary",)),
            name="moe_combine_final_norm",
        )(h, y, route, final_g.reshape(1, d))
    return h


def _moe(h, g, router_w, layer, wg, wu, wd, n_chunks):
    n, d = h.shape
    xpk, route, route_t, cnt = _router(h, g, router_w)
    parts = xpk.shape[0]

    top_idx = route_t[R_IDX0:R_IDX1 + 1].astype(jnp.int32)
    rank = route_t[R_RANK0:R_RANK1 + 1].astype(jnp.int32)
    sizes = cnt[0:N_EXPERTS, 0].astype(jnp.int32)
    padded = ((sizes + MOE_ROWS - 1) // MOE_ROWS) * MOE_ROWS
    pends = jnp.cumsum(padded)
    pstarts = pends - padded
    dest = rank
    for e in range(N_EXPERTS):
        dest = dest + jnp.where(top_idx == e, pstarts[e], 0)
    n_rows = 2 * n + N_EXPERTS * MOE_ROWS
    n_blocks = n_rows // MOE_ROWS
    block_start = jnp.arange(n_blocks, dtype=jnp.int32) * MOE_ROWS
    block_expert = jnp.minimum(
        jnp.sum((block_start[:, None] >= pends[None, :]).astype(jnp.int32), axis=1),
        N_EXPERTS - 1)
    n_used = (pends[N_EXPERTS - 1:] // MOE_ROWS).astype(jnp.int32)

    off = jnp.arange(parts, dtype=jnp.int32) * n_rows
    scatter_idx = (dest[:, None, :] + off[None, :, None]).reshape(2, parts * n)
    x_sorted = _scatter_pieces(xpk.reshape(parts * n, SC_PIECE), scatter_idx, parts * n_rows)
    y_rows = _moe_ffn(x_sorted.reshape(parts, n_rows, SC_PIECE), block_expert, n_used, layer,
                      wg, wu, wd)

    y_flat = y_rows.reshape(parts * n_rows, SC_PIECE)
    nc = n // n_chunks
    chunk_dest = dest.reshape(2, n_chunks, nc).transpose(1, 0, 2)
    gather_idx = (chunk_dest[:, None] + off[None, :, None, None]).reshape(n_chunks, -1)
    y_chunks = [_gather_pieces(y_flat, gather_idx[c]).reshape(parts, 2, nc, SC_PIECE)
                for c in range(n_chunks)]
    return y_chunks, route


def kernel(x, positions, final_norm_g, ev_norm1_g, ev_w_in, ev_conv_w, ev_ln_g, ev_ln_b, ev_spatial_w, ev_spatial_b, ev_w_out, ev_norm2_g, ev_ffn_wg, ev_ffn_wu, ev_ffn_wd, od_norm1_g, od_w_qkv, od_b_qkv, od_sinks, od_w_o, od_b_o, od_norm2_g, od_router_w, od_exp_wg, od_exp_wu, od_exp_wd):
    batch, seq, d = x.shape
    depth = ev_norm1_g.shape[0] + od_norm1_g.shape[0]
    assert depth % 2 == 0, "the final norm is fused into the last (odd) layer's MoE combine"
    n_q_heads = od_sinks.shape[1]
    h = x.reshape(batch * seq, d)
    experts = [od_exp_wg, od_exp_wu, od_exp_wd]
    ffn_steps = (batch * seq) // _row_block(batch * seq, FFN_ROWS)
    ride_along = all(_side_cast_ok(w, ffn_steps) for w in experts)
    if not ride_along:
        experts = [_to_bf16(w) for w in experts]
    pending = None
    pieces_per_token = 2 * (d // 2 // SC_PIECE)
    n_chunks = max(c for c in range(1, COMBINE_CHUNKS + 1)
                   if batch % c == 0
                   and (batch // c * seq * pieces_per_token) % (SC_WINDOW * SC_WORKERS) == 0)
    for layer in range(depth):
        i = layer // 2
        if layer % 2 == 0:
            h = _mixer(h, pending, seq, ev_norm1_g[i], ev_w_in[i], ev_conv_w[i], ev_ln_g[i],
                       ev_ln_b[i], ev_spatial_w[i], ev_spatial_b[i], ev_w_out[i])
            side = experts if (ride_along and layer == 0) else ()
            h, cast = _dense_ffn(h, ev_norm2_g[i], ev_ffn_wg[i], ev_ffn_wu[i], ev_ffn_wd[i], side)
            if side:
                experts = cast
        else:
            q, kv = _qkv(h, od_norm1_g[i], positions, od_w_qkv[i], od_b_qkv[i],
                         n_q_heads * HEAD_DIM)
            h = _attention(h, q, kv, od_sinks[i], od_w_o[i], od_b_o[i], batch, seq)
            pending = _moe(h, od_norm2_g[i], od_router_w[i], i, *experts, n_chunks)
    return _final(h, pending, final_norm_g).reshape(batch, seq, d)
```

```python
import functools

import jax
import jax.numpy as jnp
import numpy as np
from jax import lax
from jax.experimental import pallas as pl
from jax.experimental.pallas import tpu as pltpu
from jax.experimental.pallas import tpu_sc as plsc

F32 = jnp.float32
BF16 = jnp.bfloat16

EPS = 1e-5
CHUNK = 128
GMLP_HEADS = 4
CONV_WIDTH = 3
HEAD_DIM = 64
WINDOW = 128
ROPE_DIM = HEAD_DIM // 4
ROPE_THETA = 500000.0
ATTN_SCALE = HEAD_DIM ** -0.5
LOG2_E = float(np.log2(np.e))
Q_SCALE = ATTN_SCALE * LOG2_E
N_EXPERTS = 8
LANES = 128
VMEM_LIMIT = 56 * 1024 * 1024

MIXER_ROWS = 512
FFN_ROWS = 512
CAST_BLOCK_BYTES = 8 * 1024 * 1024
PROJ_ROWS = 1024
ATTN_ROWS = 1024
ATTN_STACK = 2
ROUTER_ROWS = 1024
MOE_ROWS = 1024
COMBINE_CHUNKS = 4
MOE_COL_SPLIT = 2
SC_WORKERS = 32
SC_PIECE = 256
SC_WINDOW = 128


def _row_block(n, pref):
    b = min(n, pref)
    while n % b:
        b -= LANES
    return b


def _col_block(f, pref):
    b = min(f, pref)
    b -= b % LANES
    while f % b:
        b -= LANES
    return b


def _params(sem):
    return pltpu.CompilerParams(dimension_semantics=sem, vmem_limit_bytes=VMEM_LIMIT)


def _rms(x, g):
    return x * lax.rsqrt(jnp.mean(x * x, axis=-1, keepdims=True) + EPS) * g


def _gelu(x):
    return 0.5 * x * (1.0 + lax.erf(x * np.float32(np.sqrt(0.5))))


def _full(shape):
    return pl.BlockSpec(shape, lambda *_: (0,) * len(shape))


def _mixer_kernel(*refs, blocks_per_seq, has_pending):
    if has_pending:
        h_ref, y_ref, route_ref = refs[:3]
        refs = refs[3:]
    else:
        h_ref = refs[0]
        refs = refs[1:]
    (g1_ref, win_ref, cw_ref, lng_ref, lnb_ref, ws_ref, bst_ref, wout_ref, o_ref,
     tail_ref, yb_ref) = refs
    rows = h_ref.shape[0]
    cd = cw_ref.shape[1]
    gd = lng_ref.shape[1]
    hd = gd // GMLP_HEADS
    i = pl.program_id(0)

    x = _moe_combined(h_ref, y_ref, route_ref) if has_pending else h_ref[...]
    xn = _rms(x, g1_ref[...]).astype(BF16)
    z = jnp.dot(xn, win_ref[...], preferred_element_type=F32)
    a_b = z[:, 0:cd]
    a_c = z[:, cd:2 * cd]
    a_x = z[:, 2 * cd:3 * cd]
    b_u = z[:, 3 * cd:3 * cd + gd]
    b_v = z[:, 3 * cd + gd:3 * cd + 2 * gd]

    g = a_c * a_x

    @pl.when(i % blocks_per_seq == 0)
    def _():
        tail_ref[...] = jnp.zeros_like(tail_ref)

    tail = tail_ref[...]
    row = lax.broadcasted_iota(jnp.int32, g.shape, 0)
    gm1 = jnp.where(row == 0, tail[7:8], pltpu.roll(g, 1, 0))
    gm2 = jnp.where(row == 0, tail[6:7], jnp.where(row == 1, tail[7:8], pltpu.roll(g, 2, 0)))
    tail_ref[...] = g[rows - 8:rows]
    cw = cw_ref[...]
    y_a = a_b * (gm2 * cw[0:1] + gm1 * cw[1:2] + g * cw[2:3])

    u = _gelu(b_u)
    v = _gelu(b_v)
    mu = jnp.mean(v, axis=-1, keepdims=True)
    vc = v - mu
    var = jnp.mean(vc * vc, axis=-1, keepdims=True)
    vn = (vc * lax.rsqrt(var + EPS) * lng_ref[...] + lnb_ref[...]).astype(BF16)
    ri = lax.broadcasted_iota(jnp.int32, (CHUNK, CHUNK), 0)
    ci = lax.broadcasted_iota(jnp.int32, (CHUNK, CHUNK), 1)
    causal = ri >= ci
    bst = bst_ref[...]
    for k in range(GMLP_HEADS):
        w_k = jnp.where(causal, ws_ref[k], 0.0).astype(BF16)
        b_k = bst[:, k:k + 1]
        for c in range(rows // CHUNK):
            rs = slice(c * CHUNK, (c + 1) * CHUNK)
            cs = slice(k * hd, (k + 1) * hd)
            mixed = jnp.dot(w_k, vn[rs, cs], preferred_element_type=F32) + b_k
            yb_ref[rs, cs] = (u[rs, cs] * mixed).astype(BF16)

    out = jnp.dot(y_a.astype(BF16), wout_ref[0:cd, :], preferred_element_type=F32)
    out = out + jnp.dot(yb_ref[...], wout_ref[cd:cd + gd, :], preferred_element_type=F32)
    o_ref[...] = x + out


def _mixer(h, pending, seq, *weights):
    if pending is None:
        return _mixer_call(h, None, 0, 1, seq, *weights)
    y_chunks, route = pending
    for c, y in enumerate(y_chunks):
        h = _mixer_call(h, (y, route), c, len(y_chunks), seq, *weights)
    return h


def _mixer_call(h, pending, chunk, n_chunks, seq, g1, w_in, conv_w, ln_g, ln_b, w_s, b_s, w_out):
    n, d = h.shape
    rows = _row_block(seq, MIXER_ROWS)
    cd = conv_w.shape[0]
    gd = ln_g.shape[0]
    has_pending = pending is not None
    steps = n // rows // n_chunks
    base = chunk * steps
    assert (steps * rows) % seq == 0
    kern = functools.partial(_mixer_kernel, blocks_per_seq=seq // rows, has_pending=has_pending)
    return pl.pallas_call(
        kern,
        grid=(steps,),
        in_specs=[pl.BlockSpec((rows, d), lambda i: (i + base, 0))]
        + (_moe_specs(pending, rows, base) if has_pending else [])
        + [
            _full((1, d)),
            _full(w_in.shape),
            _full((CONV_WIDTH, cd)),
            _full((1, gd)),
            _full((1, gd)),
            _full(w_s.shape),
            _full((CHUNK, GMLP_HEADS)),
            _full(w_out.shape),
        ],
        out_specs=pl.BlockSpec((rows, d), lambda i: (i + base, 0)),
        out_shape=jax.ShapeDtypeStruct((n, d), F32),
        scratch_shapes=[pltpu.VMEM((8, cd), F32), pltpu.VMEM((rows, gd), BF16)],
        input_output_aliases={0: 0} if n_chunks > 1 else {},
        compiler_params=_params(("arbitrary",)),
        name="mixer",
    )(h, *(pending or ()), g1.reshape(1, d), w_in.astype(BF16), conv_w.T, ln_g.reshape(1, gd),
      ln_b.reshape(1, gd), w_s, b_s.T, w_out.astype(BF16))


def _swiglu(xn, wg, wu, wd):
    h1 = jnp.dot(xn, wg, preferred_element_type=F32)
    h2 = jnp.dot(xn, wu, preferred_element_type=F32)
    a = (h1 / (1.0 + jnp.exp(-h1)) * h2).astype(BF16)
    return jnp.dot(a, wd, preferred_element_type=F32)


def _dense_ffn_kernel(x_ref, g_ref, wg_ref, wu_ref, wd_ref, *rest):
    n_side = (len(rest) - 1) // 2
    side_in, o_ref, side_out = rest[:n_side], rest[n_side], rest[n_side + 1:]
    x = x_ref[...]
    xn = _rms(x, g_ref[...]).astype(BF16)
    o_ref[...] = x + _swiglu(xn, wg_ref[...], wu_ref[...], wd_ref[...])
    for src, dst in zip(side_in, side_out):
        dst[...] = src[...].astype(dst.dtype)


def _side_cast_ok(w, steps):
    rows = int(np.prod(w.shape[:-1]))
    return rows % steps == 0 and (rows // steps) % 16 == 0 and w.shape[-1] % LANES == 0


def _dense_ffn(h, g, wg, wu, wd, side=()):
    n, d = h.shape
    rows = _row_block(n, FFN_ROWS)
    steps = n // rows
    side2d = [w.reshape(-1, w.shape[-1]) for w in side]
    side_specs = [pl.BlockSpec((w.shape[0] // steps, w.shape[1]), lambda i: (i, 0)) for w in side2d]
    outs = pl.pallas_call(
        _dense_ffn_kernel,
        grid=(steps,),
        in_specs=[
            pl.BlockSpec((rows, d), lambda i: (i, 0)),
            _full((1, d)),
            _full(wg.shape),
            _full(wu.shape),
            _full(wd.shape),
        ] + side_specs,
        out_specs=[pl.BlockSpec((rows, d), lambda i: (i, 0))] + side_specs,
        out_shape=[jax.ShapeDtypeStruct((n, d), F32)]
        + [jax.ShapeDtypeStruct(w.shape, BF16) for w in side2d],
        compiler_params=_params(("arbitrary",)),
        name="dense_ffn",
    )(h, g.reshape(1, d), wg.astype(BF16), wu.astype(BF16), wd.astype(BF16), *side2d)
    return outs[0], [o.reshape(w.shape) for o, w in zip(outs[1:], side)]


def _pack_bf16_pairs(x, piece):
    half = x.shape[1] // 2
    bits = lax.bitcast_convert_type(x.astype(BF16).astype(F32), jnp.uint32)
    return [(bits[:, half + p * piece:half + (p + 1) * piece] & jnp.uint32(0xFFFF0000))
            | (bits[:, p * piece:(p + 1) * piece] >> 16) for p in range(half // piece)]


def _unpack_bf16_pair(packed):
    lo = lax.bitcast_convert_type(packed << 16, F32)
    hi = lax.bitcast_convert_type(packed & jnp.uint32(0xFFFF0000), F32)
    return lo, hi


def _moe_ffn_kernel(be_ref, used_ref, x_ref, wg_ref, wu_ref, wd_ref, o_ref, xn_ref, acc_ref, *,
                    n_steps):
    del be_ref
    i = pl.program_id(0)
    f = pl.program_id(1)
    parts, _, piece = x_ref.shape
    half = parts * piece
    active = i < used_ref[0]

    def write_out(val):
        for p, words in enumerate(_pack_bf16_pairs(val, piece)):
            o_ref[p] = words

    def step(first, last):
        if first:
            for p in range(parts):
                lo, hi = _unpack_bf16_pair(x_ref[p])
                xn_ref[:, p * piece:(p + 1) * piece] = lo.astype(BF16)
                xn_ref[:, half + p * piece:half + (p + 1) * piece] = hi.astype(BF16)
        part = _swiglu(xn_ref[...], wg_ref[0, 0], wu_ref[0, 0], wd_ref[0, 0])
        if last:
            write_out(part if first else acc_ref[...] + part)
        elif first:
            acc_ref[...] = part
        else:
            acc_ref[...] += part

    if n_steps == 1:
        pl.when(active)(functools.partial(step, True, True))
    else:
        pl.when(active & (f == 0))(functools.partial(step, True, False))
        if n_steps > 2:
            pl.when(active & (f > 0) & (f < n_steps - 1))(functools.partial(step, False, False))
        pl.when(active & (f == n_steps - 1))(functools.partial(step, False, True))

    @pl.when(jnp.logical_not(active) & (f == n_steps - 1))
    def _():
        o_ref[...] = jnp.zeros_like(o_ref)


def _moe_ffn(x_sorted, block_expert, n_used, layer, wg, wu, wd):
    parts, n_rows, piece = x_sorted.shape
    d = 2 * parts * piece
    fdim = wg.shape[3]
    rows = MOE_ROWS
    cols = _col_block(fdim, fdim // MOE_COL_SPLIT)
    n_steps = fdim // cols

    def col(i, f, used):
        return jnp.where(i < used[0], f, n_steps - 1)

    grid_spec = pltpu.PrefetchScalarGridSpec(
        num_scalar_prefetch=2,
        grid=(n_rows // rows, n_steps),
        in_specs=[
            pl.BlockSpec((parts, rows, piece),
                         lambda i, f, be, used: (0, jnp.minimum(i, used[0] - 1), 0)),
            pl.BlockSpec((1, 1, d, cols), lambda i, f, be, used: (layer, be[i], 0, col(i, f, used))),
            pl.BlockSpec((1, 1, d, cols), lambda i, f, be, used: (layer, be[i], 0, col(i, f, used))),
            pl.BlockSpec((1, 1, cols, d), lambda i, f, be, used: (layer, be[i], col(i, f, used), 0)),
        ],
        out_specs=pl.BlockSpec((parts, rows, piece), lambda i, f, be, used: (0, i, 0)),
        scratch_shapes=[pltpu.VMEM((rows, d), BF16), pltpu.VMEM((rows, d), F32)],
    )
    kern = functools.partial(_moe_ffn_kernel, n_steps=n_steps)
    return pl.pallas_call(
        kern,
        grid_spec=grid_spec,
        out_shape=jax.ShapeDtypeStruct((parts, n_rows, piece), jnp.uint32),
        compiler_params=_params(("arbitrary", "arbitrary")),
        name="moe_ffn",
    )(block_expert, n_used, x_sorted, wg, wu, wd)


def _cast_kernel(x_ref, o_ref):
    o_ref[...] = x_ref[...].astype(o_ref.dtype)


def _to_bf16(w):
    shape = w.shape
    w2 = w.reshape(-1, shape[-1])
    pref = CAST_BLOCK_BYTES // (4 * shape[-1]) // LANES * LANES
    rows = _row_block(w2.shape[0], pref)
    out = pl.pallas_call(
        _cast_kernel,
        grid=(w2.shape[0] // rows,),
        in_specs=[pl.BlockSpec((rows, shape[-1]), lambda i: (i, 0))],
        out_specs=pl.BlockSpec((rows, shape[-1]), lambda i: (i, 0)),
        out_shape=jax.ShapeDtypeStruct(w2.shape, BF16),
        compiler_params=_params(("arbitrary",)),
        name="cast_bf16",
    )(w2)
    return out.reshape(shape)


def _qkv_kernel(h_ref, g_ref, pos_ref, w_ref, b_ref, invf_ref, mc_ref, q_ref, kv_ref, *, q_dim):
    x = h_ref[...]
    xn = _rms(x, g_ref[...]).astype(BF16)
    z = jnp.dot(xn, w_ref[...], preferred_element_type=F32) + b_ref[...]
    ang = pos_ref[...].astype(F32) * invf_ref[...]
    reps = LANES // ang.shape[0]
    cos = jnp.concatenate([jnp.cos(ang)] * reps, axis=0).T
    sin = jnp.concatenate([jnp.sin(ang)] * reps, axis=0).T
    cos = jnp.where(mc_ref[...] != 0.0, cos, 1.0)
    sin = sin * mc_ref[...]

    def rope(t):
        return t * cos + pltpu.roll(t, LANES // 2, 1) * sin

    for j in range(q_dim // LANES):
        cs = slice(j * LANES, (j + 1) * LANES)
        q_ref[:, cs] = (rope(z[:, cs]) * Q_SCALE).astype(BF16)
    kv_ref[:, 0:LANES] = rope(z[:, q_dim:q_dim + LANES]).astype(BF16)
    kv_ref[:, LANES:2 * LANES] = z[:, q_dim + LANES:q_dim + 2 * LANES].astype(BF16)


QK_GROUP = LANES // 4


def _qk_tile_layout():
    lane = np.arange(LANES)
    group, off = lane // QK_GROUP, lane % QK_GROUP
    half = ROPE_DIM // 2
    head = group % 2
    second = group // 2
    rest = QK_GROUP - half
    dim = np.where(off < half, second * half + off, ROPE_DIM + second * rest + (off - half))
    return head, dim


def _qk_column_order(n_cols):
    head, dim = _qk_tile_layout()
    tile = np.arange(n_cols) // LANES
    return tile * LANES + np.tile(head * HEAD_DIM + dim, n_cols // LANES)


def _rope_sign_lanes():
    _, dim = _qk_tile_layout()
    half = ROPE_DIM // 2
    sign = np.where(dim < half, -1.0, np.where(dim < ROPE_DIM, 1.0, 0.0)).astype(np.float32)
    return jnp.asarray(sign[None, :])


def _qkv(h, g, positions, w_qkv, b_qkv, q_dim):
    n, d = h.shape
    qkv_dim = w_qkv.shape[1]
    assert qkv_dim == q_dim + 2 * LANES and QK_GROUP % (ROPE_DIM // 2) == 0
    rows = _row_block(n, PROJ_ROWS)
    inv_freq = ROPE_THETA ** (-jnp.arange(0, ROPE_DIM, 2, dtype=F32) / ROPE_DIM)
    order = np.concatenate([_qk_column_order(q_dim + LANES), np.arange(q_dim + LANES, qkv_dim)])
    w_qkv = w_qkv[:, order]
    b_qkv = b_qkv[order]
    kern = functools.partial(_qkv_kernel, q_dim=q_dim)
    return pl.pallas_call(
        kern,
        grid=(n // rows,),
        in_specs=[
            pl.BlockSpec((rows, d), lambda i: (i, 0)),
            _full((1, d)),
            pl.BlockSpec((1, rows), lambda i: (0, i)),
            _full(w_qkv.shape),
            _full((1, qkv_dim)),
            _full((ROPE_DIM // 2, 1)),
            _full((1, LANES)),
        ],
        out_specs=[
            pl.BlockSpec((rows, q_dim), lambda i: (i, 0)),
            pl.BlockSpec((rows, 2 * LANES), lambda i: (i, 0)),
        ],
        out_shape=[
            jax.ShapeDtypeStruct((n, q_dim), BF16),
            jax.ShapeDtypeStruct((n, 2 * LANES), BF16),
        ],
        compiler_params=_params(("arbitrary",)),
        name="qkv_rope",
    )(h, g.reshape(1, d), positions.reshape(1, n), w_qkv.astype(BF16),
      b_qkv.reshape(1, qkv_dim), inv_freq.reshape(-1, 1), _rope_sign_lanes())


def _attn_kernel(sink_ref, q_ref, kvc_ref, kvp_ref, h_ref, wo_ref, bo_ref, out_ref,
                 kbuf, vbuf, o_buf, *, wpb):
    rows = q_ref.shape[0]
    tiles = q_ref.shape[1] // LANES // 2
    j = pl.program_id(1)
    kbuf[0:WINDOW, :] = kvp_ref[:, 0:LANES]
    kbuf[WINDOW:WINDOW + rows, :] = kvc_ref[:, 0:LANES]
    vbuf[0:WINDOW, :] = kvp_ref[:, LANES:2 * LANES]
    vbuf[WINDOW:WINDOW + rows, :] = kvc_ref[:, LANES:2 * LANES]

    from_prev = (lax.broadcasted_iota(jnp.int32, (WINDOW, WINDOW), 1)
                 > lax.broadcasted_iota(jnp.int32, (WINDOW, WINDOW), 0))
    kv_lane = lax.broadcasted_iota(jnp.int32, (2 * WINDOW, LANES), 1)
    first_half = kv_lane < HEAD_DIM
    head0_lanes = (kv_lane // QK_GROUP) % 2 == 0
    out_first_half = lax.broadcasted_iota(jnp.int32, (WINDOW, LANES), 1) < HEAD_DIM
    ones_lo = jnp.where(first_half, 1.0, 0.0)
    ones_hi = jnp.where(first_half, 0.0, 1.0)
    nt = (((1,), (1,)), ((), ()))

    def window(n, carry):
        r0 = pl.multiple_of(n * WINDOW, WINDOW)
        kt = kbuf[pl.ds(r0, 2 * WINDOW), :].astype(F32)
        vt = vbuf[pl.ds(r0, 2 * WINDOW), :].astype(F32)
        prev_bias = jnp.where(j * wpb + n > 0, 0.0, -jnp.inf)
        k0_lo = jnp.where(head0_lanes, kt, 0.0)
        k1_hi = jnp.where(head0_lanes, 0.0, kt)
        v0_lo = jnp.where(first_half, vt, 0.0)
        v1_hi = jnp.where(first_half, 0.0, vt)
        k_both = (
            jnp.concatenate([k0_lo, pltpu.roll(k0_lo, QK_GROUP, 1)], axis=0).astype(BF16),
            jnp.concatenate([pltpu.roll(k1_hi, LANES - QK_GROUP, 1), k1_hi], axis=0).astype(BF16))
        v_both = (
            jnp.concatenate([jnp.concatenate([v0_lo, ones_lo], axis=1),
                             jnp.concatenate([pltpu.roll(v0_lo, HEAD_DIM, 1), ones_hi], axis=1)],
                            axis=0).astype(BF16),
            jnp.concatenate([jnp.concatenate([pltpu.roll(v1_hi, HEAD_DIM, 1), ones_lo], axis=1),
                             jnp.concatenate([v1_hi, ones_hi], axis=1)], axis=0).astype(BF16))
        for kh in range(2):
            for t0 in range(0, tiles, ATTN_STACK):
                group = [kh * tiles + t0 + u for u in range(ATTN_STACK)]
                q_stack = jnp.concatenate(
                    [q_ref[pl.ds(r0, WINDOW), t * LANES:(t + 1) * LANES] for t in group], axis=0)
                s_all = lax.dot_general(q_stack, k_both[kh], nt, preferred_element_type=F32)
                p_rows, corr = [], []
                for u, t in enumerate(group):
                    p_cols, corr_t = [], []
                    for parity in range(2):
                        sink = sink_ref[t * 2 + parity]
                        c0 = parity * 2 * WINDOW
                        s_prev = s_all[u * WINDOW:(u + 1) * WINDOW, c0:c0 + WINDOW] + prev_bias
                        s_cur = s_all[u * WINDOW:(u + 1) * WINDOW, c0 + WINDOW:c0 + 2 * WINDOW]
                        s = jnp.where(from_prev, s_prev, s_cur)
                        m = jnp.maximum(jnp.max(s, axis=-1, keepdims=True), sink)
                        p = jnp.exp2(s - m)
                        p_cols += [jnp.where(from_prev, p, 0.0), jnp.where(from_prev, 0.0, p)]
                        corr_t.append(jnp.exp2(sink - m))
                    p_rows.append(jnp.concatenate(p_cols, axis=1).astype(BF16))
                    corr.append(corr_t)
                pv = jnp.dot(jnp.concatenate(p_rows, axis=0), v_both[kh],
                             preferred_element_type=F32)
                for u, t in enumerate(group):
                    num = pv[u * WINDOW:(u + 1) * WINDOW, 0:LANES]
                    den = (pv[u * WINDOW:(u + 1) * WINDOW, LANES:2 * LANES]
                           + jnp.where(out_first_half, corr[u][0], corr[u][1]))
                    o_buf[pl.ds(r0, WINDOW), t * LANES:(t + 1) * LANES] = (num / den).astype(BF16)
        return carry

    lax.fori_loop(0, rows // WINDOW, window, 0, unroll=True)
    out_ref[...] = (h_ref[...] + jnp.dot(o_buf[...], wo_ref[...], preferred_element_type=F32)
                    + bo_ref[...])


def _attention(h, q, kv, sinks, w_o, b_o, batch, seq):
    n, q_dim = q.shape
    d = h.shape[1]
    assert kv.shape[1] == 2 * LANES and (q_dim // HEAD_DIM) % 4 == 0
    rows = _row_block(seq, ATTN_ROWS)
    bps = seq // rows
    wpb = rows // WINDOW
    wps = seq // WINDOW

    grid_spec = pltpu.PrefetchScalarGridSpec(
        num_scalar_prefetch=1,
        grid=(batch, bps),
        in_specs=[
            pl.BlockSpec((rows, q_dim), lambda b, j, s: (b * bps + j, 0)),
            pl.BlockSpec((rows, 2 * LANES), lambda b, j, s: (b * bps + j, 0)),
            pl.BlockSpec((WINDOW, 2 * LANES),
                         lambda b, j, s: (b * wps + jnp.maximum(j * wpb - 1, 0), 0)),
            pl.BlockSpec((rows, d), lambda b, j, s: (b * bps + j, 0)),
            pl.BlockSpec((q_dim, d), lambda b, j, s: (0, 0)),
            pl.BlockSpec((1, d), lambda b, j, s: (0, 0)),
        ],
        out_specs=pl.BlockSpec((rows, d), lambda b, j, s: (b * bps + j, 0)),
        scratch_shapes=[pltpu.VMEM((rows + WINDOW, LANES), BF16),
                        pltpu.VMEM((rows + WINDOW, LANES), BF16),
                        pltpu.VMEM((rows, q_dim), BF16)],
    )
    kern = functools.partial(_attn_kernel, wpb=wpb)
    return pl.pallas_call(
        kern,
        grid_spec=grid_spec,
        out_shape=jax.ShapeDtypeStruct((n, d), F32),
        compiler_params=_params(("arbitrary", "arbitrary")),
        name="swa_attention",
    )(sinks.astype(F32) * LOG2_E, q, kv, kv, h, w_o.astype(BF16), b_o.reshape(1, d))


R_IDX0, R_IDX1, R_GATE0, R_GATE1, R_RANK0, R_RANK1 = range(6)


def _router_kernel(h_ref, g_ref, rwt_ref, xpk_ref, route_ref, route_t_ref, cnt_ref, tri_ref,
                   carry_ref):
    rows = h_ref.shape[0]
    i = pl.program_id(0)

    @pl.when(i == 0)
    def _():
        r = lax.broadcasted_iota(jnp.int32, (rows, rows), 0)
        c = lax.broadcasted_iota(jnp.int32, (rows, rows), 1)
        tri_ref[...] = jnp.where(r < c, 1.0, 0.0).astype(BF16)
        carry_ref[...] = jnp.zeros_like(carry_ref)

    xn = _rms(h_ref[...], g_ref[...])
    xb = xn.astype(BF16)
    for p, words in enumerate(_pack_bf16_pairs(xn, xpk_ref.shape[2])):
        xpk_ref[p] = words

    nt = (((1,), (1,)), ((), ()))
    logits = lax.dot_general(rwt_ref[...], xb, nt, preferred_element_type=F32)[0:N_EXPERTS]
    ex = lax.broadcasted_iota(jnp.int32, logits.shape, 0)
    m1 = jnp.max(logits, axis=0, keepdims=True)
    i1 = jnp.min(jnp.where(logits == m1, ex, N_EXPERTS), axis=0, keepdims=True)
    lg2 = jnp.where(ex == i1, -jnp.inf, logits)
    m2 = jnp.max(lg2, axis=0, keepdims=True)
    i2 = jnp.min(jnp.where(lg2 == m2, ex, N_EXPERTS), axis=0, keepdims=True)
    e = jnp.exp(m2 - m1)
    g1 = 1.0 / (1.0 + e)
    g2 = e / (1.0 + e)

    sel_f = jnp.where((ex == i1) | (ex == i2), 1.0, 0.0)
    sel_pad = jnp.concatenate([sel_f, jnp.zeros_like(sel_f)], axis=0).astype(BF16)
    carry = carry_ref[:, 0:1]
    before = jnp.dot(sel_pad, tri_ref[...], preferred_element_type=F32)[0:N_EXPERTS] + carry
    r1 = jnp.sum(jnp.where(ex == i1, before, 0.0), axis=0, keepdims=True)
    r2 = jnp.sum(jnp.where(ex == i2, before, 0.0), axis=0, keepdims=True)
    carry = carry + jnp.sum(sel_f, axis=1, keepdims=True)
    carry_ref[...] = jnp.broadcast_to(carry, carry_ref.shape)
    cnt_ref[...] = jnp.broadcast_to(carry, cnt_ref.shape)

    fields = [None] * 8
    for k, val in ((R_IDX0, i1.astype(F32)), (R_IDX1, i2.astype(F32)), (R_GATE0, g1),
                   (R_GATE1, g2), (R_RANK0, r1), (R_RANK1, r2)):
        fields[k] = val
    route_t = jnp.concatenate([f if f is not None else jnp.zeros_like(g1) for f in fields], axis=0)
    route_t_ref[...] = route_t
    route_ref[...] = jnp.concatenate([route_t] * (LANES // 8), axis=0).T


def _router(h, g, router_w):
    n, d = h.shape
    rows = _row_block(n, ROUTER_ROWS)
    parts = d // 2 // SC_PIECE
    rwt = jnp.zeros((16, d), BF16).at[0:N_EXPERTS, :].set(router_w.T.astype(BF16))
    return pl.pallas_call(
        _router_kernel,
        grid=(n // rows,),
        in_specs=[
            pl.BlockSpec((rows, d), lambda i: (i, 0)),
            _full((1, d)),
            _full((16, d)),
        ],
        out_specs=[
            pl.BlockSpec((parts, rows, SC_PIECE), lambda i: (0, i, 0)),
            pl.BlockSpec((rows, LANES), lambda i: (i, 0)),
            pl.BlockSpec((8, rows), lambda i: (0, i)),
            _full((8, LANES)),
        ],
        out_shape=[
            jax.ShapeDtypeStruct((parts, n, SC_PIECE), jnp.uint32),
            jax.ShapeDtypeStruct((n, LANES), F32),
            jax.ShapeDtypeStruct((8, n), F32),
            jax.ShapeDtypeStruct((8, LANES), F32),
        ],
        scratch_shapes=[pltpu.VMEM((rows, rows), BF16), pltpu.VMEM((8, LANES), F32)],
        compiler_params=_params(("arbitrary",)),
        name="moe_router",
    )(h, g.reshape(1, d), rwt)


def _sc_mesh():
    return plsc.VectorSubcoreMesh(core_axis_name="core", subcore_axis_name="subcore")


def _gather_pieces(src, idx):
    m = idx.shape[0]
    width = src.shape[1]
    assert m % (SC_WINDOW * SC_WORKERS) == 0

    @functools.partial(pl.kernel, out_type=jax.ShapeDtypeStruct((m, width), src.dtype),
                       mesh=_sc_mesh(), scratch_types=[])
    def gather_kernel(src_hbm, idx_hbm, out_hbm):
        def body(idx_vmem, out_vmem):
            pltpu.sync_copy(src_hbm.at[idx_vmem.at[0]], out_vmem)

        pltpu.emit_pipeline(
            body,
            grid=(m // SC_WINDOW,),
            in_specs=[pl.BlockSpec((1, SC_WINDOW), lambda i: (0, i))],
            out_specs=[pl.BlockSpec((SC_WINDOW, width), lambda i: (i, 0))],
            core_axis_name=("core", "subcore"),
            dimension_semantics=(pltpu.PARALLEL,),
        )(idx_hbm, out_hbm)

    return gather_kernel(src, idx.reshape(1, m))


def _scatter_pieces(src, idx, out_rows):
    copies, m = idx.shape
    width = src.shape[1]
    assert m == src.shape[0] and m % (SC_WINDOW * SC_WORKERS) == 0

    @functools.partial(pl.kernel, out_type=jax.ShapeDtypeStruct((out_rows, width), src.dtype),
                       mesh=_sc_mesh(), scratch_types=[])
    def scatter_kernel(src_hbm, *refs):
        idx_hbm, out_hbm = refs[:copies], refs[copies]

        def body(src_vmem, *idx_vmem):
            for iv in idx_vmem:
                pltpu.sync_copy(src_vmem, out_hbm.at[iv.at[0]])

        pltpu.emit_pipeline(
            body,
            grid=(m // SC_WINDOW,),
            in_specs=[pl.BlockSpec((SC_WINDOW, width), lambda i: (i, 0))]
            + [pl.BlockSpec((1, SC_WINDOW), lambda i: (0, i))] * copies,
            out_specs=[],
            core_axis_name=("core", "subcore"),
            dimension_semantics=(pltpu.PARALLEL,),
        )(src_hbm, *idx_hbm)

    return scatter_kernel(src, *[idx[j].reshape(1, m) for j in range(copies)])


def _moe_combined(h_ref, y_ref, route_ref):
    parts, _, _, piece = y_ref.shape
    route = route_ref[...]
    g0 = route[:, R_GATE0:R_GATE0 + 1]
    g1 = route[:, R_GATE1:R_GATE1 + 1]
    lo, hi = [], []
    for p in range(parts):
        lo0, hi0 = _unpack_bf16_pair(y_ref[p, 0])
        lo1, hi1 = _unpack_bf16_pair(y_ref[p, 1])
        lo.append(g0 * lo0 + g1 * lo1)
        hi.append(g0 * hi0 + g1 * hi1)
    return h_ref[...] + jnp.concatenate(lo + hi, axis=1)


def _moe_specs(pending, rows, base):
    y_pairs, route = pending
    parts, _, _, piece = y_pairs.shape
    return [pl.BlockSpec((parts, 2, rows, piece), lambda i: (0, 0, i, 0)),
            pl.BlockSpec((rows, LANES), lambda i: (i + base, 0))]


def _final_kernel(h_ref, y_ref, route_ref, g_ref, o_ref):
    o_ref[...] = _rms(_moe_combined(h_ref, y_ref, route_ref), g_ref[...])


def _final(h, pending, final_g):
    n, d = h.shape
    y_chunks, route = pending
    n_chunks = len(y_chunks)
    rows = _row_block(n // n_chunks, PROJ_ROWS)
    steps = n // rows // n_chunks
    for c, y in enumerate(y_chunks):
        base = c * steps
        h = pl.pallas_call(
            _final_kernel,
            grid=(steps,),
            in_specs=[pl.BlockSpec((rows, d), lambda i, base=base: (i + base, 0))]
            + _moe_specs((y, route), rows, base) + [_full((1, d))],
            out_specs=pl.BlockSpec((rows, d), lambda i, base=base: (i + base, 0)),
            out_shape=jax.ShapeDtypeStruct((n, d), F32),
            input_output_aliases={0: 0} if n_chunks > 1 else {},
            compiler_params=_params(("arbitrary",)),
            name="moe_combine_final_norm",
        )(h, y, route, final_g.reshape(1, d))
    return h


def _moe(h, g, router_w, layer, wg, wu, wd, n_chunks):
    n, d = h.shape
    xpk, route, route_t, cnt = _router(h, g, router_w)
    parts = xpk.shape[0]

    top_idx = route_t[R_IDX0:R_IDX1 + 1].astype(jnp.int32)
    rank = route_t[R_RANK0:R_RANK1 + 1].astype(jnp.int32)
    sizes = cnt[0:N_EXPERTS, 0].astype(jnp.int32)
    padded = ((sizes + MOE_ROWS - 1) // MOE_ROWS) * MOE_ROWS
    pends = jnp.cumsum(padded)
    pstarts = pends - padded
    dest = rank
    for e in range(N_EXPERTS):
        dest = dest + jnp.where(top_idx == e, pstarts[e], 0)
    n_rows = 2 * n + N_EXPERTS * MOE_ROWS
    n_blocks = n_rows // MOE_ROWS
    block_start = jnp.arange(n_blocks, dtype=jnp.int32) * MOE_ROWS
    block_expert = jnp.minimum(
        jnp.sum((block_start[:, None] >= pends[None, :]).astype(jnp.int32), axis=1),
        N_EXPERTS - 1)
    n_used = (pends[N_EXPERTS - 1:] // MOE_ROWS).astype(jnp.int32)

    off = jnp.arange(parts, dtype=jnp.int32) * n_rows
    scatter_idx = (dest[:, None, :] + off[None, :, None]).reshape(2, parts * n)
    x_sorted = _scatter_pieces(xpk.reshape(parts * n, SC_PIECE), scatter_idx, parts * n_rows)
    y_rows = _moe_ffn(x_sorted.reshape(parts, n_rows, SC_PIECE), block_expert, n_used, layer,
                      wg, wu, wd)

    y_flat = y_rows.reshape(parts * n_rows, SC_PIECE)
    nc = n // n_chunks
    chunk_dest = dest.reshape(2, n_chunks, nc).transpose(1, 0, 2)
    gather_idx = (chunk_dest[:, None] + off[None, :, None, None]).reshape(n_chunks, -1)
    y_chunks = [_gather_pieces(y_flat, gather_idx[c]).reshape(parts, 2, nc, SC_PIECE)
                for c in range(n_chunks)]
    return y_chunks, route


def kernel(x, positions, final_norm_g, ev_norm1_g, ev_w_in, ev_conv_w, ev_ln_g, ev_ln_b, ev_spatial_w, ev_spatial_b, ev_w_out, ev_norm2_g, ev_ffn_wg, ev_ffn_wu, ev_ffn_wd, od_norm1_g, od_w_qkv, od_b_qkv, od_sinks, od_w_o, od_b_o, od_norm2_g, od_router_w, od_exp_wg, od_exp_wu, od_exp_wd):
    batch, seq, d = x.shape
    depth = ev_norm1_g.shape[0] + od_norm1_g.shape[0]
    assert depth % 2 == 0, "the final norm is fused into the last (odd) layer's MoE combine"
    n_q_heads = od_sinks.shape[1]
    h = x.reshape(batch * seq, d)
    experts = [od_exp_wg, od_exp_wu, od_exp_wd]
    ffn_steps = (batch * seq) // _row_block(batch * seq, FFN_ROWS)
    ride_along = all(_side_cast_ok(w, ffn_steps) for w in experts)
    if not ride_along:
        experts = [_to_bf16(w) for w in experts]
    pending = None
    pieces_per_token = 2 * (d // 2 // SC_PIECE)
    n_chunks = max(c for c in range(1, COMBINE_CHUNKS + 1)
                   if batch % c == 0
                   and (batch // c * seq * pieces_per_token) % (SC_WINDOW * SC_WORKERS) == 0)
    for layer in range(depth):
        i = layer // 2
        if layer % 2 == 0:
            h = _mixer(h, pending, seq, ev_norm1_g[i], ev_w_in[i], ev_conv_w[i], ev_ln_g[i],
                       ev_ln_b[i], ev_spatial_w[i], ev_spatial_b[i], ev_w_out[i])
            side = experts if (ride_along and layer == 0) else ()
            h, cast = _dense_ffn(h, ev_norm2_g[i], ev_ffn_wg[i], ev_ffn_wu[i], ev_ffn_wd[i], side)
            if side:
                experts = cast
        else:
            q, kv = _qkv(h, od_norm1_g[i], positions, od_w_qkv[i], od_b_qkv[i],
                         n_q_heads * HEAD_DIM)
            h = _attention(h, q, kv, od_sinks[i], od_w_o[i], od_b_o[i], batch, seq)
            pending = _moe(h, od_norm2_g[i], od_router_w[i], i, *experts, n_chunks)
    return _final(h, pending, final_norm_g).reshape(batch, seq, d)
```

```python
import functools

import jax
import jax.numpy as jnp
import numpy as np
from jax import lax
from jax.experimental import pallas as pl
from jax.experimental.pallas import tpu as pltpu
from jax.experimental.pallas import tpu_sc as plsc

F32 = jnp.float32
BF16 = jnp.bfloat16

EPS = 1e-5
CHUNK = 128
GMLP_HEADS = 4
CONV_WIDTH = 3
HEAD_DIM = 64
WINDOW = 128
ROPE_DIM = HEAD_DIM // 4
ROPE_THETA = 500000.0
ATTN_SCALE = HEAD_DIM ** -0.5
LOG2_E = float(np.log2(np.e))
Q_SCALE = ATTN_SCALE * LOG2_E
N_EXPERTS = 8
LANES = 128
VMEM_LIMIT = 56 * 1024 * 1024

MIXER_ROWS = 512
FFN_ROWS = 512
CAST_BLOCK_BYTES = 8 * 1024 * 1024
PROJ_ROWS = 1024
ATTN_ROWS = 1024
ATTN_STACK = 2
ROUTER_ROWS = 1024
MOE_ROWS = 1024
COMBINE_CHUNKS = 4
MOE_COL_SPLIT = 2
SC_WORKERS = 32
SC_PIECE = 256
SC_WINDOW = 128


def _row_block(n, pref):
    b = min(n, pref)
    while n % b:
        b -= LANES
    return b


def _col_block(f, pref):
    b = min(f, pref)
    b -= b % LANES
    while f % b:
        b -= LANES
    return b


def _params(sem):
    return pltpu.CompilerParams(dimension_semantics=sem, vmem_limit_bytes=VMEM_LIMIT)


def _rms(x, g):
    return x * lax.rsqrt(jnp.mean(x * x, axis=-1, keepdims=True) + EPS) * g


def _gelu(x):
    return 0.5 * x * (1.0 + lax.erf(x * np.float32(np.sqrt(0.5))))


def _full(shape):
    return pl.BlockSpec(shape, lambda *_: (0,) * len(shape))


def _mixer_kernel(*refs, blocks_per_seq, has_pending):
    if has_pending:
        h_ref, y_ref, route_ref = refs[:3]
        refs = refs[3:]
    else:
        h_ref = refs[0]
        refs = refs[1:]
    (g1_ref, win_ref, cw_ref, lng_ref, lnb_ref, ws_ref, bst_ref, wout_ref, o_ref,
     tail_ref, yb_ref) = refs
    rows = h_ref.shape[0]
    cd = cw_ref.shape[1]
    gd = lng_ref.shape[1]
    hd = gd // GMLP_HEADS
    i = pl.program_id(0)

    x = _moe_combined(h_ref, y_ref, route_ref) if has_pending else h_ref[...]
    xn = _rms(x, g1_ref[...]).astype(BF16)
    z_g = jnp.dot(xn, win_ref[:, 3 * cd:3 * cd + 2 * gd], preferred_element_type=F32)
    b_u = z_g[:, 0:gd]
    b_v = z_g[:, gd:2 * gd]
    z_c = jnp.dot(xn, win_ref[:, 0:3 * cd], preferred_element_type=F32)
    a_b = z_c[:, 0:cd]
    a_c = z_c[:, cd:2 * cd]
    a_x = z_c[:, 2 * cd:3 * cd]

    g = a_c * a_x
    tail = jnp.where(i % blocks_per_seq == 0, 0.0, tail_ref[...])
    row = lax.broadcasted_iota(jnp.int32, g.shape, 0)
    gm1 = jnp.where(row == 0, tail[7:8], pltpu.roll(g, 1, 0))
    gm2 = jnp.where(row == 0, tail[6:7], jnp.where(row == 1, tail[7:8], pltpu.roll(g, 2, 0)))
    tail_ref[...] = g[rows - 8:rows]
    cw = cw_ref[...]
    y_a = a_b * (gm2 * cw[0:1] + gm1 * cw[1:2] + g * cw[2:3])

    u = _gelu(b_u)
    v = _gelu(b_v)
    mu = jnp.mean(v, axis=-1, keepdims=True)
    vc = v - mu
    var = jnp.mean(vc * vc, axis=-1, keepdims=True)
    vn = (vc * lax.rsqrt(var + EPS) * lng_ref[...] + lnb_ref[...]).astype(BF16)
    ri = lax.broadcasted_iota(jnp.int32, (CHUNK, CHUNK), 0)
    ci = lax.broadcasted_iota(jnp.int32, (CHUNK, CHUNK), 1)
    causal = ri >= ci
    bst = bst_ref[...]
    for k in range(GMLP_HEADS):
        w_k = jnp.where(causal, ws_ref[k], 0.0).astype(BF16)
        b_k = bst[:, k:k + 1]
        for c in range(rows // CHUNK):
            rs = slice(c * CHUNK, (c + 1) * CHUNK)
            cs = slice(k * hd, (k + 1) * hd)
            mixed = jnp.dot(w_k, vn[rs, cs], preferred_element_type=F32) + b_k
            yb_ref[rs, cs] = (u[rs, cs] * mixed).astype(BF16)

    out = jnp.dot(y_a.astype(BF16), wout_ref[0:cd, :], preferred_element_type=F32)
    out = out + jnp.dot(yb_ref[...], wout_ref[cd:cd + gd, :], preferred_element_type=F32)
    o_ref[...] = x + out


def _mixer(h, pending, seq, *weights):
    if pending is None:
        return _mixer_call(h, None, 0, 1, seq, *weights)
    y_chunks, route = pending
    for c, y in enumerate(y_chunks):
        h = _mixer_call(h, (y, route), c, len(y_chunks), seq, *weights)
    return h


def _mixer_call(h, pending, chunk, n_chunks, seq, g1, w_in, conv_w, ln_g, ln_b, w_s, b_s, w_out):
    n, d = h.shape
    rows = _row_block(seq, MIXER_ROWS)
    cd = conv_w.shape[0]
    gd = ln_g.shape[0]
    has_pending = pending is not None
    steps = n // rows // n_chunks
    base = chunk * steps
    assert (steps * rows) % seq == 0
    kern = functools.partial(_mixer_kernel, blocks_per_seq=seq // rows, has_pending=has_pending)
    return pl.pallas_call(
        kern,
        grid=(steps,),
        in_specs=[pl.BlockSpec((rows, d), lambda i: (i + base, 0))]
        + (_moe_specs(pending, rows, base) if has_pending else [])
        + [
            _full((1, d)),
            _full(w_in.shape),
            _full((CONV_WIDTH, cd)),
            _full((1, gd)),
            _full((1, gd)),
            _full(w_s.shape),
            _full((CHUNK, GMLP_HEADS)),
            _full(w_out.shape),
        ],
        out_specs=pl.BlockSpec((rows, d), lambda i: (i + base, 0)),
        out_shape=jax.ShapeDtypeStruct((n, d), F32),
        scratch_shapes=[pltpu.VMEM((8, cd), F32), pltpu.VMEM((rows, gd), BF16)],
        input_output_aliases={0: 0} if n_chunks > 1 else {},
        compiler_params=_params(("arbitrary",)),
        name="mixer",
    )(h, *(pending or ()), g1.reshape(1, d), w_in.astype(BF16), conv_w.T, ln_g.reshape(1, gd),
      ln_b.reshape(1, gd), w_s, b_s.T, w_out.astype(BF16))


def _swiglu(xn, wg, wu, wd):
    h1 = jnp.dot(xn, wg, preferred_element_type=F32)
    h2 = jnp.dot(xn, wu, preferred_element_type=F32)
    a = (h1 / (1.0 + jnp.exp(-h1)) * h2).astype(BF16)
    return jnp.dot(a, wd, preferred_element_type=F32)


def _dense_ffn_kernel(x_ref, g_ref, wg_ref, wu_ref, wd_ref, *rest):
    n_side = (len(rest) - 1) // 2
    side_in, o_ref, side_out = rest[:n_side], rest[n_side], rest[n_side + 1:]
    x = x_ref[...]
    xn = _rms(x, g_ref[...]).astype(BF16)
    o_ref[...] = x + _swiglu(xn, wg_ref[...], wu_ref[...], wd_ref[...])
    for src, dst in zip(side_in, side_out):
        dst[...] = src[...].astype(dst.dtype)


def _side_cast_ok(w, steps):
    rows = int(np.prod(w.shape[:-1]))
    return rows % steps == 0 and (rows // steps) % 16 == 0 and w.shape[-1] % LANES == 0


def _dense_ffn(h, g, wg, wu, wd, side=()):
    n, d = h.shape
    rows = _row_block(n, FFN_ROWS)
    steps = n // rows
    side2d = [w.reshape(-1, w.shape[-1]) for w in side]
    side_specs = [pl.BlockSpec((w.shape[0] // steps, w.shape[1]), lambda i: (i, 0)) for w in side2d]
    outs = pl.pallas_call(
        _dense_ffn_kernel,
        grid=(steps,),
        in_specs=[
            pl.BlockSpec((rows, d), lambda i: (i, 0)),
            _full((1, d)),
            _full(wg.shape),
            _full(wu.shape),
            _full(wd.shape),
        ] + side_specs,
        out_specs=[pl.BlockSpec((rows, d), lambda i: (i, 0))] + side_specs,
        out_shape=[jax.ShapeDtypeStruct((n, d), F32)]
        + [jax.ShapeDtypeStruct(w.shape, BF16) for w in side2d],
        compiler_params=_params(("arbitrary",)),
        name="dense_ffn",
    )(h, g.reshape(1, d), wg.astype(BF16), wu.astype(BF16), wd.astype(BF16), *side2d)
    return outs[0], [o.reshape(w.shape) for o, w in zip(outs[1:], side)]


def _pack_bf16_pairs(x, piece):
    half = x.shape[1] // 2
    bits = lax.bitcast_convert_type(x.astype(BF16).astype(F32), jnp.uint32)
    return [(bits[:, half + p * piece:half + (p + 1) * piece] & jnp.uint32(0xFFFF0000))
            | (bits[:, p * piece:(p + 1) * piece] >> 16) for p in range(half // piece)]


def _unpack_bf16_pair(packed):
    lo = lax.bitcast_convert_type(packed << 16, F32)
    hi = lax.bitcast_convert_type(packed & jnp.uint32(0xFFFF0000), F32)
    return lo, hi


def _moe_ffn_kernel(be_ref, used_ref, x_ref, wg_ref, wu_ref, wd_ref, o_ref, xn_ref, acc_ref, *,
                    n_steps):
    del be_ref
    i = pl.program_id(0)
    f = pl.program_id(1)
    parts, _, piece = x_ref.shape
    half = parts * piece
    active = i < used_ref[0]

    def write_out(val):
        for p, words in enumerate(_pack_bf16_pairs(val, piece)):
            o_ref[p] = words

    def step(first, last):
        if first:
            for p in range(parts):
                lo, hi = _unpack_bf16_pair(x_ref[p])
                xn_ref[:, p * piece:(p + 1) * piece] = lo.astype(BF16)
                xn_ref[:, half + p * piece:half + (p + 1) * piece] = hi.astype(BF16)
        part = _swiglu(xn_ref[...], wg_ref[0, 0], wu_ref[0, 0], wd_ref[0, 0])
        if last:
            write_out(part if first else acc_ref[...] + part)
        elif first:
            acc_ref[...] = part
        else:
            acc_ref[...] += part

    if n_steps == 1:
        pl.when(active)(functools.partial(step, True, True))
    else:
        pl.when(active & (f == 0))(functools.partial(step, True, False))
        if n_steps > 2:
            pl.when(active & (f > 0) & (f < n_steps - 1))(functools.partial(step, False, False))
        pl.when(active & (f == n_steps - 1))(functools.partial(step, False, True))

    @pl.when(jnp.logical_not(active) & (f == n_steps - 1))
    def _():
        o_ref[...] = jnp.zeros_like(o_ref)


def _moe_ffn(x_sorted, block_expert, n_used, layer, wg, wu, wd):
    parts, n_rows, piece = x_sorted.shape
    d = 2 * parts * piece
    fdim = wg.shape[3]
    rows = MOE_ROWS
    cols = _col_block(fdim, fdim // MOE_COL_SPLIT)
    n_steps = fdim // cols

    def col(i, f, used):
        return jnp.where(i < used[0], f, n_steps - 1)

    grid_spec = pltpu.PrefetchScalarGridSpec(
        num_scalar_prefetch=2,
        grid=(n_rows // rows, n_steps),
        in_specs=[
            pl.BlockSpec((parts, rows, piece),
                         lambda i, f, be, used: (0, jnp.minimum(i, used[0] - 1), 0)),
            pl.BlockSpec((1, 1, d, cols), lambda i, f, be, used: (layer, be[i], 0, col(i, f, used))),
            pl.BlockSpec((1, 1, d, cols), lambda i, f, be, used: (layer, be[i], 0, col(i, f, used))),
            pl.BlockSpec((1, 1, cols, d), lambda i, f, be, used: (layer, be[i], col(i, f, used), 0)),
        ],
        out_specs=pl.BlockSpec((parts, rows, piece), lambda i, f, be, used: (0, i, 0)),
        scratch_shapes=[pltpu.VMEM((rows, d), BF16), pltpu.VMEM((rows, d), F32)],
    )
    kern = functools.partial(_moe_ffn_kernel, n_steps=n_steps)
    return pl.pallas_call(
        kern,
        grid_spec=grid_spec,
        out_shape=jax.ShapeDtypeStruct((parts, n_rows, piece), jnp.uint32),
        compiler_params=_params(("arbitrary", "arbitrary")),
        name="moe_ffn",
    )(block_expert, n_used, x_sorted, wg, wu, wd)


def _cast_kernel(x_ref, o_ref):
    o_ref[...] = x_ref[...].astype(o_ref.dtype)


def _to_bf16(w):
    shape = w.shape
    w2 = w.reshape(-1, shape[-1])
    pref = CAST_BLOCK_BYTES // (4 * shape[-1]) // LANES * LANES
    rows = _row_block(w2.shape[0], pref)
    out = pl.pallas_call(
        _cast_kernel,
        grid=(w2.shape[0] // rows,),
        in_specs=[pl.BlockSpec((rows, shape[-1]), lambda i: (i, 0))],
        out_specs=pl.BlockSpec((rows, shape[-1]), lambda i: (i, 0)),
        out_shape=jax.ShapeDtypeStruct(w2.shape, BF16),
        compiler_params=_params(("arbitrary",)),
        name="cast_bf16",
    )(w2)
    return out.reshape(shape)


def _qkv_kernel(h_ref, g_ref, pos_ref, w_ref, b_ref, invf_ref, mc_ref, q_ref, kv_ref, *, q_dim):
    x = h_ref[...]
    xn = _rms(x, g_ref[...]).astype(BF16)
    z = jnp.dot(xn, w_ref[...], preferred_element_type=F32) + b_ref[...]
    ang = pos_ref[...].astype(F32) * invf_ref[...]
    reps = LANES // ang.shape[0]
    cos = jnp.concatenate([jnp.cos(ang)] * reps, axis=0).T
    sin = jnp.concatenate([jnp.sin(ang)] * reps, axis=0).T
    cos = jnp.where(mc_ref[...] != 0.0, cos, 1.0)
    sin = sin * mc_ref[...]

    def rope(t):
        return t * cos + pltpu.roll(t, LANES // 2, 1) * sin

    for j in range(q_dim // LANES):
        cs = slice(j * LANES, (j + 1) * LANES)
        q_ref[:, cs] = (rope(z[:, cs]) * Q_SCALE).astype(BF16)
    kv_ref[:, 0:LANES] = rope(z[:, q_dim:q_dim + LANES]).astype(BF16)
    kv_ref[:, LANES:2 * LANES] = z[:, q_dim + LANES:q_dim + 2 * LANES].astype(BF16)


QK_GROUP = LANES // 4


def _qk_tile_layout():
    lane = np.arange(LANES)
    group, off = lane // QK_GROUP, lane % QK_GROUP
    half = ROPE_DIM // 2
    head = group % 2
    second = group // 2
    rest = QK_GROUP - half
    dim = np.where(off < half, second * half + off, ROPE_DIM + second * rest + (off - half))
    return head, dim


def _qk_column_order(n_cols):
    head, dim = _qk_tile_layout()
    tile = np.arange(n_cols) // LANES
    return tile * LANES + np.tile(head * HEAD_DIM + dim, n_cols // LANES)


def _rope_sign_lanes():
    _, dim = _qk_tile_layout()
    half = ROPE_DIM // 2
    sign = np.where(dim < half, -1.0, np.where(dim < ROPE_DIM, 1.0, 0.0)).astype(np.float32)
    return jnp.asarray(sign[None, :])


def _qkv(h, g, positions, w_qkv, b_qkv, q_dim):
    n, d = h.shape
    qkv_dim = w_qkv.shape[1]
    assert qkv_dim == q_dim + 2 * LANES and QK_GROUP % (ROPE_DIM // 2) == 0
    rows = _row_block(n, PROJ_ROWS)
    inv_freq = ROPE_THETA ** (-jnp.arange(0, ROPE_DIM, 2, dtype=F32) / ROPE_DIM)
    order = np.concatenate([_qk_column_order(q_dim + LANES), np.arange(q_dim + LANES, qkv_dim)])
    w_qkv = w_qkv[:, order]
    b_qkv = b_qkv[order]
    kern = functools.partial(_qkv_kernel, q_dim=q_dim)
    return pl.pallas_call(
        kern,
        grid=(n // rows,),
        in_specs=[
            pl.BlockSpec((rows, d), lambda i: (i, 0)),
            _full((1, d)),
            pl.BlockSpec((1, rows), lambda i: (0, i)),
            _full(w_qkv.shape),
            _full((1, qkv_dim)),
            _full((ROPE_DIM // 2, 1)),
            _full((1, LANES)),
        ],
        out_specs=[
            pl.BlockSpec((rows, q_dim), lambda i: (i, 0)),
            pl.BlockSpec((rows, 2 * LANES), lambda i: (i, 0)),
        ],
        out_shape=[
            jax.ShapeDtypeStruct((n, q_dim), BF16),
            jax.ShapeDtypeStruct((n, 2 * LANES), BF16),
        ],
        compiler_params=_params(("arbitrary",)),
        name="qkv_rope",
    )(h, g.reshape(1, d), positions.reshape(1, n), w_qkv.astype(BF16),
      b_qkv.reshape(1, qkv_dim), inv_freq.reshape(-1, 1), _rope_sign_lanes())


def _attn_kernel(sink_ref, q_ref, kvc_ref, kvp_ref, h_ref, wo_ref, bo_ref, out_ref,
                 kbuf, vbuf, o_buf, *, wpb):
    rows = q_ref.shape[0]
    tiles = q_ref.shape[1] // LANES // 2
    j = pl.program_id(1)
    kbuf[0:WINDOW, :] = kvp_ref[:, 0:LANES]
    kbuf[WINDOW:WINDOW + rows, :] = kvc_ref[:, 0:LANES]
    vbuf[0:WINDOW, :] = kvp_ref[:, LANES:2 * LANES]
    vbuf[WINDOW:WINDOW + rows, :] = kvc_ref[:, LANES:2 * LANES]

    from_prev = (lax.broadcasted_iota(jnp.int32, (WINDOW, WINDOW), 1)
                 > lax.broadcasted_iota(jnp.int32, (WINDOW, WINDOW), 0))
    kv_lane = lax.broadcasted_iota(jnp.int32, (2 * WINDOW, LANES), 1)
    first_half = kv_lane < HEAD_DIM
    head0_lanes = (kv_lane // QK_GROUP) % 2 == 0
    out_first_half = lax.broadcasted_iota(jnp.int32, (WINDOW, LANES), 1) < HEAD_DIM
    ones_lo = jnp.where(first_half, 1.0, 0.0)
    ones_hi = jnp.where(first_half, 0.0, 1.0)
    nt = (((1,), (1,)), ((), ()))

    def window(n, carry):
        r0 = pl.multiple_of(n * WINDOW, WINDOW)
        kt = kbuf[pl.ds(r0, 2 * WINDOW), :].astype(F32)
        vt = vbuf[pl.ds(r0, 2 * WINDOW), :].astype(F32)
        prev_bias = jnp.where(j * wpb + n > 0, 0.0, -jnp.inf)
        k0_lo = jnp.where(head0_lanes, kt, 0.0)
        k1_hi = jnp.where(head0_lanes, 0.0, kt)
        v0_lo = jnp.where(first_half, vt, 0.0)
        v1_hi = jnp.where(first_half, 0.0, vt)
        k_both = (
            jnp.concatenate([k0_lo, pltpu.roll(k0_lo, QK_GROUP, 1)], axis=0).astype(BF16),
            jnp.concatenate([pltpu.roll(k1_hi, LANES - QK_GROUP, 1), k1_hi], axis=0).astype(BF16))
        v_both = (
            jnp.concatenate([jnp.concatenate([v0_lo, ones_lo], axis=1),
                             jnp.concatenate([pltpu.roll(v0_lo, HEAD_DIM, 1), ones_hi], axis=1)],
                            axis=0).astype(BF16),
            jnp.concatenate([jnp.concatenate([pltpu.roll(v1_hi, HEAD_DIM, 1), ones_lo], axis=1),
                             jnp.concatenate([v1_hi, ones_hi], axis=1)], axis=0).astype(BF16))
        for kh in range(2):
            for t0 in range(0, tiles, ATTN_STACK):
                group = [kh * tiles + t0 + u for u in range(ATTN_STACK)]
                q_stack = jnp.concatenate(
                    [q_ref[pl.ds(r0, WINDOW), t * LANES:(t + 1) * LANES] for t in group], axis=0)
                s_all = lax.dot_general(q_stack, k_both[kh], nt, preferred_element_type=F32)
                p_rows, corr = [], []
                for u, t in enumerate(group):
                    p_cols, corr_t = [], []
                    for parity in range(2):
                        sink = sink_ref[t * 2 + parity]
                        c0 = parity * 2 * WINDOW
                        s_prev = s_all[u * WINDOW:(u + 1) * WINDOW, c0:c0 + WINDOW] + prev_bias
                        s_cur = s_all[u * WINDOW:(u + 1) * WINDOW, c0 + WINDOW:c0 + 2 * WINDOW]
                        s = jnp.where(from_prev, s_prev, s_cur)
                        m = jnp.maximum(jnp.max(s, axis=-1, keepdims=True), sink)
                        p = jnp.exp2(s - m)
                        p_cols += [jnp.where(from_prev, p, 0.0), jnp.where(from_prev, 0.0, p)]
                        corr_t.append(jnp.exp2(sink - m))
                    p_rows.append(jnp.concatenate(p_cols, axis=1).astype(BF16))
                    corr.append(corr_t)
                pv = jnp.dot(jnp.concatenate(p_rows, axis=0), v_both[kh],
                             preferred_element_type=F32)
                for u, t in enumerate(group):
                    num = pv[u * WINDOW:(u + 1) * WINDOW, 0:LANES]
                    den = (pv[u * WINDOW:(u + 1) * WINDOW, LANES:2 * LANES]
                           + jnp.where(out_first_half, corr[u][0], corr[u][1]))
                    o_buf[pl.ds(r0, WINDOW), t * LANES:(t + 1) * LANES] = (num / den).astype(BF16)
        return carry

    lax.fori_loop(0, rows // WINDOW, window, 0, unroll=True)
    out_ref[...] = (h_ref[...] + jnp.dot(o_buf[...], wo_ref[...], preferred_element_type=F32)
                    + bo_ref[...])


def _attention(h, q, kv, sinks, w_o, b_o, batch, seq):
    n, q_dim = q.shape
    d = h.shape[1]
    assert kv.shape[1] == 2 * LANES and (q_dim // HEAD_DIM) % 4 == 0
    rows = _row_block(seq, ATTN_ROWS)
    bps = seq // rows
    wpb = rows // WINDOW
    wps = seq // WINDOW

    grid_spec = pltpu.PrefetchScalarGridSpec(
        num_scalar_prefetch=1,
        grid=(batch, bps),
        in_specs=[
            pl.BlockSpec((rows, q_dim), lambda b, j, s: (b * bps + j, 0)),
            pl.BlockSpec((rows, 2 * LANES), lambda b, j, s: (b * bps + j, 0)),
            pl.BlockSpec((WINDOW, 2 * LANES),
                         lambda b, j, s: (b * wps + jnp.maximum(j * wpb - 1, 0), 0)),
            pl.BlockSpec((rows, d), lambda b, j, s: (b * bps + j, 0)),
            pl.BlockSpec((q_dim, d), lambda b, j, s: (0, 0)),
            pl.BlockSpec((1, d), lambda b, j, s: (0, 0)),
        ],
        out_specs=pl.BlockSpec((rows, d), lambda b, j, s: (b * bps + j, 0)),
        scratch_shapes=[pltpu.VMEM((rows + WINDOW, LANES), BF16),
                        pltpu.VMEM((rows + WINDOW, LANES), BF16),
                        pltpu.VMEM((rows, q_dim), BF16)],
    )
    kern = functools.partial(_attn_kernel, wpb=wpb)
    return pl.pallas_call(
        kern,
        grid_spec=grid_spec,
        out_shape=jax.ShapeDtypeStruct((n, d), F32),
        compiler_params=_params(("arbitrary", "arbitrary")),
        name="swa_attention",
    )(sinks.astype(F32) * LOG2_E, q, kv, kv, h, w_o.astype(BF16), b_o.reshape(1, d))


R_IDX0, R_IDX1, R_GATE0, R_GATE1, R_RANK0, R_RANK1 = range(6)


def _router_kernel(h_ref, g_ref, rwt_ref, xpk_ref, route_ref, route_t_ref, cnt_ref, tri_ref,
                   carry_ref):
    rows = h_ref.shape[0]
    i = pl.program_id(0)

    @pl.when(i == 0)
    def _():
        r = lax.broadcasted_iota(jnp.int32, (rows, rows), 0)
        c = lax.broadcasted_iota(jnp.int32, (rows, rows), 1)
        tri_ref[...] = jnp.where(r < c, 1.0, 0.0).astype(BF16)
        carry_ref[...] = jnp.zeros_like(carry_ref)

    xn = _rms(h_ref[...], g_ref[...])
    xb = xn.astype(BF16)
    for p, words in enumerate(_pack_bf16_pairs(xn, xpk_ref.shape[2])):
        xpk_ref[p] = words

    nt = (((1,), (1,)), ((), ()))
    logits = lax.dot_general(rwt_ref[...], xb, nt, preferred_element_type=F32)[0:N_EXPERTS]
    ex = lax.broadcasted_iota(jnp.int32, logits.shape, 0)
    m1 = jnp.max(logits, axis=0, keepdims=True)
    i1 = jnp.min(jnp.where(logits == m1, ex, N_EXPERTS), axis=0, keepdims=True)
    lg2 = jnp.where(ex == i1, -jnp.inf, logits)
    m2 = jnp.max(lg2, axis=0, keepdims=True)
    i2 = jnp.min(jnp.where(lg2 == m2, ex, N_EXPERTS), axis=0, keepdims=True)
    e = jnp.exp(m2 - m1)
    g1 = 1.0 / (1.0 + e)
    g2 = e / (1.0 + e)

    sel_f = jnp.where((ex == i1) | (ex == i2), 1.0, 0.0)
    sel_pad = jnp.concatenate([sel_f, jnp.zeros_like(sel_f)], axis=0).astype(BF16)
    carry = carry_ref[:, 0:1]
    before = jnp.dot(sel_pad, tri_ref[...], preferred_element_type=F32)[0:N_EXPERTS] + carry
    r1 = jnp.sum(jnp.where(ex == i1, before, 0.0), axis=0, keepdims=True)
    r2 = jnp.sum(jnp.where(ex == i2, before, 0.0), axis=0, keepdims=True)
    carry = carry + jnp.sum(sel_f, axis=1, keepdims=True)
    carry_ref[...] = jnp.broadcast_to(carry, carry_ref.shape)
    cnt_ref[...] = jnp.broadcast_to(carry, cnt_ref.shape)

    fields = [None] * 8
    for k, val in ((R_IDX0, i1.astype(F32)), (R_IDX1, i2.astype(F32)), (R_GATE0, g1),
                   (R_GATE1, g2), (R_RANK0, r1), (R_RANK1, r2)):
        fields[k] = val
    route_t = jnp.concatenate([f if f is not None else jnp.zeros_like(g1) for f in fields], axis=0)
    route_t_ref[...] = route_t
    route_ref[...] = jnp.concatenate([route_t] * (LANES // 8), axis=0).T


def _router(h, g, router_w):
    n, d = h.shape
    rows = _row_block(n, ROUTER_ROWS)
    parts = d // 2 // SC_PIECE
    rwt = jnp.zeros((16, d), BF16).at[0:N_EXPERTS, :].set(router_w.T.astype(BF16))
    return pl.pallas_call(
        _router_kernel,
        grid=(n // rows,),
        in_specs=[
            pl.BlockSpec((rows, d), lambda i: (i, 0)),
            _full((1, d)),
            _full((16, d)),
        ],
        out_specs=[
            pl.BlockSpec((parts, rows, SC_PIECE), lambda i: (0, i, 0)),
            pl.BlockSpec((rows, LANES), lambda i: (i, 0)),
            pl.BlockSpec((8, rows), lambda i: (0, i)),
            _full((8, LANES)),
        ],
        out_shape=[
            jax.ShapeDtypeStruct((parts, n, SC_PIECE), jnp.uint32),
            jax.ShapeDtypeStruct((n, LANES), F32),
            jax.ShapeDtypeStruct((8, n), F32),
            jax.ShapeDtypeStruct((8, LANES), F32),
        ],
        scratch_shapes=[pltpu.VMEM((rows, rows), BF16), pltpu.VMEM((8, LANES), F32)],
        compiler_params=_params(("arbitrary",)),
        name="moe_router",
    )(h, g.reshape(1, d), rwt)


def _sc_mesh():
    return plsc.VectorSubcoreMesh(core_axis_name="core", subcore_axis_name="subcore")


def _gather_pieces(src, idx):
    m = idx.shape[0]
    width = src.shape[1]
    assert m % (SC_WINDOW * SC_WORKERS) == 0

    @functools.partial(pl.kernel, out_type=jax.ShapeDtypeStruct((m, width), src.dtype),
                       mesh=_sc_mesh(), scratch_types=[])
    def gather_kernel(src_hbm, idx_hbm, out_hbm):
        def body(idx_vmem, out_vmem):
            pltpu.sync_copy(src_hbm.at[idx_vmem.at[0]], out_vmem)

        pltpu.emit_pipeline(
            body,
            grid=(m // SC_WINDOW,),
            in_specs=[pl.BlockSpec((1, SC_WINDOW), lambda i: (0, i))],
            out_specs=[pl.BlockSpec((SC_WINDOW, width), lambda i: (i, 0))],
            core_axis_name=("core", "subcore"),
            dimension_semantics=(pltpu.PARALLEL,),
        )(idx_hbm, out_hbm)

    return gather_kernel(src, idx.reshape(1, m))


def _scatter_pieces(src, idx, out_rows):
    copies, m = idx.shape
    width = src.shape[1]
    assert m == src.shape[0] and m % (SC_WINDOW * SC_WORKERS) == 0

    @functools.partial(pl.kernel, out_type=jax.ShapeDtypeStruct((out_rows, width), src.dtype),
                       mesh=_sc_mesh(), scratch_types=[])
    def scatter_kernel(src_hbm, *refs):
        idx_hbm, out_hbm = refs[:copies], refs[copies]

        def body(src_vmem, *idx_vmem):
            for iv in idx_vmem:
                pltpu.sync_copy(src_vmem, out_hbm.at[iv.at[0]])

        pltpu.emit_pipeline(
            body,
            grid=(m // SC_WINDOW,),
            in_specs=[pl.BlockSpec((SC_WINDOW, width), lambda i: (i, 0))]
            + [pl.BlockSpec((1, SC_WINDOW), lambda i: (0, i))] * copies,
            out_specs=[],
            core_axis_name=("core", "subcore"),
            dimension_semantics=(pltpu.PARALLEL,),
        )(src_hbm, *idx_hbm)

    return scatter_kernel(src, *[idx[j].reshape(1, m) for j in range(copies)])


def _moe_combined(h_ref, y_ref, route_ref):
    parts, _, _, piece = y_ref.shape
    route = route_ref[...]
    g0 = route[:, R_GATE0:R_GATE0 + 1]
    g1 = route[:, R_GATE1:R_GATE1 + 1]
    lo, hi = [], []
    for p in range(parts):
        lo0, hi0 = _unpack_bf16_pair(y_ref[p, 0])
        lo1, hi1 = _unpack_bf16_pair(y_ref[p, 1])
        lo.append(g0 * lo0 + g1 * lo1)
        hi.append(g0 * hi0 + g1 * hi1)
    return h_ref[...] + jnp.concatenate(lo + hi, axis=1)


def _moe_specs(pending, rows, base):
    y_pairs, route = pending
    parts, _, _, piece = y_pairs.shape
    return [pl.BlockSpec((parts, 2, rows, piece), lambda i: (0, 0, i, 0)),
            pl.BlockSpec((rows, LANES), lambda i: (i + base, 0))]


def _final_kernel(h_ref, y_ref, route_ref, g_ref, o_ref):
    o_ref[...] = _rms(_moe_combined(h_ref, y_ref, route_ref), g_ref[...])


def _final(h, pending, final_g):
    n, d = h.shape
    y_chunks, route = pending
    n_chunks = len(y_chunks)
    rows = _row_block(n // n_chunks, PROJ_ROWS)
    steps = n // rows // n_chunks
    for c, y in enumerate(y_chunks):
        base = c * steps
        h = pl.pallas_call(
            _final_kernel,
            grid=(steps,),
            in_specs=[pl.BlockSpec((rows, d), lambda i, base=base: (i + base, 0))]
            + _moe_specs((y, route), rows, base) + [_full((1, d))],
            out_specs=pl.BlockSpec((rows, d), lambda i, base=base: (i + base, 0)),
            out_shape=jax.ShapeDtypeStruct((n, d), F32),
            input_output_aliases={0: 0} if n_chunks > 1 else {},
            compiler_params=_params(("arbitrary",)),
            name="moe_combine_final_norm",
        )(h, y, route, final_g.reshape(1, d))
    return h


def _moe(h, g, router_w, layer, wg, wu, wd, n_chunks):
    n, d = h.shape
    xpk, route, route_t, cnt = _router(h, g, router_w)
    parts = xpk.shape[0]

    top_idx = route_t[R_IDX0:R_IDX1 + 1].astype(jnp.int32)
    rank = route_t[R_RANK0:R_RANK1 + 1].astype(jnp.int32)
    sizes = cnt[0:N_EXPERTS, 0].astype(jnp.int32)
    padded = ((sizes + MOE_ROWS - 1) // MOE_ROWS) * MOE_ROWS
    pends = jnp.cumsum(padded)
    pstarts = pends - padded
    dest = rank
    for e in range(N_EXPERTS):
        dest = dest + jnp.where(top_idx == e, pstarts[e], 0)
    n_rows = 2 * n + N_EXPERTS * MOE_ROWS
    n_blocks = n_rows // MOE_ROWS
    block_start = jnp.arange(n_blocks, dtype=jnp.int32) * MOE_ROWS
    block_expert = jnp.minimum(
        jnp.sum((block_start[:, None] >= pends[None, :]).astype(jnp.int32), axis=1),
        N_EXPERTS - 1)
    n_used = (pends[N_EXPERTS - 1:] // MOE_ROWS).astype(jnp.int32)

    off = jnp.arange(parts, dtype=jnp.int32) * n_rows
    scatter_idx = (dest[:, None, :] + off[None, :, None]).reshape(2, parts * n)
    x_sorted = _scatter_pieces(xpk.reshape(parts * n, SC_PIECE), scatter_idx, parts * n_rows)
    y_rows = _moe_ffn(x_sorted.reshape(parts, n_rows, SC_PIECE), block_expert, n_used, layer,
                      wg, wu, wd)

    y_flat = y_rows.reshape(parts * n_rows, SC_PIECE)
    nc = n // n_chunks
    chunk_dest = dest.reshape(2, n_chunks, nc).transpose(1, 0, 2)
    gather_idx = (chunk_dest[:, None] + off[None, :, None, None]).reshape(n_chunks, -1)
    y_chunks = [_gather_pieces(y_flat, gather_idx[c]).reshape(parts, 2, nc, SC_PIECE)
                for c in range(n_chunks)]
    return y_chunks, route


def kernel(x, positions, final_norm_g, ev_norm1_g, ev_w_in, ev_conv_w, ev_ln_g, ev_ln_b, ev_spatial_w, ev_spatial_b, ev_w_out, ev_norm2_g, ev_ffn_wg, ev_ffn_wu, ev_ffn_wd, od_norm1_g, od_w_qkv, od_b_qkv, od_sinks, od_w_o, od_b_o, od_norm2_g, od_router_w, od_exp_wg, od_exp_wu, od_exp_wd):
    batch, seq, d = x.shape
    depth = ev_norm1_g.shape[0] + od_norm1_g.shape[0]
    assert depth % 2 == 0, "the final norm is fused into the last (odd) layer's MoE combine"
    n_q_heads = od_sinks.shape[1]
    h = x.reshape(batch * seq, d)
    experts = [od_exp_wg, od_exp_wu, od_exp_wd]
    ffn_steps = (batch * seq) // _row_block(batch * seq, FFN_ROWS)
    ride_along = all(_side_cast_ok(w, ffn_steps) for w in experts)
    if not ride_along:
        experts = [_to_bf16(w) for w in experts]
    pending = None
    pieces_per_token = 2 * (d // 2 // SC_PIECE)
    n_chunks = max(c for c in range(1, COMBINE_CHUNKS + 1)
                   if batch % c == 0
                   and (batch // c * seq * pieces_per_token) % (SC_WINDOW * SC_WORKERS) == 0)
    for layer in range(depth):
        i = layer // 2
        if layer % 2 == 0:
            h = _mixer(h, pending, seq, ev_norm1_g[i], ev_w_in[i], ev_conv_w[i], ev_ln_g[i],
                       ev_ln_b[i], ev_spatial_w[i], ev_spatial_b[i], ev_w_out[i])
            side = experts if (ride_along and layer == 0) else ()
            h, cast = _dense_ffn(h, ev_norm2_g[i], ev_ffn_wg[i], ev_ffn_wu[i], ev_ffn_wd[i], side)
            if side:
                experts = cast
        else:
            q, kv = _qkv(h, od_norm1_g[i], positions, od_w_qkv[i], od_b_qkv[i],
                         n_q_heads * HEAD_DIM)
            h = _attention(h, q, kv, od_sinks[i], od_w_o[i], od_b_o[i], batch, seq)
            pending = _moe(h, od_norm2_g[i], od_router_w[i], i, *experts, n_chunks)
    return _final(h, pending, final_norm_g).reshape(batch, seq, d)
```

```python
import functools

import jax
import jax.numpy as jnp
import numpy as np
from jax import lax
from jax.experimental import pallas as pl
from jax.experimental.pallas import tpu as pltpu
from jax.experimental.pallas import tpu_sc as plsc

F32 = jnp.float32
BF16 = jnp.bfloat16

EPS = 1e-5
CHUNK = 128
GMLP_HEADS = 4
CONV_WIDTH = 3
HEAD_DIM = 64
WINDOW = 128
ROPE_DIM = HEAD_DIM // 4
ROPE_THETA = 500000.0
ATTN_SCALE = HEAD_DIM ** -0.5
LOG2_E = float(np.log2(np.e))
Q_SCALE = ATTN_SCALE * LOG2_E
N_EXPERTS = 8
LANES = 128
VMEM_LIMIT = 56 * 1024 * 1024

MIXER_ROWS = 512
FFN_ROWS = 512
CAST_BLOCK_BYTES = 8 * 1024 * 1024
PROJ_ROWS = 1024
ATTN_ROWS = 1024
ATTN_STACK = 2
ROUTER_ROWS = 1024
MOE_ROWS = 1024
COMBINE_CHUNKS = 4
MOE_COL_SPLIT = 2
SC_WORKERS = 32
SC_PIECE = 256
SC_WINDOW = 128


def _row_block(n, pref):
    b = min(n, pref)
    while n % b:
        b -= LANES
    return b


def _col_block(f, pref):
    b = min(f, pref)
    b -= b % LANES
    while f % b:
        b -= LANES
    return b


def _params(sem):
    return pltpu.CompilerParams(dimension_semantics=sem, vmem_limit_bytes=VMEM_LIMIT)


def _rms(x, g):
    return x * lax.rsqrt(jnp.mean(x * x, axis=-1, keepdims=True) + EPS) * g


def _gelu(x):
    return 0.5 * x * (1.0 + lax.erf(x * np.float32(np.sqrt(0.5))))


def _full(shape):
    return pl.BlockSpec(shape, lambda *_: (0,) * len(shape))


def _mixer_kernel(*refs, blocks_per_seq, has_pending):
    if has_pending:
        h_ref, y_ref, route_ref = refs[:3]
        refs = refs[3:]
    else:
        h_ref = refs[0]
        refs = refs[1:]
    (g1_ref, win_ref, cw_ref, lng_ref, lnb_ref, ws_ref, bst_ref, wout_ref, o_ref,
     tail_ref, yb_ref) = refs
    rows = h_ref.shape[0]
    cd = cw_ref.shape[1]
    gd = lng_ref.shape[1]
    hd = gd // GMLP_HEADS
    i = pl.program_id(0)

    x = _moe_combined(h_ref, y_ref, route_ref) if has_pending else h_ref[...]
    xn = _rms(x, g1_ref[...]).astype(BF16)
    half_rows = rows // 2

    def by_halves(w):
        return jnp.concatenate([jnp.dot(xn[0:half_rows], w, preferred_element_type=F32),
                                jnp.dot(xn[half_rows:rows], w, preferred_element_type=F32)], axis=0)

    z_g = by_halves(win_ref[:, 3 * cd:3 * cd + 2 * gd])
    b_u = z_g[:, 0:gd]
    b_v = z_g[:, gd:2 * gd]
    z_c = by_halves(win_ref[:, 0:3 * cd])
    a_b = z_c[:, 0:cd]
    a_c = z_c[:, cd:2 * cd]
    a_x = z_c[:, 2 * cd:3 * cd]

    g = a_c * a_x
    tail = jnp.where(i % blocks_per_seq == 0, 0.0, tail_ref[...])
    row = lax.broadcasted_iota(jnp.int32, g.shape, 0)
    gm1 = jnp.where(row == 0, tail[7:8], pltpu.roll(g, 1, 0))
    gm2 = jnp.where(row == 0, tail[6:7], jnp.where(row == 1, tail[7:8], pltpu.roll(g, 2, 0)))
    tail_ref[...] = g[rows - 8:rows]
    cw = cw_ref[...]
    y_a = a_b * (gm2 * cw[0:1] + gm1 * cw[1:2] + g * cw[2:3])

    u = _gelu(b_u)
    v = _gelu(b_v)
    mu = jnp.mean(v, axis=-1, keepdims=True)
    vc = v - mu
    var = jnp.mean(vc * vc, axis=-1, keepdims=True)
    vn = (vc * lax.rsqrt(var + EPS) * lng_ref[...] + lnb_ref[...]).astype(BF16)
    ri = lax.broadcasted_iota(jnp.int32, (CHUNK, CHUNK), 0)
    ci = lax.broadcasted_iota(jnp.int32, (CHUNK, CHUNK), 1)
    causal = ri >= ci
    bst = bst_ref[...]
    for k in range(GMLP_HEADS):
        w_k = jnp.where(causal, ws_ref[k], 0.0).astype(BF16)
        b_k = bst[:, k:k + 1]
        for c in range(rows // CHUNK):
            rs = slice(c * CHUNK, (c + 1) * CHUNK)
            cs = slice(k * hd, (k + 1) * hd)
            mixed = jnp.dot(w_k, vn[rs, cs], preferred_element_type=F32) + b_k
            yb_ref[rs, cs] = (u[rs, cs] * mixed).astype(BF16)

    out = jnp.dot(y_a.astype(BF16), wout_ref[0:cd, :], preferred_element_type=F32)
    out = out + jnp.dot(yb_ref[...], wout_ref[cd:cd + gd, :], preferred_element_type=F32)
    o_ref[...] = x + out


def _mixer(h, pending, seq, *weights):
    if pending is None:
        return _mixer_call(h, None, 0, 1, seq, *weights)
    y_chunks, route = pending
    for c, y in enumerate(y_chunks):
        h = _mixer_call(h, (y, route), c, len(y_chunks), seq, *weights)
    return h


def _mixer_call(h, pending, chunk, n_chunks, seq, g1, w_in, conv_w, ln_g, ln_b, w_s, b_s, w_out):
    n, d = h.shape
    rows = _row_block(seq, MIXER_ROWS)
    cd = conv_w.shape[0]
    gd = ln_g.shape[0]
    has_pending = pending is not None
    steps = n // rows // n_chunks
    base = chunk * steps
    assert (steps * rows) % seq == 0
    kern = functools.partial(_mixer_kernel, blocks_per_seq=seq // rows, has_pending=has_pending)
    return pl.pallas_call(
        kern,
        grid=(steps,),
        in_specs=[pl.BlockSpec((rows, d), lambda i: (i + base, 0))]
        + (_moe_specs(pending, rows, base) if has_pending else [])
        + [
            _full((1, d)),
            _full(w_in.shape),
            _full((CONV_WIDTH, cd)),
            _full((1, gd)),
            _full((1, gd)),
            _full(w_s.shape),
            _full((CHUNK, GMLP_HEADS)),
            _full(w_out.shape),
        ],
        out_specs=pl.BlockSpec((rows, d), lambda i: (i + base, 0)),
        out_shape=jax.ShapeDtypeStruct((n, d), F32),
        scratch_shapes=[pltpu.VMEM((8, cd), F32), pltpu.VMEM((rows, gd), BF16)],
        input_output_aliases={0: 0} if n_chunks > 1 else {},
        compiler_params=_params(("arbitrary",)),
        name="mixer",
    )(h, *(pending or ()), g1.reshape(1, d), w_in.astype(BF16), conv_w.T, ln_g.reshape(1, gd),
      ln_b.reshape(1, gd), w_s, b_s.T, w_out.astype(BF16))


def _swiglu(xn, wg, wu, wd):
    h1 = jnp.dot(xn, wg, preferred_element_type=F32)
    h2 = jnp.dot(xn, wu, preferred_element_type=F32)
    a = (h1 / (1.0 + jnp.exp(-h1)) * h2).astype(BF16)
    return jnp.dot(a, wd, preferred_element_type=F32)


def _dense_ffn_kernel(x_ref, g_ref, wg_ref, wu_ref, wd_ref, *rest):
    n_side = (len(rest) - 1) // 2
    side_in, o_ref, side_out = rest[:n_side], rest[n_side], rest[n_side + 1:]
    x = x_ref[...]
    xn = _rms(x, g_ref[...]).astype(BF16)
    o_ref[...] = x + _swiglu(xn, wg_ref[...], wu_ref[...], wd_ref[...])
    for src, dst in zip(side_in, side_out):
        dst[...] = src[...].astype(dst.dtype)


def _side_cast_ok(w, steps):
    rows = int(np.prod(w.shape[:-1]))
    return rows % steps == 0 and (rows // steps) % 16 == 0 and w.shape[-1] % LANES == 0


def _dense_ffn(h, g, wg, wu, wd, side=()):
    n, d = h.shape
    rows = _row_block(n, FFN_ROWS)
    steps = n // rows
    side2d = [w.reshape(-1, w.shape[-1]) for w in side]
    side_specs = [pl.BlockSpec((w.shape[0] // steps, w.shape[1]), lambda i: (i, 0)) for w in side2d]
    outs = pl.pallas_call(
        _dense_ffn_kernel,
        grid=(steps,),
        in_specs=[
            pl.BlockSpec((rows, d), lambda i: (i, 0)),
            _full((1, d)),
            _full(wg.shape),
            _full(wu.shape),
            _full(wd.shape),
        ] + side_specs,
        out_specs=[pl.BlockSpec((rows, d), lambda i: (i, 0))] + side_specs,
        out_shape=[jax.ShapeDtypeStruct((n, d), F32)]
        + [jax.ShapeDtypeStruct(w.shape, BF16) for w in side2d],
        compiler_params=_params(("arbitrary",)),
        name="dense_ffn",
    )(h, g.reshape(1, d), wg.astype(BF16), wu.astype(BF16), wd.astype(BF16), *side2d)
    return outs[0], [o.reshape(w.shape) for o, w in zip(outs[1:], side)]


def _pack_bf16_pairs(x, piece):
    half = x.shape[1] // 2
    bits = lax.bitcast_convert_type(x.astype(BF16).astype(F32), jnp.uint32)
    return [(bits[:, half + p * piece:half + (p + 1) * piece] & jnp.uint32(0xFFFF0000))
            | (bits[:, p * piece:(p + 1) * piece] >> 16) for p in range(half // piece)]


def _unpack_bf16_pair(packed):
    lo = lax.bitcast_convert_type(packed << 16, F32)
    hi = lax.bitcast_convert_type(packed & jnp.uint32(0xFFFF0000), F32)
    return lo, hi


def _moe_ffn_kernel(be_ref, used_ref, x_ref, wg_ref, wu_ref, wd_ref, o_ref, xn_ref, acc_ref, *,
                    n_steps):
    del be_ref
    i = pl.program_id(0)
    f = pl.program_id(1)
    parts, _, piece = x_ref.shape
    half = parts * piece
    active = i < used_ref[0]

    def write_out(val):
        for p, words in enumerate(_pack_bf16_pairs(val, piece)):
            o_ref[p] = words

    def step(first, last):
        if first:
            for p in range(parts):
                lo, hi = _unpack_bf16_pair(x_ref[p])
                xn_ref[:, p * piece:(p + 1) * piece] = lo.astype(BF16)
                xn_ref[:, half + p * piece:half + (p + 1) * piece] = hi.astype(BF16)
        part = _swiglu(xn_ref[...], wg_ref[0, 0], wu_ref[0, 0], wd_ref[0, 0])
        if last:
            write_out(part if first else acc_ref[...] + part)
        elif first:
            acc_ref[...] = part
        else:
            acc_ref[...] += part

    if n_steps == 1:
        pl.when(active)(functools.partial(step, True, True))
    else:
        pl.when(active & (f == 0))(functools.partial(step, True, False))
        if n_steps > 2:
            pl.when(active & (f > 0) & (f < n_steps - 1))(functools.partial(step, False, False))
        pl.when(active & (f == n_steps - 1))(functools.partial(step, False, True))

    @pl.when(jnp.logical_not(active) & (f == n_steps - 1))
    def _():
        o_ref[...] = jnp.zeros_like(o_ref)


def _moe_ffn(x_sorted, block_expert, n_used, layer, wg, wu, wd):
    parts, n_rows, piece = x_sorted.shape
    d = 2 * parts * piece
    fdim = wg.shape[3]
    rows = MOE_ROWS
    cols = _col_block(fdim, fdim // MOE_COL_SPLIT)
    n_steps = fdim // cols

    def col(i, f, used):
        return jnp.where(i < used[0], f, n_steps - 1)

    grid_spec = pltpu.PrefetchScalarGridSpec(
        num_scalar_prefetch=2,
        grid=(n_rows // rows, n_steps),
        in_specs=[
            pl.BlockSpec((parts, rows, piece),
                         lambda i, f, be, used: (0, jnp.minimum(i, used[0] - 1), 0)),
            pl.BlockSpec((1, 1, d, cols), lambda i, f, be, used: (layer, be[i], 0, col(i, f, used))),
            pl.BlockSpec((1, 1, d, cols), lambda i, f, be, used: (layer, be[i], 0, col(i, f, used))),
            pl.BlockSpec((1, 1, cols, d), lambda i, f, be, used: (layer, be[i], col(i, f, used), 0)),
        ],
        out_specs=pl.BlockSpec((parts, rows, piece), lambda i, f, be, used: (0, i, 0)),
        scratch_shapes=[pltpu.VMEM((rows, d), BF16), pltpu.VMEM((rows, d), F32)],
    )
    kern = functools.partial(_moe_ffn_kernel, n_steps=n_steps)
    return pl.pallas_call(
        kern,
        grid_spec=grid_spec,
        out_shape=jax.ShapeDtypeStruct((parts, n_rows, piece), jnp.uint32),
        compiler_params=_params(("arbitrary", "arbitrary")),
        name="moe_ffn",
    )(block_expert, n_used, x_sorted, wg, wu, wd)


def _cast_kernel(x_ref, o_ref):
    o_ref[...] = x_ref[...].astype(o_ref.dtype)


def _to_bf16(w):
    shape = w.shape
    w2 = w.reshape(-1, shape[-1])
    pref = CAST_BLOCK_BYTES // (4 * shape[-1]) // LANES * LANES
    rows = _row_block(w2.shape[0], pref)
    out = pl.pallas_call(
        _cast_kernel,
        grid=(w2.shape[0] // rows,),
        in_specs=[pl.BlockSpec((rows, shape[-1]), lambda i: (i, 0))],
        out_specs=pl.BlockSpec((rows, shape[-1]), lambda i: (i, 0)),
        out_shape=jax.ShapeDtypeStruct(w2.shape, BF16),
        compiler_params=_params(("arbitrary",)),
        name="cast_bf16",
    )(w2)
    return out.reshape(shape)


def _qkv_kernel(h_ref, g_ref, pos_ref, w_ref, b_ref, invf_ref, mc_ref, q_ref, kv_ref, *, q_dim):
    x = h_ref[...]
    xn = _rms(x, g_ref[...]).astype(BF16)
    half_rows = x.shape[0] // 2
    z = jnp.concatenate(
        [jnp.dot(xn[0:half_rows], w_ref[...], preferred_element_type=F32),
         jnp.dot(xn[half_rows:2 * half_rows], w_ref[...], preferred_element_type=F32)],
        axis=0) + b_ref[...]
    ang = pos_ref[...].astype(F32) * invf_ref[...]
    reps = LANES // ang.shape[0]
    cos = jnp.concatenate([jnp.cos(ang)] * reps, axis=0).T
    sin = jnp.concatenate([jnp.sin(ang)] * reps, axis=0).T
    cos = jnp.where(mc_ref[...] != 0.0, cos, 1.0)
    sin = sin * mc_ref[...]

    def rope(t):
        return t * cos + pltpu.roll(t, LANES // 2, 1) * sin

    for j in range(q_dim // LANES):
        cs = slice(j * LANES, (j + 1) * LANES)
        q_ref[:, cs] = (rope(z[:, cs]) * Q_SCALE).astype(BF16)
    kv_ref[:, 0:LANES] = rope(z[:, q_dim:q_dim + LANES]).astype(BF16)
    kv_ref[:, LANES:2 * LANES] = z[:, q_dim + LANES:q_dim + 2 * LANES].astype(BF16)


QK_GROUP = LANES // 4


def _qk_tile_layout():
    lane = np.arange(LANES)
    group, off = lane // QK_GROUP, lane % QK_GROUP
    half = ROPE_DIM // 2
    head = group % 2
    second = group // 2
    rest = QK_GROUP - half
    dim = np.where(off < half, second * half + off, ROPE_DIM + second * rest + (off - half))
    return head, dim


def _qk_column_order(n_cols):
    head, dim = _qk_tile_layout()
    tile = np.arange(n_cols) // LANES
    return tile * LANES + np.tile(head * HEAD_DIM + dim, n_cols // LANES)


def _rope_sign_lanes():
    _, dim = _qk_tile_layout()
    half = ROPE_DIM // 2
    sign = np.where(dim < half, -1.0, np.where(dim < ROPE_DIM, 1.0, 0.0)).astype(np.float32)
    return jnp.asarray(sign[None, :])


def _qkv(h, g, positions, w_qkv, b_qkv, q_dim):
    n, d = h.shape
    qkv_dim = w_qkv.shape[1]
    assert qkv_dim == q_dim + 2 * LANES and QK_GROUP % (ROPE_DIM // 2) == 0
    rows = _row_block(n, PROJ_ROWS)
    inv_freq = ROPE_THETA ** (-jnp.arange(0, ROPE_DIM, 2, dtype=F32) / ROPE_DIM)
    order = np.concatenate([_qk_column_order(q_dim + LANES), np.arange(q_dim + LANES, qkv_dim)])
    w_qkv = w_qkv[:, order]
    b_qkv = b_qkv[order]
    kern = functools.partial(_qkv_kernel, q_dim=q_dim)
    return pl.pallas_call(
        kern,
        grid=(n // rows,),
        in_specs=[
            pl.BlockSpec((rows, d), lambda i: (i, 0)),
            _full((1, d)),
            pl.BlockSpec((1, rows), lambda i: (0, i)),
            _full(w_qkv.shape),
            _full((1, qkv_dim)),
            _full((ROPE_DIM // 2, 1)),
            _full((1, LANES)),
        ],
        out_specs=[
            pl.BlockSpec((rows, q_dim), lambda i: (i, 0)),
            pl.BlockSpec((rows, 2 * LANES), lambda i: (i, 0)),
        ],
        out_shape=[
            jax.ShapeDtypeStruct((n, q_dim), BF16),
            jax.ShapeDtypeStruct((n, 2 * LANES), BF16),
        ],
        compiler_params=_params(("arbitrary",)),
        name="qkv_rope",
    )(h, g.reshape(1, d), positions.reshape(1, n), w_qkv.astype(BF16),
      b_qkv.reshape(1, qkv_dim), inv_freq.reshape(-1, 1), _rope_sign_lanes())


def _attn_kernel(sink_ref, q_ref, kvc_ref, kvp_ref, h_ref, wo_ref, bo_ref, out_ref,
                 kbuf, vbuf, o_buf, *, wpb):
    rows = q_ref.shape[0]
    tiles = q_ref.shape[1] // LANES // 2
    j = pl.program_id(1)
    kbuf[0:WINDOW, :] = kvp_ref[:, 0:LANES]
    kbuf[WINDOW:WINDOW + rows, :] = kvc_ref[:, 0:LANES]
    vbuf[0:WINDOW, :] = kvp_ref[:, LANES:2 * LANES]
    vbuf[WINDOW:WINDOW + rows, :] = kvc_ref[:, LANES:2 * LANES]

    from_prev = (lax.broadcasted_iota(jnp.int32, (WINDOW, WINDOW), 1)
                 > lax.broadcasted_iota(jnp.int32, (WINDOW, WINDOW), 0))
    kv_lane = lax.broadcasted_iota(jnp.int32, (2 * WINDOW, LANES), 1)
    first_half = kv_lane < HEAD_DIM
    head0_lanes = (kv_lane // QK_GROUP) % 2 == 0
    out_first_half = lax.broadcasted_iota(jnp.int32, (WINDOW, LANES), 1) < HEAD_DIM
    ones_lo = jnp.where(first_half, 1.0, 0.0)
    ones_hi = jnp.where(first_half, 0.0, 1.0)
    nt = (((1,), (1,)), ((), ()))

    def window(n, carry):
        r0 = pl.multiple_of(n * WINDOW, WINDOW)
        kt = kbuf[pl.ds(r0, 2 * WINDOW), :].astype(F32)
        vt = vbuf[pl.ds(r0, 2 * WINDOW), :].astype(F32)
        prev_bias = jnp.where(j * wpb + n > 0, 0.0, -jnp.inf)
        k0_lo = jnp.where(head0_lanes, kt, 0.0)
        k1_hi = jnp.where(head0_lanes, 0.0, kt)
        v0_lo = jnp.where(first_half, vt, 0.0)
        v1_hi = jnp.where(first_half, 0.0, vt)
        k_both = (
            jnp.concatenate([k0_lo, pltpu.roll(k0_lo, QK_GROUP, 1)], axis=0).astype(BF16),
            jnp.concatenate([pltpu.roll(k1_hi, LANES - QK_GROUP, 1), k1_hi], axis=0).astype(BF16))
        v_both = (
            jnp.concatenate([jnp.concatenate([v0_lo, ones_lo], axis=1),
                             jnp.concatenate([pltpu.roll(v0_lo, HEAD_DIM, 1), ones_hi], axis=1)],
                            axis=0).astype(BF16),
            jnp.concatenate([jnp.concatenate([pltpu.roll(v1_hi, HEAD_DIM, 1), ones_lo], axis=1),
                             jnp.concatenate([v1_hi, ones_hi], axis=1)], axis=0).astype(BF16))
        for kh in range(2):
            for t0 in range(0, tiles, ATTN_STACK):
                group = [kh * tiles + t0 + u for u in range(ATTN_STACK)]
                q_stack = jnp.concatenate(
                    [q_ref[pl.ds(r0, WINDOW), t * LANES:(t + 1) * LANES] for t in group], axis=0)
                s_all = lax.dot_general(q_stack, k_both[kh], nt, preferred_element_type=F32)
                p_rows, corr = [], []
                for u, t in enumerate(group):
                    p_cols, corr_t = [], []
                    for parity in range(2):
                        sink = sink_ref[t * 2 + parity]
                        c0 = parity * 2 * WINDOW
                        s_prev = s_all[u * WINDOW:(u + 1) * WINDOW, c0:c0 + WINDOW] + prev_bias
                        s_cur = s_all[u * WINDOW:(u + 1) * WINDOW, c0 + WINDOW:c0 + 2 * WINDOW]
                        s = jnp.where(from_prev, s_prev, s_cur)
                        m = jnp.maximum(jnp.max(s, axis=-1, keepdims=True), sink)
                        p = jnp.exp2(s - m)
                        p_cols += [jnp.where(from_prev, p, 0.0), jnp.where(from_prev, 0.0, p)]
                        corr_t.append(jnp.exp2(sink - m))
                    p_rows.append(jnp.concatenate(p_cols, axis=1).astype(BF16))
                    corr.append(corr_t)
                pv = jnp.dot(jnp.concatenate(p_rows, axis=0), v_both[kh],
                             preferred_element_type=F32)
                for u, t in enumerate(group):
                    num = pv[u * WINDOW:(u + 1) * WINDOW, 0:LANES]
                    den = (pv[u * WINDOW:(u + 1) * WINDOW, LANES:2 * LANES]
                           + jnp.where(out_first_half, corr[u][0], corr[u][1]))
                    o_buf[pl.ds(r0, WINDOW), t * LANES:(t + 1) * LANES] = (num / den).astype(BF16)
        return carry

    lax.fori_loop(0, rows // WINDOW, window, 0, unroll=True)
    out_ref[...] = (h_ref[...] + jnp.dot(o_buf[...], wo_ref[...], preferred_element_type=F32)
                    + bo_ref[...])


def _attention(h, q, kv, sinks, w_o, b_o, batch, seq):
    n, q_dim = q.shape
    d = h.shape[1]
    assert kv.shape[1] == 2 * LANES and (q_dim // HEAD_DIM) % 4 == 0
    rows = _row_block(seq, ATTN_ROWS)
    bps = seq // rows
    wpb = rows // WINDOW
    wps = seq // WINDOW

    grid_spec = pltpu.PrefetchScalarGridSpec(
        num_scalar_prefetch=1,
        grid=(batch, bps),
        in_specs=[
            pl.BlockSpec((rows, q_dim), lambda b, j, s: (b * bps + j, 0)),
            pl.BlockSpec((rows, 2 * LANES), lambda b, j, s: (b * bps + j, 0)),
            pl.BlockSpec((WINDOW, 2 * LANES),
                         lambda b, j, s: (b * wps + jnp.maximum(j * wpb - 1, 0), 0)),
            pl.BlockSpec((rows, d), lambda b, j, s: (b * bps + j, 0)),
            pl.BlockSpec((q_dim, d), lambda b, j, s: (0, 0)),
            pl.BlockSpec((1, d), lambda b, j, s: (0, 0)),
        ],
        out_specs=pl.BlockSpec((rows, d), lambda b, j, s: (b * bps + j, 0)),
        scratch_shapes=[pltpu.VMEM((rows + WINDOW, LANES), BF16),
                        pltpu.VMEM((rows + WINDOW, LANES), BF16),
                        pltpu.VMEM((rows, q_dim), BF16)],
    )
    kern = functools.partial(_attn_kernel, wpb=wpb)
    return pl.pallas_call(
        kern,
        grid_spec=grid_spec,
        out_shape=jax.ShapeDtypeStruct((n, d), F32),
        compiler_params=_params(("arbitrary", "arbitrary")),
        name="swa_attention",
    )(sinks.astype(F32) * LOG2_E, q, kv, kv, h, w_o.astype(BF16), b_o.reshape(1, d))


R_IDX0, R_IDX1, R_GATE0, R_GATE1, R_RANK0, R_RANK1 = range(6)


def _router_kernel(h_ref, g_ref, rwt_ref, xpk_ref, route_ref, route_t_ref, cnt_ref, tri_ref,
                   carry_ref):
    rows = h_ref.shape[0]
    i = pl.program_id(0)

    @pl.when(i == 0)
    def _():
        r = lax.broadcasted_iota(jnp.int32, (rows, rows), 0)
        c = lax.broadcasted_iota(jnp.int32, (rows, rows), 1)
        tri_ref[...] = jnp.where(r < c, 1.0, 0.0).astype(BF16)
        carry_ref[...] = jnp.zeros_like(carry_ref)

    xn = _rms(h_ref[...], g_ref[...])
    xb = xn.astype(BF16)
    for p, words in enumerate(_pack_bf16_pairs(xn, xpk_ref.shape[2])):
        xpk_ref[p] = words

    nt = (((1,), (1,)), ((), ()))
    logits = lax.dot_general(rwt_ref[...], xb, nt, preferred_element_type=F32)[0:N_EXPERTS]
    ex = lax.broadcasted_iota(jnp.int32, logits.shape, 0)
    m1 = jnp.max(logits, axis=0, keepdims=True)
    i1 = jnp.min(jnp.where(logits == m1, ex, N_EXPERTS), axis=0, keepdims=True)
    lg2 = jnp.where(ex == i1, -jnp.inf, logits)
    m2 = jnp.max(lg2, axis=0, keepdims=True)
    i2 = jnp.min(jnp.where(lg2 == m2, ex, N_EXPERTS), axis=0, keepdims=True)
    e = jnp.exp(m2 - m1)
    g1 = 1.0 / (1.0 + e)
    g2 = e / (1.0 + e)

    sel_f = jnp.where((ex == i1) | (ex == i2), 1.0, 0.0)
    sel_pad = jnp.concatenate([sel_f, jnp.zeros_like(sel_f)], axis=0).astype(BF16)
    carry = carry_ref[:, 0:1]
    before = jnp.dot(sel_pad, tri_ref[...], preferred_element_type=F32)[0:N_EXPERTS] + carry
    r1 = jnp.sum(jnp.where(ex == i1, before, 0.0), axis=0, keepdims=True)
    r2 = jnp.sum(jnp.where(ex == i2, before, 0.0), axis=0, keepdims=True)
    carry = carry + jnp.sum(sel_f, axis=1, keepdims=True)
    carry_ref[...] = jnp.broadcast_to(carry, carry_ref.shape)
    cnt_ref[...] = jnp.broadcast_to(carry, cnt_ref.shape)

    fields = [None] * 8
    for k, val in ((R_IDX0, i1.astype(F32)), (R_IDX1, i2.astype(F32)), (R_GATE0, g1),
                   (R_GATE1, g2), (R_RANK0, r1), (R_RANK1, r2)):
        fields[k] = val
    route_t = jnp.concatenate([f if f is not None else jnp.zeros_like(g1) for f in fields], axis=0)
    route_t_ref[...] = route_t
    route_ref[...] = jnp.concatenate([route_t] * (LANES // 8), axis=0).T


def _router(h, g, router_w):
    n, d = h.shape
    rows = _row_block(n, ROUTER_ROWS)
    parts = d // 2 // SC_PIECE
    rwt = jnp.zeros((16, d), BF16).at[0:N_EXPERTS, :].set(router_w.T.astype(BF16))
    return pl.pallas_call(
        _router_kernel,
        grid=(n // rows,),
        in_specs=[
            pl.BlockSpec((rows, d), lambda i: (i, 0)),
            _full((1, d)),
            _full((16, d)),
        ],
        out_specs=[
            pl.BlockSpec((parts, rows, SC_PIECE), lambda i: (0, i, 0)),
            pl.BlockSpec((rows, LANES), lambda i: (i, 0)),
            pl.BlockSpec((8, rows), lambda i: (0, i)),
            _full((8, LANES)),
        ],
        out_shape=[
            jax.ShapeDtypeStruct((parts, n, SC_PIECE), jnp.uint32),
            jax.ShapeDtypeStruct((n, LANES), F32),
            jax.ShapeDtypeStruct((8, n), F32),
            jax.ShapeDtypeStruct((8, LANES), F32),
        ],
        scratch_shapes=[pltpu.VMEM((rows, rows), BF16), pltpu.VMEM((8, LANES), F32)],
        compiler_params=_params(("arbitrary",)),
        name="moe_router",
    )(h, g.reshape(1, d), rwt)


def _sc_mesh():
    return plsc.VectorSubcoreMesh(core_axis_name="core", subcore_axis_name="subcore")


def _gather_pieces(src, idx):
    m = idx.shape[0]
    width = src.shape[1]
    assert m % (SC_WINDOW * SC_WORKERS) == 0

    @functools.partial(pl.kernel, out_type=jax.ShapeDtypeStruct((m, width), src.dtype),
                       mesh=_sc_mesh(), scratch_types=[])
    def gather_kernel(src_hbm, idx_hbm, out_hbm):
        def body(idx_vmem, out_vmem):
            pltpu.sync_copy(src_hbm.at[idx_vmem.at[0]], out_vmem)

        pltpu.emit_pipeline(
            body,
            grid=(m // SC_WINDOW,),
            in_specs=[pl.BlockSpec((1, SC_WINDOW), lambda i: (0, i))],
            out_specs=[pl.BlockSpec((SC_WINDOW, width), lambda i: (i, 0))],
            core_axis_name=("core", "subcore"),
            dimension_semantics=(pltpu.PARALLEL,),
        )(idx_hbm, out_hbm)

    return gather_kernel(src, idx.reshape(1, m))


def _scatter_pieces(src, idx, out_rows):
    copies, m = idx.shape
    width = src.shape[1]
    assert m == src.shape[0] and m % (SC_WINDOW * SC_WORKERS) == 0

    @functools.partial(pl.kernel, out_type=jax.ShapeDtypeStruct((out_rows, width), src.dtype),
                       mesh=_sc_mesh(), scratch_types=[])
    def scatter_kernel(src_hbm, *refs):
        idx_hbm, out_hbm = refs[:copies], refs[copies]

        def body(src_vmem, *idx_vmem):
            for iv in idx_vmem:
                pltpu.sync_copy(src_vmem, out_hbm.at[iv.at[0]])

        pltpu.emit_pipeline(
            body,
            grid=(m // SC_WINDOW,),
            in_specs=[pl.BlockSpec((SC_WINDOW, width), lambda i: (i, 0))]
            + [pl.BlockSpec((1, SC_WINDOW), lambda i: (0, i))] * copies,
            out_specs=[],
            core_axis_name=("core", "subcore"),
            dimension_semantics=(pltpu.PARALLEL,),
        )(src_hbm, *idx_hbm)

    return scatter_kernel(src, *[idx[j].reshape(1, m) for j in range(copies)])


def _moe_combined(h_ref, y_ref, route_ref):
    parts, _, _, piece = y_ref.shape
    route = route_ref[...]
    g0 = route[:, R_GATE0:R_GATE0 + 1]
    g1 = route[:, R_GATE1:R_GATE1 + 1]
    lo, hi = [], []
    for p in range(parts):
        lo0, hi0 = _unpack_bf16_pair(y_ref[p, 0])
        lo1, hi1 = _unpack_bf16_pair(y_ref[p, 1])
        lo.append(g0 * lo0 + g1 * lo1)
        hi.append(g0 * hi0 + g1 * hi1)
    return h_ref[...] + jnp.concatenate(lo + hi, axis=1)


def _moe_specs(pending, rows, base):
    y_pairs, route = pending
    parts, _, _, piece = y_pairs.shape
    return [pl.BlockSpec((parts, 2, rows, piece), lambda i: (0, 0, i, 0)),
            pl.BlockSpec((rows, LANES), lambda i: (i + base, 0))]


def _final_kernel(h_ref, y_ref, route_ref, g_ref, o_ref):
    o_ref[...] = _rms(_moe_combined(h_ref, y_ref, route_ref), g_ref[...])


def _final(h, pending, final_g):
    n, d = h.shape
    y_chunks, route = pending
    n_chunks = len(y_chunks)
    rows = _row_block(n // n_chunks, PROJ_ROWS)
    steps = n // rows // n_chunks
    for c, y in enumerate(y_chunks):
        base = c * steps
        h = pl.pallas_call(
            _final_kernel,
            grid=(steps,),
            in_specs=[pl.BlockSpec((rows, d), lambda i, base=base: (i + base, 0))]
            + _moe_specs((y, route), rows, base) + [_full((1, d))],
            out_specs=pl.BlockSpec((rows, d), lambda i, base=base: (i + base, 0)),
            out_shape=jax.ShapeDtypeStruct((n, d), F32),
            input_output_aliases={0: 0} if n_chunks > 1 else {},
            compiler_params=_params(("arbitrary",)),
            name="moe_combine_final_norm",
        )(h, y, route, final_g.reshape(1, d))
    return h


def _moe(h, g, router_w, layer, wg, wu, wd, n_chunks):
    n, d = h.shape
    xpk, route, route_t, cnt = _router(h, g, router_w)
    parts = xpk.shape[0]

    top_idx = route_t[R_IDX0:R_IDX1 + 1].astype(jnp.int32)
    rank = route_t[R_RANK0:R_RANK1 + 1].astype(jnp.int32)
    sizes = cnt[0:N_EXPERTS, 0].astype(jnp.int32)
    padded = ((sizes + MOE_ROWS - 1) // MOE_ROWS) * MOE_ROWS
    pends = jnp.cumsum(padded)
    pstarts = pends - padded
    dest = rank
    for e in range(N_EXPERTS):
        dest = dest + jnp.where(top_idx == e, pstarts[e], 0)
    n_rows = 2 * n + N_EXPERTS * MOE_ROWS
    n_blocks = n_rows // MOE_ROWS
    block_start = jnp.arange(n_blocks, dtype=jnp.int32) * MOE_ROWS
    block_expert = jnp.minimum(
        jnp.sum((block_start[:, None] >= pends[None, :]).astype(jnp.int32), axis=1),
        N_EXPERTS - 1)
    n_used = (pends[N_EXPERTS - 1:] // MOE_ROWS).astype(jnp.int32)

    off = jnp.arange(parts, dtype=jnp.int32) * n_rows
    scatter_idx = (dest[:, None, :] + off[None, :, None]).reshape(2, parts * n)
    x_sorted = _scatter_pieces(xpk.reshape(parts * n, SC_PIECE), scatter_idx, parts * n_rows)
    y_rows = _moe_ffn(x_sorted.reshape(parts, n_rows, SC_PIECE), block_expert, n_used, layer,
                      wg, wu, wd)

    y_flat = y_rows.reshape(parts * n_rows, SC_PIECE)
    nc = n // n_chunks
    chunk_dest = dest.reshape(2, n_chunks, nc).transpose(1, 0, 2)
    gather_idx = (chunk_dest[:, None] + off[None, :, None, None]).reshape(n_chunks, -1)
    y_chunks = [_gather_pieces(y_flat, gather_idx[c]).reshape(parts, 2, nc, SC_PIECE)
                for c in range(n_chunks)]
    return y_chunks, route


def kernel(x, positions, final_norm_g, ev_norm1_g, ev_w_in, ev_conv_w, ev_ln_g, ev_ln_b, ev_spatial_w, ev_spatial_b, ev_w_out, ev_norm2_g, ev_ffn_wg, ev_ffn_wu, ev_ffn_wd, od_norm1_g, od_w_qkv, od_b_qkv, od_sinks, od_w_o, od_b_o, od_norm2_g, od_router_w, od_exp_wg, od_exp_wu, od_exp_wd):
    batch, seq, d = x.shape
    depth = ev_norm1_g.shape[0] + od_norm1_g.shape[0]
    assert depth % 2 == 0, "the final norm is fused into the last (odd) layer's MoE combine"
    n_q_heads = od_sinks.shape[1]
    h = x.reshape(batch * seq, d)
    experts = [od_exp_wg, od_exp_wu, od_exp_wd]
    ffn_steps = (batch * seq) // _row_block(batch * seq, FFN_ROWS)
    ride_along = all(_side_cast_ok(w, ffn_steps) for w in experts)
    if not ride_along:
        experts = [_to_bf16(w) for w in experts]
    pending = None
    pieces_per_token = 2 * (d // 2 // SC_PIECE)
    n_chunks = max(c for c in range(1, COMBINE_CHUNKS + 1)
                   if batch % c == 0
                   and (batch // c * seq * pieces_per_token) % (SC_WINDOW * SC_WORKERS) == 0)
    for layer in range(depth):
        i = layer // 2
        if layer % 2 == 0:
            h = _mixer(h, pending, seq, ev_norm1_g[i], ev_w_in[i], ev_conv_w[i], ev_ln_g[i],
                       ev_ln_b[i], ev_spatial_w[i], ev_spatial_b[i], ev_w_out[i])
            side = experts if (ride_along and layer == 0) else ()
            h, cast = _dense_ffn(h, ev_norm2_g[i], ev_ffn_wg[i], ev_ffn_wu[i], ev_ffn_wd[i], side)
            if side:
                experts = cast
        else:
            q, kv = _qkv(h, od_norm1_g[i], positions, od_w_qkv[i], od_b_qkv[i],
                         n_q_heads * HEAD_DIM)
            h = _attention(h, q, kv, od_sinks[i], od_w_o[i], od_b_o[i], batch, seq)
            pending = _moe(h, od_norm2_g[i], od_router_w[i], i, *experts, n_chunks)
    return _final(h, pending, final_norm_g).reshape(batch, seq, d)
```

```python
import functools

import jax
import jax.numpy as jnp
import numpy as np
from jax import lax
from jax.experimental import pallas as pl
from jax.experimental.pallas import tpu as pltpu
from jax.experimental.pallas import tpu_sc as plsc

F32 = jnp.float32
BF16 = jnp.bfloat16

EPS = 1e-5
CHUNK = 128
GMLP_HEADS = 4
CONV_WIDTH = 3
HEAD_DIM = 64
WINDOW = 128
ROPE_DIM = HEAD_DIM // 4
ROPE_THETA = 500000.0
ATTN_SCALE = HEAD_DIM ** -0.5
LOG2_E = float(np.log2(np.e))
Q_SCALE = ATTN_SCALE * LOG2_E
N_EXPERTS = 8
LANES = 128
VMEM_LIMIT = 56 * 1024 * 1024

MIXER_ROWS = 512
FFN_ROWS = 512
CAST_BLOCK_BYTES = 8 * 1024 * 1024
PROJ_ROWS = 1024
ATTN_ROWS = 1024
ATTN_STACK = 2
ROUTER_ROWS = 1024
MOE_ROWS = 1024
COMBINE_CHUNKS = 4
MOE_COL_SPLIT = 2
SC_WORKERS = 32
SC_PIECE = 256
SC_WINDOW = 128


def _row_block(n, pref):
    b = min(n, pref)
    while n % b:
        b -= LANES
    return b


def _col_block(f, pref):
    b = min(f, pref)
    b -= b % LANES
    while f % b:
        b -= LANES
    return b


def _params(sem):
    return pltpu.CompilerParams(dimension_semantics=sem, vmem_limit_bytes=VMEM_LIMIT)


def _rms(x, g):
    return x * lax.rsqrt(jnp.mean(x * x, axis=-1, keepdims=True) + EPS) * g


def _gelu(x):
    return 0.5 * x * (1.0 + lax.erf(x * np.float32(np.sqrt(0.5))))


def _full(shape):
    return pl.BlockSpec(shape, lambda *_: (0,) * len(shape))


def _mixer_kernel(*refs, blocks_per_seq, has_pending):
    if has_pending:
        h_ref, y_ref, route_ref = refs[:3]
        refs = refs[3:]
    else:
        h_ref = refs[0]
        refs = refs[1:]
    (g1_ref, win_ref, cw_ref, lng_ref, lnb_ref, ws_ref, bst_ref, wout_ref, o_ref,
     tail_ref, yb_ref) = refs
    rows = h_ref.shape[0]
    cd = cw_ref.shape[1]
    gd = lng_ref.shape[1]
    hd = gd // GMLP_HEADS
    i = pl.program_id(0)

    x = _moe_combined(h_ref, y_ref, route_ref) if has_pending else h_ref[...]
    xn = _rms(x, g1_ref[...]).astype(BF16)
    half_rows = rows // 2

    def by_halves(w):
        return jnp.concatenate([jnp.dot(xn[0:half_rows], w, preferred_element_type=F32),
                                jnp.dot(xn[half_rows:rows], w, preferred_element_type=F32)], axis=0)

    z_g = by_halves(win_ref[:, 3 * cd:3 * cd + 2 * gd])
    b_u = z_g[:, 0:gd]
    b_v = z_g[:, gd:2 * gd]
    z_c = by_halves(win_ref[:, 0:3 * cd])
    a_b = z_c[:, 0:cd]
    a_c = z_c[:, cd:2 * cd]
    a_x = z_c[:, 2 * cd:3 * cd]

    g = a_c * a_x
    tail = jnp.where(i % blocks_per_seq == 0, 0.0, tail_ref[...])
    row = lax.broadcasted_iota(jnp.int32, g.shape, 0)
    gm1 = jnp.where(row == 0, tail[7:8], pltpu.roll(g, 1, 0))
    gm2 = jnp.where(row == 0, tail[6:7], jnp.where(row == 1, tail[7:8], pltpu.roll(g, 2, 0)))
    tail_ref[...] = g[rows - 8:rows]
    cw = cw_ref[...]
    y_a = a_b * (gm2 * cw[0:1] + gm1 * cw[1:2] + g * cw[2:3])

    u = _gelu(b_u)
    v = _gelu(b_v)
    mu = jnp.mean(v, axis=-1, keepdims=True)
    vc = v - mu
    var = jnp.mean(vc * vc, axis=-1, keepdims=True)
    vn = (vc * lax.rsqrt(var + EPS) * lng_ref[...] + lnb_ref[...]).astype(BF16)
    ri = lax.broadcasted_iota(jnp.int32, (CHUNK, CHUNK), 0)
    ci = lax.broadcasted_iota(jnp.int32, (CHUNK, CHUNK), 1)
    causal = ri >= ci
    bst = bst_ref[...]
    for k in range(GMLP_HEADS):
        w_k = jnp.where(causal, ws_ref[k], 0.0).astype(BF16)
        b_k = bst[:, k:k + 1]
        for c in range(rows // CHUNK):
            rs = slice(c * CHUNK, (c + 1) * CHUNK)
            cs = slice(k * hd, (k + 1) * hd)
            mixed = jnp.dot(w_k, vn[rs, cs], preferred_element_type=F32) + b_k
            yb_ref[rs, cs] = (u[rs, cs] * mixed).astype(BF16)

    out = jnp.dot(y_a.astype(BF16), wout_ref[0:cd, :], preferred_element_type=F32)
    out = out + jnp.dot(yb_ref[...], wout_ref[cd:cd + gd, :], preferred_element_type=F32)
    o_ref[...] = x + out


def _mixer(h, pending, seq, *weights):
    if pending is None:
        return _mixer_call(h, None, 0, 1, seq, *weights)
    y_chunks, route = pending
    for c, y in enumerate(y_chunks):
        h = _mixer_call(h, (y, route), c, len(y_chunks), seq, *weights)
    return h


def _mixer_call(h, pending, chunk, n_chunks, seq, g1, w_in, conv_w, ln_g, ln_b, w_s, b_s, w_out):
    n, d = h.shape
    rows = _row_block(seq, MIXER_ROWS)
    cd = conv_w.shape[0]
    gd = ln_g.shape[0]
    has_pending = pending is not None
    steps = n // rows // n_chunks
    base = chunk * steps
    assert (steps * rows) % seq == 0
    kern = functools.partial(_mixer_kernel, blocks_per_seq=seq // rows, has_pending=has_pending)
    return pl.pallas_call(
        kern,
        grid=(steps,),
        in_specs=[pl.BlockSpec((rows, d), lambda i: (i + base, 0))]
        + (_moe_specs(pending, rows, base) if has_pending else [])
        + [
            _full((1, d)),
            _full(w_in.shape),
            _full((CONV_WIDTH, cd)),
            _full((1, gd)),
            _full((1, gd)),
            _full(w_s.shape),
            _full((CHUNK, GMLP_HEADS)),
            _full(w_out.shape),
        ],
        out_specs=pl.BlockSpec((rows, d), lambda i: (i + base, 0)),
        out_shape=jax.ShapeDtypeStruct((n, d), F32),
        scratch_shapes=[pltpu.VMEM((8, cd), F32), pltpu.VMEM((rows, gd), BF16)],
        input_output_aliases={0: 0} if n_chunks > 1 else {},
        compiler_params=_params(("arbitrary",)),
        name="mixer",
    )(h, *(pending or ()), g1.reshape(1, d), w_in.astype(BF16), conv_w.T, ln_g.reshape(1, gd),
      ln_b.reshape(1, gd), w_s, b_s.T, w_out.astype(BF16))


def _swiglu(xn, wg, wu, wd):
    h1 = jnp.dot(xn, wg, preferred_element_type=F32)
    h2 = jnp.dot(xn, wu, preferred_element_type=F32)
    a = (h1 / (1.0 + jnp.exp(-h1)) * h2).astype(BF16)
    return jnp.dot(a, wd, preferred_element_type=F32)


def _dense_ffn_kernel(x_ref, g_ref, wg_ref, wu_ref, wd_ref, *rest):
    n_side = (len(rest) - 1) // 2
    side_in, o_ref, side_out = rest[:n_side], rest[n_side], rest[n_side + 1:]
    x = x_ref[...]
    xn = _rms(x, g_ref[...]).astype(BF16)
    o_ref[...] = x + _swiglu(xn, wg_ref[...], wu_ref[...], wd_ref[...])
    for src, dst in zip(side_in, side_out):
        dst[...] = src[...].astype(dst.dtype)


def _side_cast_ok(w, steps):
    rows = int(np.prod(w.shape[:-1]))
    return rows % steps == 0 and (rows // steps) % 16 == 0 and w.shape[-1] % LANES == 0


def _dense_ffn(h, g, wg, wu, wd, side=()):
    n, d = h.shape
    rows = _row_block(n, FFN_ROWS)
    steps = n // rows
    side2d = [w.reshape(-1, w.shape[-1]) for w in side]
    side_specs = [pl.BlockSpec((w.shape[0] // steps, w.shape[1]), lambda i: (i, 0)) for w in side2d]
    outs = pl.pallas_call(
        _dense_ffn_kernel,
        grid=(steps,),
        in_specs=[
            pl.BlockSpec((rows, d), lambda i: (i, 0)),
            _full((1, d)),
            _full(wg.shape),
            _full(wu.shape),
            _full(wd.shape),
        ] + side_specs,
        out_specs=[pl.BlockSpec((rows, d), lambda i: (i, 0))] + side_specs,
        out_shape=[jax.ShapeDtypeStruct((n, d), F32)]
        + [jax.ShapeDtypeStruct(w.shape, BF16) for w in side2d],
        compiler_params=_params(("arbitrary",)),
        name="dense_ffn",
    )(h, g.reshape(1, d), wg.astype(BF16), wu.astype(BF16), wd.astype(BF16), *side2d)
    return outs[0], [o.reshape(w.shape) for o, w in zip(outs[1:], side)]


def _pack_bf16_pairs(x, piece):
    half = x.shape[1] // 2
    bits = lax.bitcast_convert_type(x.astype(BF16).astype(F32), jnp.uint32)
    return [(bits[:, half + p * piece:half + (p + 1) * piece] & jnp.uint32(0xFFFF0000))
            | (bits[:, p * piece:(p + 1) * piece] >> 16) for p in range(half // piece)]


def _unpack_bf16_pair(packed):
    lo = lax.bitcast_convert_type(packed << 16, F32)
    hi = lax.bitcast_convert_type(packed & jnp.uint32(0xFFFF0000), F32)
    return lo, hi


def _moe_ffn_kernel(be_ref, used_ref, short_ref, x_ref, wg_ref, wu_ref, wd_ref, o_ref, xn_ref,
                    acc_ref, *, n_steps):
    del be_ref
    i = pl.program_id(0)
    f = pl.program_id(1)
    parts, rows, piece = x_ref.shape
    half = parts * piece
    active = i < used_ref[0]
    short = short_ref[i] == 1

    def step(first, last, n):
        if first:
            for p in range(parts):
                lo, hi = _unpack_bf16_pair(x_ref[p, 0:n, :])
                xn_ref[0:n, p * piece:(p + 1) * piece] = lo.astype(BF16)
                xn_ref[0:n, half + p * piece:half + (p + 1) * piece] = hi.astype(BF16)
        part = _swiglu(xn_ref[0:n, :], wg_ref[0, 0], wu_ref[0, 0], wd_ref[0, 0])
        if last:
            val = part if first else acc_ref[0:n, :] + part
            for p, words in enumerate(_pack_bf16_pairs(val, piece)):
                o_ref[p, 0:n, :] = words
                if n < rows:
                    o_ref[p, n:rows, :] = jnp.zeros((rows - n, piece), o_ref.dtype)
        elif first:
            acc_ref[0:n, :] = part
        else:
            acc_ref[0:n, :] += part

    for n, cond in ((rows, active & jnp.logical_not(short)), (rows // 2, active & short)):
        if n_steps == 1:
            pl.when(cond)(functools.partial(step, True, True, n))
        else:
            pl.when(cond & (f == 0))(functools.partial(step, True, False, n))
            if n_steps > 2:
                pl.when(cond & (f > 0) & (f < n_steps - 1))(
                    functools.partial(step, False, False, n))
            pl.when(cond & (f == n_steps - 1))(functools.partial(step, False, True, n))

    @pl.when(jnp.logical_not(active) & (f == n_steps - 1))
    def _():
        o_ref[...] = jnp.zeros_like(o_ref)


def _moe_ffn(x_sorted, block_expert, n_used, block_short, layer, wg, wu, wd):
    parts, n_rows, piece = x_sorted.shape
    d = 2 * parts * piece
    fdim = wg.shape[3]
    rows = MOE_ROWS
    cols = _col_block(fdim, fdim // MOE_COL_SPLIT)
    n_steps = fdim // cols

    def col(i, f, used):
        return jnp.where(i < used[0], f, n_steps - 1)

    grid_spec = pltpu.PrefetchScalarGridSpec(
        num_scalar_prefetch=3,
        grid=(n_rows // rows, n_steps),
        in_specs=[
            pl.BlockSpec((parts, rows, piece),
                         lambda i, f, be, used, short: (0, jnp.minimum(i, used[0] - 1), 0)),
            pl.BlockSpec((1, 1, d, cols),
                         lambda i, f, be, used, short: (layer, be[i], 0, col(i, f, used))),
            pl.BlockSpec((1, 1, d, cols),
                         lambda i, f, be, used, short: (layer, be[i], 0, col(i, f, used))),
            pl.BlockSpec((1, 1, cols, d),
                         lambda i, f, be, used, short: (layer, be[i], col(i, f, used), 0)),
        ],
        out_specs=pl.BlockSpec((parts, rows, piece), lambda i, f, be, used, short: (0, i, 0)),
        scratch_shapes=[pltpu.VMEM((rows, d), BF16), pltpu.VMEM((rows, d), F32)],
    )
    kern = functools.partial(_moe_ffn_kernel, n_steps=n_steps)
    return pl.pallas_call(
        kern,
        grid_spec=grid_spec,
        out_shape=jax.ShapeDtypeStruct((parts, n_rows, piece), jnp.uint32),
        compiler_params=_params(("arbitrary", "arbitrary")),
        name="moe_ffn",
    )(block_expert, n_used, block_short, x_sorted, wg, wu, wd)


def _cast_kernel(x_ref, o_ref):
    o_ref[...] = x_ref[...].astype(o_ref.dtype)


def _to_bf16(w):
    shape = w.shape
    w2 = w.reshape(-1, shape[-1])
    pref = CAST_BLOCK_BYTES // (4 * shape[-1]) // LANES * LANES
    rows = _row_block(w2.shape[0], pref)
    out = pl.pallas_call(
        _cast_kernel,
        grid=(w2.shape[0] // rows,),
        in_specs=[pl.BlockSpec((rows, shape[-1]), lambda i: (i, 0))],
        out_specs=pl.BlockSpec((rows, shape[-1]), lambda i: (i, 0)),
        out_shape=jax.ShapeDtypeStruct(w2.shape, BF16),
        compiler_params=_params(("arbitrary",)),
        name="cast_bf16",
    )(w2)
    return out.reshape(shape)


def _qkv_kernel(h_ref, g_ref, pos_ref, w_ref, b_ref, invf_ref, mc_ref, q_ref, kv_ref, *, q_dim):
    x = h_ref[...]
    xn = _rms(x, g_ref[...]).astype(BF16)
    half_rows = x.shape[0] // 2
    z = jnp.concatenate(
        [jnp.dot(xn[0:half_rows], w_ref[...], preferred_element_type=F32),
         jnp.dot(xn[half_rows:2 * half_rows], w_ref[...], preferred_element_type=F32)],
        axis=0) + b_ref[...]
    ang = pos_ref[...].astype(F32) * invf_ref[...]
    reps = LANES // ang.shape[0]
    cos = jnp.concatenate([jnp.cos(ang)] * reps, axis=0).T
    sin = jnp.concatenate([jnp.sin(ang)] * reps, axis=0).T
    cos = jnp.where(mc_ref[...] != 0.0, cos, 1.0)
    sin = sin * mc_ref[...]

    def rope(t):
        return t * cos + pltpu.roll(t, LANES // 2, 1) * sin

    for j in range(q_dim // LANES):
        cs = slice(j * LANES, (j + 1) * LANES)
        q_ref[:, cs] = (rope(z[:, cs]) * Q_SCALE).astype(BF16)
    kv_ref[:, 0:LANES] = rope(z[:, q_dim:q_dim + LANES]).astype(BF16)
    kv_ref[:, LANES:2 * LANES] = z[:, q_dim + LANES:q_dim + 2 * LANES].astype(BF16)


QK_GROUP = LANES // 4


def _qk_tile_layout():
    lane = np.arange(LANES)
    group, off = lane // QK_GROUP, lane % QK_GROUP
    half = ROPE_DIM // 2
    head = group % 2
    second = group // 2
    rest = QK_GROUP - half
    dim = np.where(off < half, second * half + off, ROPE_DIM + second * rest + (off - half))
    return head, dim


def _qk_column_order(n_cols):
    head, dim = _qk_tile_layout()
    tile = np.arange(n_cols) // LANES
    return tile * LANES + np.tile(head * HEAD_DIM + dim, n_cols // LANES)


def _rope_sign_lanes():
    _, dim = _qk_tile_layout()
    half = ROPE_DIM // 2
    sign = np.where(dim < half, -1.0, np.where(dim < ROPE_DIM, 1.0, 0.0)).astype(np.float32)
    return jnp.asarray(sign[None, :])


def _qkv(h, g, positions, w_qkv, b_qkv, q_dim):
    n, d = h.shape
    qkv_dim = w_qkv.shape[1]
    assert qkv_dim == q_dim + 2 * LANES and QK_GROUP % (ROPE_DIM // 2) == 0
    rows = _row_block(n, PROJ_ROWS)
    inv_freq = ROPE_THETA ** (-jnp.arange(0, ROPE_DIM, 2, dtype=F32) / ROPE_DIM)
    order = np.concatenate([_qk_column_order(q_dim + LANES), np.arange(q_dim + LANES, qkv_dim)])
    w_qkv = w_qkv[:, order]
    b_qkv = b_qkv[order]
    kern = functools.partial(_qkv_kernel, q_dim=q_dim)
    return pl.pallas_call(
        kern,
        grid=(n // rows,),
        in_specs=[
            pl.BlockSpec((rows, d), lambda i: (i, 0)),
            _full((1, d)),
            pl.BlockSpec((1, rows), lambda i: (0, i)),
            _full(w_qkv.shape),
            _full((1, qkv_dim)),
            _full((ROPE_DIM // 2, 1)),
            _full((1, LANES)),
        ],
        out_specs=[
            pl.BlockSpec((rows, q_dim), lambda i: (i, 0)),
            pl.BlockSpec((rows, 2 * LANES), lambda i: (i, 0)),
        ],
        out_shape=[
            jax.ShapeDtypeStruct((n, q_dim), BF16),
            jax.ShapeDtypeStruct((n, 2 * LANES), BF16),
        ],
        compiler_params=_params(("arbitrary",)),
        name="qkv_rope",
    )(h, g.reshape(1, d), positions.reshape(1, n), w_qkv.astype(BF16),
      b_qkv.reshape(1, qkv_dim), inv_freq.reshape(-1, 1), _rope_sign_lanes())


def _attn_kernel(sink_ref, q_ref, kvc_ref, kvp_ref, h_ref, wo_ref, bo_ref, out_ref,
                 kbuf, vbuf, o_buf, *, wpb):
    rows = q_ref.shape[0]
    tiles = q_ref.shape[1] // LANES // 2
    j = pl.program_id(1)
    kbuf[0:WINDOW, :] = kvp_ref[:, 0:LANES]
    kbuf[WINDOW:WINDOW + rows, :] = kvc_ref[:, 0:LANES]
    vbuf[0:WINDOW, :] = kvp_ref[:, LANES:2 * LANES]
    vbuf[WINDOW:WINDOW + rows, :] = kvc_ref[:, LANES:2 * LANES]

    from_prev = (lax.broadcasted_iota(jnp.int32, (WINDOW, WINDOW), 1)
                 > lax.broadcasted_iota(jnp.int32, (WINDOW, WINDOW), 0))
    kv_lane = lax.broadcasted_iota(jnp.int32, (2 * WINDOW, LANES), 1)
    first_half = kv_lane < HEAD_DIM
    head0_lanes = (kv_lane // QK_GROUP) % 2 == 0
    out_first_half = lax.broadcasted_iota(jnp.int32, (WINDOW, LANES), 1) < HEAD_DIM
    ones_lo = jnp.where(first_half, 1.0, 0.0)
    ones_hi = jnp.where(first_half, 0.0, 1.0)
    nt = (((1,), (1,)), ((), ()))

    def window(n, carry):
        r0 = pl.multiple_of(n * WINDOW, WINDOW)
        kt = kbuf[pl.ds(r0, 2 * WINDOW), :].astype(F32)
        vt = vbuf[pl.ds(r0, 2 * WINDOW), :].astype(F32)
        prev_bias = jnp.where(j * wpb + n > 0, 0.0, -jnp.inf)
        k0_lo = jnp.where(head0_lanes, kt, 0.0)
        k1_hi = jnp.where(head0_lanes, 0.0, kt)
        v0_lo = jnp.where(first_half, vt, 0.0)
        v1_hi = jnp.where(first_half, 0.0, vt)
        k_both = (
            jnp.concatenate([k0_lo, pltpu.roll(k0_lo, QK_GROUP, 1)], axis=0).astype(BF16),
            jnp.concatenate([pltpu.roll(k1_hi, LANES - QK_GROUP, 1), k1_hi], axis=0).astype(BF16))
        v_both = (
            jnp.concatenate([jnp.concatenate([v0_lo, ones_lo], axis=1),
                             jnp.concatenate([pltpu.roll(v0_lo, HEAD_DIM, 1), ones_hi], axis=1)],
                            axis=0).astype(BF16),
            jnp.concatenate([jnp.concatenate([pltpu.roll(v1_hi, HEAD_DIM, 1), ones_lo], axis=1),
                             jnp.concatenate([v1_hi, ones_hi], axis=1)], axis=0).astype(BF16))
        for kh in range(2):
            for t0 in range(0, tiles, ATTN_STACK):
                group = [kh * tiles + t0 + u for u in range(ATTN_STACK)]
                q_stack = jnp.concatenate(
                    [q_ref[pl.ds(r0, WINDOW), t * LANES:(t + 1) * LANES] for t in group], axis=0)
                s_all = lax.dot_general(q_stack, k_both[kh], nt, preferred_element_type=F32)
                p_rows, corr = [], []
                for u, t in enumerate(group):
                    p_cols, corr_t = [], []
                    for parity in range(2):
                        sink = sink_ref[t * 2 + parity]
                        c0 = parity * 2 * WINDOW
                        s_prev = s_all[u * WINDOW:(u + 1) * WINDOW, c0:c0 + WINDOW] + prev_bias
                        s_cur = s_all[u * WINDOW:(u + 1) * WINDOW, c0 + WINDOW:c0 + 2 * WINDOW]
                        s = jnp.where(from_prev, s_prev, s_cur)
                        m = jnp.maximum(jnp.max(s, axis=-1, keepdims=True), sink)
                        p = jnp.exp2(s - m)
                        p_cols += [jnp.where(from_prev, p, 0.0), jnp.where(from_prev, 0.0, p)]
                        corr_t.append(jnp.exp2(sink - m))
                    p_rows.append(jnp.concatenate(p_cols, axis=1).astype(BF16))
                    corr.append(corr_t)
                pv = jnp.dot(jnp.concatenate(p_rows, axis=0), v_both[kh],
                             preferred_element_type=F32)
                for u, t in enumerate(group):
                    num = pv[u * WINDOW:(u + 1) * WINDOW, 0:LANES]
                    den = (pv[u * WINDOW:(u + 1) * WINDOW, LANES:2 * LANES]
                           + jnp.where(out_first_half, corr[u][0], corr[u][1]))
                    o_buf[pl.ds(r0, WINDOW), t * LANES:(t + 1) * LANES] = (num / den).astype(BF16)
        return carry

    lax.fori_loop(0, rows // WINDOW, window, 0, unroll=True)
    out_ref[...] = (h_ref[...] + jnp.dot(o_buf[...], wo_ref[...], preferred_element_type=F32)
                    + bo_ref[...])


def _attention(h, q, kv, sinks, w_o, b_o, batch, seq):
    n, q_dim = q.shape
    d = h.shape[1]
    assert kv.shape[1] == 2 * LANES and (q_dim // HEAD_DIM) % 4 == 0
    rows = _row_block(seq, ATTN_ROWS)
    bps = seq // rows
    wpb = rows // WINDOW
    wps = seq // WINDOW

    grid_spec = pltpu.PrefetchScalarGridSpec(
        num_scalar_prefetch=1,
        grid=(batch, bps),
        in_specs=[
            pl.BlockSpec((rows, q_dim), lambda b, j, s: (b * bps + j, 0)),
            pl.BlockSpec((rows, 2 * LANES), lambda b, j, s: (b * bps + j, 0)),
            pl.BlockSpec((WINDOW, 2 * LANES),
                         lambda b, j, s: (b * wps + jnp.maximum(j * wpb - 1, 0), 0)),
            pl.BlockSpec((rows, d), lambda b, j, s: (b * bps + j, 0)),
            pl.BlockSpec((q_dim, d), lambda b, j, s: (0, 0)),
            pl.BlockSpec((1, d), lambda b, j, s: (0, 0)),
        ],
        out_specs=pl.BlockSpec((rows, d), lambda b, j, s: (b * bps + j, 0)),
        scratch_shapes=[pltpu.VMEM((rows + WINDOW, LANES), BF16),
                        pltpu.VMEM((rows + WINDOW, LANES), BF16),
                        pltpu.VMEM((rows, q_dim), BF16)],
    )
    kern = functools.partial(_attn_kernel, wpb=wpb)
    return pl.pallas_call(
        kern,
        grid_spec=grid_spec,
        out_shape=jax.ShapeDtypeStruct((n, d), F32),
        compiler_params=_params(("arbitrary", "arbitrary")),
        name="swa_attention",
    )(sinks.astype(F32) * LOG2_E, q, kv, kv, h, w_o.astype(BF16), b_o.reshape(1, d))


R_IDX0, R_IDX1, R_GATE0, R_GATE1, R_RANK0, R_RANK1 = range(6)


def _router_kernel(h_ref, g_ref, rwt_ref, xpk_ref, route_ref, route_t_ref, cnt_ref, tri_ref,
                   carry_ref):
    rows = h_ref.shape[0]
    i = pl.program_id(0)

    @pl.when(i == 0)
    def _():
        r = lax.broadcasted_iota(jnp.int32, (rows, rows), 0)
        c = lax.broadcasted_iota(jnp.int32, (rows, rows), 1)
        tri_ref[...] = jnp.where(r < c, 1.0, 0.0).astype(BF16)
        carry_ref[...] = jnp.zeros_like(carry_ref)

    xn = _rms(h_ref[...], g_ref[...])
    xb = xn.astype(BF16)
    for p, words in enumerate(_pack_bf16_pairs(xn, xpk_ref.shape[2])):
        xpk_ref[p] = words

    nt = (((1,), (1,)), ((), ()))
    logits = lax.dot_general(rwt_ref[...], xb, nt, preferred_element_type=F32)[0:N_EXPERTS]
    ex = lax.broadcasted_iota(jnp.int32, logits.shape, 0)
    m1 = jnp.max(logits, axis=0, keepdims=True)
    i1 = jnp.min(jnp.where(logits == m1, ex, N_EXPERTS), axis=0, keepdims=True)
    lg2 = jnp.where(ex == i1, -jnp.inf, logits)
    m2 = jnp.max(lg2, axis=0, keepdims=True)
    i2 = jnp.min(jnp.where(lg2 == m2, ex, N_EXPERTS), axis=0, keepdims=True)
    e = jnp.exp(m2 - m1)
    g1 = 1.0 / (1.0 + e)
    g2 = e / (1.0 + e)

    sel_f = jnp.where((ex == i1) | (ex == i2), 1.0, 0.0)
    sel_pad = jnp.concatenate([sel_f, jnp.zeros_like(sel_f)], axis=0).astype(BF16)
    carry = carry_ref[:, 0:1]
    before = jnp.dot(sel_pad, tri_ref[...], preferred_element_type=F32)[0:N_EXPERTS] + carry
    r1 = jnp.sum(jnp.where(ex == i1, before, 0.0), axis=0, keepdims=True)
    r2 = jnp.sum(jnp.where(ex == i2, before, 0.0), axis=0, keepdims=True)
    carry = carry + jnp.sum(sel_f, axis=1, keepdims=True)
    carry_ref[...] = jnp.broadcast_to(carry, carry_ref.shape)
    cnt_ref[...] = jnp.broadcast_to(carry, cnt_ref.shape)

    fields = [None] * 8
    for k, val in ((R_IDX0, i1.astype(F32)), (R_IDX1, i2.astype(F32)), (R_GATE0, g1),
                   (R_GATE1, g2), (R_RANK0, r1), (R_RANK1, r2)):
        fields[k] = val
    route_t = jnp.concatenate([f if f is not None else jnp.zeros_like(g1) for f in fields], axis=0)
    route_t_ref[...] = route_t
    route_ref[...] = jnp.concatenate([route_t] * (LANES // 8), axis=0).T


def _router(h, g, router_w):
    n, d = h.shape
    rows = _row_block(n, ROUTER_ROWS)
    parts = d // 2 // SC_PIECE
    rwt = jnp.zeros((16, d), BF16).at[0:N_EXPERTS, :].set(router_w.T.astype(BF16))
    return pl.pallas_call(
        _router_kernel,
        grid=(n // rows,),
        in_specs=[
            pl.BlockSpec((rows, d), lambda i: (i, 0)),
            _full((1, d)),
            _full((16, d)),
        ],
        out_specs=[
            pl.BlockSpec((parts, rows, SC_PIECE), lambda i: (0, i, 0)),
            pl.BlockSpec((rows, LANES), lambda i: (i, 0)),
            pl.BlockSpec((8, rows), lambda i: (0, i)),
            _full((8, LANES)),
        ],
        out_shape=[
            jax.ShapeDtypeStruct((parts, n, SC_PIECE), jnp.uint32),
            jax.ShapeDtypeStruct((n, LANES), F32),
            jax.ShapeDtypeStruct((8, n), F32),
            jax.ShapeDtypeStruct((8, LANES), F32),
        ],
        scratch_shapes=[pltpu.VMEM((rows, rows), BF16), pltpu.VMEM((8, LANES), F32)],
        compiler_params=_params(("arbitrary",)),
        name="moe_router",
    )(h, g.reshape(1, d), rwt)


def _sc_mesh():
    return plsc.VectorSubcoreMesh(core_axis_name="core", subcore_axis_name="subcore")


def _gather_pieces(src, idx):
    m = idx.shape[0]
    width = src.shape[1]
    assert m % (SC_WINDOW * SC_WORKERS) == 0

    @functools.partial(pl.kernel, out_type=jax.ShapeDtypeStruct((m, width), src.dtype),
                       mesh=_sc_mesh(), scratch_types=[])
    def gather_kernel(src_hbm, idx_hbm, out_hbm):
        def body(idx_vmem, out_vmem):
            pltpu.sync_copy(src_hbm.at[idx_vmem.at[0]], out_vmem)

        pltpu.emit_pipeline(
            body,
            grid=(m // SC_WINDOW,),
            in_specs=[pl.BlockSpec((1, SC_WINDOW), lambda i: (0, i))],
            out_specs=[pl.BlockSpec((SC_WINDOW, width), lambda i: (i, 0))],
            core_axis_name=("core", "subcore"),
            dimension_semantics=(pltpu.PARALLEL,),
        )(idx_hbm, out_hbm)

    return gather_kernel(src, idx.reshape(1, m))


def _scatter_pieces(src, idx, out_rows):
    copies, m = idx.shape
    width = src.shape[1]
    assert m == src.shape[0] and m % (SC_WINDOW * SC_WORKERS) == 0

    @functools.partial(pl.kernel, out_type=jax.ShapeDtypeStruct((out_rows, width), src.dtype),
                       mesh=_sc_mesh(), scratch_types=[])
    def scatter_kernel(src_hbm, *refs):
        idx_hbm, out_hbm = refs[:copies], refs[copies]

        def body(src_vmem, *idx_vmem):
            for iv in idx_vmem:
                pltpu.sync_copy(src_vmem, out_hbm.at[iv.at[0]])

        pltpu.emit_pipeline(
            body,
            grid=(m // SC_WINDOW,),
            in_specs=[pl.BlockSpec((SC_WINDOW, width), lambda i: (i, 0))]
            + [pl.BlockSpec((1, SC_WINDOW), lambda i: (0, i))] * copies,
            out_specs=[],
            core_axis_name=("core", "subcore"),
            dimension_semantics=(pltpu.PARALLEL,),
        )(src_hbm, *idx_hbm)

    return scatter_kernel(src, *[idx[j].reshape(1, m) for j in range(copies)])


def _moe_combined(h_ref, y_ref, route_ref):
    parts, _, _, piece = y_ref.shape
    route = route_ref[...]
    g0 = route[:, R_GATE0:R_GATE0 + 1]
    g1 = route[:, R_GATE1:R_GATE1 + 1]
    lo, hi = [], []
    for p in range(parts):
        lo0, hi0 = _unpack_bf16_pair(y_ref[p, 0])
        lo1, hi1 = _unpack_bf16_pair(y_ref[p, 1])
        lo.append(g0 * lo0 + g1 * lo1)
        hi.append(g0 * hi0 + g1 * hi1)
    return h_ref[...] + jnp.concatenate(lo + hi, axis=1)


def _moe_specs(pending, rows, base):
    y_pairs, route = pending
    parts, _, _, piece = y_pairs.shape
    return [pl.BlockSpec((parts, 2, rows, piece), lambda i: (0, 0, i, 0)),
            pl.BlockSpec((rows, LANES), lambda i: (i + base, 0))]


def _final_kernel(h_ref, y_ref, route_ref, g_ref, o_ref):
    o_ref[...] = _rms(_moe_combined(h_ref, y_ref, route_ref), g_ref[...])


def _final(h, pending, final_g):
    n, d = h.shape
    y_chunks, route = pending
    n_chunks = len(y_chunks)
    rows = _row_block(n // n_chunks, PROJ_ROWS)
    steps = n // rows // n_chunks
    for c, y in enumerate(y_chunks):
        base = c * steps
        h = pl.pallas_call(
            _final_kernel,
            grid=(steps,),
            in_specs=[pl.BlockSpec((rows, d), lambda i, base=base: (i + base, 0))]
            + _moe_specs((y, route), rows, base) + [_full((1, d))],
            out_specs=pl.BlockSpec((rows, d), lambda i, base=base: (i + base, 0)),
            out_shape=jax.ShapeDtypeStruct((n, d), F32),
            input_output_aliases={0: 0} if n_chunks > 1 else {},
            compiler_params=_params(("arbitrary",)),
            name="moe_combine_final_norm",
        )(h, y, route, final_g.reshape(1, d))
    return h


def _moe(h, g, router_w, layer, wg, wu, wd, n_chunks):
    n, d = h.shape
    xpk, route, route_t, cnt = _router(h, g, router_w)
    parts = xpk.shape[0]

    top_idx = route_t[R_IDX0:R_IDX1 + 1].astype(jnp.int32)
    rank = route_t[R_RANK0:R_RANK1 + 1].astype(jnp.int32)
    sizes = cnt[0:N_EXPERTS, 0].astype(jnp.int32)
    padded = ((sizes + MOE_ROWS - 1) // MOE_ROWS) * MOE_ROWS
    pends = jnp.cumsum(padded)
    pstarts = pends - padded
    dest = rank
    for e in range(N_EXPERTS):
        dest = dest + jnp.where(top_idx == e, pstarts[e], 0)
    n_rows = 2 * n + N_EXPERTS * MOE_ROWS
    n_blocks = n_rows // MOE_ROWS
    block_start = jnp.arange(n_blocks, dtype=jnp.int32) * MOE_ROWS
    block_expert = jnp.minimum(
        jnp.sum((block_start[:, None] >= pends[None, :]).astype(jnp.int32), axis=1),
        N_EXPERTS - 1)
    n_used = (pends[N_EXPERTS - 1:] // MOE_ROWS).astype(jnp.int32)
    group_end = jnp.sum(jnp.where(block_expert[:, None] == jnp.arange(N_EXPERTS)[None, :],
                                  (pstarts + sizes)[None, :], 0), axis=1)
    block_short = (group_end - block_start <= MOE_ROWS // 2).astype(jnp.int32)

    off = jnp.arange(parts, dtype=jnp.int32) * n_rows
    scatter_idx = (dest[:, None, :] + off[None, :, None]).reshape(2, parts * n)
    x_sorted = _scatter_pieces(xpk.reshape(parts * n, SC_PIECE), scatter_idx, parts * n_rows)
    y_rows = _moe_ffn(x_sorted.reshape(parts, n_rows, SC_PIECE), block_expert, n_used,
                      block_short, layer, wg, wu, wd)

    y_flat = y_rows.reshape(parts * n_rows, SC_PIECE)
    nc = n // n_chunks
    chunk_dest = dest.reshape(2, n_chunks, nc).transpose(1, 0, 2)
    gather_idx = (chunk_dest[:, None] + off[None, :, None, None]).reshape(n_chunks, -1)
    y_chunks = [_gather_pieces(y_flat, gather_idx[c]).reshape(parts, 2, nc, SC_PIECE)
                for c in range(n_chunks)]
    return y_chunks, route


def kernel(x, positions, final_norm_g, ev_norm1_g, ev_w_in, ev_conv_w, ev_ln_g, ev_ln_b, ev_spatial_w, ev_spatial_b, ev_w_out, ev_norm2_g, ev_ffn_wg, ev_ffn_wu, ev_ffn_wd, od_norm1_g, od_w_qkv, od_b_qkv, od_sinks, od_w_o, od_b_o, od_norm2_g, od_router_w, od_exp_wg, od_exp_wu, od_exp_wd):
    batch, seq, d = x.shape
    depth = ev_norm1_g.shape[0] + od_norm1_g.shape[0]
    assert depth % 2 == 0, "the final norm is fused into the last (odd) layer's MoE combine"
    n_q_heads = od_sinks.shape[1]
    h = x.reshape(batch * seq, d)
    experts = [od_exp_wg, od_exp_wu, od_exp_wd]
    ffn_steps = (batch * seq) // _row_block(batch * seq, FFN_ROWS)
    ride_along = all(_side_cast_ok(w, ffn_steps) for w in experts)
    if not ride_along:
        experts = [_to_bf16(w) for w in experts]
    pending = None
    pieces_per_token = 2 * (d // 2 // SC_PIECE)
    n_chunks = max(c for c in range(1, COMBINE_CHUNKS + 1)
                   if batch % c == 0
                   and (batch // c * seq * pieces_per_token) % (SC_WINDOW * SC_WORKERS) == 0)
    for layer in range(depth):
        i = layer // 2
        if layer % 2 == 0:
            h = _mixer(h, pending, seq, ev_norm1_g[i], ev_w_in[i], ev_conv_w[i], ev_ln_g[i],
                       ev_ln_b[i], ev_spatial_w[i], ev_spatial_b[i], ev_w_out[i])
            side = experts if (ride_along and layer == 0) else ()
            h, cast = _dense_ffn(h, ev_norm2_g[i], ev_ffn_wg[i], ev_ffn_wu[i], ev_ffn_wd[i], side)
            if side:
                experts = cast
        else:
            q, kv = _qkv(h, od_norm1_g[i], positions, od_w_qkv[i], od_b_qkv[i],
                         n_q_heads * HEAD_DIM)
            h = _attention(h, q, kv, od_sinks[i], od_w_o[i], od_b_o[i], batch, seq)
            pending = _moe(h, od_norm2_g[i], od_router_w[i], i, *experts, n_chunks)
    return _final(h, pending, final_norm_g).reshape(batch, seq, d)
```

```python
import functools

import jax
import jax.numpy as jnp
import numpy as np
from jax import lax
from jax.experimental import pallas as pl
from jax.experimental.pallas import tpu as pltpu
from jax.experimental.pallas import tpu_sc as plsc

F32 = jnp.float32
BF16 = jnp.bfloat16

EPS = 1e-5
CHUNK = 128
GMLP_HEADS = 4
CONV_WIDTH = 3
HEAD_DIM = 64
WINDOW = 128
ROPE_DIM = HEAD_DIM // 4
ROPE_THETA = 500000.0
ATTN_SCALE = HEAD_DIM ** -0.5
LOG2_E = float(np.log2(np.e))
Q_SCALE = ATTN_SCALE * LOG2_E
N_EXPERTS = 8
LANES = 128
VMEM_LIMIT = 56 * 1024 * 1024

MIXER_ROWS = 512
FFN_ROWS = 512
CAST_BLOCK_BYTES = 8 * 1024 * 1024
PROJ_ROWS = 1024
ATTN_ROWS = 1024
ATTN_STACK = 2
ROUTER_ROWS = 1024
MOE_ROWS = 1024
MOE_FILL_STEPS = 4
COMBINE_CHUNKS = 4
MOE_COL_SPLIT = 2
SC_WORKERS = 32
SC_PIECE = 256
SC_WINDOW = 128


def _row_block(n, pref):
    b = min(n, pref)
    while n % b:
        b -= LANES
    return b


def _col_block(f, pref):
    b = min(f, pref)
    b -= b % LANES
    while f % b:
        b -= LANES
    return b


def _params(sem):
    return pltpu.CompilerParams(dimension_semantics=sem, vmem_limit_bytes=VMEM_LIMIT)


def _rms(x, g):
    return x * lax.rsqrt(jnp.mean(x * x, axis=-1, keepdims=True) + EPS) * g


def _gelu(x):
    return 0.5 * x * (1.0 + lax.erf(x * np.float32(np.sqrt(0.5))))


def _full(shape):
    return pl.BlockSpec(shape, lambda *_: (0,) * len(shape))


def _mixer_kernel(*refs, blocks_per_seq, has_pending):
    if has_pending:
        h_ref, y_ref, route_ref = refs[:3]
        refs = refs[3:]
    else:
        h_ref = refs[0]
        refs = refs[1:]
    (g1_ref, win_ref, cw_ref, lng_ref, lnb_ref, ws_ref, bst_ref, wout_ref, o_ref,
     tail_ref, yb_ref) = refs
    rows = h_ref.shape[0]
    cd = cw_ref.shape[1]
    gd = lng_ref.shape[1]
    hd = gd // GMLP_HEADS
    i = pl.program_id(0)

    x = _moe_combined(h_ref, y_ref, route_ref) if has_pending else h_ref[...]
    xn = _rms(x, g1_ref[...]).astype(BF16)
    half_rows = rows // 2

    def by_halves(w):
        return jnp.concatenate([jnp.dot(xn[0:half_rows], w, preferred_element_type=F32),
                                jnp.dot(xn[half_rows:rows], w, preferred_element_type=F32)], axis=0)

    z_g = by_halves(win_ref[:, 3 * cd:3 * cd + 2 * gd])
    b_u = z_g[:, 0:gd]
    b_v = z_g[:, gd:2 * gd]
    z_c = by_halves(win_ref[:, 0:3 * cd])
    a_b = z_c[:, 0:cd]
    a_c = z_c[:, cd:2 * cd]
    a_x = z_c[:, 2 * cd:3 * cd]

    g = a_c * a_x
    tail = jnp.where(i % blocks_per_seq == 0, 0.0, tail_ref[...])
    row = lax.broadcasted_iota(jnp.int32, g.shape, 0)
    gm1 = jnp.where(row == 0, tail[7:8], pltpu.roll(g, 1, 0))
    gm2 = jnp.where(row == 0, tail[6:7], jnp.where(row == 1, tail[7:8], pltpu.roll(g, 2, 0)))
    tail_ref[...] = g[rows - 8:rows]
    cw = cw_ref[...]
    y_a = a_b * (gm2 * cw[0:1] + gm1 * cw[1:2] + g * cw[2:3])

    u = _gelu(b_u)
    v = _gelu(b_v)
    mu = jnp.mean(v, axis=-1, keepdims=True)
    vc = v - mu
    var = jnp.mean(vc * vc, axis=-1, keepdims=True)
    vn = (vc * lax.rsqrt(var + EPS) * lng_ref[...] + lnb_ref[...]).astype(BF16)
    ri = lax.broadcasted_iota(jnp.int32, (CHUNK, CHUNK), 0)
    ci = lax.broadcasted_iota(jnp.int32, (CHUNK, CHUNK), 1)
    causal = ri >= ci
    bst = bst_ref[...]
    for k in range(GMLP_HEADS):
        w_k = jnp.where(causal, ws_ref[k], 0.0).astype(BF16)
        b_k = bst[:, k:k + 1]
        for c in range(rows // CHUNK):
            rs = slice(c * CHUNK, (c + 1) * CHUNK)
            cs = slice(k * hd, (k + 1) * hd)
            mixed = jnp.dot(w_k, vn[rs, cs], preferred_element_type=F32) + b_k
            yb_ref[rs, cs] = (u[rs, cs] * mixed).astype(BF16)

    out = jnp.dot(y_a.astype(BF16), wout_ref[0:cd, :], preferred_element_type=F32)
    out = out + jnp.dot(yb_ref[...], wout_ref[cd:cd + gd, :], preferred_element_type=F32)
    o_ref[...] = x + out


def _mixer(h, pending, seq, *weights):
    if pending is None:
        return _mixer_call(h, None, 0, 1, seq, *weights)
    y_chunks, route = pending
    for c, y in enumerate(y_chunks):
        h = _mixer_call(h, (y, route), c, len(y_chunks), seq, *weights)
    return h


def _mixer_call(h, pending, chunk, n_chunks, seq, g1, w_in, conv_w, ln_g, ln_b, w_s, b_s, w_out):
    n, d = h.shape
    rows = _row_block(seq, MIXER_ROWS)
    cd = conv_w.shape[0]
    gd = ln_g.shape[0]
    has_pending = pending is not None
    steps = n // rows // n_chunks
    base = chunk * steps
    assert (steps * rows) % seq == 0
    kern = functools.partial(_mixer_kernel, blocks_per_seq=seq // rows, has_pending=has_pending)
    return pl.pallas_call(
        kern,
        grid=(steps,),
        in_specs=[pl.BlockSpec((rows, d), lambda i: (i + base, 0))]
        + (_moe_specs(pending, rows, base) if has_pending else [])
        + [
            _full((1, d)),
            _full(w_in.shape),
            _full((CONV_WIDTH, cd)),
            _full((1, gd)),
            _full((1, gd)),
            _full(w_s.shape),
            _full((CHUNK, GMLP_HEADS)),
            _full(w_out.shape),
        ],
        out_specs=pl.BlockSpec((rows, d), lambda i: (i + base, 0)),
        out_shape=jax.ShapeDtypeStruct((n, d), F32),
        scratch_shapes=[pltpu.VMEM((8, cd), F32), pltpu.VMEM((rows, gd), BF16)],
        input_output_aliases={0: 0} if n_chunks > 1 else {},
        compiler_params=_params(("arbitrary",)),
        name="mixer",
    )(h, *(pending or ()), g1.reshape(1, d), w_in.astype(BF16), conv_w.T, ln_g.reshape(1, gd),
      ln_b.reshape(1, gd), w_s, b_s.T, w_out.astype(BF16))


def _swiglu(xn, wg, wu, wd):
    h1 = jnp.dot(xn, wg, preferred_element_type=F32)
    h2 = jnp.dot(xn, wu, preferred_element_type=F32)
    a = (h1 / (1.0 + jnp.exp(-h1)) * h2).astype(BF16)
    return jnp.dot(a, wd, preferred_element_type=F32)


def _dense_ffn_kernel(x_ref, g_ref, wg_ref, wu_ref, wd_ref, *rest):
    n_side = (len(rest) - 1) // 2
    side_in, o_ref, side_out = rest[:n_side], rest[n_side], rest[n_side + 1:]
    x = x_ref[...]
    xn = _rms(x, g_ref[...]).astype(BF16)
    o_ref[...] = x + _swiglu(xn, wg_ref[...], wu_ref[...], wd_ref[...])
    for src, dst in zip(side_in, side_out):
        dst[...] = src[...].astype(dst.dtype)


def _side_cast_ok(w, steps):
    rows = int(np.prod(w.shape[:-1]))
    return rows % steps == 0 and (rows // steps) % 16 == 0 and w.shape[-1] % LANES == 0


def _dense_ffn(h, g, wg, wu, wd, side=()):
    n, d = h.shape
    rows = _row_block(n, FFN_ROWS)
    steps = n // rows
    side2d = [w.reshape(-1, w.shape[-1]) for w in side]
    side_specs = [pl.BlockSpec((w.shape[0] // steps, w.shape[1]), lambda i: (i, 0)) for w in side2d]
    outs = pl.pallas_call(
        _dense_ffn_kernel,
        grid=(steps,),
        in_specs=[
            pl.BlockSpec((rows, d), lambda i: (i, 0)),
            _full((1, d)),
            _full(wg.shape),
            _full(wu.shape),
            _full(wd.shape),
        ] + side_specs,
        out_specs=[pl.BlockSpec((rows, d), lambda i: (i, 0))] + side_specs,
        out_shape=[jax.ShapeDtypeStruct((n, d), F32)]
        + [jax.ShapeDtypeStruct(w.shape, BF16) for w in side2d],
        compiler_params=_params(("arbitrary",)),
        name="dense_ffn",
    )(h, g.reshape(1, d), wg.astype(BF16), wu.astype(BF16), wd.astype(BF16), *side2d)
    return outs[0], [o.reshape(w.shape) for o, w in zip(outs[1:], side)]


def _pack_bf16_pairs(x, piece):
    half = x.shape[1] // 2
    bits = lax.bitcast_convert_type(x.astype(BF16).astype(F32), jnp.uint32)
    return [(bits[:, half + p * piece:half + (p + 1) * piece] & jnp.uint32(0xFFFF0000))
            | (bits[:, p * piece:(p + 1) * piece] >> 16) for p in range(half // piece)]


def _unpack_bf16_pair(packed):
    lo = lax.bitcast_convert_type(packed << 16, F32)
    hi = lax.bitcast_convert_type(packed & jnp.uint32(0xFFFF0000), F32)
    return lo, hi


def _moe_ffn_kernel(be_ref, used_ref, fill_ref, x_ref, wg_ref, wu_ref, wd_ref, o_ref, xn_ref,
                    acc_ref, *, n_steps):
    del be_ref
    i = pl.program_id(0)
    f = pl.program_id(1)
    parts, rows, piece = x_ref.shape
    half = parts * piece
    active = i < used_ref[0]
    filled = fill_ref[i]

    def step(first, last, n):
        if first:
            for p in range(parts):
                lo, hi = _unpack_bf16_pair(x_ref[p, 0:n, :])
                xn_ref[0:n, p * piece:(p + 1) * piece] = lo.astype(BF16)
                xn_ref[0:n, half + p * piece:half + (p + 1) * piece] = hi.astype(BF16)
        part = _swiglu(xn_ref[0:n, :], wg_ref[0, 0], wu_ref[0, 0], wd_ref[0, 0])
        if last:
            val = part if first else acc_ref[0:n, :] + part
            for p, words in enumerate(_pack_bf16_pairs(val, piece)):
                o_ref[p, 0:n, :] = words
                if n < rows:
                    o_ref[p, n:rows, :] = jnp.zeros((rows - n, piece), o_ref.dtype)
        elif first:
            acc_ref[0:n, :] = part
        else:
            acc_ref[0:n, :] += part

    for q in range(1, MOE_FILL_STEPS + 1):
        n = rows * q // MOE_FILL_STEPS
        cond = active & (filled == q)
        if n_steps == 1:
            pl.when(cond)(functools.partial(step, True, True, n))
        else:
            pl.when(cond & (f == 0))(functools.partial(step, True, False, n))
            if n_steps > 2:
                pl.when(cond & (f > 0) & (f < n_steps - 1))(
                    functools.partial(step, False, False, n))
            pl.when(cond & (f == n_steps - 1))(functools.partial(step, False, True, n))

    @pl.when(jnp.logical_not(active) & (f == n_steps - 1))
    def _():
        o_ref[...] = jnp.zeros_like(o_ref)


def _moe_ffn(x_sorted, block_expert, n_used, block_fill, layer, wg, wu, wd):
    parts, n_rows, piece = x_sorted.shape
    d = 2 * parts * piece
    fdim = wg.shape[3]
    rows = MOE_ROWS
    cols = _col_block(fdim, fdim // MOE_COL_SPLIT)
    n_steps = fdim // cols

    def col(i, f, used):
        return jnp.where(i < used[0], f, n_steps - 1)

    grid_spec = pltpu.PrefetchScalarGridSpec(
        num_scalar_prefetch=3,
        grid=(n_rows // rows, n_steps),
        in_specs=[
            pl.BlockSpec((parts, rows, piece),
                         lambda i, f, be, used, fill: (0, jnp.minimum(i, used[0] - 1), 0)),
            pl.BlockSpec((1, 1, d, cols),
                         lambda i, f, be, used, fill: (layer, be[i], 0, col(i, f, used))),
            pl.BlockSpec((1, 1, d, cols),
                         lambda i, f, be, used, fill: (layer, be[i], 0, col(i, f, used))),
            pl.BlockSpec((1, 1, cols, d),
                         lambda i, f, be, used, fill: (layer, be[i], col(i, f, used), 0)),
        ],
        out_specs=pl.BlockSpec((parts, rows, piece), lambda i, f, be, used, fill: (0, i, 0)),
        scratch_shapes=[pltpu.VMEM((rows, d), BF16), pltpu.VMEM((rows, d), F32)],
    )
    kern = functools.partial(_moe_ffn_kernel, n_steps=n_steps)
    return pl.pallas_call(
        kern,
        grid_spec=grid_spec,
        out_shape=jax.ShapeDtypeStruct((parts, n_rows, piece), jnp.uint32),
        compiler_params=_params(("arbitrary", "arbitrary")),
        name="moe_ffn",
    )(block_expert, n_used, block_fill, x_sorted, wg, wu, wd)


def _cast_kernel(x_ref, o_ref):
    o_ref[...] = x_ref[...].astype(o_ref.dtype)


def _to_bf16(w):
    shape = w.shape
    w2 = w.reshape(-1, shape[-1])
    pref = CAST_BLOCK_BYTES // (4 * shape[-1]) // LANES * LANES
    rows = _row_block(w2.shape[0], pref)
    out = pl.pallas_call(
        _cast_kernel,
        grid=(w2.shape[0] // rows,),
        in_specs=[pl.BlockSpec((rows, shape[-1]), lambda i: (i, 0))],
        out_specs=pl.BlockSpec((rows, shape[-1]), lambda i: (i, 0)),
        out_shape=jax.ShapeDtypeStruct(w2.shape, BF16),
        compiler_params=_params(("arbitrary",)),
        name="cast_bf16",
    )(w2)
    return out.reshape(shape)


def _qkv_kernel(h_ref, g_ref, pos_ref, w_ref, b_ref, invf_ref, mc_ref, q_ref, kv_ref, *, q_dim):
    x = h_ref[...]
    xn = _rms(x, g_ref[...]).astype(BF16)
    half_rows = x.shape[0] // 2
    z = jnp.concatenate(
        [jnp.dot(xn[0:half_rows], w_ref[...], preferred_element_type=F32),
         jnp.dot(xn[half_rows:2 * half_rows], w_ref[...], preferred_element_type=F32)],
        axis=0) + b_ref[...]
    ang = pos_ref[...].astype(F32) * invf_ref[...]
    reps = LANES // ang.shape[0]
    cos = jnp.concatenate([jnp.cos(ang)] * reps, axis=0).T
    sin = jnp.concatenate([jnp.sin(ang)] * reps, axis=0).T
    cos = jnp.where(mc_ref[...] != 0.0, cos, 1.0)
    sin = sin * mc_ref[...]

    def rope(t):
        return t * cos + pltpu.roll(t, LANES // 2, 1) * sin

    for j in range(q_dim // LANES):
        cs = slice(j * LANES, (j + 1) * LANES)
        q_ref[:, cs] = (rope(z[:, cs]) * Q_SCALE).astype(BF16)
    kv_ref[:, 0:LANES] = rope(z[:, q_dim:q_dim + LANES]).astype(BF16)
    kv_ref[:, LANES:2 * LANES] = z[:, q_dim + LANES:q_dim + 2 * LANES].astype(BF16)


QK_GROUP = LANES // 4


def _qk_tile_layout():
    lane = np.arange(LANES)
    group, off = lane // QK_GROUP, lane % QK_GROUP
    half = ROPE_DIM // 2
    head = group % 2
    second = group // 2
    rest = QK_GROUP - half
    dim = np.where(off < half, second * half + off, ROPE_DIM + second * rest + (off - half))
    return head, dim


def _qk_column_order(n_cols):
    head, dim = _qk_tile_layout()
    tile = np.arange(n_cols) // LANES
    return tile * LANES + np.tile(head * HEAD_DIM + dim, n_cols // LANES)


def _rope_sign_lanes():
    _, dim = _qk_tile_layout()
    half = ROPE_DIM // 2
    sign = np.where(dim < half, -1.0, np.where(dim < ROPE_DIM, 1.0, 0.0)).astype(np.float32)
    return jnp.asarray(sign[None, :])


def _qkv(h, g, positions, w_qkv, b_qkv, q_dim):
    n, d = h.shape
    qkv_dim = w_qkv.shape[1]
    assert qkv_dim == q_dim + 2 * LANES and QK_GROUP % (ROPE_DIM // 2) == 0
    rows = _row_block(n, PROJ_ROWS)
    inv_freq = ROPE_THETA ** (-jnp.arange(0, ROPE_DIM, 2, dtype=F32) / ROPE_DIM)
    order = np.concatenate([_qk_column_order(q_dim + LANES), np.arange(q_dim + LANES, qkv_dim)])
    w_qkv = w_qkv[:, order]
    b_qkv = b_qkv[order]
    kern = functools.partial(_qkv_kernel, q_dim=q_dim)
    return pl.pallas_call(
        kern,
        grid=(n // rows,),
        in_specs=[
            pl.BlockSpec((rows, d), lambda i: (i, 0)),
            _full((1, d)),
            pl.BlockSpec((1, rows), lambda i: (0, i)),
            _full(w_qkv.shape),
            _full((1, qkv_dim)),
            _full((ROPE_DIM // 2, 1)),
            _full((1, LANES)),
        ],
        out_specs=[
            pl.BlockSpec((rows, q_dim), lambda i: (i, 0)),
            pl.BlockSpec((rows, 2 * LANES), lambda i: (i, 0)),
        ],
        out_shape=[
            jax.ShapeDtypeStruct((n, q_dim), BF16),
            jax.ShapeDtypeStruct((n, 2 * LANES), BF16),
        ],
        compiler_params=_params(("arbitrary",)),
        name="qkv_rope",
    )(h, g.reshape(1, d), positions.reshape(1, n), w_qkv.astype(BF16),
      b_qkv.reshape(1, qkv_dim), inv_freq.reshape(-1, 1), _rope_sign_lanes())


def _attn_kernel(sink_ref, q_ref, kvc_ref, kvp_ref, h_ref, wo_ref, bo_ref, out_ref,
                 kbuf, vbuf, o_buf, *, wpb):
    rows = q_ref.shape[0]
    tiles = q_ref.shape[1] // LANES // 2
    j = pl.program_id(1)
    kbuf[0:WINDOW, :] = kvp_ref[:, 0:LANES]
    kbuf[WINDOW:WINDOW + rows, :] = kvc_ref[:, 0:LANES]
    vbuf[0:WINDOW, :] = kvp_ref[:, LANES:2 * LANES]
    vbuf[WINDOW:WINDOW + rows, :] = kvc_ref[:, LANES:2 * LANES]

    from_prev = (lax.broadcasted_iota(jnp.int32, (WINDOW, WINDOW), 1)
                 > lax.broadcasted_iota(jnp.int32, (WINDOW, WINDOW), 0))
    kv_lane = lax.broadcasted_iota(jnp.int32, (2 * WINDOW, LANES), 1)
    first_half = kv_lane < HEAD_DIM
    head0_lanes = (kv_lane // QK_GROUP) % 2 == 0
    out_first_half = lax.broadcasted_iota(jnp.int32, (WINDOW, LANES), 1) < HEAD_DIM
    ones_lo = jnp.where(first_half, 1.0, 0.0)
    ones_hi = jnp.where(first_half, 0.0, 1.0)
    nt = (((1,), (1,)), ((), ()))

    def window(n, carry):
        r0 = pl.multiple_of(n * WINDOW, WINDOW)
        kt = kbuf[pl.ds(r0, 2 * WINDOW), :].astype(F32)
        vt = vbuf[pl.ds(r0, 2 * WINDOW), :].astype(F32)
        prev_bias = jnp.where(j * wpb + n > 0, 0.0, -jnp.inf)
        k0_lo = jnp.where(head0_lanes, kt, 0.0)
        k1_hi = jnp.where(head0_lanes, 0.0, kt)
        v0_lo = jnp.where(first_half, vt, 0.0)
        v1_hi = jnp.where(first_half, 0.0, vt)
        k_both = (
            jnp.concatenate([k0_lo, pltpu.roll(k0_lo, QK_GROUP, 1)], axis=0).astype(BF16),
            jnp.concatenate([pltpu.roll(k1_hi, LANES - QK_GROUP, 1), k1_hi], axis=0).astype(BF16))
        v_both = (
            jnp.concatenate([jnp.concatenate([v0_lo, ones_lo], axis=1),
                             jnp.concatenate([pltpu.roll(v0_lo, HEAD_DIM, 1), ones_hi], axis=1)],
                            axis=0).astype(BF16),
            jnp.concatenate([jnp.concatenate([pltpu.roll(v1_hi, HEAD_DIM, 1), ones_lo], axis=1),
                             jnp.concatenate([v1_hi, ones_hi], axis=1)], axis=0).astype(BF16))
        for kh in range(2):
            for t0 in range(0, tiles, ATTN_STACK):
                group = [kh * tiles + t0 + u for u in range(ATTN_STACK)]
                q_stack = jnp.concatenate(
                    [q_ref[pl.ds(r0, WINDOW), t * LANES:(t + 1) * LANES] for t in group], axis=0)
                s_all = lax.dot_general(q_stack, k_both[kh], nt, preferred_element_type=F32)
                p_rows, corr = [], []
                for u, t in enumerate(group):
                    p_cols, corr_t = [], []
                    for parity in range(2):
                        sink = sink_ref[t * 2 + parity]
                        c0 = parity * 2 * WINDOW
                        s_prev = s_all[u * WINDOW:(u + 1) * WINDOW, c0:c0 + WINDOW] + prev_bias
                        s_cur = s_all[u * WINDOW:(u + 1) * WINDOW, c0 + WINDOW:c0 + 2 * WINDOW]
                        s = jnp.where(from_prev, s_prev, s_cur)
                        m = jnp.maximum(jnp.max(s, axis=-1, keepdims=True), sink)
                        p = jnp.exp2(s - m)
                        p_cols += [jnp.where(from_prev, p, 0.0), jnp.where(from_prev, 0.0, p)]
                        corr_t.append(jnp.exp2(sink - m))
                    p_rows.append(jnp.concatenate(p_cols, axis=1).astype(BF16))
                    corr.append(corr_t)
                pv = jnp.dot(jnp.concatenate(p_rows, axis=0), v_both[kh],
                             preferred_element_type=F32)
                for u, t in enumerate(group):
                    num = pv[u * WINDOW:(u + 1) * WINDOW, 0:LANES]
                    den = (pv[u * WINDOW:(u + 1) * WINDOW, LANES:2 * LANES]
                           + jnp.where(out_first_half, corr[u][0], corr[u][1]))
                    o_buf[pl.ds(r0, WINDOW), t * LANES:(t + 1) * LANES] = (num / den).astype(BF16)
        return carry

    lax.fori_loop(0, rows // WINDOW, window, 0, unroll=True)
    out_ref[...] = (h_ref[...] + jnp.dot(o_buf[...], wo_ref[...], preferred_element_type=F32)
                    + bo_ref[...])


def _attention(h, q, kv, sinks, w_o, b_o, batch, seq):
    n, q_dim = q.shape
    d = h.shape[1]
    assert kv.shape[1] == 2 * LANES and (q_dim // HEAD_DIM) % 4 == 0
    rows = _row_block(seq, ATTN_ROWS)
    bps = seq // rows
    wpb = rows // WINDOW
    wps = seq // WINDOW

    grid_spec = pltpu.PrefetchScalarGridSpec(
        num_scalar_prefetch=1,
        grid=(batch, bps),
        in_specs=[
            pl.BlockSpec((rows, q_dim), lambda b, j, s: (b * bps + j, 0)),
            pl.BlockSpec((rows, 2 * LANES), lambda b, j, s: (b * bps + j, 0)),
            pl.BlockSpec((WINDOW, 2 * LANES),
                         lambda b, j, s: (b * wps + jnp.maximum(j * wpb - 1, 0), 0)),
            pl.BlockSpec((rows, d), lambda b, j, s: (b * bps + j, 0)),
            pl.BlockSpec((q_dim, d), lambda b, j, s: (0, 0)),
            pl.BlockSpec((1, d), lambda b, j, s: (0, 0)),
        ],
        out_specs=pl.BlockSpec((rows, d), lambda b, j, s: (b * bps + j, 0)),
        scratch_shapes=[pltpu.VMEM((rows + WINDOW, LANES), BF16),
                        pltpu.VMEM((rows + WINDOW, LANES), BF16),
                        pltpu.VMEM((rows, q_dim), BF16)],
    )
    kern = functools.partial(_attn_kernel, wpb=wpb)
    return pl.pallas_call(
        kern,
        grid_spec=grid_spec,
        out_shape=jax.ShapeDtypeStruct((n, d), F32),
        compiler_params=_params(("arbitrary", "arbitrary")),
        name="swa_attention",
    )(sinks.astype(F32) * LOG2_E, q, kv, kv, h, w_o.astype(BF16), b_o.reshape(1, d))


R_IDX0, R_IDX1, R_GATE0, R_GATE1, R_RANK0, R_RANK1 = range(6)


def _router_kernel(h_ref, g_ref, rwt_ref, xpk_ref, route_ref, route_t_ref, cnt_ref, tri_ref,
                   carry_ref):
    rows = h_ref.shape[0]
    i = pl.program_id(0)

    @pl.when(i == 0)
    def _():
        r = lax.broadcasted_iota(jnp.int32, (rows, rows), 0)
        c = lax.broadcasted_iota(jnp.int32, (rows, rows), 1)
        tri_ref[...] = jnp.where(r < c, 1.0, 0.0).astype(BF16)
        carry_ref[...] = jnp.zeros_like(carry_ref)

    xn = _rms(h_ref[...], g_ref[...])
    xb = xn.astype(BF16)
    for p, words in enumerate(_pack_bf16_pairs(xn, xpk_ref.shape[2])):
        xpk_ref[p] = words

    nt = (((1,), (1,)), ((), ()))
    logits = lax.dot_general(rwt_ref[...], xb, nt, preferred_element_type=F32)[0:N_EXPERTS]
    ex = lax.broadcasted_iota(jnp.int32, logits.shape, 0)
    m1 = jnp.max(logits, axis=0, keepdims=True)
    i1 = jnp.min(jnp.where(logits == m1, ex, N_EXPERTS), axis=0, keepdims=True)
    lg2 = jnp.where(ex == i1, -jnp.inf, logits)
    m2 = jnp.max(lg2, axis=0, keepdims=True)
    i2 = jnp.min(jnp.where(lg2 == m2, ex, N_EXPERTS), axis=0, keepdims=True)
    e = jnp.exp(m2 - m1)
    g1 = 1.0 / (1.0 + e)
    g2 = e / (1.0 + e)

    sel_f = jnp.where((ex == i1) | (ex == i2), 1.0, 0.0)
    sel_pad = jnp.concatenate([sel_f, jnp.zeros_like(sel_f)], axis=0).astype(BF16)
    carry = carry_ref[:, 0:1]
    before = jnp.dot(sel_pad, tri_ref[...], preferred_element_type=F32)[0:N_EXPERTS] + carry
    r1 = jnp.sum(jnp.where(ex == i1, before, 0.0), axis=0, keepdims=True)
    r2 = jnp.sum(jnp.where(ex == i2, before, 0.0), axis=0, keepdims=True)
    carry = carry + jnp.sum(sel_f, axis=1, keepdims=True)
    carry_ref[...] = jnp.broadcast_to(carry, carry_ref.shape)
    cnt_ref[...] = jnp.broadcast_to(carry, cnt_ref.shape)

    fields = [None] * 8
    for k, val in ((R_IDX0, i1.astype(F32)), (R_IDX1, i2.astype(F32)), (R_GATE0, g1),
                   (R_GATE1, g2), (R_RANK0, r1), (R_RANK1, r2)):
        fields[k] = val
    route_t = jnp.concatenate([f if f is not None else jnp.zeros_like(g1) for f in fields], axis=0)
    route_t_ref[...] = route_t
    route_ref[...] = jnp.concatenate([route_t] * (LANES // 8), axis=0).T


def _router(h, g, router_w):
    n, d = h.shape
    rows = _row_block(n, ROUTER_ROWS)
    parts = d // 2 // SC_PIECE
    rwt = jnp.zeros((16, d), BF16).at[0:N_EXPERTS, :].set(router_w.T.astype(BF16))
    return pl.pallas_call(
        _router_kernel,
        grid=(n // rows,),
        in_specs=[
            pl.BlockSpec((rows, d), lambda i: (i, 0)),
            _full((1, d)),
            _full((16, d)),
        ],
        out_specs=[
            pl.BlockSpec((parts, rows, SC_PIECE), lambda i: (0, i, 0)),
            pl.BlockSpec((rows, LANES), lambda i: (i, 0)),
            pl.BlockSpec((8, rows), lambda i: (0, i)),
            _full((8, LANES)),
        ],
        out_shape=[
            jax.ShapeDtypeStruct((parts, n, SC_PIECE), jnp.uint32),
            jax.ShapeDtypeStruct((n, LANES), F32),
            jax.ShapeDtypeStruct((8, n), F32),
            jax.ShapeDtypeStruct((8, LANES), F32),
        ],
        scratch_shapes=[pltpu.VMEM((rows, rows), BF16), pltpu.VMEM((8, LANES), F32)],
        compiler_params=_params(("arbitrary",)),
        name="moe_router",
    )(h, g.reshape(1, d), rwt)


def _sc_mesh():
    return plsc.VectorSubcoreMesh(core_axis_name="core", subcore_axis_name="subcore")


def _gather_pieces(src, idx):
    m = idx.shape[0]
    width = src.shape[1]
    assert m % (SC_WINDOW * SC_WORKERS) == 0

    @functools.partial(pl.kernel, out_type=jax.ShapeDtypeStruct((m, width), src.dtype),
                       mesh=_sc_mesh(), scratch_types=[])
    def gather_kernel(src_hbm, idx_hbm, out_hbm):
        def body(idx_vmem, out_vmem):
            pltpu.sync_copy(src_hbm.at[idx_vmem.at[0]], out_vmem)

        pltpu.emit_pipeline(
            body,
            grid=(m // SC_WINDOW,),
            in_specs=[pl.BlockSpec((1, SC_WINDOW), lambda i: (0, i))],
            out_specs=[pl.BlockSpec((SC_WINDOW, width), lambda i: (i, 0))],
            core_axis_name=("core", "subcore"),
            dimension_semantics=(pltpu.PARALLEL,),
        )(idx_hbm, out_hbm)

    return gather_kernel(src, idx.reshape(1, m))


def _scatter_pieces(src, idx, out_rows):
    copies, m = idx.shape
    width = src.shape[1]
    assert m == src.shape[0] and m % (SC_WINDOW * SC_WORKERS) == 0

    @functools.partial(pl.kernel, out_type=jax.ShapeDtypeStruct((out_rows, width), src.dtype),
                       mesh=_sc_mesh(), scratch_types=[])
    def scatter_kernel(src_hbm, *refs):
        idx_hbm, out_hbm = refs[:copies], refs[copies]

        def body(src_vmem, *idx_vmem):
            for iv in idx_vmem:
                pltpu.sync_copy(src_vmem, out_hbm.at[iv.at[0]])

        pltpu.emit_pipeline(
            body,
            grid=(m // SC_WINDOW,),
            in_specs=[pl.BlockSpec((SC_WINDOW, width), lambda i: (i, 0))]
            + [pl.BlockSpec((1, SC_WINDOW), lambda i: (0, i))] * copies,
            out_specs=[],
            core_axis_name=("core", "subcore"),
            dimension_semantics=(pltpu.PARALLEL,),
        )(src_hbm, *idx_hbm)

    return scatter_kernel(src, *[idx[j].reshape(1, m) for j in range(copies)])


def _moe_combined(h_ref, y_ref, route_ref):
    parts, _, _, piece = y_ref.shape
    route = route_ref[...]
    g0 = route[:, R_GATE0:R_GATE0 + 1]
    g1 = route[:, R_GATE1:R_GATE1 + 1]
    lo, hi = [], []
    for p in range(parts):
        lo0, hi0 = _unpack_bf16_pair(y_ref[p, 0])
        lo1, hi1 = _unpack_bf16_pair(y_ref[p, 1])
        lo.append(g0 * lo0 + g1 * lo1)
        hi.append(g0 * hi0 + g1 * hi1)
    return h_ref[...] + jnp.concatenate(lo + hi, axis=1)


def _moe_specs(pending, rows, base):
    y_pairs, route = pending
    parts, _, _, piece = y_pairs.shape
    return [pl.BlockSpec((parts, 2, rows, piece), lambda i: (0, 0, i, 0)),
            pl.BlockSpec((rows, LANES), lambda i: (i + base, 0))]


def _final_kernel(h_ref, y_ref, route_ref, g_ref, o_ref):
    o_ref[...] = _rms(_moe_combined(h_ref, y_ref, route_ref), g_ref[...])


def _final(h, pending, final_g):
    n, d = h.shape
    y_chunks, route = pending
    n_chunks = len(y_chunks)
    rows = _row_block(n // n_chunks, PROJ_ROWS)
    steps = n // rows // n_chunks
    for c, y in enumerate(y_chunks):
        base = c * steps
        h = pl.pallas_call(
            _final_kernel,
            grid=(steps,),
            in_specs=[pl.BlockSpec((rows, d), lambda i, base=base: (i + base, 0))]
            + _moe_specs((y, route), rows, base) + [_full((1, d))],
            out_specs=pl.BlockSpec((rows, d), lambda i, base=base: (i + base, 0)),
            out_shape=jax.ShapeDtypeStruct((n, d), F32),
            input_output_aliases={0: 0} if n_chunks > 1 else {},
            compiler_params=_params(("arbitrary",)),
            name="moe_combine_final_norm",
        )(h, y, route, final_g.reshape(1, d))
    return h


def _moe(h, g, router_w, layer, wg, wu, wd, n_chunks):
    n, d = h.shape
    xpk, route, route_t, cnt = _router(h, g, router_w)
    parts = xpk.shape[0]

    top_idx = route_t[R_IDX0:R_IDX1 + 1].astype(jnp.int32)
    rank = route_t[R_RANK0:R_RANK1 + 1].astype(jnp.int32)
    sizes = cnt[0:N_EXPERTS, 0].astype(jnp.int32)
    padded = ((sizes + MOE_ROWS - 1) // MOE_ROWS) * MOE_ROWS
    pends = jnp.cumsum(padded)
    pstarts = pends - padded
    dest = rank
    for e in range(N_EXPERTS):
        dest = dest + jnp.where(top_idx == e, pstarts[e], 0)
    n_rows = 2 * n + N_EXPERTS * MOE_ROWS
    n_blocks = n_rows // MOE_ROWS
    block_start = jnp.arange(n_blocks, dtype=jnp.int32) * MOE_ROWS
    block_expert = jnp.minimum(
        jnp.sum((block_start[:, None] >= pends[None, :]).astype(jnp.int32), axis=1),
        N_EXPERTS - 1)
    n_used = (pends[N_EXPERTS - 1:] // MOE_ROWS).astype(jnp.int32)
    group_end = jnp.sum(jnp.where(block_expert[:, None] == jnp.arange(N_EXPERTS)[None, :],
                                  (pstarts + sizes)[None, :], 0), axis=1)
    slice_rows = MOE_ROWS // MOE_FILL_STEPS
    block_fill = jnp.clip((group_end - block_start + slice_rows - 1) // slice_rows,
                           1, MOE_FILL_STEPS).astype(jnp.int32)

    off = jnp.arange(parts, dtype=jnp.int32) * n_rows
    scatter_idx = (dest[:, None, :] + off[None, :, None]).reshape(2, parts * n)
    x_sorted = _scatter_pieces(xpk.reshape(parts * n, SC_PIECE), scatter_idx, parts * n_rows)
    y_rows = _moe_ffn(x_sorted.reshape(parts, n_rows, SC_PIECE), block_expert, n_used,
                      block_fill, layer, wg, wu, wd)

    y_flat = y_rows.reshape(parts * n_rows, SC_PIECE)
    nc = n // n_chunks
    chunk_dest = dest.reshape(2, n_chunks, nc).transpose(1, 0, 2)
    gather_idx = (chunk_dest[:, None] + off[None, :, None, None]).reshape(n_chunks, -1)
    y_chunks = [_gather_pieces(y_flat, gather_idx[c]).reshape(parts, 2, nc, SC_PIECE)
                for c in range(n_chunks)]
    return y_chunks, route


def kernel(x, positions, final_norm_g, ev_norm1_g, ev_w_in, ev_conv_w, ev_ln_g, ev_ln_b, ev_spatial_w, ev_spatial_b, ev_w_out, ev_norm2_g, ev_ffn_wg, ev_ffn_wu, ev_ffn_wd, od_norm1_g, od_w_qkv, od_b_qkv, od_sinks, od_w_o, od_b_o, od_norm2_g, od_router_w, od_exp_wg, od_exp_wu, od_exp_wd):
    batch, seq, d = x.shape
    depth = ev_norm1_g.shape[0] + od_norm1_g.shape[0]
    assert depth % 2 == 0, "the final norm is fused into the last (odd) layer's MoE combine"
    n_q_heads = od_sinks.shape[1]
    h = x.reshape(batch * seq, d)
    experts = [od_exp_wg, od_exp_wu, od_exp_wd]
    ffn_steps = (batch * seq) // _row_block(batch * seq, FFN_ROWS)
    ride_along = all(_side_cast_ok(w, ffn_steps) for w in experts)
    if not ride_along:
        experts = [_to_bf16(w) for w in experts]
    pending = None
    pieces_per_token = 2 * (d // 2 // SC_PIECE)
    n_chunks = max(c for c in range(1, COMBINE_CHUNKS + 1)
                   if batch % c == 0
                   and (batch // c * seq * pieces_per_token) % (SC_WINDOW * SC_WORKERS) == 0)
    for layer in range(depth):
        i = layer // 2
        if layer % 2 == 0:
            h = _mixer(h, pending, seq, ev_norm1_g[i], ev_w_in[i], ev_conv_w[i], ev_ln_g[i],
                       ev_ln_b[i], ev_spatial_w[i], ev_spatial_b[i], ev_w_out[i])
            side = experts if (ride_along and layer == 0) else ()
            h, cast = _dense_ffn(h, ev_norm2_g[i], ev_ffn_wg[i], ev_ffn_wu[i], ev_ffn_wd[i], side)
            if side:
                experts = cast
        else:
            q, kv = _qkv(h, od_norm1_g[i], positions, od_w_qkv[i], od_b_qkv[i],
                         n_q_heads * HEAD_DIM)
            h = _attention(h, q, kv, od_sinks[i], od_w_o[i], od_b_o[i], batch, seq)
            pending = _moe(h, od_norm2_g[i], od_router_w[i], i, *experts, n_chunks)
    return _final(h, pending, final_norm_g).reshape(batch, seq, d)
```

```python
import functools

import jax
import jax.numpy as jnp
import numpy as np
from jax import lax
from jax.experimental import pallas as pl
from jax.experimental.pallas import tpu as pltpu
from jax.experimental.pallas import tpu_sc as plsc

F32 = jnp.float32
BF16 = jnp.bfloat16

EPS = 1e-5
CHUNK = 128
GMLP_HEADS = 4
CONV_WIDTH = 3
HEAD_DIM = 64
WINDOW = 128
ROPE_DIM = HEAD_DIM // 4
ROPE_THETA = 500000.0
ATTN_SCALE = HEAD_DIM ** -0.5
LOG2_E = float(np.log2(np.e))
Q_SCALE = ATTN_SCALE * LOG2_E
N_EXPERTS = 8
LANES = 128
VMEM_LIMIT = 56 * 1024 * 1024

MIXER_ROWS = 512
FFN_ROWS = 512
CAST_BLOCK_BYTES = 8 * 1024 * 1024
PROJ_ROWS = 1024
ATTN_ROWS = 1024
ATTN_STACK = 2
ROUTER_ROWS = 1024
MOE_ROWS = 1024
MOE_FILL_STEPS = 2
COMBINE_CHUNKS = 4
MOE_COL_SPLIT = 2
SC_WORKERS = 32
SC_PIECE = 256
SC_WINDOW = 128


def _row_block(n, pref):
    b = min(n, pref)
    while n % b:
        b -= LANES
    return b


def _col_block(f, pref):
    b = min(f, pref)
    b -= b % LANES
    while f % b:
        b -= LANES
    return b


def _params(sem):
    return pltpu.CompilerParams(dimension_semantics=sem, vmem_limit_bytes=VMEM_LIMIT)


def _rms(x, g):
    return x * lax.rsqrt(jnp.mean(x * x, axis=-1, keepdims=True) + EPS) * g


def _gelu(x):
    return 0.5 * x * (1.0 + lax.erf(x * np.float32(np.sqrt(0.5))))


def _full(shape):
    return pl.BlockSpec(shape, lambda *_: (0,) * len(shape))


def _mixer_kernel(*refs, blocks_per_seq, has_pending):
    if has_pending:
        h_ref, y_ref, route_ref = refs[:3]
        refs = refs[3:]
    else:
        h_ref = refs[0]
        refs = refs[1:]
    (g1_ref, win_ref, cw_ref, lng_ref, lnb_ref, ws_ref, bst_ref, wout_ref, o_ref,
     tail_ref, yb_ref) = refs
    rows = h_ref.shape[0]
    cd = cw_ref.shape[1]
    gd = lng_ref.shape[1]
    hd = gd // GMLP_HEADS
    i = pl.program_id(0)

    x = _moe_combined(h_ref, y_ref, route_ref) if has_pending else h_ref[...]
    xn = _rms(x, g1_ref[...]).astype(BF16)
    half_rows = rows // 2

    def by_halves(w):
        return jnp.concatenate([jnp.dot(xn[0:half_rows], w, preferred_element_type=F32),
                                jnp.dot(xn[half_rows:rows], w, preferred_element_type=F32)], axis=0)

    z_g = by_halves(win_ref[:, 3 * cd:3 * cd + 2 * gd])
    b_u = z_g[:, 0:gd]
    b_v = z_g[:, gd:2 * gd]
    z_c = by_halves(win_ref[:, 0:3 * cd])
    a_b = z_c[:, 0:cd]
    a_c = z_c[:, cd:2 * cd]
    a_x = z_c[:, 2 * cd:3 * cd]

    g = a_c * a_x
    tail = jnp.where(i % blocks_per_seq == 0, 0.0, tail_ref[...])
    row = lax.broadcasted_iota(jnp.int32, g.shape, 0)
    gm1 = jnp.where(row == 0, tail[7:8], pltpu.roll(g, 1, 0))
    gm2 = jnp.where(row == 0, tail[6:7], jnp.where(row == 1, tail[7:8], pltpu.roll(g, 2, 0)))
    tail_ref[...] = g[rows - 8:rows]
    cw = cw_ref[...]
    y_a = a_b * (gm2 * cw[0:1] + gm1 * cw[1:2] + g * cw[2:3])

    u = _gelu(b_u)
    v = _gelu(b_v)
    mu = jnp.mean(v, axis=-1, keepdims=True)
    vc = v - mu
    var = jnp.mean(vc * vc, axis=-1, keepdims=True)
    vn = (vc * lax.rsqrt(var + EPS) * lng_ref[...] + lnb_ref[...]).astype(BF16)
    ri = lax.broadcasted_iota(jnp.int32, (CHUNK, CHUNK), 0)
    ci = lax.broadcasted_iota(jnp.int32, (CHUNK, CHUNK), 1)
    causal = ri >= ci
    bst = bst_ref[...]
    for k in range(GMLP_HEADS):
        w_k = jnp.where(causal, ws_ref[k], 0.0).astype(BF16)
        b_k = bst[:, k:k + 1]
        for c in range(rows // CHUNK):
            rs = slice(c * CHUNK, (c + 1) * CHUNK)
            cs = slice(k * hd, (k + 1) * hd)
            mixed = jnp.dot(w_k, vn[rs, cs], preferred_element_type=F32) + b_k
            yb_ref[rs, cs] = (u[rs, cs] * mixed).astype(BF16)

    out = jnp.dot(y_a.astype(BF16), wout_ref[0:cd, :], preferred_element_type=F32)
    out = out + jnp.dot(yb_ref[...], wout_ref[cd:cd + gd, :], preferred_element_type=F32)
    o_ref[...] = x + out


def _mixer(h, pending, seq, *weights):
    if pending is None:
        return _mixer_call(h, None, 0, 1, seq, *weights)
    y_chunks, route = pending
    for c, y in enumerate(y_chunks):
        h = _mixer_call(h, (y, route), c, len(y_chunks), seq, *weights)
    return h


def _mixer_call(h, pending, chunk, n_chunks, seq, g1, w_in, conv_w, ln_g, ln_b, w_s, b_s, w_out):
    n, d = h.shape
    rows = _row_block(seq, MIXER_ROWS)
    cd = conv_w.shape[0]
    gd = ln_g.shape[0]
    has_pending = pending is not None
    steps = n // rows // n_chunks
    base = chunk * steps
    assert (steps * rows) % seq == 0
    kern = functools.partial(_mixer_kernel, blocks_per_seq=seq // rows, has_pending=has_pending)
    return pl.pallas_call(
        kern,
        grid=(steps,),
        in_specs=[pl.BlockSpec((rows, d), lambda i: (i + base, 0))]
        + (_moe_specs(pending, rows, base) if has_pending else [])
        + [
            _full((1, d)),
            _full(w_in.shape),
            _full((CONV_WIDTH, cd)),
            _full((1, gd)),
            _full((1, gd)),
            _full(w_s.shape),
            _full((CHUNK, GMLP_HEADS)),
            _full(w_out.shape),
        ],
        out_specs=pl.BlockSpec((rows, d), lambda i: (i + base, 0)),
        out_shape=jax.ShapeDtypeStruct((n, d), F32),
        scratch_shapes=[pltpu.VMEM((8, cd), F32), pltpu.VMEM((rows, gd), BF16)],
        input_output_aliases={0: 0} if n_chunks > 1 else {},
        compiler_params=_params(("arbitrary",)),
        name="mixer",
    )(h, *(pending or ()), g1.reshape(1, d), w_in.astype(BF16), conv_w.T, ln_g.reshape(1, gd),
      ln_b.reshape(1, gd), w_s, b_s.T, w_out.astype(BF16))


def _swiglu(xn, wg, wu, wd):
    h1 = jnp.dot(xn, wg, preferred_element_type=F32)
    h2 = jnp.dot(xn, wu, preferred_element_type=F32)
    a = (h1 / (1.0 + jnp.exp(-h1)) * h2).astype(BF16)
    return jnp.dot(a, wd, preferred_element_type=F32)


def _dense_ffn_kernel(x_ref, g_ref, wg_ref, wu_ref, wd_ref, *rest):
    n_side = (len(rest) - 1) // 2
    side_in, o_ref, side_out = rest[:n_side], rest[n_side], rest[n_side + 1:]
    x = x_ref[...]
    xn = _rms(x, g_ref[...]).astype(BF16)
    o_ref[...] = x + _swiglu(xn, wg_ref[...], wu_ref[...], wd_ref[...])
    for src, dst in zip(side_in, side_out):
        dst[...] = src[...].astype(dst.dtype)


def _side_cast_ok(w, steps):
    rows = int(np.prod(w.shape[:-1]))
    return rows % steps == 0 and (rows // steps) % 16 == 0 and w.shape[-1] % LANES == 0


def _dense_ffn(h, g, wg, wu, wd, side=()):
    n, d = h.shape
    rows = _row_block(n, FFN_ROWS)
    steps = n // rows
    side2d = [w.reshape(-1, w.shape[-1]) for w in side]
    side_specs = [pl.BlockSpec((w.shape[0] // steps, w.shape[1]), lambda i: (i, 0)) for w in side2d]
    outs = pl.pallas_call(
        _dense_ffn_kernel,
        grid=(steps,),
        in_specs=[
            pl.BlockSpec((rows, d), lambda i: (i, 0)),
            _full((1, d)),
            _full(wg.shape),
            _full(wu.shape),
            _full(wd.shape),
        ] + side_specs,
        out_specs=[pl.BlockSpec((rows, d), lambda i: (i, 0))] + side_specs,
        out_shape=[jax.ShapeDtypeStruct((n, d), F32)]
        + [jax.ShapeDtypeStruct(w.shape, BF16) for w in side2d],
        compiler_params=_params(("arbitrary",)),
        name="dense_ffn",
    )(h, g.reshape(1, d), wg.astype(BF16), wu.astype(BF16), wd.astype(BF16), *side2d)
    return outs[0], [o.reshape(w.shape) for o, w in zip(outs[1:], side)]


def _pack_bf16_pairs(x, piece):
    half = x.shape[1] // 2
    bits = lax.bitcast_convert_type(x.astype(BF16).astype(F32), jnp.uint32)
    return [(bits[:, half + p * piece:half + (p + 1) * piece] & jnp.uint32(0xFFFF0000))
            | (bits[:, p * piece:(p + 1) * piece] >> 16) for p in range(half // piece)]


def _unpack_bf16_pair(packed):
    lo = lax.bitcast_convert_type(packed << 16, F32)
    hi = lax.bitcast_convert_type(packed & jnp.uint32(0xFFFF0000), F32)
    return lo, hi


def _moe_ffn_kernel(be_ref, used_ref, fill_ref, x_ref, wg_ref, wu_ref, wd_ref, o_ref, xn_ref,
                    acc_ref, *, n_steps):
    del be_ref
    i = pl.program_id(0)
    f = pl.program_id(1)
    parts, rows, piece = x_ref.shape
    half = parts * piece
    active = i < used_ref[0]
    filled = fill_ref[i]

    def step(first, last, n):
        if first:
            for p in range(parts):
                lo, hi = _unpack_bf16_pair(x_ref[p, 0:n, :])
                xn_ref[0:n, p * piece:(p + 1) * piece] = lo.astype(BF16)
                xn_ref[0:n, half + p * piece:half + (p + 1) * piece] = hi.astype(BF16)
        part = _swiglu(xn_ref[0:n, :], wg_ref[0, 0], wu_ref[0, 0], wd_ref[0, 0])
        if last:
            val = part if first else acc_ref[0:n, :] + part
            for p, words in enumerate(_pack_bf16_pairs(val, piece)):
                o_ref[p, 0:n, :] = words
                if n < rows:
                    o_ref[p, n:rows, :] = jnp.zeros((rows - n, piece), o_ref.dtype)
        elif first:
            acc_ref[0:n, :] = part
        else:
            acc_ref[0:n, :] += part

    for q in range(1, MOE_FILL_STEPS + 1):
        n = rows * q // MOE_FILL_STEPS
        cond = active & (filled == q)
        if n_steps == 1:
            pl.when(cond)(functools.partial(step, True, True, n))
        else:
            pl.when(cond & (f == 0))(functools.partial(step, True, False, n))
            if n_steps > 2:
                pl.when(cond & (f > 0) & (f < n_steps - 1))(
                    functools.partial(step, False, False, n))
            pl.when(cond & (f == n_steps - 1))(functools.partial(step, False, True, n))

    @pl.when(jnp.logical_not(active) & (f == n_steps - 1))
    def _():
        o_ref[...] = jnp.zeros_like(o_ref)


def _moe_ffn(x_sorted, block_expert, n_used, block_fill, layer, wg, wu, wd):
    parts, n_rows, piece = x_sorted.shape
    d = 2 * parts * piece
    fdim = wg.shape[3]
    rows = MOE_ROWS
    cols = _col_block(fdim, fdim // MOE_COL_SPLIT)
    n_steps = fdim // cols

    def col(i, f, used):
        return jnp.where(i < used[0], f, n_steps - 1)

    grid_spec = pltpu.PrefetchScalarGridSpec(
        num_scalar_prefetch=3,
        grid=(n_rows // rows, n_steps),
        in_specs=[
            pl.BlockSpec((parts, rows, piece),
                         lambda i, f, be, used, fill: (0, jnp.minimum(i, used[0] - 1), 0)),
            pl.BlockSpec((1, 1, d, cols),
                         lambda i, f, be, used, fill: (layer, be[i], 0, col(i, f, used))),
            pl.BlockSpec((1, 1, d, cols),
                         lambda i, f, be, used, fill: (layer, be[i], 0, col(i, f, used))),
            pl.BlockSpec((1, 1, cols, d),
                         lambda i, f, be, used, fill: (layer, be[i], col(i, f, used), 0)),
        ],
        out_specs=pl.BlockSpec((parts, rows, piece), lambda i, f, be, used, fill: (0, i, 0)),
        scratch_shapes=[pltpu.VMEM((rows, d), BF16), pltpu.VMEM((rows, d), F32)],
    )
    kern = functools.partial(_moe_ffn_kernel, n_steps=n_steps)
    return pl.pallas_call(
        kern,
        grid_spec=grid_spec,
        out_shape=jax.ShapeDtypeStruct((parts, n_rows, piece), jnp.uint32),
        compiler_params=_params(("arbitrary", "arbitrary")),
        name="moe_ffn",
    )(block_expert, n_used, block_fill, x_sorted, wg, wu, wd)


def _cast_kernel(x_ref, o_ref):
    o_ref[...] = x_ref[...].astype(o_ref.dtype)


def _to_bf16(w):
    shape = w.shape
    w2 = w.reshape(-1, shape[-1])
    pref = CAST_BLOCK_BYTES // (4 * shape[-1]) // LANES * LANES
    rows = _row_block(w2.shape[0], pref)
    out = pl.pallas_call(
        _cast_kernel,
        grid=(w2.shape[0] // rows,),
        in_specs=[pl.BlockSpec((rows, shape[-1]), lambda i: (i, 0))],
        out_specs=pl.BlockSpec((rows, shape[-1]), lambda i: (i, 0)),
        out_shape=jax.ShapeDtypeStruct(w2.shape, BF16),
        compiler_params=_params(("arbitrary",)),
        name="cast_bf16",
    )(w2)
    return out.reshape(shape)


def _qkv_kernel(h_ref, g_ref, pos_ref, w_ref, b_ref, invf_ref, mc_ref, q_ref, kv_ref, *, q_dim):
    x = h_ref[...]
    xn = _rms(x, g_ref[...]).astype(BF16)
    half_rows = x.shape[0] // 2
    z = jnp.concatenate(
        [jnp.dot(xn[0:half_rows], w_ref[...], preferred_element_type=F32),
         jnp.dot(xn[half_rows:2 * half_rows], w_ref[...], preferred_element_type=F32)],
        axis=0) + b_ref[...]
    ang = pos_ref[...].astype(F32) * invf_ref[...]
    reps = LANES // ang.shape[0]
    cos = jnp.concatenate([jnp.cos(ang)] * reps, axis=0).T
    sin = jnp.concatenate([jnp.sin(ang)] * reps, axis=0).T
    cos = jnp.where(mc_ref[...] != 0.0, cos, 1.0)
    sin = sin * mc_ref[...]

    def rope(t):
        return t * cos + pltpu.roll(t, LANES // 2, 1) * sin

    for j in range(q_dim // LANES):
        cs = slice(j * LANES, (j + 1) * LANES)
        q_ref[:, cs] = (rope(z[:, cs]) * Q_SCALE).astype(BF16)
    kv_ref[:, 0:LANES] = rope(z[:, q_dim:q_dim + LANES]).astype(BF16)
    kv_ref[:, LANES:2 * LANES] = z[:, q_dim + LANES:q_dim + 2 * LANES].astype(BF16)


QK_GROUP = LANES // 4


def _qk_tile_layout():
    lane = np.arange(LANES)
    group, off = lane // QK_GROUP, lane % QK_GROUP
    half = ROPE_DIM // 2
    head = group % 2
    second = group // 2
    rest = QK_GROUP - half
    dim = np.where(off < half, second * half + off, ROPE_DIM + second * rest + (off - half))
    return head, dim


def _qk_column_order(n_cols):
    head, dim = _qk_tile_layout()
    tile = np.arange(n_cols) // LANES
    return tile * LANES + np.tile(head * HEAD_DIM + dim, n_cols // LANES)


def _rope_sign_lanes():
    _, dim = _qk_tile_layout()
    half = ROPE_DIM // 2
    sign = np.where(dim < half, -1.0, np.where(dim < ROPE_DIM, 1.0, 0.0)).astype(np.float32)
    return jnp.asarray(sign[None, :])


def _qkv(h, g, positions, w_qkv, b_qkv, q_dim):
    n, d = h.shape
    qkv_dim = w_qkv.shape[1]
    assert qkv_dim == q_dim + 2 * LANES and QK_GROUP % (ROPE_DIM // 2) == 0
    rows = _row_block(n, PROJ_ROWS)
    inv_freq = ROPE_THETA ** (-jnp.arange(0, ROPE_DIM, 2, dtype=F32) / ROPE_DIM)
    order = np.concatenate([_qk_column_order(q_dim + LANES), np.arange(q_dim + LANES, qkv_dim)])
    w_qkv = w_qkv[:, order]
    b_qkv = b_qkv[order]
    kern = functools.partial(_qkv_kernel, q_dim=q_dim)
    return pl.pallas_call(
        kern,
        grid=(n // rows,),
        in_specs=[
            pl.BlockSpec((rows, d), lambda i: (i, 0)),
            _full((1, d)),
            pl.BlockSpec((1, rows), lambda i: (0, i)),
            _full(w_qkv.shape),
            _full((1, qkv_dim)),
            _full((ROPE_DIM // 2, 1)),
            _full((1, LANES)),
        ],
        out_specs=[
            pl.BlockSpec((rows, q_dim), lambda i: (i, 0)),
            pl.BlockSpec((rows, 2 * LANES), lambda i: (i, 0)),
        ],
        out_shape=[
            jax.ShapeDtypeStruct((n, q_dim), BF16),
            jax.ShapeDtypeStruct((n, 2 * LANES), BF16),
        ],
        compiler_params=_params(("arbitrary",)),
        name="qkv_rope",
    )(h, g.reshape(1, d), positions.reshape(1, n), w_qkv.astype(BF16),
      b_qkv.reshape(1, qkv_dim), inv_freq.reshape(-1, 1), _rope_sign_lanes())


def _attn_kernel(sink_ref, q_ref, kvc_ref, kvp_ref, h_ref, wo_ref, bo_ref, out_ref,
                 kbuf, vbuf, o_buf, *, wpb):
    rows = q_ref.shape[0]
    tiles = q_ref.shape[1] // LANES // 2
    j = pl.program_id(1)
    kbuf[0:WINDOW, :] = kvp_ref[:, 0:LANES]
    kbuf[WINDOW:WINDOW + rows, :] = kvc_ref[:, 0:LANES]
    vbuf[0:WINDOW, :] = kvp_ref[:, LANES:2 * LANES]
    vbuf[WINDOW:WINDOW + rows, :] = kvc_ref[:, LANES:2 * LANES]

    from_prev = (lax.broadcasted_iota(jnp.int32, (WINDOW, WINDOW), 1)
                 > lax.broadcasted_iota(jnp.int32, (WINDOW, WINDOW), 0))
    kv_lane = lax.broadcasted_iota(jnp.int32, (2 * WINDOW, LANES), 1)
    first_half = kv_lane < HEAD_DIM
    head0_lanes = (kv_lane // QK_GROUP) % 2 == 0
    out_first_half = lax.broadcasted_iota(jnp.int32, (WINDOW, LANES), 1) < HEAD_DIM
    ones_lo = jnp.where(first_half, 1.0, 0.0)
    ones_hi = jnp.where(first_half, 0.0, 1.0)
    nt = (((1,), (1,)), ((), ()))

    def window(n, carry):
        r0 = pl.multiple_of(n * WINDOW, WINDOW)
        kt = kbuf[pl.ds(r0, 2 * WINDOW), :].astype(F32)
        vt = vbuf[pl.ds(r0, 2 * WINDOW), :].astype(F32)
        prev_bias = jnp.where(j * wpb + n > 0, 0.0, -jnp.inf)
        k0_lo = jnp.where(head0_lanes, kt, 0.0)
        k1_hi = jnp.where(head0_lanes, 0.0, kt)
        v0_lo = jnp.where(first_half, vt, 0.0)
        v1_hi = jnp.where(first_half, 0.0, vt)
        k_both = (
            jnp.concatenate([k0_lo, pltpu.roll(k0_lo, QK_GROUP, 1)], axis=0).astype(BF16),
            jnp.concatenate([pltpu.roll(k1_hi, LANES - QK_GROUP, 1), k1_hi], axis=0).astype(BF16))
        v_both = (
            jnp.concatenate([jnp.concatenate([v0_lo, ones_lo], axis=1),
                             jnp.concatenate([pltpu.roll(v0_lo, HEAD_DIM, 1), ones_hi], axis=1)],
                            axis=0).astype(BF16),
            jnp.concatenate([jnp.concatenate([pltpu.roll(v1_hi, HEAD_DIM, 1), ones_lo], axis=1),
                             jnp.concatenate([v1_hi, ones_hi], axis=1)], axis=0).astype(BF16))
        for kh in range(2):
            for t0 in range(0, tiles, ATTN_STACK):
                group = [kh * tiles + t0 + u for u in range(ATTN_STACK)]
                q_stack = jnp.concatenate(
                    [q_ref[pl.ds(r0, WINDOW), t * LANES:(t + 1) * LANES] for t in group], axis=0)
                s_all = lax.dot_general(q_stack, k_both[kh], nt, preferred_element_type=F32)
                p_rows, corr = [], []
                for u, t in enumerate(group):
                    p_cols, corr_t = [], []
                    for parity in range(2):
                        sink = sink_ref[t * 2 + parity]
                        c0 = parity * 2 * WINDOW
                        s_prev = s_all[u * WINDOW:(u + 1) * WINDOW, c0:c0 + WINDOW] + prev_bias
                        s_cur = s_all[u * WINDOW:(u + 1) * WINDOW, c0 + WINDOW:c0 + 2 * WINDOW]
                        s = jnp.where(from_prev, s_prev, s_cur)
                        m = jnp.maximum(jnp.max(s, axis=-1, keepdims=True), sink)
                        p = jnp.exp2(s - m)
                        p_cols += [jnp.where(from_prev, p, 0.0), jnp.where(from_prev, 0.0, p)]
                        corr_t.append(jnp.exp2(sink - m))
                    p_rows.append(jnp.concatenate(p_cols, axis=1).astype(BF16))
                    corr.append(corr_t)
                pv = jnp.dot(jnp.concatenate(p_rows, axis=0), v_both[kh],
                             preferred_element_type=F32)
                for u, t in enumerate(group):
                    num = pv[u * WINDOW:(u + 1) * WINDOW, 0:LANES]
                    den = (pv[u * WINDOW:(u + 1) * WINDOW, LANES:2 * LANES]
                           + jnp.where(out_first_half, corr[u][0], corr[u][1]))
                    o_buf[pl.ds(r0, WINDOW), t * LANES:(t + 1) * LANES] = (num / den).astype(BF16)
        return carry

    lax.fori_loop(0, rows // WINDOW, window, 0, unroll=True)
    out_ref[...] = (h_ref[...] + jnp.dot(o_buf[...], wo_ref[...], preferred_element_type=F32)
                    + bo_ref[...])


def _attention(h, q, kv, sinks, w_o, b_o, batch, seq):
    n, q_dim = q.shape
    d = h.shape[1]
    assert kv.shape[1] == 2 * LANES and (q_dim // HEAD_DIM) % 4 == 0
    rows = _row_block(seq, ATTN_ROWS)
    bps = seq // rows
    wpb = rows // WINDOW
    wps = seq // WINDOW

    grid_spec = pltpu.PrefetchScalarGridSpec(
        num_scalar_prefetch=1,
        grid=(batch, bps),
        in_specs=[
            pl.BlockSpec((rows, q_dim), lambda b, j, s: (b * bps + j, 0)),
            pl.BlockSpec((rows, 2 * LANES), lambda b, j, s: (b * bps + j, 0)),
            pl.BlockSpec((WINDOW, 2 * LANES),
                         lambda b, j, s: (b * wps + jnp.maximum(j * wpb - 1, 0), 0)),
            pl.BlockSpec((rows, d), lambda b, j, s: (b * bps + j, 0)),
            pl.BlockSpec((q_dim, d), lambda b, j, s: (0, 0)),
            pl.BlockSpec((1, d), lambda b, j, s: (0, 0)),
        ],
        out_specs=pl.BlockSpec((rows, d), lambda b, j, s: (b * bps + j, 0)),
        scratch_shapes=[pltpu.VMEM((rows + WINDOW, LANES), BF16),
                        pltpu.VMEM((rows + WINDOW, LANES), BF16),
                        pltpu.VMEM((rows, q_dim), BF16)],
    )
    kern = functools.partial(_attn_kernel, wpb=wpb)
    return pl.pallas_call(
        kern,
        grid_spec=grid_spec,
        out_shape=jax.ShapeDtypeStruct((n, d), F32),
        compiler_params=_params(("arbitrary", "arbitrary")),
        name="swa_attention",
    )(sinks.astype(F32) * LOG2_E, q, kv, kv, h, w_o.astype(BF16), b_o.reshape(1, d))


R_IDX0, R_IDX1, R_GATE0, R_GATE1, R_RANK0, R_RANK1 = range(6)


def _router_kernel(h_ref, g_ref, rwt_ref, xpk_ref, route_ref, route_t_ref, cnt_ref, tri_ref,
                   carry_ref):
    rows = h_ref.shape[0]
    i = pl.program_id(0)

    @pl.when(i == 0)
    def _():
        r = lax.broadcasted_iota(jnp.int32, (rows, rows), 0)
        c = lax.broadcasted_iota(jnp.int32, (rows, rows), 1)
        tri_ref[...] = jnp.where(r < c, 1.0, 0.0).astype(BF16)
        carry_ref[...] = jnp.zeros_like(carry_ref)

    xn = _rms(h_ref[...], g_ref[...])
    xb = xn.astype(BF16)
    for p, words in enumerate(_pack_bf16_pairs(xn, xpk_ref.shape[2])):
        xpk_ref[p] = words

    nt = (((1,), (1,)), ((), ()))
    logits = lax.dot_general(rwt_ref[...], xb, nt, preferred_element_type=F32)[0:N_EXPERTS]
    ex = lax.broadcasted_iota(jnp.int32, logits.shape, 0)
    m1 = jnp.max(logits, axis=0, keepdims=True)
    i1 = jnp.min(jnp.where(logits == m1, ex, N_EXPERTS), axis=0, keepdims=True)
    lg2 = jnp.where(ex == i1, -jnp.inf, logits)
    m2 = jnp.max(lg2, axis=0, keepdims=True)
    i2 = jnp.min(jnp.where(lg2 == m2, ex, N_EXPERTS), axis=0, keepdims=True)
    e = jnp.exp(m2 - m1)
    g1 = 1.0 / (1.0 + e)
    g2 = e / (1.0 + e)

    sel_f = jnp.where((ex == i1) | (ex == i2), 1.0, 0.0)
    sel_pad = jnp.concatenate([sel_f, jnp.zeros_like(sel_f)], axis=0).astype(BF16)
    carry = carry_ref[:, 0:1]
    before = jnp.dot(sel_pad, tri_ref[...], preferred_element_type=F32)[0:N_EXPERTS] + carry
    r1 = jnp.sum(jnp.where(ex == i1, before, 0.0), axis=0, keepdims=True)
    r2 = jnp.sum(jnp.where(ex == i2, before, 0.0), axis=0, keepdims=True)
    carry = carry + jnp.sum(sel_f, axis=1, keepdims=True)
    carry_ref[...] = jnp.broadcast_to(carry, carry_ref.shape)
    cnt_ref[...] = jnp.broadcast_to(carry, cnt_ref.shape)

    fields = [None] * 8
    for k, val in ((R_IDX0, i1.astype(F32)), (R_IDX1, i2.astype(F32)), (R_GATE0, g1),
                   (R_GATE1, g2), (R_RANK0, r1), (R_RANK1, r2)):
        fields[k] = val
    route_t = jnp.concatenate([f if f is not None else jnp.zeros_like(g1) for f in fields], axis=0)
    route_t_ref[...] = route_t
    route_ref[...] = jnp.concatenate([route_t] * (LANES // 8), axis=0).T


def _router(h, g, router_w):
    n, d = h.shape
    rows = _row_block(n, ROUTER_ROWS)
    parts = d // 2 // SC_PIECE
    rwt = jnp.zeros((16, d), BF16).at[0:N_EXPERTS, :].set(router_w.T.astype(BF16))
    return pl.pallas_call(
        _router_kernel,
        grid=(n // rows,),
        in_specs=[
            pl.BlockSpec((rows, d), lambda i: (i, 0)),
            _full((1, d)),
            _full((16, d)),
        ],
        out_specs=[
            pl.BlockSpec((parts, rows, SC_PIECE), lambda i: (0, i, 0)),
            pl.BlockSpec((rows, LANES), lambda i: (i, 0)),
            pl.BlockSpec((8, rows), lambda i: (0, i)),
            _full((8, LANES)),
        ],
        out_shape=[
            jax.ShapeDtypeStruct((parts, n, SC_PIECE), jnp.uint32),
            jax.ShapeDtypeStruct((n, LANES), F32),
            jax.ShapeDtypeStruct((8, n), F32),
            jax.ShapeDtypeStruct((8, LANES), F32),
        ],
        scratch_shapes=[pltpu.VMEM((rows, rows), BF16), pltpu.VMEM((8, LANES), F32)],
        compiler_params=_params(("arbitrary",)),
        name="moe_router",
    )(h, g.reshape(1, d), rwt)


def _sc_mesh():
    return plsc.VectorSubcoreMesh(core_axis_name="core", subcore_axis_name="subcore")


def _gather_pieces(src, idx):
    m = idx.shape[0]
    width = src.shape[1]
    assert m % (SC_WINDOW * SC_WORKERS) == 0

    @functools.partial(pl.kernel, out_type=jax.ShapeDtypeStruct((m, width), src.dtype),
                       mesh=_sc_mesh(), scratch_types=[])
    def gather_kernel(src_hbm, idx_hbm, out_hbm):
        def body(idx_vmem, out_vmem):
            pltpu.sync_copy(src_hbm.at[idx_vmem.at[0]], out_vmem)

        pltpu.emit_pipeline(
            body,
            grid=(m // SC_WINDOW,),
            in_specs=[pl.BlockSpec((1, SC_WINDOW), lambda i: (0, i))],
            out_specs=[pl.BlockSpec((SC_WINDOW, width), lambda i: (i, 0))],
            core_axis_name=("core", "subcore"),
            dimension_semantics=(pltpu.PARALLEL,),
        )(idx_hbm, out_hbm)

    return gather_kernel(src, idx.reshape(1, m))


def _scatter_pieces(src, idx, out_rows):
    copies, m = idx.shape
    width = src.shape[1]
    assert m == src.shape[0] and m % (SC_WINDOW * SC_WORKERS) == 0

    @functools.partial(pl.kernel, out_type=jax.ShapeDtypeStruct((out_rows, width), src.dtype),
                       mesh=_sc_mesh(), scratch_types=[])
    def scatter_kernel(src_hbm, *refs):
        idx_hbm, out_hbm = refs[:copies], refs[copies]

        def body(src_vmem, *idx_vmem):
            for iv in idx_vmem:
                pltpu.sync_copy(src_vmem, out_hbm.at[iv.at[0]])

        pltpu.emit_pipeline(
            body,
            grid=(m // SC_WINDOW,),
            in_specs=[pl.BlockSpec((SC_WINDOW, width), lambda i: (i, 0))]
            + [pl.BlockSpec((1, SC_WINDOW), lambda i: (0, i))] * copies,
            out_specs=[],
            core_axis_name=("core", "subcore"),
            dimension_semantics=(pltpu.PARALLEL,),
        )(src_hbm, *idx_hbm)

    return scatter_kernel(src, *[idx[j].reshape(1, m) for j in range(copies)])


def _moe_combined(h_ref, y_ref, route_ref):
    parts, _, _, piece = y_ref.shape
    route = route_ref[...]
    g0 = route[:, R_GATE0:R_GATE0 + 1]
    g1 = route[:, R_GATE1:R_GATE1 + 1]
    lo, hi = [], []
    for p in range(parts):
        lo0, hi0 = _unpack_bf16_pair(y_ref[p, 0])
        lo1, hi1 = _unpack_bf16_pair(y_ref[p, 1])
        lo.append(g0 * lo0 + g1 * lo1)
        hi.append(g0 * hi0 + g1 * hi1)
    return h_ref[...] + jnp.concatenate(lo + hi, axis=1)


def _moe_specs(pending, rows, base):
    y_pairs, route = pending
    parts, _, _, piece = y_pairs.shape
    return [pl.BlockSpec((parts, 2, rows, piece), lambda i: (0, 0, i, 0)),
            pl.BlockSpec((rows, LANES), lambda i: (i + base, 0))]


def _final_kernel(h_ref, y_ref, route_ref, g_ref, o_ref):
    o_ref[...] = _rms(_moe_combined(h_ref, y_ref, route_ref), g_ref[...])


def _final(h, pending, final_g):
    n, d = h.shape
    y_chunks, route = pending
    n_chunks = len(y_chunks)
    rows = _row_block(n // n_chunks, PROJ_ROWS)
    steps = n // rows // n_chunks
    for c, y in enumerate(y_chunks):
        base = c * steps
        h = pl.pallas_call(
            _final_kernel,
            grid=(steps,),
            in_specs=[pl.BlockSpec((rows, d), lambda i, base=base: (i + base, 0))]
            + _moe_specs((y, route), rows, base) + [_full((1, d))],
            out_specs=pl.BlockSpec((rows, d), lambda i, base=base: (i + base, 0)),
            out_shape=jax.ShapeDtypeStruct((n, d), F32),
            input_output_aliases={0: 0} if n_chunks > 1 else {},
            compiler_params=_params(("arbitrary",)),
            name="moe_combine_final_norm",
        )(h, y, route, final_g.reshape(1, d))
    return h


def _moe(h, g, router_w, layer, wg, wu, wd, n_chunks):
    n, d = h.shape
    xpk, route, route_t, cnt = _router(h, g, router_w)
    parts = xpk.shape[0]

    top_idx = route_t[R_IDX0:R_IDX1 + 1].astype(jnp.int32)
    rank = route_t[R_RANK0:R_RANK1 + 1].astype(jnp.int32)
    sizes = cnt[0:N_EXPERTS, 0].astype(jnp.int32)
    padded = ((sizes + MOE_ROWS - 1) // MOE_ROWS) * MOE_ROWS
    pends = jnp.cumsum(padded)
    pstarts = pends - padded
    dest = rank
    for e in range(N_EXPERTS):
        dest = dest + jnp.where(top_idx == e, pstarts[e], 0)
    n_rows = 2 * n + N_EXPERTS * MOE_ROWS
    n_blocks = n_rows // MOE_ROWS
    block_start = jnp.arange(n_blocks, dtype=jnp.int32) * MOE_ROWS
    block_expert = jnp.minimum(
        jnp.sum((block_start[:, None] >= pends[None, :]).astype(jnp.int32), axis=1),
        N_EXPERTS - 1)
    n_used = (pends[N_EXPERTS - 1:] // MOE_ROWS).astype(jnp.int32)
    group_end = jnp.sum(jnp.where(block_expert[:, None] == jnp.arange(N_EXPERTS)[None, :],
                                  (pstarts + sizes)[None, :], 0), axis=1)
    slice_rows = MOE_ROWS // MOE_FILL_STEPS
    block_fill = jnp.clip((group_end - block_start + slice_rows - 1) // slice_rows,
                           1, MOE_FILL_STEPS).astype(jnp.int32)

    off = jnp.arange(parts, dtype=jnp.int32) * n_rows
    scatter_idx = (dest[:, None, :] + off[None, :, None]).reshape(2, parts * n)
    x_sorted = _scatter_pieces(xpk.reshape(parts * n, SC_PIECE), scatter_idx, parts * n_rows)
    y_rows = _moe_ffn(x_sorted.reshape(parts, n_rows, SC_PIECE), block_expert, n_used,
                      block_fill, layer, wg, wu, wd)

    y_flat = y_rows.reshape(parts * n_rows, SC_PIECE)
    nc = n // n_chunks
    chunk_dest = dest.reshape(2, n_chunks, nc).transpose(1, 0, 2)
    gather_idx = (chunk_dest[:, None] + off[None, :, None, None]).reshape(n_chunks, -1)
    y_chunks = [_gather_pieces(y_flat, gather_idx[c]).reshape(parts, 2, nc, SC_PIECE)
                for c in range(n_chunks)]
    return y_chunks, route


def kernel(x, positions, final_norm_g, ev_norm1_g, ev_w_in, ev_conv_w, ev_ln_g, ev_ln_b, ev_spatial_w, ev_spatial_b, ev_w_out, ev_norm2_g, ev_ffn_wg, ev_ffn_wu, ev_ffn_wd, od_norm1_g, od_w_qkv, od_b_qkv, od_sinks, od_w_o, od_b_o, od_norm2_g, od_router_w, od_exp_wg, od_exp_wu, od_exp_wd):
    batch, seq, d = x.shape
    depth = ev_norm1_g.shape[0] + od_norm1_g.shape[0]
    assert depth % 2 == 0, "the final norm is fused into the last (odd) layer's MoE combine"
    n_q_heads = od_sinks.shape[1]
    h = x.reshape(batch * seq, d)
    experts = [od_exp_wg, od_exp_wu, od_exp_wd]
    ffn_steps = (batch * seq) // _row_block(batch * seq, FFN_ROWS)
    ride_along = all(_side_cast_ok(w, ffn_steps) for w in experts)
    if not ride_along:
        experts = [_to_bf16(w) for w in experts]
    pending = None
    pieces_per_token = 2 * (d // 2 // SC_PIECE)
    n_chunks = max(c for c in range(1, COMBINE_CHUNKS + 1)
                   if batch % c == 0
                   and (batch // c * seq * pieces_per_token) % (SC_WINDOW * SC_WORKERS) == 0)
    for layer in range(depth):
        i = layer // 2
        if layer % 2 == 0:
            h = _mixer(h, pending, seq, ev_norm1_g[i], ev_w_in[i], ev_conv_w[i], ev_ln_g[i],
                       ev_ln_b[i], ev_spatial_w[i], ev_spatial_b[i], ev_w_out[i])
            side = experts if (ride_along and layer == 0) else ()
            h, cast = _dense_ffn(h, ev_norm2_g[i], ev_ffn_wg[i], ev_ffn_wu[i], ev_ffn_wd[i], side)
            if side:
                experts = cast
        else:
            q, kv = _qkv(h, od_norm1_g[i], positions, od_w_qkv[i], od_b_qkv[i],
                         n_q_heads * HEAD_DIM)
            h = _attention(h, q, kv, od_sinks[i], od_w_o[i], od_b_o[i], batch, seq)
            pending = _moe(h, od_norm2_g[i], od_router_w[i], i, *experts, n_chunks)
    return _final(h, pending, final_norm_g).reshape(batch, seq, d)
```

```python
import functools

import jax
import jax.numpy as jnp
import numpy as np
from jax import lax
from jax.experimental import pallas as pl
from jax.experimental.pallas import tpu as pltpu
from jax.experimental.pallas import tpu_sc as plsc

F32 = jnp.float32
BF16 = jnp.bfloat16

EPS = 1e-5
CHUNK = 128
GMLP_HEADS = 4
CONV_WIDTH = 3
HEAD_DIM = 64
WINDOW = 128
ROPE_DIM = HEAD_DIM // 4
ROPE_THETA = 500000.0
ATTN_SCALE = HEAD_DIM ** -0.5
LOG2_E = float(np.log2(np.e))
Q_SCALE = ATTN_SCALE * LOG2_E
N_EXPERTS = 8
LANES = 128
VMEM_LIMIT = 56 * 1024 * 1024

MIXER_ROWS = 512
FFN_ROWS = 512
CAST_BLOCK_BYTES = 8 * 1024 * 1024
PROJ_ROWS = 1024
ATTN_ROWS = 1024
ATTN_STACK = 2
ROUTER_ROWS = 1024
MOE_ROWS = 1024
MOE_FILL_STEPS = 2
COMBINE_CHUNKS = 4
MOE_COL_SPLIT = 2
SC_WORKERS = 32
SC_PIECE = 256
SC_WINDOW = 128


def _row_block(n, pref):
    b = min(n, pref)
    while n % b:
        b -= LANES
    return b


def _col_block(f, pref):
    b = min(f, pref)
    b -= b % LANES
    while f % b:
        b -= LANES
    return b


def _params(sem):
    return pltpu.CompilerParams(dimension_semantics=sem, vmem_limit_bytes=VMEM_LIMIT)


def _rms(x, g):
    return x * lax.rsqrt(jnp.mean(x * x, axis=-1, keepdims=True) + EPS) * g


def _gelu(x):
    return 0.5 * x * (1.0 + lax.erf(x * np.float32(np.sqrt(0.5))))


def _full(shape):
    return pl.BlockSpec(shape, lambda *_: (0,) * len(shape))


def _mixer_kernel(*refs, blocks_per_seq, has_pending):
    if has_pending:
        h_ref, y_ref, route_ref = refs[:3]
        refs = refs[3:]
    else:
        h_ref = refs[0]
        refs = refs[1:]
    (g1_ref, win_ref, cw_ref, lng_ref, lnb_ref, ws_ref, bst_ref, wout_ref, o_ref,
     tail_ref, yb_ref) = refs
    rows = h_ref.shape[0]
    cd = cw_ref.shape[1]
    gd = lng_ref.shape[1]
    hd = gd // GMLP_HEADS
    i = pl.program_id(0)

    x = _moe_combined(h_ref, y_ref, route_ref) if has_pending else h_ref[...]
    xn = _rms(x, g1_ref[...]).astype(BF16)
    half_rows = rows // 2

    def by_halves(w):
        return jnp.concatenate([jnp.dot(xn[0:half_rows], w, preferred_element_type=F32),
                                jnp.dot(xn[half_rows:rows], w, preferred_element_type=F32)], axis=0)

    z_g = by_halves(win_ref[:, 3 * cd:3 * cd + 2 * gd])
    b_u = z_g[:, 0:gd]
    b_v = z_g[:, gd:2 * gd]
    z_c = by_halves(win_ref[:, 0:3 * cd])
    a_b = z_c[:, 0:cd]
    a_c = z_c[:, cd:2 * cd]
    a_x = z_c[:, 2 * cd:3 * cd]

    g = a_c * a_x
    tail = jnp.where(i % blocks_per_seq == 0, 0.0, tail_ref[...])
    row = lax.broadcasted_iota(jnp.int32, g.shape, 0)
    gm1 = jnp.where(row == 0, tail[7:8], pltpu.roll(g, 1, 0))
    gm2 = jnp.where(row == 0, tail[6:7], jnp.where(row == 1, tail[7:8], pltpu.roll(g, 2, 0)))
    tail_ref[...] = g[rows - 8:rows]
    cw = cw_ref[...]
    y_a = a_b * (gm2 * cw[0:1] + gm1 * cw[1:2] + g * cw[2:3])

    u = _gelu(b_u)
    v = _gelu(b_v)
    mu = jnp.mean(v, axis=-1, keepdims=True)
    vc = v - mu
    var = jnp.mean(vc * vc, axis=-1, keepdims=True)
    vn = (vc * lax.rsqrt(var + EPS) * lng_ref[...] + lnb_ref[...]).astype(BF16)
    ri = lax.broadcasted_iota(jnp.int32, (CHUNK, CHUNK), 0)
    ci = lax.broadcasted_iota(jnp.int32, (CHUNK, CHUNK), 1)
    causal = ri >= ci
    bst = bst_ref[...]
    for k in range(GMLP_HEADS):
        w_k = jnp.where(causal, ws_ref[k], 0.0).astype(BF16)
        b_k = bst[:, k:k + 1]
        for c in range(rows // CHUNK):
            rs = slice(c * CHUNK, (c + 1) * CHUNK)
            cs = slice(k * hd, (k + 1) * hd)
            mixed = jnp.dot(w_k, vn[rs, cs], preferred_element_type=F32) + b_k
            yb_ref[rs, cs] = (u[rs, cs] * mixed).astype(BF16)

    out = jnp.dot(y_a.astype(BF16), wout_ref[0:cd, :], preferred_element_type=F32)
    out = out + jnp.dot(yb_ref[...], wout_ref[cd:cd + gd, :], preferred_element_type=F32)
    o_ref[...] = x + out


def _mixer(h, pending, seq, *weights):
    if pending is None:
        return _mixer_call(h, None, 0, 1, seq, *weights)
    y_chunks, route = pending
    for c, y in enumerate(y_chunks):
        h = _mixer_call(h, (y, route), c, len(y_chunks), seq, *weights)
    return h


def _mixer_call(h, pending, chunk, n_chunks, seq, g1, w_in, conv_w, ln_g, ln_b, w_s, b_s, w_out):
    n, d = h.shape
    rows = _row_block(seq, MIXER_ROWS)
    cd = conv_w.shape[0]
    gd = ln_g.shape[0]
    has_pending = pending is not None
    steps = n // rows // n_chunks
    base = chunk * steps
    assert (steps * rows) % seq == 0
    kern = functools.partial(_mixer_kernel, blocks_per_seq=seq // rows, has_pending=has_pending)
    return pl.pallas_call(
        kern,
        grid=(steps,),
        in_specs=[pl.BlockSpec((rows, d), lambda i: (i + base, 0))]
        + (_moe_specs(pending, rows, base) if has_pending else [])
        + [
            _full((1, d)),
            _full(w_in.shape),
            _full((CONV_WIDTH, cd)),
            _full((1, gd)),
            _full((1, gd)),
            _full(w_s.shape),
            _full((CHUNK, GMLP_HEADS)),
            _full(w_out.shape),
        ],
        out_specs=pl.BlockSpec((rows, d), lambda i: (i + base, 0)),
        out_shape=jax.ShapeDtypeStruct((n, d), F32),
        scratch_shapes=[pltpu.VMEM((8, cd), F32), pltpu.VMEM((rows, gd), BF16)],
        input_output_aliases={0: 0} if n_chunks > 1 else {},
        compiler_params=_params(("arbitrary",)),
        name="mixer",
    )(h, *(pending or ()), g1.reshape(1, d), w_in.astype(BF16), conv_w.T, ln_g.reshape(1, gd),
      ln_b.reshape(1, gd), w_s, b_s.T, w_out.astype(BF16))


def _swiglu(xn, wg, wu, wd):
    h1 = jnp.dot(xn, wg, preferred_element_type=F32)
    h2 = jnp.dot(xn, wu, preferred_element_type=F32)
    a = (h1 / (1.0 + jnp.exp(-h1)) * h2).astype(BF16)
    return jnp.dot(a, wd, preferred_element_type=F32)


def _dense_ffn_kernel(x_ref, g_ref, wg_ref, wu_ref, wd_ref, *rest):
    n_side = (len(rest) - 1) // 2
    side_in, o_ref, side_out = rest[:n_side], rest[n_side], rest[n_side + 1:]
    half_rows = x_ref.shape[0] // 2
    for r0 in (0, half_rows):
        x = x_ref[r0:r0 + half_rows, :]
        xn = _rms(x, g_ref[...]).astype(BF16)
        o_ref[r0:r0 + half_rows, :] = x + _swiglu(xn, wg_ref[...], wu_ref[...], wd_ref[...])
    for src, dst in zip(side_in, side_out):
        dst[...] = src[...].astype(dst.dtype)


def _side_cast_ok(w, steps):
    rows = int(np.prod(w.shape[:-1]))
    return rows % steps == 0 and (rows // steps) % 16 == 0 and w.shape[-1] % LANES == 0


def _dense_ffn(h, g, wg, wu, wd, side=()):
    n, d = h.shape
    rows = _row_block(n, FFN_ROWS)
    steps = n // rows
    side2d = [w.reshape(-1, w.shape[-1]) for w in side]
    side_specs = [pl.BlockSpec((w.shape[0] // steps, w.shape[1]), lambda i: (i, 0)) for w in side2d]
    outs = pl.pallas_call(
        _dense_ffn_kernel,
        grid=(steps,),
        in_specs=[
            pl.BlockSpec((rows, d), lambda i: (i, 0)),
            _full((1, d)),
            _full(wg.shape),
            _full(wu.shape),
            _full(wd.shape),
        ] + side_specs,
        out_specs=[pl.BlockSpec((rows, d), lambda i: (i, 0))] + side_specs,
        out_shape=[jax.ShapeDtypeStruct((n, d), F32)]
        + [jax.ShapeDtypeStruct(w.shape, BF16) for w in side2d],
        compiler_params=_params(("arbitrary",)),
        name="dense_ffn",
    )(h, g.reshape(1, d), wg.astype(BF16), wu.astype(BF16), wd.astype(BF16), *side2d)
    return outs[0], [o.reshape(w.shape) for o, w in zip(outs[1:], side)]


def _pack_bf16_pairs(x, piece):
    half = x.shape[1] // 2
    bits = lax.bitcast_convert_type(x.astype(BF16).astype(F32), jnp.uint32)
    return [(bits[:, half + p * piece:half + (p + 1) * piece] & jnp.uint32(0xFFFF0000))
            | (bits[:, p * piece:(p + 1) * piece] >> 16) for p in range(half // piece)]


def _unpack_bf16_pair(packed):
    lo = lax.bitcast_convert_type(packed << 16, F32)
    hi = lax.bitcast_convert_type(packed & jnp.uint32(0xFFFF0000), F32)
    return lo, hi


def _moe_ffn_kernel(be_ref, used_ref, fill_ref, x_ref, wg_ref, wu_ref, wd_ref, o_ref, xn_ref,
                    acc_ref, *, n_steps):
    del be_ref
    i = pl.program_id(0)
    f = pl.program_id(1)
    parts, rows, piece = x_ref.shape
    half = parts * piece
    active = i < used_ref[0]
    filled = fill_ref[i]

    def step(first, last, n):
        if first:
            for p in range(parts):
                lo, hi = _unpack_bf16_pair(x_ref[p, 0:n, :])
                xn_ref[0:n, p * piece:(p + 1) * piece] = lo.astype(BF16)
                xn_ref[0:n, half + p * piece:half + (p + 1) * piece] = hi.astype(BF16)
        part = _swiglu(xn_ref[0:n, :], wg_ref[0, 0], wu_ref[0, 0], wd_ref[0, 0])
        if last:
            val = part if first else acc_ref[0:n, :] + part
            for p, words in enumerate(_pack_bf16_pairs(val, piece)):
                o_ref[p, 0:n, :] = words
                if n < rows:
                    o_ref[p, n:rows, :] = jnp.zeros((rows - n, piece), o_ref.dtype)
        elif first:
            acc_ref[0:n, :] = part
        else:
            acc_ref[0:n, :] += part

    for q in range(1, MOE_FILL_STEPS + 1):
        n = rows * q // MOE_FILL_STEPS
        cond = active & (filled == q)
        if n_steps == 1:
            pl.when(cond)(functools.partial(step, True, True, n))
        else:
            pl.when(cond & (f == 0))(functools.partial(step, True, False, n))
            if n_steps > 2:
                pl.when(cond & (f > 0) & (f < n_steps - 1))(
                    functools.partial(step, False, False, n))
            pl.when(cond & (f == n_steps - 1))(functools.partial(step, False, True, n))

    @pl.when(jnp.logical_not(active) & (f == n_steps - 1))
    def _():
        o_ref[...] = jnp.zeros_like(o_ref)


def _moe_ffn(x_sorted, block_expert, n_used, block_fill, layer, wg, wu, wd):
    parts, n_rows, piece = x_sorted.shape
    d = 2 * parts * piece
    fdim = wg.shape[3]
    rows = MOE_ROWS
    cols = _col_block(fdim, fdim // MOE_COL_SPLIT)
    n_steps = fdim // cols

    def col(i, f, used):
        return jnp.where(i < used[0], f, n_steps - 1)

    grid_spec = pltpu.PrefetchScalarGridSpec(
        num_scalar_prefetch=3,
        grid=(n_rows // rows, n_steps),
        in_specs=[
            pl.BlockSpec((parts, rows, piece),
                         lambda i, f, be, used, fill: (0, jnp.minimum(i, used[0] - 1), 0)),
            pl.BlockSpec((1, 1, d, cols),
                         lambda i, f, be, used, fill: (layer, be[i], 0, col(i, f, used))),
            pl.BlockSpec((1, 1, d, cols),
                         lambda i, f, be, used, fill: (layer, be[i], 0, col(i, f, used))),
            pl.BlockSpec((1, 1, cols, d),
                         lambda i, f, be, used, fill: (layer, be[i], col(i, f, used), 0)),
        ],
        out_specs=pl.BlockSpec((parts, rows, piece), lambda i, f, be, used, fill: (0, i, 0)),
        scratch_shapes=[pltpu.VMEM((rows, d), BF16), pltpu.VMEM((rows, d), F32)],
    )
    kern = functools.partial(_moe_ffn_kernel, n_steps=n_steps)
    return pl.pallas_call(
        kern,
        grid_spec=grid_spec,
        out_shape=jax.ShapeDtypeStruct((parts, n_rows, piece), jnp.uint32),
        compiler_params=_params(("arbitrary", "arbitrary")),
        name="moe_ffn",
    )(block_expert, n_used, block_fill, x_sorted, wg, wu, wd)


def _cast_kernel(x_ref, o_ref):
    o_ref[...] = x_ref[...].astype(o_ref.dtype)


def _to_bf16(w):
    shape = w.shape
    w2 = w.reshape(-1, shape[-1])
    pref = CAST_BLOCK_BYTES // (4 * shape[-1]) // LANES * LANES
    rows = _row_block(w2.shape[0], pref)
    out = pl.pallas_call(
        _cast_kernel,
        grid=(w2.shape[0] // rows,),
        in_specs=[pl.BlockSpec((rows, shape[-1]), lambda i: (i, 0))],
        out_specs=pl.BlockSpec((rows, shape[-1]), lambda i: (i, 0)),
        out_shape=jax.ShapeDtypeStruct(w2.shape, BF16),
        compiler_params=_params(("arbitrary",)),
        name="cast_bf16",
    )(w2)
    return out.reshape(shape)


def _qkv_kernel(h_ref, g_ref, pos_ref, w_ref, b_ref, invf_ref, mc_ref, q_ref, kv_ref, *, q_dim):
    x = h_ref[...]
    xn = _rms(x, g_ref[...]).astype(BF16)
    half_rows = x.shape[0] // 2
    z = jnp.concatenate(
        [jnp.dot(xn[0:half_rows], w_ref[...], preferred_element_type=F32),
         jnp.dot(xn[half_rows:2 * half_rows], w_ref[...], preferred_element_type=F32)],
        axis=0) + b_ref[...]
    ang = pos_ref[...].astype(F32) * invf_ref[...]
    reps = LANES // ang.shape[0]
    cos = jnp.concatenate([jnp.cos(ang)] * reps, axis=0).T
    sin = jnp.concatenate([jnp.sin(ang)] * reps, axis=0).T
    cos = jnp.where(mc_ref[...] != 0.0, cos, 1.0)
    sin = sin * mc_ref[...]

    def rope(t):
        return t * cos + pltpu.roll(t, LANES // 2, 1) * sin

    for j in range(q_dim // LANES):
        cs = slice(j * LANES, (j + 1) * LANES)
        q_ref[:, cs] = (rope(z[:, cs]) * Q_SCALE).astype(BF16)
    kv_ref[:, 0:LANES] = rope(z[:, q_dim:q_dim + LANES]).astype(BF16)
    kv_ref[:, LANES:2 * LANES] = z[:, q_dim + LANES:q_dim + 2 * LANES].astype(BF16)


QK_GROUP = LANES // 4


def _qk_tile_layout():
    lane = np.arange(LANES)
    group, off = lane // QK_GROUP, lane % QK_GROUP
    half = ROPE_DIM // 2
    head = group % 2
    second = group // 2
    rest = QK_GROUP - half
    dim = np.where(off < half, second * half + off, ROPE_DIM + second * rest + (off - half))
    return head, dim


def _qk_column_order(n_cols):
    head, dim = _qk_tile_layout()
    tile = np.arange(n_cols) // LANES
    return tile * LANES + np.tile(head * HEAD_DIM + dim, n_cols // LANES)


def _rope_sign_lanes():
    _, dim = _qk_tile_layout()
    half = ROPE_DIM // 2
    sign = np.where(dim < half, -1.0, np.where(dim < ROPE_DIM, 1.0, 0.0)).astype(np.float32)
    return jnp.asarray(sign[None, :])


def _qkv(h, g, positions, w_qkv, b_qkv, q_dim):
    n, d = h.shape
    qkv_dim = w_qkv.shape[1]
    assert qkv_dim == q_dim + 2 * LANES and QK_GROUP % (ROPE_DIM // 2) == 0
    rows = _row_block(n, PROJ_ROWS)
    inv_freq = ROPE_THETA ** (-jnp.arange(0, ROPE_DIM, 2, dtype=F32) / ROPE_DIM)
    order = np.concatenate([_qk_column_order(q_dim + LANES), np.arange(q_dim + LANES, qkv_dim)])
    w_qkv = w_qkv[:, order]
    b_qkv = b_qkv[order]
    kern = functools.partial(_qkv_kernel, q_dim=q_dim)
    return pl.pallas_call(
        kern,
        grid=(n // rows,),
        in_specs=[
            pl.BlockSpec((rows, d), lambda i: (i, 0)),
            _full((1, d)),
            pl.BlockSpec((1, rows), lambda i: (0, i)),
            _full(w_qkv.shape),
            _full((1, qkv_dim)),
            _full((ROPE_DIM // 2, 1)),
            _full((1, LANES)),
        ],
        out_specs=[
            pl.BlockSpec((rows, q_dim), lambda i: (i, 0)),
            pl.BlockSpec((rows, 2 * LANES), lambda i: (i, 0)),
        ],
        out_shape=[
            jax.ShapeDtypeStruct((n, q_dim), BF16),
            jax.ShapeDtypeStruct((n, 2 * LANES), BF16),
        ],
        compiler_params=_params(("arbitrary",)),
        name="qkv_rope",
    )(h, g.reshape(1, d), positions.reshape(1, n), w_qkv.astype(BF16),
      b_qkv.reshape(1, qkv_dim), inv_freq.reshape(-1, 1), _rope_sign_lanes())


def _attn_kernel(sink_ref, q_ref, kvc_ref, kvp_ref, h_ref, wo_ref, bo_ref, out_ref,
                 kbuf, vbuf, o_buf, *, wpb):
    rows = q_ref.shape[0]
    tiles = q_ref.shape[1] // LANES // 2
    j = pl.program_id(1)
    kbuf[0:WINDOW, :] = kvp_ref[:, 0:LANES]
    kbuf[WINDOW:WINDOW + rows, :] = kvc_ref[:, 0:LANES]
    vbuf[0:WINDOW, :] = kvp_ref[:, LANES:2 * LANES]
    vbuf[WINDOW:WINDOW + rows, :] = kvc_ref[:, LANES:2 * LANES]

    from_prev = (lax.broadcasted_iota(jnp.int32, (WINDOW, WINDOW), 1)
                 > lax.broadcasted_iota(jnp.int32, (WINDOW, WINDOW), 0))
    kv_lane = lax.broadcasted_iota(jnp.int32, (2 * WINDOW, LANES), 1)
    first_half = kv_lane < HEAD_DIM
    head0_lanes = (kv_lane // QK_GROUP) % 2 == 0
    out_first_half = lax.broadcasted_iota(jnp.int32, (WINDOW, LANES), 1) < HEAD_DIM
    ones_lo = jnp.where(first_half, 1.0, 0.0)
    ones_hi = jnp.where(first_half, 0.0, 1.0)
    nt = (((1,), (1,)), ((), ()))

    def window(n, carry):
        r0 = pl.multiple_of(n * WINDOW, WINDOW)
        kt = kbuf[pl.ds(r0, 2 * WINDOW), :].astype(F32)
        vt = vbuf[pl.ds(r0, 2 * WINDOW), :].astype(F32)
        prev_bias = jnp.where(j * wpb + n > 0, 0.0, -jnp.inf)
        k0_lo = jnp.where(head0_lanes, kt, 0.0)
        k1_hi = jnp.where(head0_lanes, 0.0, kt)
        v0_lo = jnp.where(first_half, vt, 0.0)
        v1_hi = jnp.where(first_half, 0.0, vt)
        k_both = (
            jnp.concatenate([k0_lo, pltpu.roll(k0_lo, QK_GROUP, 1)], axis=0).astype(BF16),
            jnp.concatenate([pltpu.roll(k1_hi, LANES - QK_GROUP, 1), k1_hi], axis=0).astype(BF16))
        v_both = (
            jnp.concatenate([jnp.concatenate([v0_lo, ones_lo], axis=1),
                             jnp.concatenate([pltpu.roll(v0_lo, HEAD_DIM, 1), ones_hi], axis=1)],
                            axis=0).astype(BF16),
            jnp.concatenate([jnp.concatenate([pltpu.roll(v1_hi, HEAD_DIM, 1), ones_lo], axis=1),
                             jnp.concatenate([v1_hi, ones_hi], axis=1)], axis=0).astype(BF16))
        for kh in range(2):
            for t0 in range(0, tiles, ATTN_STACK):
                group = [kh * tiles + t0 + u for u in range(ATTN_STACK)]
                q_stack = jnp.concatenate(
                    [q_ref[pl.ds(r0, WINDOW), t * LANES:(t + 1) * LANES] for t in group], axis=0)
                s_all = lax.dot_general(q_stack, k_both[kh], nt, preferred_element_type=F32)
                p_rows, corr = [], []
                for u, t in enumerate(group):
                    p_cols, corr_t = [], []
                    for parity in range(2):
                        sink = sink_ref[t * 2 + parity]
                        c0 = parity * 2 * WINDOW
                        s_prev = s_all[u * WINDOW:(u + 1) * WINDOW, c0:c0 + WINDOW] + prev_bias
                        s_cur = s_all[u * WINDOW:(u + 1) * WINDOW, c0 + WINDOW:c0 + 2 * WINDOW]
                        s = jnp.where(from_prev, s_prev, s_cur)
                        m = jnp.maximum(jnp.max(s, axis=-1, keepdims=True), sink)
                        p = jnp.exp2(s - m)
                        p_cols += [jnp.where(from_prev, p, 0.0), jnp.where(from_prev, 0.0, p)]
                        corr_t.append(jnp.exp2(sink - m))
                    p_rows.append(jnp.concatenate(p_cols, axis=1).astype(BF16))
                    corr.append(corr_t)
                pv = jnp.dot(jnp.concatenate(p_rows, axis=0), v_both[kh],
                             preferred_element_type=F32)
                for u, t in enumerate(group):
                    num = pv[u * WINDOW:(u + 1) * WINDOW, 0:LANES]
                    den = (pv[u * WINDOW:(u + 1) * WINDOW, LANES:2 * LANES]
                           + jnp.where(out_first_half, corr[u][0], corr[u][1]))
                    o_buf[pl.ds(r0, WINDOW), t * LANES:(t + 1) * LANES] = (num / den).astype(BF16)
        return carry

    lax.fori_loop(0, rows // WINDOW, window, 0, unroll=True)
    out_ref[...] = (h_ref[...] + jnp.dot(o_buf[...], wo_ref[...], preferred_element_type=F32)
                    + bo_ref[...])


def _attention(h, q, kv, sinks, w_o, b_o, batch, seq):
    n, q_dim = q.shape
    d = h.shape[1]
    assert kv.shape[1] == 2 * LANES and (q_dim // HEAD_DIM) % 4 == 0
    rows = _row_block(seq, ATTN_ROWS)
    bps = seq // rows
    wpb = rows // WINDOW
    wps = seq // WINDOW

    grid_spec = pltpu.PrefetchScalarGridSpec(
        num_scalar_prefetch=1,
        grid=(batch, bps),
        in_specs=[
            pl.BlockSpec((rows, q_dim), lambda b, j, s: (b * bps + j, 0)),
            pl.BlockSpec((rows, 2 * LANES), lambda b, j, s: (b * bps + j, 0)),
            pl.BlockSpec((WINDOW, 2 * LANES),
                         lambda b, j, s: (b * wps + jnp.maximum(j * wpb - 1, 0), 0)),
            pl.BlockSpec((rows, d), lambda b, j, s: (b * bps + j, 0)),
            pl.BlockSpec((q_dim, d), lambda b, j, s: (0, 0)),
            pl.BlockSpec((1, d), lambda b, j, s: (0, 0)),
        ],
        out_specs=pl.BlockSpec((rows, d), lambda b, j, s: (b * bps + j, 0)),
        scratch_shapes=[pltpu.VMEM((rows + WINDOW, LANES), BF16),
                        pltpu.VMEM((rows + WINDOW, LANES), BF16),
                        pltpu.VMEM((rows, q_dim), BF16)],
    )
    kern = functools.partial(_attn_kernel, wpb=wpb)
    return pl.pallas_call(
        kern,
        grid_spec=grid_spec,
        out_shape=jax.ShapeDtypeStruct((n, d), F32),
        compiler_params=_params(("arbitrary", "arbitrary")),
        name="swa_attention",
    )(sinks.astype(F32) * LOG2_E, q, kv, kv, h, w_o.astype(BF16), b_o.reshape(1, d))


R_IDX0, R_IDX1, R_GATE0, R_GATE1, R_RANK0, R_RANK1 = range(6)


def _router_kernel(h_ref, g_ref, rwt_ref, xpk_ref, route_ref, route_t_ref, cnt_ref, tri_ref,
                   carry_ref):
    rows = h_ref.shape[0]
    i = pl.program_id(0)

    @pl.when(i == 0)
    def _():
        r = lax.broadcasted_iota(jnp.int32, (rows, rows), 0)
        c = lax.broadcasted_iota(jnp.int32, (rows, rows), 1)
        tri_ref[...] = jnp.where(r < c, 1.0, 0.0).astype(BF16)
        carry_ref[...] = jnp.zeros_like(carry_ref)

    xn = _rms(h_ref[...], g_ref[...])
    xb = xn.astype(BF16)
    for p, words in enumerate(_pack_bf16_pairs(xn, xpk_ref.shape[2])):
        xpk_ref[p] = words

    nt = (((1,), (1,)), ((), ()))
    logits = lax.dot_general(rwt_ref[...], xb, nt, preferred_element_type=F32)[0:N_EXPERTS]
    ex = lax.broadcasted_iota(jnp.int32, logits.shape, 0)
    m1 = jnp.max(logits, axis=0, keepdims=True)
    i1 = jnp.min(jnp.where(logits == m1, ex, N_EXPERTS), axis=0, keepdims=True)
    lg2 = jnp.where(ex == i1, -jnp.inf, logits)
    m2 = jnp.max(lg2, axis=0, keepdims=True)
    i2 = jnp.min(jnp.where(lg2 == m2, ex, N_EXPERTS), axis=0, keepdims=True)
    e = jnp.exp(m2 - m1)
    g1 = 1.0 / (1.0 + e)
    g2 = e / (1.0 + e)

    sel_f = jnp.where((ex == i1) | (ex == i2), 1.0, 0.0)
    sel_pad = jnp.concatenate([sel_f, jnp.zeros_like(sel_f)], axis=0).astype(BF16)
    carry = carry_ref[:, 0:1]
    before = jnp.dot(sel_pad, tri_ref[...], preferred_element_type=F32)[0:N_EXPERTS] + carry
    r1 = jnp.sum(jnp.where(ex == i1, before, 0.0), axis=0, keepdims=True)
    r2 = jnp.sum(jnp.where(ex == i2, before, 0.0), axis=0, keepdims=True)
    carry = carry + jnp.sum(sel_f, axis=1, keepdims=True)
    carry_ref[...] = jnp.broadcast_to(carry, carry_ref.shape)
    cnt_ref[...] = jnp.broadcast_to(carry, cnt_ref.shape)

    fields = [None] * 8
    for k, val in ((R_IDX0, i1.astype(F32)), (R_IDX1, i2.astype(F32)), (R_GATE0, g1),
                   (R_GATE1, g2), (R_RANK0, r1), (R_RANK1, r2)):
        fields[k] = val
    route_t = jnp.concatenate([f if f is not None else jnp.zeros_like(g1) for f in fields], axis=0)
    route_t_ref[...] = route_t
    route_ref[...] = jnp.concatenate([route_t] * (LANES // 8), axis=0).T


def _router(h, g, router_w):
    n, d = h.shape
    rows = _row_block(n, ROUTER_ROWS)
    parts = d // 2 // SC_PIECE
    rwt = jnp.zeros((16, d), BF16).at[0:N_EXPERTS, :].set(router_w.T.astype(BF16))
    return pl.pallas_call(
        _router_kernel,
        grid=(n // rows,),
        in_specs=[
            pl.BlockSpec((rows, d), lambda i: (i, 0)),
            _full((1, d)),
            _full((16, d)),
        ],
        out_specs=[
            pl.BlockSpec((parts, rows, SC_PIECE), lambda i: (0, i, 0)),
            pl.BlockSpec((rows, LANES), lambda i: (i, 0)),
            pl.BlockSpec((8, rows), lambda i: (0, i)),
            _full((8, LANES)),
        ],
        out_shape=[
            jax.ShapeDtypeStruct((parts, n, SC_PIECE), jnp.uint32),
            jax.ShapeDtypeStruct((n, LANES), F32),
            jax.ShapeDtypeStruct((8, n), F32),
            jax.ShapeDtypeStruct((8, LANES), F32),
        ],
        scratch_shapes=[pltpu.VMEM((rows, rows), BF16), pltpu.VMEM((8, LANES), F32)],
        compiler_params=_params(("arbitrary",)),
        name="moe_router",
    )(h, g.reshape(1, d), rwt)


def _sc_mesh():
    return plsc.VectorSubcoreMesh(core_axis_name="core", subcore_axis_name="subcore")


def _gather_pieces(src, idx):
    m = idx.shape[0]
    width = src.shape[1]
    assert m % (SC_WINDOW * SC_WORKERS) == 0

    @functools.partial(pl.kernel, out_type=jax.ShapeDtypeStruct((m, width), src.dtype),
                       mesh=_sc_mesh(), scratch_types=[])
    def gather_kernel(src_hbm, idx_hbm, out_hbm):
        def body(idx_vmem, out_vmem):
            pltpu.sync_copy(src_hbm.at[idx_vmem.at[0]], out_vmem)

        pltpu.emit_pipeline(
            body,
            grid=(m // SC_WINDOW,),
            in_specs=[pl.BlockSpec((1, SC_WINDOW), lambda i: (0, i))],
            out_specs=[pl.BlockSpec((SC_WINDOW, width), lambda i: (i, 0))],
            core_axis_name=("core", "subcore"),
            dimension_semantics=(pltpu.PARALLEL,),
        )(idx_hbm, out_hbm)

    return gather_kernel(src, idx.reshape(1, m))


def _scatter_pieces(src, idx, out_rows):
    copies, m = idx.shape
    width = src.shape[1]
    assert m == src.shape[0] and m % (SC_WINDOW * SC_WORKERS) == 0

    @functools.partial(pl.kernel, out_type=jax.ShapeDtypeStruct((out_rows, width), src.dtype),
                       mesh=_sc_mesh(), scratch_types=[])
    def scatter_kernel(src_hbm, *refs):
        idx_hbm, out_hbm = refs[:copies], refs[copies]

        def body(src_vmem, *idx_vmem):
            for iv in idx_vmem:
                pltpu.sync_copy(src_vmem, out_hbm.at[iv.at[0]])

        pltpu.emit_pipeline(
            body,
            grid=(m // SC_WINDOW,),
            in_specs=[pl.BlockSpec((SC_WINDOW, width), lambda i: (i, 0))]
            + [pl.BlockSpec((1, SC_WINDOW), lambda i: (0, i))] * copies,
            out_specs=[],
            core_axis_name=("core", "subcore"),
            dimension_semantics=(pltpu.PARALLEL,),
        )(src_hbm, *idx_hbm)

    return scatter_kernel(src, *[idx[j].reshape(1, m) for j in range(copies)])


def _moe_combined(h_ref, y_ref, route_ref):
    parts, _, _, piece = y_ref.shape
    route = route_ref[...]
    g0 = route[:, R_GATE0:R_GATE0 + 1]
    g1 = route[:, R_GATE1:R_GATE1 + 1]
    lo, hi = [], []
    for p in range(parts):
        lo0, hi0 = _unpack_bf16_pair(y_ref[p, 0])
        lo1, hi1 = _unpack_bf16_pair(y_ref[p, 1])
        lo.append(g0 * lo0 + g1 * lo1)
        hi.append(g0 * hi0 + g1 * hi1)
    return h_ref[...] + jnp.concatenate(lo + hi, axis=1)


def _moe_specs(pending, rows, base):
    y_pairs, route = pending
    parts, _, _, piece = y_pairs.shape
    return [pl.BlockSpec((parts, 2, rows, piece), lambda i: (0, 0, i, 0)),
            pl.BlockSpec((rows, LANES), lambda i: (i + base, 0))]


def _final_kernel(h_ref, y_ref, route_ref, g_ref, o_ref):
    o_ref[...] = _rms(_moe_combined(h_ref, y_ref, route_ref), g_ref[...])


def _final(h, pending, final_g):
    n, d = h.shape
    y_chunks, route = pending
    n_chunks = len(y_chunks)
    rows = _row_block(n // n_chunks, PROJ_ROWS)
    steps = n // rows // n_chunks
    for c, y in enumerate(y_chunks):
        base = c * steps
        h = pl.pallas_call(
            _final_kernel,
            grid=(steps,),
            in_specs=[pl.BlockSpec((rows, d), lambda i, base=base: (i + base, 0))]
            + _moe_specs((y, route), rows, base) + [_full((1, d))],
            out_specs=pl.BlockSpec((rows, d), lambda i, base=base: (i + base, 0)),
            out_shape=jax.ShapeDtypeStruct((n, d), F32),
            input_output_aliases={0: 0} if n_chunks > 1 else {},
            compiler_params=_params(("arbitrary",)),
            name="moe_combine_final_norm",
        )(h, y, route, final_g.reshape(1, d))
    return h


def _moe(h, g, router_w, layer, wg, wu, wd, n_chunks):
    n, d = h.shape
    xpk, route, route_t, cnt = _router(h, g, router_w)
    parts = xpk.shape[0]

    top_idx = route_t[R_IDX0:R_IDX1 + 1].astype(jnp.int32)
    rank = route_t[R_RANK0:R_RANK1 + 1].astype(jnp.int32)
    sizes = cnt[0:N_EXPERTS, 0].astype(jnp.int32)
    padded = ((sizes + MOE_ROWS - 1) // MOE_ROWS) * MOE_ROWS
    pends = jnp.cumsum(padded)
    pstarts = pends - padded
    dest = rank
    for e in range(N_EXPERTS):
        dest = dest + jnp.where(top_idx == e, pstarts[e], 0)
    n_rows = 2 * n + N_EXPERTS * MOE_ROWS
    n_blocks = n_rows // MOE_ROWS
    block_start = jnp.arange(n_blocks, dtype=jnp.int32) * MOE_ROWS
    block_expert = jnp.minimum(
        jnp.sum((block_start[:, None] >= pends[None, :]).astype(jnp.int32), axis=1),
        N_EXPERTS - 1)
    n_used = (pends[N_EXPERTS - 1:] // MOE_ROWS).astype(jnp.int32)
    group_end = jnp.sum(jnp.where(block_expert[:, None] == jnp.arange(N_EXPERTS)[None, :],
                                  (pstarts + sizes)[None, :], 0), axis=1)
    slice_rows = MOE_ROWS // MOE_FILL_STEPS
    block_fill = jnp.clip((group_end - block_start + slice_rows - 1) // slice_rows,
                           1, MOE_FILL_STEPS).astype(jnp.int32)

    off = jnp.arange(parts, dtype=jnp.int32) * n_rows
    scatter_idx = (dest[:, None, :] + off[None, :, None]).reshape(2, parts * n)
    x_sorted = _scatter_pieces(xpk.reshape(parts * n, SC_PIECE), scatter_idx, parts * n_rows)
    y_rows = _moe_ffn(x_sorted.reshape(parts, n_rows, SC_PIECE), block_expert, n_used,
                      block_fill, layer, wg, wu, wd)

    y_flat = y_rows.reshape(parts * n_rows, SC_PIECE)
    nc = n // n_chunks
    chunk_dest = dest.reshape(2, n_chunks, nc).transpose(1, 0, 2)
    gather_idx = (chunk_dest[:, None] + off[None, :, None, None]).reshape(n_chunks, -1)
    y_chunks = [_gather_pieces(y_flat, gather_idx[c]).reshape(parts, 2, nc, SC_PIECE)
                for c in range(n_chunks)]
    return y_chunks, route


def kernel(x, positions, final_norm_g, ev_norm1_g, ev_w_in, ev_conv_w, ev_ln_g, ev_ln_b, ev_spatial_w, ev_spatial_b, ev_w_out, ev_norm2_g, ev_ffn_wg, ev_ffn_wu, ev_ffn_wd, od_norm1_g, od_w_qkv, od_b_qkv, od_sinks, od_w_o, od_b_o, od_norm2_g, od_router_w, od_exp_wg, od_exp_wu, od_exp_wd):
    batch, seq, d = x.shape
    depth = ev_norm1_g.shape[0] + od_norm1_g.shape[0]
    assert depth % 2 == 0, "the final norm is fused into the last (odd) layer's MoE combine"
    n_q_heads = od_sinks.shape[1]
    h = x.reshape(batch * seq, d)
    experts = [od_exp_wg, od_exp_wu, od_exp_wd]
    ffn_steps = (batch * seq) // _row_block(batch * seq, FFN_ROWS)
    ride_along = all(_side_cast_ok(w, ffn_steps) for w in experts)
    if not ride_along:
        experts = [_to_bf16(w) for w in experts]
    pending = None
    pieces_per_token = 2 * (d // 2 // SC_PIECE)
    n_chunks = max(c for c in range(1, COMBINE_CHUNKS + 1)
                   if batch % c == 0
                   and (batch // c * seq * pieces_per_token) % (SC_WINDOW * SC_WORKERS) == 0)
    for layer in range(depth):
        i = layer // 2
        if layer % 2 == 0:
            h = _mixer(h, pending, seq, ev_norm1_g[i], ev_w_in[i], ev_conv_w[i], ev_ln_g[i],
                       ev_ln_b[i], ev_spatial_w[i], ev_spatial_b[i], ev_w_out[i])
            side = experts if (ride_along and layer == 0) else ()
            h, cast = _dense_ffn(h, ev_norm2_g[i], ev_ffn_wg[i], ev_ffn_wu[i], ev_ffn_wd[i], side)
            if side:
                experts = cast
        else:
            q, kv = _qkv(h, od_norm1_g[i], positions, od_w_qkv[i], od_b_qkv[i],
                         n_q_heads * HEAD_DIM)
            h = _attention(h, q, kv, od_sinks[i], od_w_o[i], od_b_o[i], batch, seq)
            pending = _moe(h, od_norm2_g[i], od_router_w[i], i, *experts, n_chunks)
    return _final(h, pending, final_norm_g).reshape(batch, seq, d)
```

```python
import functools

import jax
import jax.numpy as jnp
import numpy as np
from jax import lax
from jax.experimental import pallas as pl
from jax.experimental.pallas import tpu as pltpu
from jax.experimental.pallas import tpu_sc as plsc

F32 = jnp.float32
BF16 = jnp.bfloat16

EPS = 1e-5
CHUNK = 128
GMLP_HEADS = 4
CONV_WIDTH = 3
HEAD_DIM = 64
WINDOW = 128
ROPE_DIM = HEAD_DIM // 4
ROPE_THETA = 500000.0
ATTN_SCALE = HEAD_DIM ** -0.5
LOG2_E = float(np.log2(np.e))
Q_SCALE = ATTN_SCALE * LOG2_E
N_EXPERTS = 8
LANES = 128
VMEM_LIMIT = 56 * 1024 * 1024

MIXER_ROWS = 512
FFN_ROWS = 512
CAST_BLOCK_BYTES = 8 * 1024 * 1024
PROJ_ROWS = 1024
ATTN_ROWS = 1024
ATTN_STACK = 2
ROUTER_ROWS = 1024
MOE_ROWS = 1024
MOE_FILL_STEPS = 2
COMBINE_CHUNKS = 4
MOE_COL_SPLIT = 2
SC_WORKERS = 32
SC_PIECE = 256
SC_WINDOW = 128


def _row_block(n, pref):
    b = min(n, pref)
    while n % b:
        b -= LANES
    return b


def _col_block(f, pref):
    b = min(f, pref)
    b -= b % LANES
    while f % b:
        b -= LANES
    return b


def _params(sem):
    return pltpu.CompilerParams(dimension_semantics=sem, vmem_limit_bytes=VMEM_LIMIT)


def _rms(x, g):
    return x * lax.rsqrt(jnp.mean(x * x, axis=-1, keepdims=True) + EPS) * g


def _gelu(x):
    return 0.5 * x * (1.0 + lax.erf(x * np.float32(np.sqrt(0.5))))


def _full(shape):
    return pl.BlockSpec(shape, lambda *_: (0,) * len(shape))


def _mixer_kernel(*refs, blocks_per_seq, has_pending):
    if has_pending:
        h_ref, y_ref, route_ref = refs[:3]
        refs = refs[3:]
    else:
        h_ref = refs[0]
        refs = refs[1:]
    (g1_ref, win_ref, cw_ref, lng_ref, lnb_ref, ws_ref, bst_ref, wout_ref, o_ref,
     tail_ref, yb_ref) = refs
    rows = h_ref.shape[0]
    cd = cw_ref.shape[1]
    gd = lng_ref.shape[1]
    hd = gd // GMLP_HEADS
    i = pl.program_id(0)

    x = _moe_combined(h_ref, y_ref, route_ref) if has_pending else h_ref[...]
    xn = _rms(x, g1_ref[...]).astype(BF16)
    half_rows = rows // 2

    def by_halves(w):
        return jnp.concatenate([jnp.dot(xn[0:half_rows], w, preferred_element_type=F32),
                                jnp.dot(xn[half_rows:rows], w, preferred_element_type=F32)], axis=0)

    z_g = by_halves(win_ref[:, 3 * cd:3 * cd + 2 * gd])
    b_u = z_g[:, 0:gd]
    b_v = z_g[:, gd:2 * gd]
    z_c = by_halves(win_ref[:, 0:3 * cd])
    a_b = z_c[:, 0:cd]
    a_c = z_c[:, cd:2 * cd]
    a_x = z_c[:, 2 * cd:3 * cd]

    g = a_c * a_x
    tail = jnp.where(i % blocks_per_seq == 0, 0.0, tail_ref[...])
    row = lax.broadcasted_iota(jnp.int32, g.shape, 0)
    gm1 = jnp.where(row == 0, tail[7:8], pltpu.roll(g, 1, 0))
    gm2 = jnp.where(row == 0, tail[6:7], jnp.where(row == 1, tail[7:8], pltpu.roll(g, 2, 0)))
    tail_ref[...] = g[rows - 8:rows]
    cw = cw_ref[...]
    y_a = a_b * (gm2 * cw[0:1] + gm1 * cw[1:2] + g * cw[2:3])

    u = _gelu(b_u)
    v = _gelu(b_v)
    mu = jnp.mean(v, axis=-1, keepdims=True)
    vc = v - mu
    var = jnp.mean(vc * vc, axis=-1, keepdims=True)
    vn = (vc * lax.rsqrt(var + EPS) * lng_ref[...] + lnb_ref[...]).astype(BF16)
    ri = lax.broadcasted_iota(jnp.int32, (CHUNK, CHUNK), 0)
    ci = lax.broadcasted_iota(jnp.int32, (CHUNK, CHUNK), 1)
    causal = ri >= ci
    bst = bst_ref[...]
    for k in range(GMLP_HEADS):
        w_k = jnp.where(causal, ws_ref[k], 0.0).astype(BF16)
        b_k = bst[:, k:k + 1]
        for c in range(rows // CHUNK):
            rs = slice(c * CHUNK, (c + 1) * CHUNK)
            cs = slice(k * hd, (k + 1) * hd)
            mixed = jnp.dot(w_k, vn[rs, cs], preferred_element_type=F32) + b_k
            yb_ref[rs, cs] = (u[rs, cs] * mixed).astype(BF16)

    out = jnp.dot(y_a.astype(BF16), wout_ref[0:cd, :], preferred_element_type=F32)
    out = out + jnp.dot(yb_ref[...], wout_ref[cd:cd + gd, :], preferred_element_type=F32)
    o_ref[...] = x + out


def _mixer(h, pending, seq, *weights):
    if pending is None:
        return _mixer_call(h, None, 0, 1, seq, *weights)
    y_chunks, route = pending
    for c, y in enumerate(y_chunks):
        h = _mixer_call(h, (y, route), c, len(y_chunks), seq, *weights)
    return h


def _mixer_call(h, pending, chunk, n_chunks, seq, g1, w_in, conv_w, ln_g, ln_b, w_s, b_s, w_out):
    n, d = h.shape
    rows = _row_block(seq, MIXER_ROWS)
    cd = conv_w.shape[0]
    gd = ln_g.shape[0]
    has_pending = pending is not None
    steps = n // rows // n_chunks
    base = chunk * steps
    assert (steps * rows) % seq == 0
    kern = functools.partial(_mixer_kernel, blocks_per_seq=seq // rows, has_pending=has_pending)
    return pl.pallas_call(
        kern,
        grid=(steps,),
        in_specs=[pl.BlockSpec((rows, d), lambda i: (i + base, 0))]
        + (_moe_specs(pending, rows, base) if has_pending else [])
        + [
            _full((1, d)),
            _full(w_in.shape),
            _full((CONV_WIDTH, cd)),
            _full((1, gd)),
            _full((1, gd)),
            _full(w_s.shape),
            _full((CHUNK, GMLP_HEADS)),
            _full(w_out.shape),
        ],
        out_specs=pl.BlockSpec((rows, d), lambda i: (i + base, 0)),
        out_shape=jax.ShapeDtypeStruct((n, d), F32),
        scratch_shapes=[pltpu.VMEM((8, cd), F32), pltpu.VMEM((rows, gd), BF16)],
        input_output_aliases={0: 0} if n_chunks > 1 else {},
        compiler_params=_params(("arbitrary",)),
        name="mixer",
    )(h, *(pending or ()), g1.reshape(1, d), w_in.astype(BF16), conv_w.T, ln_g.reshape(1, gd),
      ln_b.reshape(1, gd), w_s, b_s.T, w_out.astype(BF16))


def _swiglu(xn, wg, wu, wd):
    h1 = jnp.dot(xn, wg, preferred_element_type=F32)
    h2 = jnp.dot(xn, wu, preferred_element_type=F32)
    a = (h1 / (1.0 + jnp.exp(-h1)) * h2).astype(BF16)
    return jnp.dot(a, wd, preferred_element_type=F32)


def _dense_ffn_kernel(x_ref, g_ref, wg_ref, wu_ref, wd_ref, *rest):
    n_side = (len(rest) - 1) // 2
    side_in, o_ref, side_out = rest[:n_side], rest[n_side], rest[n_side + 1:]
    half_rows = x_ref.shape[0] // 2
    for r0 in (0, half_rows):
        x = x_ref[r0:r0 + half_rows, :]
        xn = _rms(x, g_ref[...]).astype(BF16)
        o_ref[r0:r0 + half_rows, :] = x + _swiglu(xn, wg_ref[...], wu_ref[...], wd_ref[...])
    for src, dst in zip(side_in, side_out):
        dst[...] = src[...].astype(dst.dtype)


def _side_cast_ok(w, steps):
    rows = int(np.prod(w.shape[:-1]))
    return rows % steps == 0 and (rows // steps) % 16 == 0 and w.shape[-1] % LANES == 0


def _dense_ffn(h, g, wg, wu, wd, side=()):
    n, d = h.shape
    rows = _row_block(n, FFN_ROWS)
    steps = n // rows
    side2d = [w.reshape(-1, w.shape[-1]) for w in side]
    side_specs = [pl.BlockSpec((w.shape[0] // steps, w.shape[1]), lambda i: (i, 0)) for w in side2d]
    outs = pl.pallas_call(
        _dense_ffn_kernel,
        grid=(steps,),
        in_specs=[
            pl.BlockSpec((rows, d), lambda i: (i, 0)),
            _full((1, d)),
            _full(wg.shape),
            _full(wu.shape),
            _full(wd.shape),
        ] + side_specs,
        out_specs=[pl.BlockSpec((rows, d), lambda i: (i, 0))] + side_specs,
        out_shape=[jax.ShapeDtypeStruct((n, d), F32)]
        + [jax.ShapeDtypeStruct(w.shape, BF16) for w in side2d],
        compiler_params=_params(("arbitrary",)),
        name="dense_ffn",
    )(h, g.reshape(1, d), wg.astype(BF16), wu.astype(BF16), wd.astype(BF16), *side2d)
    return outs[0], [o.reshape(w.shape) for o, w in zip(outs[1:], side)]


def _pack_bf16_pairs(x, piece):
    half = x.shape[1] // 2
    bits = lax.bitcast_convert_type(x.astype(BF16).astype(F32), jnp.uint32)
    return [(bits[:, half + p * piece:half + (p + 1) * piece] & jnp.uint32(0xFFFF0000))
            | (bits[:, p * piece:(p + 1) * piece] >> 16) for p in range(half // piece)]


def _unpack_bf16_pair(packed):
    lo = lax.bitcast_convert_type(packed << 16, F32)
    hi = lax.bitcast_convert_type(packed & jnp.uint32(0xFFFF0000), F32)
    return lo, hi


def _moe_ffn_kernel(be_ref, used_ref, fill_ref, x_ref, wg_ref, wu_ref, wd_ref, o_ref, xn_ref,
                    acc_ref, *, n_steps):
    del be_ref
    i = pl.program_id(0)
    f = pl.program_id(1)
    parts, rows, piece = x_ref.shape
    half = parts * piece
    active = i < used_ref[0]
    filled = fill_ref[i]

    def step(first, last, n):
        if first:
            for p in range(parts):
                lo, hi = _unpack_bf16_pair(x_ref[p, 0:n, :])
                xn_ref[0:n, p * piece:(p + 1) * piece] = lo.astype(BF16)
                xn_ref[0:n, half + p * piece:half + (p + 1) * piece] = hi.astype(BF16)
        part = _swiglu(xn_ref[0:n, :], wg_ref[0, 0], wu_ref[0, 0], wd_ref[0, 0])
        if last:
            val = part if first else acc_ref[0:n, :] + part
            for p, words in enumerate(_pack_bf16_pairs(val, piece)):
                o_ref[p, 0:n, :] = words
                if n < rows:
                    o_ref[p, n:rows, :] = jnp.zeros((rows - n, piece), o_ref.dtype)
        elif first:
            acc_ref[0:n, :] = part
        else:
            acc_ref[0:n, :] += part

    for q in range(1, MOE_FILL_STEPS + 1):
        n = rows * q // MOE_FILL_STEPS
        cond = active & (filled == q)
        if n_steps == 1:
            pl.when(cond)(functools.partial(step, True, True, n))
        else:
            pl.when(cond & (f == 0))(functools.partial(step, True, False, n))
            if n_steps > 2:
                pl.when(cond & (f > 0) & (f < n_steps - 1))(
                    functools.partial(step, False, False, n))
            pl.when(cond & (f == n_steps - 1))(functools.partial(step, False, True, n))

    @pl.when(jnp.logical_not(active) & (f == n_steps - 1))
    def _():
        o_ref[...] = jnp.zeros_like(o_ref)


def _moe_ffn(x_sorted, block_expert, n_used, block_fill, layer, wg, wu, wd):
    parts, n_rows, piece = x_sorted.shape
    d = 2 * parts * piece
    fdim = wg.shape[3]
    rows = MOE_ROWS
    cols = _col_block(fdim, fdim // MOE_COL_SPLIT)
    n_steps = fdim // cols

    def col(i, f, used):
        return jnp.where(i < used[0], f, n_steps - 1)

    grid_spec = pltpu.PrefetchScalarGridSpec(
        num_scalar_prefetch=3,
        grid=(n_rows // rows, n_steps),
        in_specs=[
            pl.BlockSpec((parts, rows, piece),
                         lambda i, f, be, used, fill: (0, jnp.minimum(i, used[0] - 1), 0)),
            pl.BlockSpec((1, 1, d, cols),
                         lambda i, f, be, used, fill: (layer, be[i], 0, col(i, f, used))),
            pl.BlockSpec((1, 1, d, cols),
                         lambda i, f, be, used, fill: (layer, be[i], 0, col(i, f, used))),
            pl.BlockSpec((1, 1, cols, d),
                         lambda i, f, be, used, fill: (layer, be[i], col(i, f, used), 0)),
        ],
        out_specs=pl.BlockSpec((parts, rows, piece), lambda i, f, be, used, fill: (0, i, 0)),
        scratch_shapes=[pltpu.VMEM((rows, d), BF16), pltpu.VMEM((rows, d), F32)],
    )
    kern = functools.partial(_moe_ffn_kernel, n_steps=n_steps)
    return pl.pallas_call(
        kern,
        grid_spec=grid_spec,
        out_shape=jax.ShapeDtypeStruct((parts, n_rows, piece), jnp.uint32),
        compiler_params=_params(("arbitrary", "arbitrary")),
        name="moe_ffn",
    )(block_expert, n_used, block_fill, x_sorted, wg, wu, wd)


def _cast_kernel(x_ref, o_ref):
    o_ref[...] = x_ref[...].astype(o_ref.dtype)


def _to_bf16(w):
    shape = w.shape
    w2 = w.reshape(-1, shape[-1])
    pref = CAST_BLOCK_BYTES // (4 * shape[-1]) // LANES * LANES
    rows = _row_block(w2.shape[0], pref)
    out = pl.pallas_call(
        _cast_kernel,
        grid=(w2.shape[0] // rows,),
        in_specs=[pl.BlockSpec((rows, shape[-1]), lambda i: (i, 0))],
        out_specs=pl.BlockSpec((rows, shape[-1]), lambda i: (i, 0)),
        out_shape=jax.ShapeDtypeStruct(w2.shape, BF16),
        compiler_params=_params(("arbitrary",)),
        name="cast_bf16",
    )(w2)
    return out.reshape(shape)


def _qkv_kernel(h_ref, g_ref, pos_ref, w_ref, b_ref, invf_ref, mc_ref, q_ref, kv_ref, *, q_dim):
    x = h_ref[...]
    xn = _rms(x, g_ref[...]).astype(BF16)
    half_rows = x.shape[0] // 2
    z = jnp.concatenate(
        [jnp.dot(xn[0:half_rows], w_ref[...], preferred_element_type=F32),
         jnp.dot(xn[half_rows:2 * half_rows], w_ref[...], preferred_element_type=F32)],
        axis=0) + b_ref[...]
    ang = pos_ref[...].astype(F32) * invf_ref[...]
    reps = LANES // ang.shape[0]
    cos = jnp.concatenate([jnp.cos(ang)] * reps, axis=0).T
    sin = jnp.concatenate([jnp.sin(ang)] * reps, axis=0).T
    cos = jnp.where(mc_ref[...] != 0.0, cos, 1.0)
    sin = sin * mc_ref[...]

    def rope(t):
        return t * cos + pltpu.roll(t, LANES // 2, 1) * sin

    for j in range(q_dim // LANES):
        cs = slice(j * LANES, (j + 1) * LANES)
        q_ref[:, cs] = (rope(z[:, cs]) * Q_SCALE).astype(BF16)
    kv_ref[:, 0:LANES] = rope(z[:, q_dim:q_dim + LANES]).astype(BF16)
    kv_ref[:, LANES:2 * LANES] = z[:, q_dim + LANES:q_dim + 2 * LANES].astype(BF16)


QK_GROUP = LANES // 4


def _qk_tile_layout():
    lane = np.arange(LANES)
    group, off = lane // QK_GROUP, lane % QK_GROUP
    half = ROPE_DIM // 2
    head = group % 2
    second = group // 2
    rest = QK_GROUP - half
    dim = np.where(off < half, second * half + off, ROPE_DIM + second * rest + (off - half))
    return head, dim


def _qk_column_order(n_cols):
    head, dim = _qk_tile_layout()
    tile = np.arange(n_cols) // LANES
    return tile * LANES + np.tile(head * HEAD_DIM + dim, n_cols // LANES)


def _rope_sign_lanes():
    _, dim = _qk_tile_layout()
    half = ROPE_DIM // 2
    sign = np.where(dim < half, -1.0, np.where(dim < ROPE_DIM, 1.0, 0.0)).astype(np.float32)
    return jnp.asarray(sign[None, :])


def _qkv(h, g, positions, w_qkv, b_qkv, q_dim):
    n, d = h.shape
    qkv_dim = w_qkv.shape[1]
    assert qkv_dim == q_dim + 2 * LANES and QK_GROUP % (ROPE_DIM // 2) == 0
    rows = _row_block(n, PROJ_ROWS)
    inv_freq = ROPE_THETA ** (-jnp.arange(0, ROPE_DIM, 2, dtype=F32) / ROPE_DIM)
    order = np.concatenate([_qk_column_order(q_dim + LANES), np.arange(q_dim + LANES, qkv_dim)])
    w_qkv = w_qkv[:, order]
    b_qkv = b_qkv[order]
    kern = functools.partial(_qkv_kernel, q_dim=q_dim)
    return pl.pallas_call(
        kern,
        grid=(n // rows,),
        in_specs=[
            pl.BlockSpec((rows, d), lambda i: (i, 0)),
            _full((1, d)),
            pl.BlockSpec((1, rows), lambda i: (0, i)),
            _full(w_qkv.shape),
            _full((1, qkv_dim)),
            _full((ROPE_DIM // 2, 1)),
            _full((1, LANES)),
        ],
        out_specs=[
            pl.BlockSpec((rows, q_dim), lambda i: (i, 0)),
            pl.BlockSpec((rows, 2 * LANES), lambda i: (i, 0)),
        ],
        out_shape=[
            jax.ShapeDtypeStruct((n, q_dim), BF16),
            jax.ShapeDtypeStruct((n, 2 * LANES), BF16),
        ],
        compiler_params=_params(("arbitrary",)),
        name="qkv_rope",
    )(h, g.reshape(1, d), positions.reshape(1, n), w_qkv.astype(BF16),
      b_qkv.reshape(1, qkv_dim), inv_freq.reshape(-1, 1), _rope_sign_lanes())


def _attn_kernel(sink_ref, q_ref, kvc_ref, kvp_ref, h_ref, wo_ref, bo_ref, out_ref,
                 kbuf, vbuf, o_buf, *, wpb):
    rows = q_ref.shape[0]
    tiles = q_ref.shape[1] // LANES // 2
    j = pl.program_id(1)
    kbuf[0:WINDOW, :] = kvp_ref[:, 0:LANES]
    kbuf[WINDOW:WINDOW + rows, :] = kvc_ref[:, 0:LANES]
    vbuf[0:WINDOW, :] = kvp_ref[:, LANES:2 * LANES]
    vbuf[WINDOW:WINDOW + rows, :] = kvc_ref[:, LANES:2 * LANES]

    from_prev = (lax.broadcasted_iota(jnp.int32, (WINDOW, WINDOW), 1)
                 > lax.broadcasted_iota(jnp.int32, (WINDOW, WINDOW), 0))
    kv_lane = lax.broadcasted_iota(jnp.int32, (2 * WINDOW, LANES), 1)
    first_half = kv_lane < HEAD_DIM
    head0_lanes = (kv_lane // QK_GROUP) % 2 == 0
    out_first_half = lax.broadcasted_iota(jnp.int32, (WINDOW, LANES), 1) < HEAD_DIM
    ones_lo = jnp.where(first_half, 1.0, 0.0)
    ones_hi = jnp.where(first_half, 0.0, 1.0)
    nt = (((1,), (1,)), ((), ()))

    def window(n, carry):
        r0 = pl.multiple_of(n * WINDOW, WINDOW)
        kt = kbuf[pl.ds(r0, 2 * WINDOW), :].astype(F32)
        vt = vbuf[pl.ds(r0, 2 * WINDOW), :].astype(F32)
        prev_bias = jnp.where(j * wpb + n > 0, 0.0, -jnp.inf)
        k0_lo = jnp.where(head0_lanes, kt, 0.0)
        k1_hi = jnp.where(head0_lanes, 0.0, kt)
        v0_lo = jnp.where(first_half, vt, 0.0)
        v1_hi = jnp.where(first_half, 0.0, vt)
        k_both = (
            jnp.concatenate([k0_lo, pltpu.roll(k0_lo, QK_GROUP, 1)], axis=0).astype(BF16),
            jnp.concatenate([pltpu.roll(k1_hi, LANES - QK_GROUP, 1), k1_hi], axis=0).astype(BF16))
        v_both = (
            jnp.concatenate([jnp.concatenate([v0_lo, ones_lo], axis=1),
                             jnp.concatenate([pltpu.roll(v0_lo, HEAD_DIM, 1), ones_hi], axis=1)],
                            axis=0).astype(BF16),
            jnp.concatenate([jnp.concatenate([pltpu.roll(v1_hi, HEAD_DIM, 1), ones_lo], axis=1),
                             jnp.concatenate([v1_hi, ones_hi], axis=1)], axis=0).astype(BF16))
        for kh in range(2):
            for t0 in range(0, tiles, ATTN_STACK):
                group = [kh * tiles + t0 + u for u in range(ATTN_STACK)]
                q_stack = jnp.concatenate(
                    [q_ref[pl.ds(r0, WINDOW), t * LANES:(t + 1) * LANES] for t in group], axis=0)
                s_all = lax.dot_general(q_stack, k_both[kh], nt, preferred_element_type=F32)
                p_rows, corr = [], []
                for u, t in enumerate(group):
                    p_cols, corr_t = [], []
                    for parity in range(2):
                        sink = sink_ref[t * 2 + parity]
                        c0 = parity * 2 * WINDOW
                        s_prev = s_all[u * WINDOW:(u + 1) * WINDOW, c0:c0 + WINDOW] + prev_bias
                        s_cur = s_all[u * WINDOW:(u + 1) * WINDOW, c0 + WINDOW:c0 + 2 * WINDOW]
                        s = jnp.where(from_prev, s_prev, s_cur)
                        m = jnp.maximum(jnp.max(s, axis=-1, keepdims=True), sink)
                        p = jnp.exp2(s - m)
                        p_cols += [jnp.where(from_prev, p, 0.0), jnp.where(from_prev, 0.0, p)]
                        corr_t.append(jnp.exp2(sink - m))
                    p_rows.append(jnp.concatenate(p_cols, axis=1).astype(BF16))
                    corr.append(corr_t)
                pv = jnp.dot(jnp.concatenate(p_rows, axis=0), v_both[kh],
                             preferred_element_type=F32)
                for u, t in enumerate(group):
                    num = pv[u * WINDOW:(u + 1) * WINDOW, 0:LANES]
                    den = (pv[u * WINDOW:(u + 1) * WINDOW, LANES:2 * LANES]
                           + jnp.where(out_first_half, corr[u][0], corr[u][1]))
                    o_buf[pl.ds(r0, WINDOW), t * LANES:(t + 1) * LANES] = (num / den).astype(BF16)
        return carry

    lax.fori_loop(0, rows // WINDOW, window, 0, unroll=True)
    out_ref[...] = (h_ref[...] + jnp.dot(o_buf[...], wo_ref[...], preferred_element_type=F32)
                    + bo_ref[...])


def _attention(h, q, kv, sinks, w_o, b_o, batch, seq):
    n, q_dim = q.shape
    d = h.shape[1]
    assert kv.shape[1] == 2 * LANES and (q_dim // HEAD_DIM) % 4 == 0
    rows = _row_block(seq, ATTN_ROWS)
    bps = seq // rows
    wpb = rows // WINDOW
    wps = seq // WINDOW

    grid_spec = pltpu.PrefetchScalarGridSpec(
        num_scalar_prefetch=1,
        grid=(batch, bps),
        in_specs=[
            pl.BlockSpec((rows, q_dim), lambda b, j, s: (b * bps + j, 0)),
            pl.BlockSpec((rows, 2 * LANES), lambda b, j, s: (b * bps + j, 0)),
            pl.BlockSpec((WINDOW, 2 * LANES),
                         lambda b, j, s: (b * wps + jnp.maximum(j * wpb - 1, 0), 0)),
            pl.BlockSpec((rows, d), lambda b, j, s: (b * bps + j, 0)),
            pl.BlockSpec((q_dim, d), lambda b, j, s: (0, 0)),
            pl.BlockSpec((1, d), lambda b, j, s: (0, 0)),
        ],
        out_specs=pl.BlockSpec((rows, d), lambda b, j, s: (b * bps + j, 0)),
        scratch_shapes=[pltpu.VMEM((rows + WINDOW, LANES), BF16),
                        pltpu.VMEM((rows + WINDOW, LANES), BF16),
                        pltpu.VMEM((rows, q_dim), BF16)],
    )
    kern = functools.partial(_attn_kernel, wpb=wpb)
    return pl.pallas_call(
        kern,
        grid_spec=grid_spec,
        out_shape=jax.ShapeDtypeStruct((n, d), F32),
        compiler_params=_params(("arbitrary", "arbitrary")),
        name="swa_attention",
    )(sinks.astype(F32) * LOG2_E, q, kv, kv, h, w_o.astype(BF16), b_o.reshape(1, d))


R_IDX0, R_IDX1, R_GATE0, R_GATE1, R_RANK0, R_RANK1 = range(6)


def _router_kernel(h_ref, g_ref, rwt_ref, xpk_ref, route_t_ref, cnt_ref, tri_ref, carry_ref):
    rows = h_ref.shape[0]
    i = pl.program_id(0)

    @pl.when(i == 0)
    def _():
        r = lax.broadcasted_iota(jnp.int32, (rows, rows), 0)
        c = lax.broadcasted_iota(jnp.int32, (rows, rows), 1)
        tri_ref[...] = jnp.where(r < c, 1.0, 0.0).astype(BF16)
        carry_ref[...] = jnp.zeros_like(carry_ref)

    xn = _rms(h_ref[...], g_ref[...])
    xb = xn.astype(BF16)
    for p, words in enumerate(_pack_bf16_pairs(xn, xpk_ref.shape[2])):
        xpk_ref[p] = words

    nt = (((1,), (1,)), ((), ()))
    logits = lax.dot_general(rwt_ref[...], xb, nt, preferred_element_type=F32)[0:N_EXPERTS]
    ex = lax.broadcasted_iota(jnp.int32, logits.shape, 0)
    m1 = jnp.max(logits, axis=0, keepdims=True)
    i1 = jnp.min(jnp.where(logits == m1, ex, N_EXPERTS), axis=0, keepdims=True)
    lg2 = jnp.where(ex == i1, -jnp.inf, logits)
    m2 = jnp.max(lg2, axis=0, keepdims=True)
    i2 = jnp.min(jnp.where(lg2 == m2, ex, N_EXPERTS), axis=0, keepdims=True)
    e = jnp.exp(m2 - m1)
    g1 = 1.0 / (1.0 + e)
    g2 = e / (1.0 + e)

    sel_f = jnp.where((ex == i1) | (ex == i2), 1.0, 0.0)
    sel_pad = jnp.concatenate([sel_f, jnp.zeros_like(sel_f)], axis=0).astype(BF16)
    carry = carry_ref[:, 0:1]
    before = jnp.dot(sel_pad, tri_ref[...], preferred_element_type=F32)[0:N_EXPERTS] + carry
    r1 = jnp.sum(jnp.where(ex == i1, before, 0.0), axis=0, keepdims=True)
    r2 = jnp.sum(jnp.where(ex == i2, before, 0.0), axis=0, keepdims=True)
    carry = carry + jnp.sum(sel_f, axis=1, keepdims=True)
    carry_ref[...] = jnp.broadcast_to(carry, carry_ref.shape)
    cnt_ref[...] = jnp.broadcast_to(carry, cnt_ref.shape)

    fields = [None] * 8
    for k, val in ((R_IDX0, i1.astype(F32)), (R_IDX1, i2.astype(F32)), (R_GATE0, g1),
                   (R_GATE1, g2), (R_RANK0, r1), (R_RANK1, r2)):
        fields[k] = val
    route_t = jnp.concatenate([f if f is not None else jnp.zeros_like(g1) for f in fields], axis=0)
    route_t_ref[...] = route_t


def _router(h, g, router_w):
    n, d = h.shape
    rows = _row_block(n, ROUTER_ROWS)
    parts = d // 2 // SC_PIECE
    rwt = jnp.zeros((16, d), BF16).at[0:N_EXPERTS, :].set(router_w.T.astype(BF16))
    return pl.pallas_call(
        _router_kernel,
        grid=(n // rows,),
        in_specs=[
            pl.BlockSpec((rows, d), lambda i: (i, 0)),
            _full((1, d)),
            _full((16, d)),
        ],
        out_specs=[
            pl.BlockSpec((parts, rows, SC_PIECE), lambda i: (0, i, 0)),
            pl.BlockSpec((8, rows), lambda i: (0, i)),
            _full((8, LANES)),
        ],
        out_shape=[
            jax.ShapeDtypeStruct((parts, n, SC_PIECE), jnp.uint32),
            jax.ShapeDtypeStruct((8, n), F32),
            jax.ShapeDtypeStruct((8, LANES), F32),
        ],
        scratch_shapes=[pltpu.VMEM((rows, rows), BF16), pltpu.VMEM((8, LANES), F32)],
        compiler_params=_params(("arbitrary",)),
        name="moe_router",
    )(h, g.reshape(1, d), rwt)


def _sc_mesh():
    return plsc.VectorSubcoreMesh(core_axis_name="core", subcore_axis_name="subcore")


def _gather_pieces(src, idx):
    m = idx.shape[0]
    width = src.shape[1]
    assert m % (SC_WINDOW * SC_WORKERS) == 0

    @functools.partial(pl.kernel, out_type=jax.ShapeDtypeStruct((m, width), src.dtype),
                       mesh=_sc_mesh(), scratch_types=[])
    def gather_kernel(src_hbm, idx_hbm, out_hbm):
        def body(idx_vmem, out_vmem):
            pltpu.sync_copy(src_hbm.at[idx_vmem.at[0]], out_vmem)

        pltpu.emit_pipeline(
            body,
            grid=(m // SC_WINDOW,),
            in_specs=[pl.BlockSpec((1, SC_WINDOW), lambda i: (0, i))],
            out_specs=[pl.BlockSpec((SC_WINDOW, width), lambda i: (i, 0))],
            core_axis_name=("core", "subcore"),
            dimension_semantics=(pltpu.PARALLEL,),
        )(idx_hbm, out_hbm)

    return gather_kernel(src, idx.reshape(1, m))


def _scatter_pieces(src, idx, out_rows):
    copies, m = idx.shape
    width = src.shape[1]
    assert m == src.shape[0] and m % (SC_WINDOW * SC_WORKERS) == 0

    @functools.partial(pl.kernel, out_type=jax.ShapeDtypeStruct((out_rows, width), src.dtype),
                       mesh=_sc_mesh(), scratch_types=[])
    def scatter_kernel(src_hbm, *refs):
        idx_hbm, out_hbm = refs[:copies], refs[copies]

        def body(src_vmem, *idx_vmem):
            for iv in idx_vmem:
                pltpu.sync_copy(src_vmem, out_hbm.at[iv.at[0]])

        pltpu.emit_pipeline(
            body,
            grid=(m // SC_WINDOW,),
            in_specs=[pl.BlockSpec((SC_WINDOW, width), lambda i: (i, 0))]
            + [pl.BlockSpec((1, SC_WINDOW), lambda i: (0, i))] * copies,
            out_specs=[],
            core_axis_name=("core", "subcore"),
            dimension_semantics=(pltpu.PARALLEL,),
        )(src_hbm, *idx_hbm)

    return scatter_kernel(src, *[idx[j].reshape(1, m) for j in range(copies)])


def _moe_combined(h_ref, y_ref, route_ref):
    parts, _, _, piece = y_ref.shape
    route = jnp.concatenate([route_ref[...]] * (LANES // 8), axis=0).T
    g0 = route[:, R_GATE0:R_GATE0 + 1]
    g1 = route[:, R_GATE1:R_GATE1 + 1]
    lo, hi = [], []
    for p in range(parts):
        lo0, hi0 = _unpack_bf16_pair(y_ref[p, 0])
        lo1, hi1 = _unpack_bf16_pair(y_ref[p, 1])
        lo.append(g0 * lo0 + g1 * lo1)
        hi.append(g0 * hi0 + g1 * hi1)
    return h_ref[...] + jnp.concatenate(lo + hi, axis=1)


def _moe_specs(pending, rows, base):
    y_pairs, route = pending
    parts, _, _, piece = y_pairs.shape
    return [pl.BlockSpec((parts, 2, rows, piece), lambda i: (0, 0, i, 0)),
            pl.BlockSpec((8, rows), lambda i: (0, i + base))]


def _final_kernel(h_ref, y_ref, route_ref, g_ref, o_ref):
    o_ref[...] = _rms(_moe_combined(h_ref, y_ref, route_ref), g_ref[...])


def _final(h, pending, final_g):
    n, d = h.shape
    y_chunks, route = pending
    n_chunks = len(y_chunks)
    rows = _row_block(n // n_chunks, PROJ_ROWS)
    steps = n // rows // n_chunks
    for c, y in enumerate(y_chunks):
        base = c * steps
        h = pl.pallas_call(
            _final_kernel,
            grid=(steps,),
            in_specs=[pl.BlockSpec((rows, d), lambda i, base=base: (i + base, 0))]
            + _moe_specs((y, route), rows, base) + [_full((1, d))],
            out_specs=pl.BlockSpec((rows, d), lambda i, base=base: (i + base, 0)),
            out_shape=jax.ShapeDtypeStruct((n, d), F32),
            input_output_aliases={0: 0} if n_chunks > 1 else {},
            compiler_params=_params(("arbitrary",)),
            name="moe_combine_final_norm",
        )(h, y, route, final_g.reshape(1, d))
    return h


def _moe(h, g, router_w, layer, wg, wu, wd, n_chunks):
    n, d = h.shape
    xpk, route_t, cnt = _router(h, g, router_w)
    parts = xpk.shape[0]

    top_idx = route_t[R_IDX0:R_IDX1 + 1].astype(jnp.int32)
    rank = route_t[R_RANK0:R_RANK1 + 1].astype(jnp.int32)
    sizes = cnt[0:N_EXPERTS, 0].astype(jnp.int32)
    padded = ((sizes + MOE_ROWS - 1) // MOE_ROWS) * MOE_ROWS
    pends = jnp.cumsum(padded)
    pstarts = pends - padded
    dest = rank
    for e in range(N_EXPERTS):
        dest = dest + jnp.where(top_idx == e, pstarts[e], 0)
    n_rows = 2 * n + N_EXPERTS * MOE_ROWS
    n_blocks = n_rows // MOE_ROWS
    block_start = jnp.arange(n_blocks, dtype=jnp.int32) * MOE_ROWS
    block_expert = jnp.minimum(
        jnp.sum((block_start[:, None] >= pends[None, :]).astype(jnp.int32), axis=1),
        N_EXPERTS - 1)
    n_used = (pends[N_EXPERTS - 1:] // MOE_ROWS).astype(jnp.int32)
    group_end = jnp.sum(jnp.where(block_expert[:, None] == jnp.arange(N_EXPERTS)[None, :],
                                  (pstarts + sizes)[None, :], 0), axis=1)
    slice_rows = MOE_ROWS // MOE_FILL_STEPS
    block_fill = jnp.clip((group_end - block_start + slice_rows - 1) // slice_rows,
                           1, MOE_FILL_STEPS).astype(jnp.int32)

    off = jnp.arange(parts, dtype=jnp.int32) * n_rows
    scatter_idx = (dest[:, None, :] + off[None, :, None]).reshape(2, parts * n)
    x_sorted = _scatter_pieces(xpk.reshape(parts * n, SC_PIECE), scatter_idx, parts * n_rows)
    y_rows = _moe_ffn(x_sorted.reshape(parts, n_rows, SC_PIECE), block_expert, n_used,
                      block_fill, layer, wg, wu, wd)

    y_flat = y_rows.reshape(parts * n_rows, SC_PIECE)
    nc = n // n_chunks
    chunk_dest = dest.reshape(2, n_chunks, nc).transpose(1, 0, 2)
    gather_idx = (chunk_dest[:, None] + off[None, :, None, None]).reshape(n_chunks, -1)
    y_chunks = [_gather_pieces(y_flat, gather_idx[c]).reshape(parts, 2, nc, SC_PIECE)
                for c in range(n_chunks)]
    return y_chunks, route_t


def kernel(x, positions, final_norm_g, ev_norm1_g, ev_w_in, ev_conv_w, ev_ln_g, ev_ln_b, ev_spatial_w, ev_spatial_b, ev_w_out, ev_norm2_g, ev_ffn_wg, ev_ffn_wu, ev_ffn_wd, od_norm1_g, od_w_qkv, od_b_qkv, od_sinks, od_w_o, od_b_o, od_norm2_g, od_router_w, od_exp_wg, od_exp_wu, od_exp_wd):
    batch, seq, d = x.shape
    depth = ev_norm1_g.shape[0] + od_norm1_g.shape[0]
    assert depth % 2 == 0, "the final norm is fused into the last (odd) layer's MoE combine"
    n_q_heads = od_sinks.shape[1]
    h = x.reshape(batch * seq, d)
    experts = [od_exp_wg, od_exp_wu, od_exp_wd]
    ffn_steps = (batch * seq) // _row_block(batch * seq, FFN_ROWS)
    ride_along = all(_side_cast_ok(w, ffn_steps) for w in experts)
    if not ride_along:
        experts = [_to_bf16(w) for w in experts]
    pending = None
    pieces_per_token = 2 * (d // 2 // SC_PIECE)
    n_chunks = max(c for c in range(1, COMBINE_CHUNKS + 1)
                   if batch % c == 0
                   and (batch // c * seq * pieces_per_token) % (SC_WINDOW * SC_WORKERS) == 0)
    for layer in range(depth):
        i = layer // 2
        if layer % 2 == 0:
            h = _mixer(h, pending, seq, ev_norm1_g[i], ev_w_in[i], ev_conv_w[i], ev_ln_g[i],
                       ev_ln_b[i], ev_spatial_w[i], ev_spatial_b[i], ev_w_out[i])
            side = experts if (ride_along and layer == 0) else ()
            h, cast = _dense_ffn(h, ev_norm2_g[i], ev_ffn_wg[i], ev_ffn_wu[i], ev_ffn_wd[i], side)
            if side:
                experts = cast
        else:
            q, kv = _qkv(h, od_norm1_g[i], positions, od_w_qkv[i], od_b_qkv[i],
                         n_q_heads * HEAD_DIM)
            h = _attention(h, q, kv, od_sinks[i], od_w_o[i], od_b_o[i], batch, seq)
            pending = _moe(h, od_norm2_g[i], od_router_w[i], i, *experts, n_chunks)
    return _final(h, pending, final_norm_g).reshape(batch, seq, d)
```
